```python
import jax, jax.numpy as jnp
from jax import lax
import numpy as np

D_MODEL = 1024
BATCH = 8
SEQ = 4096
DEPTH = 1

ATT_HEADS = 8
ATT_HEAD_DIM = 64
ATT_WIDTH = ATT_HEADS * ATT_HEAD_DIM
KV_LATENT = 256
IDX_HEADS = 8
IDX_DIM = 64
TOPK_MAX = 256
Q_BLOCK = 128
HG_HEADS = 4
HG_KEY_DIM = 128
HG_VAL_DIM = 128
HG_WIDTH = HG_HEADS * HG_VAL_DIM
HG_CHUNK = 32
D_MIX = ATT_WIDTH + HG_WIDTH
N_EXPERTS = 32
TOP_K = 4
D_FF = 1024
SWIGLU_LIMIT = 7.0
SWIGLU_ALPHA = 1.702
MOE_BLOCK = 128
PLE_DIM = 256
LN_EPS = 1e-5
RMS_EPS = 1e-6
IN_SIZES = (ATT_WIDTH, KV_LATENT, IDX_HEADS * IDX_DIM, IDX_DIM, IDX_HEADS,
            HG_HEADS * HG_KEY_DIM, HG_HEADS * HG_KEY_DIM, HG_WIDTH, HG_WIDTH)
D_IN = (ATT_WIDTH + KV_LATENT + IDX_HEADS * IDX_DIM + IDX_DIM + IDX_HEADS
        + 2 * HG_HEADS * HG_KEY_DIM + 2 * HG_WIDTH)

kernel_name = "hybrid_dsa_hgrn2_moe_deepnorm"


def _layer_norm(x, g, b):
    xf = x.astype(jnp.float32)
    mu = jnp.mean(xf, axis=-1, keepdims=True)
    var = jnp.mean(jnp.square(xf - mu), axis=-1, keepdims=True)
    y = (xf - mu) * lax.rsqrt(var + LN_EPS) * g.astype(jnp.float32) + b.astype(jnp.float32)
    return y.astype(x.dtype)


def _rms_norm(x, g):
    xf = x.astype(jnp.float32)
    y = xf * lax.rsqrt(jnp.mean(xf * xf, axis=-1, keepdims=True) + RMS_EPS) * g.astype(jnp.float32)
    return y.astype(x.dtype)


def _split_cols(h, sizes):
    offs = np.cumsum(np.array(sizes))[:-1].tolist()
    return jnp.split(h, offs, axis=-1)


def _dsa_attention(q, c, iq, ik, iw, w_uk, w_uv):
    B, T, H, _ = q.shape
    topk = min(TOPK_MAX, T // 4)
    n_blocks = T // Q_BLOCK
    slopes = jnp.exp2(-(8.0 / H) * jnp.arange(1, H + 1, dtype=jnp.float32))
    q_lat = jnp.einsum('bthd,rhd->bthr', q, w_uk) * (ATT_HEAD_DIM ** -0.5)
    key_pos = jnp.arange(T, dtype=jnp.int32)

    def block(n):
        t0 = n * Q_BLOCK
        qb = lax.dynamic_slice_in_dim(q_lat, t0, Q_BLOCK, axis=1)
        iqb = lax.dynamic_slice_in_dim(iq, t0, Q_BLOCK, axis=1)
        iwb = lax.dynamic_slice_in_dim(iw, t0, Q_BLOCK, axis=1)
        q_pos = t0 + jnp.arange(Q_BLOCK, dtype=jnp.int32)
        relu_logits = jax.nn.relu(jnp.einsum('bthd,bsd->bths', iqb, ik))
        isc = jnp.einsum('bth,bths->bts', iwb, relu_logits).astype(jnp.float32)
        causal = key_pos[None, :] <= q_pos[:, None]
        isc = jnp.where(causal[None], isc, -jnp.inf)
        _, sel = lax.top_k(isc, topk)
        valid = sel <= q_pos[None, :, None]
        c_sel = jax.vmap(lambda cb, sb: cb[sb])(c, sel)
        s = jnp.einsum('bthr,btkr->bthk', qb, c_sel).astype(jnp.float32)
        dist = (q_pos[None, :, None] - sel).astype(jnp.float32)
        s = s - slopes[None, None, :, None] * dist[:, :, None, :]
        s = jnp.where(valid[:, :, None, :], s, -jnp.inf)
        pr = jax.nn.softmax(s, axis=-1).astype(c.dtype)
        o_lat = jnp.einsum('bthk,btkr->bthr', pr, c_sel)
        return jnp.einsum('bthr,rhd->bthd', o_lat, w_uv)

    out = lax.map(block, jnp.arange(n_blocks, dtype=jnp.int32))
    return out.transpose(1, 0, 2, 3, 4).reshape(B, T, H * ATT_HEAD_DIM)


def _hgrn2(q, f_logit, i, g, lower_bound, norm_g):
    B, T, _ = q.shape
    dt = q.dtype
    f32 = jnp.float32
    n_chunks = T // HG_CHUNK

    def to_chunks(z, d):
        return z.reshape(B, n_chunks, HG_CHUNK, HG_HEADS, d).transpose(0, 3, 1, 2, 4)

    forget = lower_bound + (1.0 - lower_bound) * jax.nn.sigmoid(f_logit.astype(f32))
    qc = to_chunks(jax.nn.silu(q.astype(f32)), HG_KEY_DIM)
    kc = to_chunks(1.0 - forget, HG_KEY_DIM)
    vc = to_chunks(i.astype(f32), HG_VAL_DIM)
    b = jnp.cumsum(to_chunks(jnp.log(forget), HG_KEY_DIM), axis=3)
    b_last = b[:, :, :, -1:, :]
    q_dec = qc * jnp.exp(b)
    k_inv = kc * jnp.exp(-b)
    k_end = kc * jnp.exp(b_last - b)
    causal = jnp.tril(jnp.ones((HG_CHUNK, HG_CHUNK), f32))
    scores = jnp.einsum('bhncd,bhnsd->bhncs', q_dec, k_inv) * causal
    o = jnp.einsum('bhncs,bhnse->bhnce', scores, vc)
    u = jnp.einsum('bhncd,bhnce->bhnde', k_end, vc)
    decay = jnp.exp(b_last[:, :, :, 0, :])

    def step(state, xs):
        dec, un = xs
        return dec[..., None] * state + un, state

    s0 = jnp.zeros((B, HG_HEADS, HG_KEY_DIM, HG_VAL_DIM), f32)
    _, s_prev = lax.scan(step, s0, (jnp.moveaxis(decay, 2, 0), jnp.moveaxis(u, 2, 0)))
    o = o + jnp.einsum('bhncd,nbhde->bhnce', q_dec, s_prev)
    o = o.transpose(0, 2, 3, 1, 4).reshape(B, T, HG_HEADS, HG_VAL_DIM)
    o = o * lax.rsqrt(jnp.mean(o * o, axis=-1, keepdims=True) + RMS_EPS)
    o = o.reshape(B, T, HG_WIDTH) * norm_g.astype(f32) * jax.nn.silu(g.astype(f32))
    return o.astype(dt)


def _moe_ffn(x, w_router, b_router, w_up, b_up, w_down, b_down):
    B, T, D = x.shape
    dt = x.dtype
    n_tok = B * T
    n_assign = n_tok * TOP_K
    xt = x.reshape(n_tok, D)
    logits = (xt @ w_router + b_router).astype(jnp.float32)
    top_val, top_idx = lax.top_k(logits, TOP_K)
    gates = jax.nn.softmax(top_val, axis=-1)
    e_flat = top_idx.reshape(-1)
    g_flat = gates.reshape(-1)
    t_flat = jnp.arange(n_assign, dtype=jnp.int32) // TOP_K
    order = jnp.argsort(e_flat)
    e_sorted = e_flat[order]
    counts = jnp.bincount(e_flat, length=N_EXPERTS)
    padded = (counts + MOE_BLOCK - 1) // MOE_BLOCK * MOE_BLOCK
    pad_end = jnp.cumsum(padded)
    pad_start = pad_end - padded
    raw_start = jnp.cumsum(counts) - counts
    rank = jnp.arange(n_assign, dtype=jnp.int32) - raw_start[e_sorted]
    dest = pad_start[e_sorted] + rank
    cap = (n_assign + MOE_BLOCK - 1) // MOE_BLOCK * MOE_BLOCK + N_EXPERTS * MOE_BLOCK
    n_blocks = cap // MOE_BLOCK
    slot_tok = jnp.zeros((cap,), jnp.int32).at[dest].set(t_flat[order])
    slot_gate = jnp.zeros((cap,), jnp.float32).at[dest].set(g_flat[order])
    starts = jnp.arange(n_blocks, dtype=jnp.int32) * MOE_BLOCK
    block_expert = jnp.minimum(jnp.searchsorted(pad_end, starts, side='right'), N_EXPERTS - 1)

    def run_block(args):
        tok, gate, e = args
        h = xt[tok] @ w_up[e] + b_up[e]
        glu = jnp.minimum(h[:, :D_FF], SWIGLU_LIMIT)
        lin = jnp.clip(h[:, D_FF:], -SWIGLU_LIMIT, SWIGLU_LIMIT)
        a = glu * jax.nn.sigmoid(SWIGLU_ALPHA * glu) * (lin + 1.0)
        return (a @ w_down[e] + b_down[e]) * gate[:, None].astype(dt)

    out = lax.map(run_block, (slot_tok.reshape(n_blocks, MOE_BLOCK),
                              slot_gate.reshape(n_blocks, MOE_BLOCK), block_expert))
    y = jnp.zeros_like(xt).at[slot_tok].add(out.reshape(cap, D))
    return y.reshape(B, T, D)


def setup_inputs(seed: int = 0) -> dict:
    key = jax.random.key(seed)
    ks = jax.random.split(key, 26)
    f32 = jnp.float32
    beta = (8.0 * DEPTH) ** -0.25
    L = DEPTH

    def nrm(k, shape, scale):
        return jax.random.normal(k, shape, f32) * scale

    def gain(k, shape):
        return 1.0 + 0.02 * jax.random.normal(k, shape, f32)

    def bias(k, shape):
        return 0.01 * jax.random.normal(k, shape, f32)

    return {
        "x": nrm(ks[0], (BATCH, SEQ, D_MODEL), 1.0),
        "p": nrm(ks[1], (DEPTH, BATCH, SEQ, PLE_DIM), 1.0),
        "w_in": nrm(ks[2], (L, D_MODEL, D_IN), D_MODEL ** -0.5),
        "kv_norm_g": gain(ks[3], (L, KV_LATENT)),
        "idx_k_norm_g": gain(ks[4], (L, IDX_DIM)),
        "idx_k_norm_b": bias(ks[5], (L, IDX_DIM)),
        "w_uk": nrm(ks[6], (L, KV_LATENT, ATT_HEADS, ATT_HEAD_DIM), KV_LATENT ** -0.5),
        "w_uv": nrm(ks[7], (L, KV_LATENT, ATT_HEADS, ATT_HEAD_DIM), KV_LATENT ** -0.5),
        "hg_lb_logits": 1.0 + 0.1 * jax.random.normal(ks[8], (L + 1, HG_HEADS * HG_KEY_DIM), f32),
        "hg_norm_g": gain(ks[9], (L, HG_WIDTH)),
        "w_o": nrm(ks[10], (L, D_MIX, D_MODEL), D_MIX ** -0.5 * beta),
        "ln_mix_g": gain(ks[11], (L, D_MODEL)),
        "ln_mix_b": bias(ks[12], (L, D_MODEL)),
        "w_router": nrm(ks[13], (L, D_MODEL, N_EXPERTS), D_MODEL ** -0.5),
        "b_router": bias(ks[14], (L, N_EXPERTS)),
        "w_up": nrm(ks[15], (L, N_EXPERTS, D_MODEL, 2 * D_FF), D_MODEL ** -0.5),
        "b_up": bias(ks[16], (L, N_EXPERTS, 2 * D_FF)),
        "w_down": nrm(ks[17], (L, N_EXPERTS, D_FF, D_MODEL), D_FF ** -0.5 * beta),
        "b_down": bias(ks[18], (L, N_EXPERTS, D_MODEL)),
        "ln_ffn_g": gain(ks[19], (L, D_MODEL)),
        "ln_ffn_b": bias(ks[20], (L, D_MODEL)),
        "w_ple_proj": nrm(ks[21], (L, PLE_DIM, D_MODEL), PLE_DIM ** -0.5 * beta),
        "w_ple_gate": nrm(ks[22], (L, D_MODEL, D_MODEL), D_MODEL ** -0.5),
        "ln_ple_g": gain(ks[23], (L, D_MODEL)),
        "ln_ple_b": bias(ks[24], (L, D_MODEL)),
    }


def reference(x, p, w_in, kv_norm_g, idx_k_norm_g, idx_k_norm_b, w_uk, w_uv, hg_lb_logits,
              hg_norm_g, w_o, ln_mix_g, ln_mix_b, w_router, b_router, w_up, b_up, w_down, b_down,
              ln_ffn_g, ln_ffn_b, w_ple_proj, w_ple_gate, ln_ple_g, ln_ple_b):
    B, T, _ = x.shape
    alpha = (2.0 * DEPTH) ** 0.25
    lower_bounds = jnp.cumsum(jax.nn.softmax(hg_lb_logits.astype(jnp.float32), axis=0), axis=0)
    for l in range(DEPTH):
        h = x @ w_in[l]
        a_q, a_c, i_q, i_k, i_w, g_q, g_f, g_i, g_g = _split_cols(h, IN_SIZES)
        a_q = a_q.reshape(B, T, ATT_HEADS, ATT_HEAD_DIM)
        a_c = _rms_norm(a_c, kv_norm_g[l])
        i_q = i_q.reshape(B, T, IDX_HEADS, IDX_DIM) * (IDX_DIM ** -0.5)
        i_k = _layer_norm(i_k, idx_k_norm_g[l], idx_k_norm_b[l])
        i_w = i_w * (IDX_HEADS ** -0.5)
        y_att = _dsa_attention(a_q, a_c, i_q, i_k, i_w, w_uk[l], w_uv[l])
        y_rec = _hgrn2(g_q, g_f, g_i, g_g, lower_bounds[l], hg_norm_g[l])
        mix = jnp.concatenate([y_att, y_rec], axis=-1) @ w_o[l]
        x = _layer_norm(alpha * x + mix, ln_mix_g[l], ln_mix_b[l])
        ffn = _moe_ffn(x, w_router[l], b_router[l], w_up[l], b_up[l], w_down[l], b_down[l])
        x = _layer_norm(alpha * x + ffn, ln_ffn_g[l], ln_ffn_b[l])
        ple = (p[l] @ w_ple_proj[l]) * jax.nn.sigmoid(x @ w_ple_gate[l])
        x = _layer_norm(alpha * x + ple, ln_ple_g[l], ln_ple_b[l])
    return x
```

```python
import functools

import jax
import jax.numpy as jnp
from jax import lax
from jax.experimental import pallas as pl
from jax.experimental.pallas import tpu as pltpu

F32 = jnp.float32
BF16 = jnp.bfloat16
I32 = jnp.int32

ATT_HEADS = 8
ATT_HEAD_DIM = 64
KV_LATENT = 256
IDX_HEADS = 8
IDX_DIM = 64
TOPK_MAX = 256
HG_HEADS = 4
HG_DIM = 128
HG_CHUNK = 32
N_EXPERTS = 32
TOP_K = 4
SWIGLU_LIMIT = 7.0
SWIGLU_ALPHA = 1.702
LN_EPS = 1e-5
RMS_EPS = 1e-6

LANES = 128
Q_ROWS = 128
KEY_CHUNK = 512
MOE_ROWS = 128
MOE_WIN = 256
VMEM_LIMIT = 56 * 1024 * 1024

INT_MIN = -(2 ** 31)
NEG_BIG = -1e30

_C_AQ = 0
_C_KV = _C_AQ + ATT_HEADS * ATT_HEAD_DIM
_C_IQ = _C_KV + KV_LATENT
_C_IKW = _C_IQ + IDX_HEADS * LANES
_C_HG = _C_IKW + LANES
_C_END = _C_HG + 4 * HG_HEADS * HG_DIM


def _dot(a, b):
    return jnp.dot(a, b, preferred_element_type=F32)


def _dot_nt(a, b):
    return lax.dot_general(a, b, (((1,), (1,)), ((), ())), preferred_element_type=F32)


def _dot_tn(a, b):
    return lax.dot_general(a, b, (((0,), (0,)), ((), ())), preferred_element_type=F32)


def _layer_norm(z, g, b):
    mu = jnp.mean(z, axis=-1, keepdims=True)
    d = z - mu
    var = jnp.mean(d * d, axis=-1, keepdims=True)
    return d * lax.rsqrt(var + LN_EPS) * g + b


def _sigmoid(x):
    return 1.0 / (1.0 + jnp.exp(-x))


def _inproj_kernel(x_ref, w_ref, wukbd_ref, kvg_ref, ikg_ref, ikb_ref,
                   ql_ref, c_ref, iq_ref, ik_ref, iw_ref, hg_ref):
    xb = x_ref[...].astype(BF16)

    def mm(lo, hi):
        return _dot(xb, w_ref[:, lo:hi])

    aq = mm(_C_AQ, _C_KV).astype(BF16)
    qlat = _dot(aq, wukbd_ref[...]) * (ATT_HEAD_DIM ** -0.5)
    for h in range(ATT_HEADS):
        ql_ref[h] = qlat[:, h * KV_LATENT:(h + 1) * KV_LATENT].astype(BF16)

    ac = mm(_C_KV, _C_IQ)
    c = ac * lax.rsqrt(jnp.mean(ac * ac, axis=-1, keepdims=True) + RMS_EPS) * kvg_ref[...]
    c_ref[...] = c.astype(BF16)

    iq_ref[...] = (mm(_C_IQ, _C_IKW) * (IDX_DIM ** -0.5)).astype(BF16)

    ikw = mm(_C_IKW, _C_HG)
    lane = lax.broadcasted_iota(I32, ikw.shape, 1)
    is_k = lane < IDX_DIM
    mu = jnp.sum(jnp.where(is_k, ikw, 0.0), axis=-1, keepdims=True) * (1.0 / IDX_DIM)
    d = jnp.where(is_k, ikw - mu, 0.0)
    var = jnp.sum(d * d, axis=-1, keepdims=True) * (1.0 / IDX_DIM)
    ik = d * lax.rsqrt(var + LN_EPS) * ikg_ref[...] + ikb_ref[...]
    ik_ref[...] = ik.astype(BF16)
    iw_ref[...] = ikw * (IDX_HEADS ** -0.5)

    hg_ref[...] = mm(_C_HG, _C_END)


def _inproj(x2, w_p, wuk_bd, kvg, ikg, ikb, *, tm):
    n, d = x2.shape
    grid = (n // tm,)
    full = lambda shape: pl.BlockSpec(shape, lambda i: (0,) * len(shape))
    rows = lambda w: pl.BlockSpec((tm, w), lambda i: (i, 0))
    return pl.pallas_call(
        _inproj_kernel,
        grid=grid,
        in_specs=[rows(d), full(w_p.shape), full(wuk_bd.shape), full(kvg.shape),
                  full(ikg.shape), full(ikb.shape)],
        out_specs=[pl.BlockSpec((ATT_HEADS, tm, KV_LATENT), lambda i: (0, i, 0)),
                   rows(KV_LATENT), rows(IDX_HEADS * LANES), rows(LANES), rows(LANES),
                   rows(4 * HG_HEADS * HG_DIM)],
        out_shape=[jax.ShapeDtypeStruct((ATT_HEADS, n, KV_LATENT), BF16),
                   jax.ShapeDtypeStruct((n, KV_LATENT), BF16),
                   jax.ShapeDtypeStruct((n, IDX_HEADS * LANES), BF16),
                   jax.ShapeDtypeStruct((n, LANES), BF16),
                   jax.ShapeDtypeStruct((n, LANES), F32),
                   jax.ShapeDtypeStruct((n, 4 * HG_HEADS * HG_DIM), F32)],
        compiler_params=pltpu.CompilerParams(dimension_semantics=("arbitrary",),
                                             vmem_limit_bytes=VMEM_LIMIT),
        name="inproj",
    )(x2, w_p, wuk_bd, kvg, ikg, ikb)


def _hgrn2_kernel(hg_ref, lb_ref, ng_ref, o_ref, state_ref, *, tb):
    @pl.when(pl.program_id(1) == 0)
    def _():
        state_ref[...] = jnp.zeros_like(state_ref)

    nchunk = tb // HG_CHUNK
    width = HG_HEADS * HG_DIM
    row = lax.broadcasted_iota(I32, (tb, HG_DIM), 0)
    rin = jnp.bitwise_and(row, HG_CHUNK - 1)
    r2 = lax.broadcasted_iota(I32, (tb, tb), 0)
    c2 = lax.broadcasted_iota(I32, (tb, tb), 1)
    intra = jnp.logical_and(r2 // HG_CHUNK == c2 // HG_CHUNK, c2 <= r2)

    for h in range(HG_HEADS):
        sl = slice(h * HG_DIM, (h + 1) * HG_DIM)
        gq = hg_ref[0, :, h * HG_DIM:(h + 1) * HG_DIM]
        gf = hg_ref[0, :, width + h * HG_DIM:width + (h + 1) * HG_DIM]
        gi = hg_ref[0, :, 2 * width + h * HG_DIM:2 * width + (h + 1) * HG_DIM]
        gg = hg_ref[0, :, 3 * width + h * HG_DIM:3 * width + (h + 1) * HG_DIM]
        lb = lb_ref[:, sl]
        forget = lb + (1.0 - lb) * _sigmoid(gf)
        logf = jnp.log(forget)
        b = logf
        s = 1
        while s < HG_CHUNK:
            b = b + jnp.where(rin >= s, pltpu.roll(b, s, axis=0), 0.0)
            s *= 2
        tot = jnp.sum(logf.reshape(nchunk, HG_CHUNK, HG_DIM), axis=1, keepdims=True)
        bl = jnp.broadcast_to(tot, (nchunk, HG_CHUNK, HG_DIM)).reshape(tb, HG_DIM)
        kk = 1.0 - forget
        q_dec = (gq * _sigmoid(gq) * jnp.exp(b)).astype(BF16)
        k_inv = (kk * jnp.exp(-b)).astype(BF16)
        k_end = (kk * jnp.exp(bl - b)).astype(BF16)
        vb = gi.astype(BF16)
        scores = jnp.where(intra, _dot_nt(q_dec, k_inv), 0.0).astype(BF16)
        o = _dot(scores, vb)
        decay = jnp.exp(tot)
        st = state_ref[h]
        inter = []
        for n in range(nchunk):
            rs = slice(n * HG_CHUNK, (n + 1) * HG_CHUNK)
            inter.append(_dot_nt(q_dec[rs], st.astype(BF16)))
            st = st * decay[n] + _dot_tn(vb[rs], k_end[rs])
        state_ref[h] = st
        o = o + jnp.concatenate(inter, axis=0)
        o = o * lax.rsqrt(jnp.mean(o * o, axis=-1, keepdims=True) + RMS_EPS)
        o = o * ng_ref[:, sl] * (gg * _sigmoid(gg))
        o_ref[0, :, sl] = o.astype(o_ref.dtype)


def _hgrn2(hg3, lb, ng, *, tb):
    bsz, t, w4 = hg3.shape
    width = HG_HEADS * HG_DIM
    return pl.pallas_call(
        functools.partial(_hgrn2_kernel, tb=tb),
        grid=(bsz, t // tb),
        in_specs=[pl.BlockSpec((1, tb, w4), lambda b, i: (b, i, 0)),
                  pl.BlockSpec((1, width), lambda b, i: (0, 0)),
                  pl.BlockSpec((1, width), lambda b, i: (0, 0))],
        out_specs=pl.BlockSpec((1, tb, width), lambda b, i: (b, i, 0)),
        out_shape=jax.ShapeDtypeStruct((bsz, t, width), BF16),
        scratch_shapes=[pltpu.VMEM((HG_HEADS, HG_DIM, HG_DIM), F32)],
        compiler_params=pltpu.CompilerParams(dimension_semantics=("arbitrary", "arbitrary"),
                                             vmem_limit_bytes=VMEM_LIMIT),
        name="hgrn2",
    )(hg3, lb, ng)


def _dsa_kernel(ql_ref, iq_ref, iw_ref, ik_ref, c_ref, wuv_ref, o_ref,
                keys_ref, thr_ref, cut_ref, m_ref, l_ref, acc_ref, *, topk, pos_bits):
    n = pl.program_id(1)
    t0 = n * Q_ROWS
    nch = (t0 + Q_ROWS + KEY_CHUNK - 1) // KEY_CHUNK
    kf = float(topk)
    tq = t0 + lax.broadcasted_iota(I32, (Q_ROWS, KEY_CHUNK), 0)
    lane_k = lax.broadcasted_iota(I32, (Q_ROWS, KEY_CHUNK), 1)

    head_w = [iw_ref[0, :, IDX_DIM + h:IDX_DIM + h + 1] for h in range(IDX_HEADS)]

    def index_body(j, carry):
        k0 = pl.multiple_of(j * KEY_CHUNK, KEY_CHUNK)
        ikc = ik_ref[0, pl.ds(k0, KEY_CHUNK), :]
        isc = jnp.zeros((Q_ROWS, KEY_CHUNK), F32)
        for h in range(IDX_HEADS):
            z = _dot_nt(iq_ref[0, :, h * LANES:(h + 1) * LANES], ikc)
            isc = isc + head_w[h] * jnp.maximum(z, 0.0)
        isc = jnp.where(isc == 0.0, 0.0, isc)
        bits = pltpu.bitcast(isc, I32)
        key = jnp.bitwise_xor(bits, jnp.bitwise_and(jnp.right_shift(bits, 31), 0x7FFFFFFF))
        key = jnp.where(k0 + lane_k <= tq, key, INT_MIN)
        keys_ref[:, pl.ds(k0, KEY_CHUNK)] = key
        return carry

    lax.fori_loop(0, nch, index_body, 0)

    def count(pred):
        def body(j, acc):
            k0 = pl.multiple_of(j * KEY_CHUNK, KEY_CHUNK)
            hit = jnp.where(pred(keys_ref[:, pl.ds(k0, KEY_CHUNK)], k0), 1.0, 0.0)
            for q in range(KEY_CHUNK // LANES):
                acc = acc + hit[:, q * LANES:(q + 1) * LANES]
            return acc
        acc = lax.fori_loop(0, nch, body, jnp.zeros((Q_ROWS, LANES), F32))
        return jnp.sum(acc, axis=-1, keepdims=True)

    def count_ge(cand):
        return count(lambda k, k0: k >= cand)

    thr0 = jnp.where(count_ge(jnp.zeros((Q_ROWS, 1), I32)) >= kf, 0, INT_MIN).astype(I32)

    def bit_body(i, thr):
        cand = thr + lax.shift_left(jnp.int32(1), 30 - i)
        return jnp.where(count_ge(cand) >= kf, cand, thr)

    thr = lax.fori_loop(0, 31, bit_body, thr0)
    live = thr > INT_MIN
    thr_ref[...] = jnp.broadcast_to(thr, thr_ref.shape)
    cut_ref[...] = jnp.broadcast_to(jnp.where(live, jnp.int32(2 ** 30), -1), cut_ref.shape)
    n_ge = count_ge(thr)
    tie = jnp.max(jnp.where(jnp.logical_and(live, n_ge > kf), 1.0, 0.0))

    @pl.when(tie > 0.0)
    def _():
        need = kf - count(lambda k, k0: k > thr)

        def cut_body(i, cut):
            cand = cut + lax.shift_left(jnp.int32(1), pos_bits - 1 - i)
            below = count(lambda k, k0: jnp.logical_and(k == thr, k0 + lane_k < cand))
            return jnp.where(below < need, cand, cut)

        cut = lax.fori_loop(0, pos_bits, cut_body, jnp.zeros((Q_ROWS, 1), I32))
        cut_ref[...] = jnp.broadcast_to(jnp.where(live, cut, -1), cut_ref.shape)

    m_ref[...] = jnp.full(m_ref.shape, NEG_BIG, F32)
    l_ref[...] = jnp.zeros(l_ref.shape, F32)
    acc_ref[...] = jnp.zeros(acc_ref.shape, F32)
    thr_b = thr_ref[:, 0:1]
    cut_b = cut_ref[:, 0:1]
    q_all = ql_ref[...].reshape(ATT_HEADS * Q_ROWS, KV_LATENT)

    def attn_body(j, carry):
        k0 = pl.multiple_of(j * KEY_CHUNK, KEY_CHUNK)
        key = keys_ref[:, pl.ds(k0, KEY_CHUNK)]
        pos = k0 + lane_k
        sel = jnp.logical_or(key > thr_b, jnp.logical_and(key == thr_b, pos <= cut_b))
        cc = c_ref[0, pl.ds(k0, KEY_CHUNK), :]
        s_all = _dot_nt(q_all, cc)
        rel = (pos[0:1, :] - t0).astype(F32)
        ps = []
        for h in range(ATT_HEADS):
            rs = slice(h * Q_ROWS, (h + 1) * Q_ROWS)
            slope = 2.0 ** (-(8.0 / ATT_HEADS) * (h + 1))
            s = jnp.where(sel, s_all[rs] + slope * rel, NEG_BIG)
            m_old = m_ref[rs, 0:1]
            m_new = jnp.maximum(m_old, jnp.max(s, axis=-1, keepdims=True))
            alpha = jnp.exp(m_old - m_new)
            p = jnp.exp(s - m_new)
            l_ref[rs, :] = jnp.broadcast_to(alpha * l_ref[rs, 0:1] + jnp.sum(p, axis=-1, keepdims=True),
                                            (Q_ROWS, LANES))
            m_ref[rs, :] = jnp.broadcast_to(m_new, (Q_ROWS, LANES))
            acc_ref[rs, :] = acc_ref[rs, :] * alpha
            ps.append(p.astype(BF16))
        acc_ref[...] += _dot(jnp.concatenate(ps, axis=0), cc)
        return carry

    lax.fori_loop(0, nch, attn_body, 0)

    y = jnp.zeros((Q_ROWS, ATT_HEADS * ATT_HEAD_DIM), F32)
    for h in range(ATT_HEADS):
        rs = slice(h * Q_ROWS, (h + 1) * Q_ROWS)
        o_lat = (acc_ref[rs, :] / l_ref[rs, 0:1]).astype(BF16)
        y = y + _dot(o_lat, wuv_ref[h])
    o_ref[0] = y.astype(o_ref.dtype)


def _dsa(ql, iq3, iw3, ik3, c3, wuv_p):
    bsz, t, _ = c3.shape
    nq = t // Q_ROWS
    topk = min(TOPK_MAX, t // 4)
    t_pad = -(-t // KEY_CHUNK) * KEY_CHUNK
    assert t_pad == t
    return pl.pallas_call(
        functools.partial(_dsa_kernel, topk=topk, pos_bits=(t - 1).bit_length()),
        grid=(bsz, nq),
        in_specs=[pl.BlockSpec((ATT_HEADS, Q_ROWS, KV_LATENT), lambda b, n: (0, b * nq + n, 0)),
                  pl.BlockSpec((1, Q_ROWS, IDX_HEADS * LANES), lambda b, n: (b, n, 0)),
                  pl.BlockSpec((1, Q_ROWS, LANES), lambda b, n: (b, n, 0)),
                  pl.BlockSpec((1, t, LANES), lambda b, n: (b, 0, 0)),
                  pl.BlockSpec((1, t, KV_LATENT), lambda b, n: (b, 0, 0)),
                  pl.BlockSpec(wuv_p.shape, lambda b, n: (0, 0, 0))],
        out_specs=pl.BlockSpec((1, Q_ROWS, ATT_HEADS * ATT_HEAD_DIM), lambda b, n: (b, n, 0)),
        out_shape=jax.ShapeDtypeStruct((bsz, t, ATT_HEADS * ATT_HEAD_DIM), BF16),
        scratch_shapes=[pltpu.VMEM((Q_ROWS, t), I32),
                        pltpu.VMEM((Q_ROWS, LANES), I32),
                        pltpu.VMEM((Q_ROWS, LANES), I32),
                        pltpu.VMEM((ATT_HEADS * Q_ROWS, LANES), F32),
                        pltpu.VMEM((ATT_HEADS * Q_ROWS, LANES), F32),
                        pltpu.VMEM((ATT_HEADS * Q_ROWS, KV_LATENT), F32)],
        compiler_params=pltpu.CompilerParams(dimension_semantics=("arbitrary", "arbitrary"),
                                             vmem_limit_bytes=VMEM_LIMIT),
        name="dsa",
    )(ql, iq3, iw3, ik3, c3, wuv_p)


def _mix_router_kernel(x_ref, ya_ref, yr_ref, wo_ref, g_ref, b_ref, wr_ref, br_ref,
                       x1_ref, x1b_ref, pos_ref, post_ref, gate_ref, cnt_ref, carry_ref,
                       *, alpha, steps_per_group, tm):
    i = pl.program_id(0)

    @pl.when(i % steps_per_group == 0)
    def _():
        carry_ref[...] = jnp.zeros_like(carry_ref)

    half = ya_ref.shape[1]
    mix = _dot(ya_ref[...], wo_ref[0:half, :]) + _dot(yr_ref[...], wo_ref[half:, :])
    x1 = _layer_norm(alpha * x_ref[...] + mix, g_ref[...], b_ref[...])
    x1_ref[...] = x1
    x1b_ref[...] = x1.astype(BF16)

    logits = jnp.dot(x1, wr_ref[...], preferred_element_type=F32,
                     precision=lax.Precision.HIGHEST) + br_ref[...]
    lane = lax.broadcasted_iota(I32, logits.shape, 1)
    lane_f = lane.astype(F32)
    work = jnp.where(lane < N_EXPERTS, logits, -jnp.inf)
    sel_f = jnp.zeros(logits.shape, F32)
    dense = jnp.zeros(logits.shape, F32)
    denom = jnp.zeros((tm, 1), F32)
    top = None
    for _ in range(TOP_K):
        mx = jnp.max(work, axis=-1, keepdims=True)
        first = jnp.min(jnp.where(work == mx, lane_f, float(LANES)), axis=-1, keepdims=True)
        hit = lane_f == first
        if top is None:
            top = mx
        e = jnp.exp(mx - top)
        dense = dense + jnp.where(hit, e, 0.0)
        denom = denom + e
        sel_f = sel_f + jnp.where(hit, 1.0, 0.0)
        work = jnp.where(hit, -jnp.inf, work)
    gate = dense / denom
    gate_ref[...] = gate

    sel = sel_f > 0.0
    r2 = lax.broadcasted_iota(I32, (tm, tm), 0)
    c2 = lax.broadcasted_iota(I32, (tm, tm), 1)
    before = jnp.where(c2 < r2, 1.0, 0.0).astype(BF16)
    rank = _dot(before, sel_f.astype(BF16)) + carry_ref[...]
    pos = jnp.where(sel, rank, -1.0)
    pos_ref[...] = pos
    post_ref[...] = pos.T[0:N_EXPERTS, :]
    carry_ref[...] = carry_ref[...] + jnp.sum(sel_f, axis=0, keepdims=True)
    cnt_ref[0] = jnp.sum(sel_f.reshape(tm // LANES, LANES, LANES), axis=1)


def _mix_router(x2, ya, yr, wo, g, b, wr_p, br_p, *, alpha, tm, group):
    n, d = x2.shape
    half = ya.shape[1]
    rows = lambda w: pl.BlockSpec((tm, w), lambda i: (i, 0))
    full = lambda shape: pl.BlockSpec(shape, lambda i: (0,) * len(shape))
    return pl.pallas_call(
        functools.partial(_mix_router_kernel, alpha=alpha, steps_per_group=group // tm, tm=tm),
        grid=(n // tm,),
        in_specs=[rows(d), rows(half), rows(half), full(wo.shape), full(g.shape), full(b.shape),
                  full(wr_p.shape), full(br_p.shape)],
        out_specs=[rows(d), rows(d), rows(LANES),
                   pl.BlockSpec((N_EXPERTS, tm), lambda i: (0, i)),
                   rows(LANES),
                   pl.BlockSpec((1, tm // LANES, LANES), lambda i: (i, 0, 0))],
        out_shape=[jax.ShapeDtypeStruct((n, d), F32),
                   jax.ShapeDtypeStruct((n, d), BF16),
                   jax.ShapeDtypeStruct((n, LANES), F32),
                   jax.ShapeDtypeStruct((N_EXPERTS, n), F32),
                   jax.ShapeDtypeStruct((n, LANES), F32),
                   jax.ShapeDtypeStruct((n // tm, tm // LANES, LANES), F32)],
        scratch_shapes=[pltpu.VMEM((1, LANES), F32)],
        compiler_params=pltpu.CompilerParams(dimension_semantics=("arbitrary",),
                                             vmem_limit_bytes=VMEM_LIMIT),
        name="mix_router",
    )(x2, ya, yr, wo, g, b, wr_p, br_p)


def _moe_kernel(grp_ref, exp_ref, r0_ref, c0_ref, nc_ref, first_ref, valid_ref,
                x_ref, post_ref, pos_ref, gate_ref, wup_ref, bup_ref, wdn_ref, bdn_ref,
                y_ref, xg_ref, *, d_ff):
    i = pl.program_id(0)

    @pl.when(first_ref[i] == 1)
    def _():
        y_ref[...] = jnp.zeros_like(y_ref)

    @pl.when(valid_ref[i] == 1)
    def _():
        e = exp_ref[i]
        r0 = r0_ref[i].astype(F32)
        c0 = c0_ref[i]
        nc = nc_ref[i]
        slot_s = lax.broadcasted_iota(I32, (MOE_ROWS, MOE_WIN), 0).astype(F32)
        slot_l = lax.broadcasted_iota(I32, (MOE_WIN, MOE_ROWS), 1).astype(F32)
        lane_e = lax.broadcasted_iota(I32, (MOE_WIN, LANES), 1)

        xg_ref[...] = jnp.zeros_like(xg_ref)

        def gather_body(j, carry):
            w0 = pl.multiple_of((c0 + j) * MOE_WIN, MOE_WIN)
            prow = post_ref[0, :, pl.ds(w0, MOE_WIN)] - r0
            onehot = jnp.where(prow == slot_s, 1.0, 0.0).astype(BF16)
            xg_ref[...] += _dot(onehot, x_ref[pl.ds(w0, MOE_WIN), :])
            return carry

        lax.fori_loop(0, nc, gather_body, 0)

        h = _dot(xg_ref[...].astype(BF16), wup_ref[0]) + bup_ref[0]
        glu = jnp.minimum(h[:, :d_ff], SWIGLU_LIMIT)
        lin = jnp.clip(h[:, d_ff:], -SWIGLU_LIMIT, SWIGLU_LIMIT)
        a = glu * _sigmoid(SWIGLU_ALPHA * glu) * (lin + 1.0)
        out = (_dot(a.astype(BF16), wdn_ref[0]) + bdn_ref[0]).astype(BF16)

        def scatter_body(j, carry):
            w0 = pl.multiple_of((c0 + j) * MOE_WIN, MOE_WIN)
            is_e = lane_e == e
            pcol = jnp.sum(jnp.where(is_e, pos_ref[pl.ds(w0, MOE_WIN), :], 0.0),
                           axis=-1, keepdims=True) - r0
            gcol = jnp.sum(jnp.where(is_e, gate_ref[pl.ds(w0, MOE_WIN), :], 0.0),
                           axis=-1, keepdims=True)
            onehot_t = jnp.where(pcol == slot_l, 1.0, 0.0).astype(BF16)
            y_ref[pl.ds(w0, MOE_WIN), :] += gcol * _dot(onehot_t, out)
            return carry

        lax.fori_loop(0, nc, scatter_body, 0)


def _moe(items, x1b, post3, pos_tm, gate_tm, wup, bup, wdn, bdn, *, group, max_items):
    n, d = x1b.shape
    d_ff = wdn.shape[1]
    grp_map = lambda i, grp, exp, *_: (grp[i], 0)
    exp_map3 = lambda i, grp, exp, *_: (exp[i], 0, 0)
    grid_spec = pltpu.PrefetchScalarGridSpec(
        num_scalar_prefetch=7,
        grid=(max_items,),
        in_specs=[pl.BlockSpec((group, d), grp_map),
                  pl.BlockSpec((1, 1, group), lambda i, grp, exp, *_: (exp[i], 0, grp[i])),
                  pl.BlockSpec((group, LANES), grp_map),
                  pl.BlockSpec((group, LANES), grp_map),
                  pl.BlockSpec((1, d, 2 * d_ff), exp_map3),
                  pl.BlockSpec((1, 1, 2 * d_ff), exp_map3),
                  pl.BlockSpec((1, d_ff, d), exp_map3),
                  pl.BlockSpec((1, 1, d), exp_map3)],
        out_specs=pl.BlockSpec((group, d), grp_map),
        scratch_shapes=[pltpu.VMEM((MOE_ROWS, d), F32)],
    )
    return pl.pallas_call(
        functools.partial(_moe_kernel, d_ff=d_ff),
        grid_spec=grid_spec,
        out_shape=jax.ShapeDtypeStruct((n, d), F32),
        compiler_params=pltpu.CompilerParams(dimension_semantics=("arbitrary",),
                                             vmem_limit_bytes=VMEM_LIMIT),
        name="moe",
    )(*items, x1b, post3, pos_tm, gate_tm, wup, bup, wdn, bdn)


def _moe_items(cnt, *, group, max_items):
    n_chunks = cnt.shape[0]
    cpg = group // LANES
    n_grp = n_chunks // cpg
    wpg = group // MOE_WIN
    c = cnt[:, :N_EXPERTS].astype(I32).reshape(n_grp, wpg, MOE_WIN // LANES, N_EXPERTS).sum(axis=2)
    cum = jnp.cumsum(c, axis=1)
    tot = cum[:, -1, :]
    nblk = (tot + MOE_ROWS - 1) // MOE_ROWS
    ends = jnp.cumsum(nblk.reshape(-1))
    starts = ends - nblk.reshape(-1)
    n_items = ends[-1]
    it = jnp.arange(max_items, dtype=I32)
    valid = it < n_items
    pair = jnp.minimum(jnp.searchsorted(ends, it, side="right"), n_grp * N_EXPERTS - 1).astype(I32)
    last_pair = jnp.minimum(jnp.searchsorted(ends, n_items - 1, side="right"),
                            n_grp * N_EXPERTS - 1).astype(I32)
    pair = jnp.where(valid, pair, last_pair)
    grp = pair // N_EXPERTS
    exp = pair % N_EXPERTS
    r0 = jnp.where(valid, (it - starts[pair]) * MOE_ROWS, 0).astype(I32)
    cum_pe = cum[grp, :, exp]
    tot_pe = tot[grp, exp]
    r_last = jnp.minimum(r0 + MOE_ROWS, tot_pe) - 1
    c0 = jnp.sum(cum_pe <= r0[:, None], axis=1).astype(I32)
    c1 = jnp.sum(cum_pe <= r_last[:, None], axis=1).astype(I32)
    c0 = jnp.minimum(c0, wpg - 1)
    c1 = jnp.minimum(c1, wpg - 1)
    nc = jnp.where(valid, c1 - c0 + 1, 0).astype(I32)
    prev_grp = jnp.concatenate([jnp.full((1,), -1, I32), grp[:-1]])
    first = jnp.logical_and(valid, grp != prev_grp).astype(I32)
    return grp, exp, r0, c0, nc, first, valid.astype(I32)


def _final_kernel(x1_ref, y_ref, p_ref, g1_ref, b1_ref, wp_ref, wg_ref, g2_ref, b2_ref, o_ref, *, alpha):
    x2 = _layer_norm(alpha * x1_ref[...] + y_ref[...], g1_ref[...], b1_ref[...])
    gate = _sigmoid(_dot(x2.astype(BF16), wg_ref[...]))
    ple = _dot(p_ref[...].astype(BF16), wp_ref[...]) * gate
    o_ref[...] = _layer_norm(alpha * x2 + ple, g2_ref[...], b2_ref[...])


def _final(x1, y, p2, g1, b1, wp, wg, g2, b2, *, alpha, tm):
    n, d = x1.shape
    rows = lambda w: pl.BlockSpec((tm, w), lambda i: (i, 0))
    full = lambda shape: pl.BlockSpec(shape, lambda i: (0,) * len(shape))
    return pl.pallas_call(
        functools.partial(_final_kernel, alpha=alpha),
        grid=(n // tm,),
        in_specs=[rows(d), rows(d), rows(p2.shape[1]), full(g1.shape), full(b1.shape),
                  full(wp.shape), full(wg.shape), full(g2.shape), full(b2.shape)],
        out_specs=rows(d),
        out_shape=jax.ShapeDtypeStruct((n, d), F32),
        compiler_params=pltpu.CompilerParams(dimension_semantics=("arbitrary",),
                                             vmem_limit_bytes=VMEM_LIMIT),
        name="final",
    )(x1, y, p2, g1, b1, wp, wg, g2, b2)


def _pack_w_in(w_in):
    d = w_in.shape[0]
    sizes = (ATT_HEADS * ATT_HEAD_DIM, KV_LATENT, IDX_HEADS * IDX_DIM, IDX_DIM, IDX_HEADS,
             HG_HEADS * HG_DIM, HG_HEADS * HG_DIM, HG_HEADS * HG_DIM, HG_HEADS * HG_DIM)
    offs = [0]
    for s in sizes:
        offs.append(offs[-1] + s)
    sec = [w_in[:, offs[k]:offs[k + 1]] for k in range(len(sizes))]
    iq = sec[2].reshape(d, IDX_HEADS, IDX_DIM)
    iq = jnp.pad(iq, ((0, 0), (0, 0), (0, LANES - IDX_DIM))).reshape(d, IDX_HEADS * LANES)
    ikw = jnp.pad(jnp.concatenate([sec[3], sec[4]], axis=1), ((0, 0), (0, LANES - IDX_DIM - IDX_HEADS)))
    return jnp.concatenate([sec[0], sec[1], iq, ikw, sec[5], sec[6], sec[7], sec[8]], axis=1).astype(BF16)


def _block_diag_uk(w_uk):
    eye = jnp.eye(ATT_HEADS, dtype=w_uk.dtype)
    bd = jnp.einsum("rhd,hg->hdgr", w_uk, eye)
    return bd.reshape(ATT_HEADS * ATT_HEAD_DIM, ATT_HEADS * KV_LATENT).astype(BF16)


def _padded_uv(w_uv):
    eye = jnp.eye(ATT_HEADS, dtype=w_uv.dtype)
    return jnp.einsum("rhd,hg->hrgd", w_uv, eye).reshape(
        ATT_HEADS, KV_LATENT, ATT_HEADS * ATT_HEAD_DIM).astype(BF16)


def _layer(x, p_l, w_in, kv_g, ik_g, ik_b, w_uk, w_uv, lb, hg_ng, w_o, ln_mix_g, ln_mix_b,
           w_router, b_router, w_up, b_up, w_down, b_down, ln_ffn_g, ln_ffn_b,
           w_ple_proj, w_ple_gate, ln_ple_g, ln_ple_b, *, alpha, tm, hg_tb, group):
    bsz, t, d = x.shape
    n = bsz * t
    x2 = x.reshape(n, d)
    row = lambda v: v.reshape(1, -1).astype(F32)
    pad_lane = lambda v: jnp.pad(row(v), ((0, 0), (0, LANES - v.shape[-1])))

    ql, c, iq, ik, iw, hg = _inproj(x2, _pack_w_in(w_in), _block_diag_uk(w_uk), row(kv_g),
                                    pad_lane(ik_g), pad_lane(ik_b), tm=tm)
    y_rec = _hgrn2(hg.reshape(bsz, t, -1), row(lb), row(hg_ng), tb=hg_tb)
    y_att = _dsa(ql, iq.reshape(bsz, t, -1), iw.reshape(bsz, t, -1), ik.reshape(bsz, t, -1),
                 c.reshape(bsz, t, -1), _padded_uv(w_uv))

    wr_p = jnp.pad(w_router.astype(F32), ((0, 0), (0, LANES - N_EXPERTS)))
    x1, x1b, pos_tm, post, gate_tm, cnt = _mix_router(
        x2, y_att.reshape(n, -1), y_rec.reshape(n, -1), w_o.astype(BF16), row(ln_mix_g), row(ln_mix_b),
        wr_p, pad_lane(b_router), alpha=alpha, tm=tm, group=group)

    max_items = (n // group) * (group * TOP_K // MOE_ROWS + N_EXPERTS)
    items = _moe_items(cnt.reshape(n // LANES, LANES), group=group, max_items=max_items)
    d_ff = w_down.shape[1]
    y = _moe(items, x1b, post.reshape(N_EXPERTS, 1, n), pos_tm, gate_tm,
             w_up.astype(BF16), b_up.reshape(N_EXPERTS, 1, 2 * d_ff).astype(F32),
             w_down.astype(BF16), b_down.reshape(N_EXPERTS, 1, d).astype(F32),
             group=group, max_items=max_items)

    out = _final(x1, y, p_l.reshape(n, -1), row(ln_ffn_g), row(ln_ffn_b), w_ple_proj.astype(BF16),
                 w_ple_gate.astype(BF16), row(ln_ple_g), row(ln_ple_b), alpha=alpha, tm=tm)
    return out.reshape(bsz, t, d)


def kernel(x, p, w_in, kv_norm_g, idx_k_norm_g, idx_k_norm_b, w_uk, w_uv, hg_lb_logits, hg_norm_g, w_o,
           ln_mix_g, ln_mix_b, w_router, b_router, w_up, b_up, w_down, b_down, ln_ffn_g, ln_ffn_b,
           w_ple_proj, w_ple_gate, ln_ple_g, ln_ple_b):
    depth = w_in.shape[0]
    alpha = (2.0 * depth) ** 0.25
    lower_bounds = jnp.cumsum(jax.nn.softmax(hg_lb_logits.astype(F32), axis=0), axis=0)
    n = x.shape[0] * x.shape[1]
    tm = min(512, n)
    hg_tb = min(512, x.shape[1])
    group = min(2048, n)
    for l in range(depth):
        x = _layer(x, p[l], w_in[l], kv_norm_g[l], idx_k_norm_g[l], idx_k_norm_b[l], w_uk[l], w_uv[l],
                   lower_bounds[l], hg_norm_g[l], w_o[l], ln_mix_g[l], ln_mix_b[l], w_router[l],
                   b_router[l], w_up[l], b_up[l], w_down[l], b_down[l], ln_ffn_g[l], ln_ffn_b[l],
                   w_ple_proj[l], w_ple_gate[l], ln_ple_g[l], ln_ple_b[l],
                   alpha=alpha, tm=tm, hg_tb=hg_tb, group=group)
    return x
```

```python
import functools

import jax
import jax.numpy as jnp
from jax import lax
from jax.experimental import pallas as pl
from jax.experimental.pallas import tpu as pltpu

F32 = jnp.float32
BF16 = jnp.bfloat16
I32 = jnp.int32

ATT_HEADS = 8
ATT_HEAD_DIM = 64
KV_LATENT = 256
IDX_HEADS = 8
IDX_DIM = 64
TOPK_MAX = 256
HG_HEADS = 4
HG_DIM = 128
HG_CHUNK = 32
N_EXPERTS = 32
TOP_K = 4
SWIGLU_LIMIT = 7.0
SWIGLU_ALPHA = 1.702
LN_EPS = 1e-5
RMS_EPS = 1e-6

LANES = 128
Q_ROWS = 128
KEY_CHUNK = 512
COUNT_ROWS = 128
MOE_ROWS = 128
MOE_WIN = 256
VMEM_LIMIT = 56 * 1024 * 1024

INT_MIN = -(2 ** 31)
NEG_BIG = -1e30
LOG2_E = 1.4426950408889634

_N_KV = 0
_N_IKW = _N_KV + KV_LATENT
_N_HG = _N_IKW + LANES
_N_END = _N_HG + 4 * HG_HEADS * HG_DIM
_T_AQ = 0
_T_IQ = _T_AQ + ATT_HEADS * ATT_HEAD_DIM
_T_IKW = _T_IQ + IDX_HEADS * LANES
_T_KV = _T_IKW + LANES
_T_END = _T_KV + KV_LATENT


def _dot(a, b):
    return jnp.dot(a, b, preferred_element_type=F32)


def _dot_nt(a, b):
    return lax.dot_general(a, b, (((1,), (1,)), ((), ())), preferred_element_type=F32)


def _dot_tn(a, b):
    return lax.dot_general(a, b, (((0,), (0,)), ((), ())), preferred_element_type=F32)


def _layer_norm(z, g, b):
    mu = jnp.mean(z, axis=-1, keepdims=True)
    d = z - mu
    var = jnp.mean(d * d, axis=-1, keepdims=True)
    return d * lax.rsqrt(var + LN_EPS) * g + b


def _sigmoid(x):
    return 1.0 / (1.0 + jnp.exp(-x))


def _inproj_kernel(x_ref, wn_ref, wt_ref, wukt_ref, kvg_ref, kvgc_ref, ikg_ref, ikb_ref,
                   qt_ref, iqt_ref, iwt_ref, c_ref, ct_ref, ik_ref, hg_ref):
    xb = x_ref[...].astype(BF16)

    aqt = _dot_nt(wt_ref[_T_AQ:_T_IQ, :], xb).astype(BF16)
    qt_ref[...] = (_dot(wukt_ref[...], aqt) * (LOG2_E * ATT_HEAD_DIM ** -0.5)).astype(BF16)
    iqt_ref[...] = (_dot_nt(wt_ref[_T_IQ:_T_IKW, :], xb) * (IDX_DIM ** -0.5)).astype(BF16)
    iwt_ref[...] = _dot_nt(wt_ref[_T_IKW:_T_KV, :], xb) * (IDX_HEADS ** -0.5)
    act = _dot_nt(wt_ref[_T_KV:_T_END, :], xb)
    ct = act * lax.rsqrt(jnp.mean(act * act, axis=0, keepdims=True) + RMS_EPS) * kvgc_ref[...]
    ct_ref[...] = ct.astype(BF16)

    ac = _dot(xb, wn_ref[:, _N_KV:_N_IKW])
    c = ac * lax.rsqrt(jnp.mean(ac * ac, axis=-1, keepdims=True) + RMS_EPS) * kvg_ref[...]
    c_ref[...] = c.astype(BF16)

    ikw = _dot(xb, wn_ref[:, _N_IKW:_N_HG])
    lane = lax.broadcasted_iota(I32, ikw.shape, 1)
    is_k = lane < IDX_DIM
    mu = jnp.sum(jnp.where(is_k, ikw, 0.0), axis=-1, keepdims=True) * (1.0 / IDX_DIM)
    d = jnp.where(is_k, ikw - mu, 0.0)
    var = jnp.sum(d * d, axis=-1, keepdims=True) * (1.0 / IDX_DIM)
    ik = d * lax.rsqrt(var + LN_EPS) * ikg_ref[...] + ikb_ref[...]
    ik_ref[...] = ik.astype(BF16)

    hg_ref[...] = _dot(xb, wn_ref[:, _N_HG:_N_END])


def _inproj(x2, w_n, w_t, wuk_t, kvg, kvg_col, ikg, ikb, *, tm):
    n, d = x2.shape
    grid = (n // tm,)
    full = lambda shape: pl.BlockSpec(shape, lambda i: (0,) * len(shape))
    rows = lambda w: pl.BlockSpec((tm, w), lambda i: (i, 0))
    cols = lambda h: pl.BlockSpec((h, tm), lambda i: (0, i))
    return pl.pallas_call(
        _inproj_kernel,
        grid=grid,
        in_specs=[rows(d), full(w_n.shape), full(w_t.shape), full(wuk_t.shape), full(kvg.shape),
                  full(kvg_col.shape), full(ikg.shape), full(ikb.shape)],
        out_specs=[cols(ATT_HEADS * KV_LATENT), cols(IDX_HEADS * LANES), cols(LANES),
                   rows(KV_LATENT), cols(KV_LATENT), rows(LANES), rows(4 * HG_HEADS * HG_DIM)],
        out_shape=[jax.ShapeDtypeStruct((ATT_HEADS * KV_LATENT, n), BF16),
                   jax.ShapeDtypeStruct((IDX_HEADS * LANES, n), BF16),
                   jax.ShapeDtypeStruct((LANES, n), F32),
                   jax.ShapeDtypeStruct((n, KV_LATENT), BF16),
                   jax.ShapeDtypeStruct((KV_LATENT, n), BF16),
                   jax.ShapeDtypeStruct((n, LANES), BF16),
                   jax.ShapeDtypeStruct((n, 4 * HG_HEADS * HG_DIM), F32)],
        compiler_params=pltpu.CompilerParams(dimension_semantics=("arbitrary",),
                                             vmem_limit_bytes=VMEM_LIMIT),
        name="inproj",
    )(x2, w_n, w_t, wuk_t, kvg, kvg_col, ikg, ikb)


def _hgrn2_kernel(hg_ref, lb_ref, ng_ref, o_ref, state_ref, *, tb):
    @pl.when(pl.program_id(1) == 0)
    def _():
        state_ref[...] = jnp.zeros_like(state_ref)

    nchunk = tb // HG_CHUNK
    width = HG_HEADS * HG_DIM
    row = lax.broadcasted_iota(I32, (tb, HG_DIM), 0)
    rin = jnp.bitwise_and(row, HG_CHUNK - 1)
    r2 = lax.broadcasted_iota(I32, (tb, tb), 0)
    c2 = lax.broadcasted_iota(I32, (tb, tb), 1)
    intra = jnp.logical_and(r2 // HG_CHUNK == c2 // HG_CHUNK, c2 <= r2)

    for h in range(HG_HEADS):
        sl = slice(h * HG_DIM, (h + 1) * HG_DIM)
        gq = hg_ref[0, :, h * HG_DIM:(h + 1) * HG_DIM]
        gf = hg_ref[0, :, width + h * HG_DIM:width + (h + 1) * HG_DIM]
        gi = hg_ref[0, :, 2 * width + h * HG_DIM:2 * width + (h + 1) * HG_DIM]
        gg = hg_ref[0, :, 3 * width + h * HG_DIM:3 * width + (h + 1) * HG_DIM]
        lb = lb_ref[:, sl]
        forget = lb + (1.0 - lb) * _sigmoid(gf)
        logf = jnp.log(forget)
        b = logf
        s = 1
        while s < HG_CHUNK:
            b = b + jnp.where(rin >= s, pltpu.roll(b, s, axis=0), 0.0)
            s *= 2
        tot = jnp.sum(logf.reshape(nchunk, HG_CHUNK, HG_DIM), axis=1, keepdims=True)
        bl = jnp.broadcast_to(tot, (nchunk, HG_CHUNK, HG_DIM)).reshape(tb, HG_DIM)
        kk = 1.0 - forget
        q_dec = (gq * _sigmoid(gq) * jnp.exp(b)).astype(BF16)
        k_inv = (kk * jnp.exp(-b)).astype(BF16)
        k_end = (kk * jnp.exp(bl - b)).astype(BF16)
        vb = gi.astype(BF16)
        scores = jnp.where(intra, _dot_nt(q_dec, k_inv), 0.0).astype(BF16)
        o = _dot(scores, vb)
        decay = jnp.exp(tot)
        st = state_ref[h]
        inter = []
        for n in range(nchunk):
            rs = slice(n * HG_CHUNK, (n + 1) * HG_CHUNK)
            inter.append(_dot_nt(q_dec[rs], st.astype(BF16)))
            st = st * decay[n] + _dot_tn(vb[rs], k_end[rs])
        state_ref[h] = st
        o = o + jnp.concatenate(inter, axis=0)
        o = o * lax.rsqrt(jnp.mean(o * o, axis=-1, keepdims=True) + RMS_EPS)
        o = o * ng_ref[:, sl] * (gg * _sigmoid(gg))
        o_ref[0, :, sl] = o.astype(o_ref.dtype)


def _hgrn2(hg3, lb, ng, *, tb):
    bsz, t, w4 = hg3.shape
    width = HG_HEADS * HG_DIM
    return pl.pallas_call(
        functools.partial(_hgrn2_kernel, tb=tb),
        grid=(bsz, t // tb),
        in_specs=[pl.BlockSpec((1, tb, w4), lambda b, i: (b, i, 0)),
                  pl.BlockSpec((1, width), lambda b, i: (0, 0)),
                  pl.BlockSpec((1, width), lambda b, i: (0, 0))],
        out_specs=pl.BlockSpec((1, tb, width), lambda b, i: (b, i, 0)),
        out_shape=jax.ShapeDtypeStruct((bsz, t, width), BF16),
        scratch_shapes=[pltpu.VMEM((HG_HEADS, HG_DIM, HG_DIM), F32)],
        compiler_params=pltpu.CompilerParams(dimension_semantics=("arbitrary", "arbitrary"),
                                             vmem_limit_bytes=VMEM_LIMIT),
        name="hgrn2",
    )(hg3, lb, ng)


def _dsa_kernel(qt_ref, iqt_ref, iwt_ref, ik_ref, c_ref, ct_ref, wuvt_ref, o_ref,
                keys_ref, cut_ref, qall_ref, iqall_ref, m_ref, l_ref, acc_ref, *, topk, pos_bits):
    n = pl.program_id(1)
    t0 = n * Q_ROWS
    nch = (t0 + Q_ROWS + KEY_CHUNK - 1) // KEY_CHUNK
    kf = float(topk)
    key_i = lax.broadcasted_iota(I32, (KEY_CHUNK, Q_ROWS), 0)
    tq = t0 + lax.broadcasted_iota(I32, (KEY_CHUNK, Q_ROWS), 1)
    pair = 2 * Q_ROWS

    for h in range(ATT_HEADS):
        qall_ref[:, h * Q_ROWS:(h + 1) * Q_ROWS] = qt_ref[h * KV_LATENT:(h + 1) * KV_LATENT, :]
    for h in range(IDX_HEADS):
        iqall_ref[:, h * Q_ROWS:(h + 1) * Q_ROWS] = iqt_ref[h * LANES:(h + 1) * LANES, :]
    head_w = [iwt_ref[IDX_DIM + h:IDX_DIM + h + 1, :] for h in range(IDX_HEADS)]

    def index_body(j, carry):
        k0 = pl.multiple_of(j * KEY_CHUNK, KEY_CHUNK)
        ikc = ik_ref[pl.ds(k0, KEY_CHUNK), :]
        isc = jnp.zeros((KEY_CHUNK, Q_ROWS), F32)
        for hp in range(IDX_HEADS // 2):
            z = _dot(ikc, iqall_ref[:, hp * pair:(hp + 1) * pair])
            for hh in range(2):
                isc = isc + head_w[2 * hp + hh] * jnp.maximum(z[:, hh * Q_ROWS:(hh + 1) * Q_ROWS], 0.0)
        isc = jnp.where(isc == 0.0, 0.0, isc)
        bits = pltpu.bitcast(isc, I32)
        key = jnp.bitwise_xor(bits, jnp.bitwise_and(jnp.right_shift(bits, 31), 0x7FFFFFFF))
        key = jnp.where(k0 + key_i <= tq, key, INT_MIN)
        keys_ref[pl.ds(k0, KEY_CHUNK), :] = key
        return carry

    lax.fori_loop(0, nch, index_body, 0)

    def count(pred):
        def body(j, acc):
            k0 = pl.multiple_of(j * KEY_CHUNK, KEY_CHUNK)
            hit = jnp.where(pred(keys_ref[pl.ds(k0, KEY_CHUNK), :], k0), 1.0, 0.0)
            return acc + jnp.sum(hit.reshape(KEY_CHUNK // COUNT_ROWS, COUNT_ROWS, Q_ROWS), axis=0)
        acc = lax.fori_loop(0, nch, body, jnp.zeros((COUNT_ROWS, Q_ROWS), F32))
        return jnp.sum(acc, axis=0, keepdims=True)

    def count_ge(cand):
        return count(lambda k, k0: k >= cand)

    thr0 = jnp.where(count_ge(jnp.zeros((1, Q_ROWS), I32)) >= kf, 0, INT_MIN).astype(I32)

    def bit_body(i, thr):
        cand = thr + lax.shift_left(jnp.int32(1), 30 - i)
        return jnp.where(count_ge(cand) >= kf, cand, thr)

    thr = lax.fori_loop(0, 31, bit_body, thr0)
    live = thr > INT_MIN
    cut_ref[...] = jnp.broadcast_to(jnp.where(live, jnp.int32(2 ** 30), -1), cut_ref.shape)
    n_ge = count_ge(thr)
    tie = jnp.max(jnp.where(jnp.logical_and(live, n_ge > kf), 1.0, 0.0))

    @pl.when(tie > 0.0)
    def _():
        need = kf - count(lambda k, k0: k > thr)

        def cut_body(i, cut):
            cand = cut + lax.shift_left(jnp.int32(1), pos_bits - 1 - i)
            below = count(lambda k, k0: jnp.logical_and(k == thr, k0 + key_i < cand))
            return jnp.where(below < need, cand, cut)

        cut = lax.fori_loop(0, pos_bits, cut_body, jnp.zeros((1, Q_ROWS), I32))
        cut_ref[...] = jnp.broadcast_to(jnp.where(live, cut, -1), cut_ref.shape)

    m_ref[...] = jnp.full(m_ref.shape, NEG_BIG, F32)
    l_ref[...] = jnp.zeros(l_ref.shape, F32)
    acc_ref[...] = jnp.zeros(acc_ref.shape, F32)
    cut_b = cut_ref[0:1, :]

    def attn_body(j, carry):
        k0 = pl.multiple_of(j * KEY_CHUNK, KEY_CHUNK)
        key = keys_ref[pl.ds(k0, KEY_CHUNK), :]
        pos = k0 + key_i
        sel = jnp.logical_or(key > thr, jnp.logical_and(key == thr, pos <= cut_b))
        rel = jnp.where(sel, (pos - t0).astype(F32), NEG_BIG)
        cc = c_ref[pl.ds(k0, KEY_CHUNK), :]
        cct = ct_ref[:, pl.ds(k0, KEY_CHUNK)]
        for hp in range(ATT_HEADS // 2):
            cs = slice(hp * pair, (hp + 1) * pair)
            s2 = _dot(cc, qall_ref[:, cs])
            ps, alphas = [], []
            for hh in range(2):
                h = 2 * hp + hh
                slope = LOG2_E * 2.0 ** (-(8.0 / ATT_HEADS) * (h + 1))
                s = s2[:, hh * Q_ROWS:(hh + 1) * Q_ROWS] + slope * rel
                m_old = m_ref[h:h + 1, :]
                m_new = jnp.maximum(m_old, jnp.max(s, axis=0, keepdims=True))
                alpha = jnp.exp2(m_old - m_new)
                p = jnp.exp2(s - m_new)
                l_ref[h:h + 1, :] = alpha * l_ref[h:h + 1, :] + jnp.sum(p, axis=0, keepdims=True)
                m_ref[h:h + 1, :] = m_new
                ps.append(p.astype(BF16))
                alphas.append(alpha)
            acc_ref[:, cs] = (acc_ref[:, cs] * jnp.concatenate(alphas, axis=1)
                              + _dot(cct, jnp.concatenate(ps, axis=1)))
        return carry

    lax.fori_loop(0, nch, attn_body, 0)

    yt = jnp.zeros((ATT_HEADS * ATT_HEAD_DIM, Q_ROWS), F32)
    for h in range(ATT_HEADS):
        o_lat = (acc_ref[:, h * Q_ROWS:(h + 1) * Q_ROWS] / l_ref[h:h + 1, :]).astype(BF16)
        yt = yt + _dot(wuvt_ref[h], o_lat)
    o_ref[...] = yt.T.astype(o_ref.dtype)


def _dsa(qt, iqt, iwt, ik, c, ct, wuvt_p, *, bsz, t):
    n = bsz * t
    nq = t // Q_ROWS
    topk = min(TOPK_MAX, t // 4)
    assert t % KEY_CHUNK == 0
    width = ATT_HEADS * ATT_HEAD_DIM
    qcols = lambda h: pl.BlockSpec((h, Q_ROWS), lambda b, i: (0, b * nq + i))
    return pl.pallas_call(
        functools.partial(_dsa_kernel, topk=topk, pos_bits=(t - 1).bit_length()),
        grid=(bsz, nq),
        in_specs=[qcols(ATT_HEADS * KV_LATENT), qcols(IDX_HEADS * LANES), qcols(LANES),
                  pl.BlockSpec((t, LANES), lambda b, i: (b, 0)),
                  pl.BlockSpec((t, KV_LATENT), lambda b, i: (b, 0)),
                  pl.BlockSpec((KV_LATENT, t), lambda b, i: (0, b)),
                  pl.BlockSpec(wuvt_p.shape, lambda b, i: (0, 0, 0))],
        out_specs=pl.BlockSpec((Q_ROWS, width), lambda b, i: (b * nq + i, 0)),
        out_shape=jax.ShapeDtypeStruct((n, width), BF16),
        scratch_shapes=[pltpu.VMEM((t, Q_ROWS), I32),
                        pltpu.VMEM((8, Q_ROWS), I32),
                        pltpu.VMEM((KV_LATENT, ATT_HEADS * Q_ROWS), BF16),
                        pltpu.VMEM((LANES, IDX_HEADS * Q_ROWS), BF16),
                        pltpu.VMEM((8, Q_ROWS), F32),
                        pltpu.VMEM((8, Q_ROWS), F32),
                        pltpu.VMEM((KV_LATENT, ATT_HEADS * Q_ROWS), F32)],
        compiler_params=pltpu.CompilerParams(dimension_semantics=("arbitrary", "arbitrary"),
                                             vmem_limit_bytes=VMEM_LIMIT),
        name="dsa",
    )(qt, iqt, iwt, ik, c, ct, wuvt_p)


def _mix_router_kernel(x_ref, ya_ref, yr_ref, wo_ref, g_ref, b_ref, wr_ref, br_ref,
                       x1_ref, x1b_ref, pos_ref, post_ref, gate_ref, cnt_ref, carry_ref,
                       *, alpha, steps_per_group, tm):
    i = pl.program_id(0)

    @pl.when(i % steps_per_group == 0)
    def _():
        carry_ref[...] = jnp.zeros_like(carry_ref)

    half = ya_ref.shape[1]
    mix = _dot(ya_ref[...], wo_ref[0:half, :]) + _dot(yr_ref[...], wo_ref[half:, :])
    x1 = _layer_norm(alpha * x_ref[...] + mix, g_ref[...], b_ref[...])
    x1_ref[...] = x1
    x1b_ref[...] = x1.astype(BF16)

    logits = jnp.dot(x1, wr_ref[...], preferred_element_type=F32,
                     precision=lax.Precision.HIGHEST) + br_ref[...]
    lane = lax.broadcasted_iota(I32, logits.shape, 1)
    lane_f = lane.astype(F32)
    work = jnp.where(lane < N_EXPERTS, logits, -jnp.inf)
    sel_f = jnp.zeros(logits.shape, F32)
    dense = jnp.zeros(logits.shape, F32)
    denom = jnp.zeros((tm, 1), F32)
    top = None
    for _ in range(TOP_K):
        mx = jnp.max(work, axis=-1, keepdims=True)
        first = jnp.min(jnp.where(work == mx, lane_f, float(LANES)), axis=-1, keepdims=True)
        hit = lane_f == first
        if top is None:
            top = mx
        e = jnp.exp(mx - top)
        dense = dense + jnp.where(hit, e, 0.0)
        denom = denom + e
        sel_f = sel_f + jnp.where(hit, 1.0, 0.0)
        work = jnp.where(hit, -jnp.inf, work)
    gate = dense / denom
    gate_ref[...] = gate

    sel = sel_f > 0.0
    r2 = lax.broadcasted_iota(I32, (tm, tm), 0)
    c2 = lax.broadcasted_iota(I32, (tm, tm), 1)
    before = jnp.where(c2 < r2, 1.0, 0.0).astype(BF16)
    rank = _dot(before, sel_f.astype(BF16)) + carry_ref[...]
    pos = jnp.where(sel, rank, -1.0)
    pos_ref[...] = pos
    post_ref[...] = pos.T[0:N_EXPERTS, :]
    carry_ref[...] = carry_ref[...] + jnp.sum(sel_f, axis=0, keepdims=True)
    cnt_ref[0] = jnp.sum(sel_f.reshape(tm // LANES, LANES, LANES), axis=1)


def _mix_router(x2, ya, yr, wo, g, b, wr_p, br_p, *, alpha, tm, group):
    n, d = x2.shape
    half = ya.shape[1]
    rows = lambda w: pl.BlockSpec((tm, w), lambda i: (i, 0))
    full = lambda shape: pl.BlockSpec(shape, lambda i: (0,) * len(shape))
    return pl.pallas_call(
        functools.partial(_mix_router_kernel, alpha=alpha, steps_per_group=group // tm, tm=tm),
        grid=(n // tm,),
        in_specs=[rows(d), rows(half), rows(half), full(wo.shape), full(g.shape), full(b.shape),
                  full(wr_p.shape), full(br_p.shape)],
        out_specs=[rows(d), rows(d), rows(LANES),
                   pl.BlockSpec((N_EXPERTS, tm), lambda i: (0, i)),
                   rows(LANES),
                   pl.BlockSpec((1, tm // LANES, LANES), lambda i: (i, 0, 0))],
        out_shape=[jax.ShapeDtypeStruct((n, d), F32),
                   jax.ShapeDtypeStruct((n, d), BF16),
                   jax.ShapeDtypeStruct((n, LANES), F32),
                   jax.ShapeDtypeStruct((N_EXPERTS, n), F32),
                   jax.ShapeDtypeStruct((n, LANES), F32),
                   jax.ShapeDtypeStruct((n // tm, tm // LANES, LANES), F32)],
        scratch_shapes=[pltpu.VMEM((1, LANES), F32)],
        compiler_params=pltpu.CompilerParams(dimension_semantics=("arbitrary",),
                                             vmem_limit_bytes=VMEM_LIMIT),
        name="mix_router",
    )(x2, ya, yr, wo, g, b, wr_p, br_p)


def _moe_kernel(grp_ref, exp_ref, r0_ref, c0_ref, nc_ref, first_ref, valid_ref,
                x_ref, post_ref, pos_ref, gate_ref, wup_ref, bup_ref, wdn_ref, bdn_ref,
                y_ref, xg_ref, *, d_ff):
    i = pl.program_id(0)

    @pl.when(first_ref[i] == 1)
    def _():
        y_ref[...] = jnp.zeros_like(y_ref)

    @pl.when(valid_ref[i] == 1)
    def _():
        e = exp_ref[i]
        r0 = r0_ref[i].astype(F32)
        c0 = c0_ref[i]
        nc = nc_ref[i]
        slot_s = lax.broadcasted_iota(I32, (MOE_ROWS, MOE_WIN), 0).astype(F32)
        slot_l = lax.broadcasted_iota(I32, (MOE_WIN, MOE_ROWS), 1).astype(F32)
        lane_e = lax.broadcasted_iota(I32, (MOE_WIN, LANES), 1)

        xg_ref[...] = jnp.zeros_like(xg_ref)

        def gather_body(j, carry):
            w0 = pl.multiple_of((c0 + j) * MOE_WIN, MOE_WIN)
            prow = post_ref[0, :, pl.ds(w0, MOE_WIN)] - r0
            onehot = jnp.where(prow == slot_s, 1.0, 0.0).astype(BF16)
            xg_ref[...] += _dot(onehot, x_ref[pl.ds(w0, MOE_WIN), :])
            return carry

        lax.fori_loop(0, nc, gather_body, 0)

        h = _dot(xg_ref[...].astype(BF16), wup_ref[0]) + bup_ref[0]
        glu = jnp.minimum(h[:, :d_ff], SWIGLU_LIMIT)
        lin = jnp.clip(h[:, d_ff:], -SWIGLU_LIMIT, SWIGLU_LIMIT)
        a = glu * _sigmoid(SWIGLU_ALPHA * glu) * (lin + 1.0)
        out = (_dot(a.astype(BF16), wdn_ref[0]) + bdn_ref[0]).astype(BF16)

        def scatter_body(j, carry):
            w0 = pl.multiple_of((c0 + j) * MOE_WIN, MOE_WIN)
            is_e = lane_e == e
            pcol = jnp.sum(jnp.where(is_e, pos_ref[pl.ds(w0, MOE_WIN), :], 0.0),
                           axis=-1, keepdims=True) - r0
            gcol = jnp.sum(jnp.where(is_e, gate_ref[pl.ds(w0, MOE_WIN), :], 0.0),
                           axis=-1, keepdims=True)
            onehot_t = jnp.where(pcol == slot_l, 1.0, 0.0).astype(BF16)
            y_ref[pl.ds(w0, MOE_WIN), :] += gcol * _dot(onehot_t, out)
            return carry

        lax.fori_loop(0, nc, scatter_body, 0)


def _moe(items, x1b, post3, pos_tm, gate_tm, wup, bup, wdn, bdn, *, group, max_items):
    n, d = x1b.shape
    d_ff = wdn.shape[1]
    grp_map = lambda i, grp, exp, *_: (grp[i], 0)
    exp_map3 = lambda i, grp, exp, *_: (exp[i], 0, 0)
    grid_spec = pltpu.PrefetchScalarGridSpec(
        num_scalar_prefetch=7,
        grid=(max_items,),
        in_specs=[pl.BlockSpec((group, d), grp_map),
                  pl.BlockSpec((1, 1, group), lambda i, grp, exp, *_: (exp[i], 0, grp[i])),
                  pl.BlockSpec((group, LANES), grp_map),
                  pl.BlockSpec((group, LANES), grp_map),
                  pl.BlockSpec((1, d, 2 * d_ff), exp_map3),
                  pl.BlockSpec((1, 1, 2 * d_ff), exp_map3),
                  pl.BlockSpec((1, d_ff, d), exp_map3),
                  pl.BlockSpec((1, 1, d), exp_map3)],
        out_specs=pl.BlockSpec((group, d), grp_map),
        scratch_shapes=[pltpu.VMEM((MOE_ROWS, d), F32)],
    )
    return pl.pallas_call(
        functools.partial(_moe_kernel, d_ff=d_ff),
        grid_spec=grid_spec,
        out_shape=jax.ShapeDtypeStruct((n, d), F32),
        compiler_params=pltpu.CompilerParams(dimension_semantics=("arbitrary",),
                                             vmem_limit_bytes=VMEM_LIMIT),
        name="moe",
    )(*items, x1b, post3, pos_tm, gate_tm, wup, bup, wdn, bdn)


def _moe_items(cnt, *, group, max_items):
    n_chunks = cnt.shape[0]
    cpg = group // LANES
    n_grp = n_chunks // cpg
    wpg = group // MOE_WIN
    c = cnt[:, :N_EXPERTS].astype(I32).reshape(n_grp, wpg, MOE_WIN // LANES, N_EXPERTS).sum(axis=2)
    cum = jnp.cumsum(c, axis=1)
    tot = cum[:, -1, :]
    nblk = (tot + MOE_ROWS - 1) // MOE_ROWS
    ends = jnp.cumsum(nblk.reshape(-1))
    starts = ends - nblk.reshape(-1)
    n_items = ends[-1]
    it = jnp.arange(max_items, dtype=I32)
    valid = it < n_items
    pair = jnp.sum(ends[None, :] <= jnp.minimum(it, n_items - 1)[:, None], axis=1).astype(I32)
    grp = pair // N_EXPERTS
    exp = pair % N_EXPERTS
    r0 = jnp.where(valid, (it - starts[pair]) * MOE_ROWS, 0).astype(I32)
    cum_pe = cum[grp, :, exp]
    tot_pe = tot[grp, exp]
    r_last = jnp.minimum(r0 + MOE_ROWS, tot_pe) - 1
    c0 = jnp.sum(cum_pe <= r0[:, None], axis=1).astype(I32)
    c1 = jnp.sum(cum_pe <= r_last[:, None], axis=1).astype(I32)
    c0 = jnp.minimum(c0, wpg - 1)
    c1 = jnp.minimum(c1, wpg - 1)
    nc = jnp.where(valid, c1 - c0 + 1, 0).astype(I32)
    prev_grp = jnp.concatenate([jnp.full((1,), -1, I32), grp[:-1]])
    first = jnp.logical_and(valid, grp != prev_grp).astype(I32)
    return grp, exp, r0, c0, nc, first, valid.astype(I32)


def _final_kernel(x1_ref, y_ref, p_ref, g1_ref, b1_ref, wp_ref, wg_ref, g2_ref, b2_ref, o_ref, *, alpha):
    x2 = _layer_norm(alpha * x1_ref[...] + y_ref[...], g1_ref[...], b1_ref[...])
    gate = _sigmoid(_dot(x2.astype(BF16), wg_ref[...]))
    ple = _dot(p_ref[...].astype(BF16), wp_ref[...]) * gate
    o_ref[...] = _layer_norm(alpha * x2 + ple, g2_ref[...], b2_ref[...])


def _final(x1, y, p2, g1, b1, wp, wg, g2, b2, *, alpha, tm):
    n, d = x1.shape
    rows = lambda w: pl.BlockSpec((tm, w), lambda i: (i, 0))
    full = lambda shape: pl.BlockSpec(shape, lambda i: (0,) * len(shape))
    return pl.pallas_call(
        functools.partial(_final_kernel, alpha=alpha),
        grid=(n // tm,),
        in_specs=[rows(d), rows(d), rows(p2.shape[1]), full(g1.shape), full(b1.shape),
                  full(wp.shape), full(wg.shape), full(g2.shape), full(b2.shape)],
        out_specs=rows(d),
        out_shape=jax.ShapeDtypeStruct((n, d), F32),
        compiler_params=pltpu.CompilerParams(dimension_semantics=("arbitrary",),
                                             vmem_limit_bytes=VMEM_LIMIT),
        name="final",
    )(x1, y, p2, g1, b1, wp, wg, g2, b2)


def _pack_w_in(w_in):
    d = w_in.shape[0]
    sizes = (ATT_HEADS * ATT_HEAD_DIM, KV_LATENT, IDX_HEADS * IDX_DIM, IDX_DIM, IDX_HEADS,
             HG_HEADS * HG_DIM, HG_HEADS * HG_DIM, HG_HEADS * HG_DIM, HG_HEADS * HG_DIM)
    offs = [0]
    for s in sizes:
        offs.append(offs[-1] + s)
    sec = [w_in[:, offs[k]:offs[k + 1]] for k in range(len(sizes))]
    iq = sec[2].reshape(d, IDX_HEADS, IDX_DIM)
    iq = jnp.pad(iq, ((0, 0), (0, 0), (0, LANES - IDX_DIM))).reshape(d, IDX_HEADS * LANES)
    ikw = jnp.pad(jnp.concatenate([sec[3], sec[4]], axis=1), ((0, 0), (0, LANES - IDX_DIM - IDX_HEADS)))
    w_n = jnp.concatenate([sec[1], ikw, sec[5], sec[6], sec[7], sec[8]], axis=1).astype(BF16)
    w_t = jnp.concatenate([sec[0], iq, ikw, sec[1]], axis=1).T.astype(BF16)
    return w_n, w_t


def _block_diag_uk_t(w_uk):
    eye = jnp.eye(ATT_HEADS, dtype=w_uk.dtype)
    bd = jnp.einsum("rhd,hg->hrgd", w_uk, eye)
    return bd.reshape(ATT_HEADS * KV_LATENT, ATT_HEADS * ATT_HEAD_DIM).astype(BF16)


def _padded_uv_t(w_uv):
    eye = jnp.eye(ATT_HEADS, dtype=w_uv.dtype)
    return jnp.einsum("rhd,hg->hgdr", w_uv, eye).reshape(
        ATT_HEADS, ATT_HEADS * ATT_HEAD_DIM, KV_LATENT).astype(BF16)


def _layer(x, p_l, w_in, kv_g, ik_g, ik_b, w_uk, w_uv, lb, hg_ng, w_o, ln_mix_g, ln_mix_b,
           w_router, b_router, w_up, b_up, w_down, b_down, ln_ffn_g, ln_ffn_b,
           w_ple_proj, w_ple_gate, ln_ple_g, ln_ple_b, *, alpha, tm, hg_tb, group):
    bsz, t, d = x.shape
    n = bsz * t
    x2 = x.reshape(n, d)
    row = lambda v: v.reshape(1, -1).astype(F32)
    pad_lane = lambda v: jnp.pad(row(v), ((0, 0), (0, LANES - v.shape[-1])))

    w_n, w_t = _pack_w_in(w_in)
    qt, iqt, iwt, c, ct, ik, hg = _inproj(x2, w_n, w_t, _block_diag_uk_t(w_uk), row(kv_g),
                                          kv_g.reshape(-1, 1).astype(F32), pad_lane(ik_g), pad_lane(ik_b), tm=tm)
    y_rec = _hgrn2(hg.reshape(bsz, t, -1), row(lb), row(hg_ng), tb=hg_tb)
    y_att = _dsa(qt, iqt, iwt, ik, c, ct, _padded_uv_t(w_uv), bsz=bsz, t=t)

    wr_p = jnp.pad(w_router.astype(F32), ((0, 0), (0, LANES - N_EXPERTS)))
    x1, x1b, pos_tm, post, gate_tm, cnt = _mix_router(
        x2, y_att, y_rec.reshape(n, -1), w_o.astype(BF16), row(ln_mix_g), row(ln_mix_b),
        wr_p, pad_lane(b_router), alpha=alpha, tm=tm, group=group)

    max_items = (n // group) * (group * TOP_K // MOE_ROWS + N_EXPERTS)
    items = _moe_items(cnt.reshape(n // LANES, LANES), group=group, max_items=max_items)
    d_ff = w_down.shape[1]
    y = _moe(items, x1b, post.reshape(N_EXPERTS, 1, n), pos_tm, gate_tm,
             w_up.astype(BF16), b_up.reshape(N_EXPERTS, 1, 2 * d_ff).astype(F32),
             w_down.astype(BF16), b_down.reshape(N_EXPERTS, 1, d).astype(F32),
             group=group, max_items=max_items)

    out = _final(x1, y, p_l.reshape(n, -1), row(ln_ffn_g), row(ln_ffn_b), w_ple_proj.astype(BF16),
                 w_ple_gate.astype(BF16), row(ln_ple_g), row(ln_ple_b), alpha=alpha, tm=tm)
    return out.reshape(bsz, t, d)


def kernel(x, p, w_in, kv_norm_g, idx_k_norm_g, idx_k_norm_b, w_uk, w_uv, hg_lb_logits, hg_norm_g, w_o,
           ln_mix_g, ln_mix_b, w_router, b_router, w_up, b_up, w_down, b_down, ln_ffn_g, ln_ffn_b,
           w_ple_proj, w_ple_gate, ln_ple_g, ln_ple_b):
    depth = w_in.shape[0]
    alpha = (2.0 * depth) ** 0.25
    lower_bounds = jnp.cumsum(jax.nn.softmax(hg_lb_logits.astype(F32), axis=0), axis=0)
    n = x.shape[0] * x.shape[1]
    tm = min(512, n)
    hg_tb = min(512, x.shape[1])
    group = min(2048, n)
    for l in range(depth):
        x = _layer(x, p[l], w_in[l], kv_norm_g[l], idx_k_norm_g[l], idx_k_norm_b[l], w_uk[l], w_uv[l],
                   lower_bounds[l], hg_norm_g[l], w_o[l], ln_mix_g[l], ln_mix_b[l], w_router[l],
                   b_router[l], w_up[l], b_up[l], w_down[l], b_down[l], ln_ffn_g[l], ln_ffn_b[l],
                   w_ple_proj[l], w_ple_gate[l], ln_ple_g[l], ln_ple_b[l],
                   alpha=alpha, tm=tm, hg_tb=hg_tb, group=group)
    return x
```

```python
import functools

import jax
import jax.numpy as jnp
from jax import lax
from jax.experimental import pallas as pl
from jax.experimental.pallas import tpu as pltpu

F32 = jnp.float32
BF16 = jnp.bfloat16
I32 = jnp.int32

ATT_HEADS = 8
ATT_HEAD_DIM = 64
KV_LATENT = 256
IDX_HEADS = 8
IDX_DIM = 64
TOPK_MAX = 256
HG_HEADS = 4
HG_DIM = 128
HG_CHUNK = 32
N_EXPERTS = 32
TOP_K = 4
SWIGLU_LIMIT = 7.0
SWIGLU_ALPHA = 1.702
LN_EPS = 1e-5
RMS_EPS = 1e-6

LANES = 128
Q_ROWS = 128
KEY_CHUNK = 512
COUNT_ROWS = 128
MOE_ROWS = 144
MOE_WIN = 1536
VMEM_LIMIT = 56 * 1024 * 1024

INT_MIN = -(2 ** 31)
NEG_BIG = -1e30
LOG2_E = 1.4426950408889634

_N_KV = 0
_N_IKW = _N_KV + KV_LATENT
_N_HG = _N_IKW + LANES
_N_END = _N_HG + 4 * HG_HEADS * HG_DIM
_T_AQ = 0
_T_IQ = _T_AQ + ATT_HEADS * ATT_HEAD_DIM
_T_IKW = _T_IQ + IDX_HEADS * LANES
_T_KV = _T_IKW + LANES
_T_END = _T_KV + KV_LATENT


def _dot(a, b):
    return jnp.dot(a, b, preferred_element_type=F32)


def _dot_nt(a, b):
    return lax.dot_general(a, b, (((1,), (1,)), ((), ())), preferred_element_type=F32)


def _dot_tn(a, b):
    return lax.dot_general(a, b, (((0,), (0,)), ((), ())), preferred_element_type=F32)


def _layer_norm(z, g, b):
    mu = jnp.mean(z, axis=-1, keepdims=True)
    d = z - mu
    var = jnp.mean(d * d, axis=-1, keepdims=True)
    return d * lax.rsqrt(var + LN_EPS) * g + b


def _sigmoid(x):
    return 1.0 / (1.0 + jnp.exp(-x))


def _inproj_kernel(x_ref, wn_ref, wt_ref, wukt_ref, kvg_ref, kvgc_ref, ikg_ref, ikb_ref,
                   qt_ref, iqt_ref, iwt_ref, c_ref, ct_ref, ik_ref, hg_ref):
    xb = x_ref[...].astype(BF16)

    aqt = _dot_nt(wt_ref[_T_AQ:_T_IQ, :], xb).astype(BF16)
    qt_ref[...] = (_dot(wukt_ref[...], aqt) * (LOG2_E * ATT_HEAD_DIM ** -0.5)).astype(BF16)
    iqt_ref[...] = (_dot_nt(wt_ref[_T_IQ:_T_IKW, :], xb) * (IDX_DIM ** -0.5)).astype(BF16)
    iwt_ref[...] = _dot_nt(wt_ref[_T_IKW:_T_KV, :], xb) * (IDX_HEADS ** -0.5)
    act = _dot_nt(wt_ref[_T_KV:_T_END, :], xb)
    ct = act * lax.rsqrt(jnp.mean(act * act, axis=0, keepdims=True) + RMS_EPS) * kvgc_ref[...]
    ct_ref[...] = ct.astype(BF16)

    ac = _dot(xb, wn_ref[:, _N_KV:_N_IKW])
    c = ac * lax.rsqrt(jnp.mean(ac * ac, axis=-1, keepdims=True) + RMS_EPS) * kvg_ref[...]
    c_ref[...] = c.astype(BF16)

    ikw = _dot(xb, wn_ref[:, _N_IKW:_N_HG])
    lane = lax.broadcasted_iota(I32, ikw.shape, 1)
    is_k = lane < IDX_DIM
    mu = jnp.sum(jnp.where(is_k, ikw, 0.0), axis=-1, keepdims=True) * (1.0 / IDX_DIM)
    d = jnp.where(is_k, ikw - mu, 0.0)
    var = jnp.sum(d * d, axis=-1, keepdims=True) * (1.0 / IDX_DIM)
    ik = d * lax.rsqrt(var + LN_EPS) * ikg_ref[...] + ikb_ref[...]
    ik_ref[...] = ik.astype(BF16)

    hg_ref[...] = _dot(xb, wn_ref[:, _N_HG:_N_END])


def _inproj(x2, w_n, w_t, wuk_t, kvg, kvg_col, ikg, ikb, *, tm):
    n, d = x2.shape
    grid = (n // tm,)
    full = lambda shape: pl.BlockSpec(shape, lambda i: (0,) * len(shape))
    rows = lambda w: pl.BlockSpec((tm, w), lambda i: (i, 0))
    cols = lambda h: pl.BlockSpec((h, tm), lambda i: (0, i))
    return pl.pallas_call(
        _inproj_kernel,
        grid=grid,
        in_specs=[rows(d), full(w_n.shape), full(w_t.shape), full(wuk_t.shape), full(kvg.shape),
                  full(kvg_col.shape), full(ikg.shape), full(ikb.shape)],
        out_specs=[cols(ATT_HEADS * KV_LATENT), cols(IDX_HEADS * LANES), cols(LANES),
                   rows(KV_LATENT), cols(KV_LATENT), rows(LANES), rows(4 * HG_HEADS * HG_DIM)],
        out_shape=[jax.ShapeDtypeStruct((ATT_HEADS * KV_LATENT, n), BF16),
                   jax.ShapeDtypeStruct((IDX_HEADS * LANES, n), BF16),
                   jax.ShapeDtypeStruct((LANES, n), F32),
                   jax.ShapeDtypeStruct((n, KV_LATENT), BF16),
                   jax.ShapeDtypeStruct((KV_LATENT, n), BF16),
                   jax.ShapeDtypeStruct((n, LANES), BF16),
                   jax.ShapeDtypeStruct((n, 4 * HG_HEADS * HG_DIM), F32)],
        compiler_params=pltpu.CompilerParams(dimension_semantics=("arbitrary",),
                                             vmem_limit_bytes=VMEM_LIMIT),
        name="inproj",
    )(x2, w_n, w_t, wuk_t, kvg, kvg_col, ikg, ikb)


def _hgrn2_kernel(hg_ref, lb_ref, ng_ref, o_ref, state_ref, *, tb):
    @pl.when(pl.program_id(1) == 0)
    def _():
        state_ref[...] = jnp.zeros_like(state_ref)

    nchunk = tb // HG_CHUNK
    width = HG_HEADS * HG_DIM
    row = lax.broadcasted_iota(I32, (tb, HG_DIM), 0)
    rin = jnp.bitwise_and(row, HG_CHUNK - 1)
    r2 = lax.broadcasted_iota(I32, (tb, tb), 0)
    c2 = lax.broadcasted_iota(I32, (tb, tb), 1)
    intra = jnp.logical_and(r2 // HG_CHUNK == c2 // HG_CHUNK, c2 <= r2)

    for h in range(HG_HEADS):
        sl = slice(h * HG_DIM, (h + 1) * HG_DIM)
        gq = hg_ref[0, :, h * HG_DIM:(h + 1) * HG_DIM]
        gf = hg_ref[0, :, width + h * HG_DIM:width + (h + 1) * HG_DIM]
        gi = hg_ref[0, :, 2 * width + h * HG_DIM:2 * width + (h + 1) * HG_DIM]
        gg = hg_ref[0, :, 3 * width + h * HG_DIM:3 * width + (h + 1) * HG_DIM]
        lb = lb_ref[:, sl]
        forget = lb + (1.0 - lb) * _sigmoid(gf)
        logf = jnp.log(forget)
        b = logf
        s = 1
        while s < HG_CHUNK:
            b = b + jnp.where(rin >= s, pltpu.roll(b, s, axis=0), 0.0)
            s *= 2
        tot = jnp.sum(logf.reshape(nchunk, HG_CHUNK, HG_DIM), axis=1, keepdims=True)
        bl = jnp.broadcast_to(tot, (nchunk, HG_CHUNK, HG_DIM)).reshape(tb, HG_DIM)
        kk = 1.0 - forget
        q_dec = (gq * _sigmoid(gq) * jnp.exp(b)).astype(BF16)
        k_inv = (kk * jnp.exp(-b)).astype(BF16)
        k_end = (kk * jnp.exp(bl - b)).astype(BF16)
        vb = gi.astype(BF16)
        scores = jnp.where(intra, _dot_nt(q_dec, k_inv), 0.0).astype(BF16)
        o = _dot(scores, vb)
        decay = jnp.exp(tot)
        st = state_ref[h]
        inter = []
        for n in range(nchunk):
            rs = slice(n * HG_CHUNK, (n + 1) * HG_CHUNK)
            inter.append(_dot_nt(q_dec[rs], st.astype(BF16)))
            st = st * decay[n] + _dot_tn(vb[rs], k_end[rs])
        state_ref[h] = st
        o = o + jnp.concatenate(inter, axis=0)
        o = o * lax.rsqrt(jnp.mean(o * o, axis=-1, keepdims=True) + RMS_EPS)
        o = o * ng_ref[:, sl] * (gg * _sigmoid(gg))
        o_ref[0, :, sl] = o.astype(o_ref.dtype)


def _hgrn2(hg3, lb, ng, *, tb):
    bsz, t, w4 = hg3.shape
    width = HG_HEADS * HG_DIM
    return pl.pallas_call(
        functools.partial(_hgrn2_kernel, tb=tb),
        grid=(bsz, t // tb),
        in_specs=[pl.BlockSpec((1, tb, w4), lambda b, i: (b, i, 0)),
                  pl.BlockSpec((1, width), lambda b, i: (0, 0)),
                  pl.BlockSpec((1, width), lambda b, i: (0, 0))],
        out_specs=pl.BlockSpec((1, tb, width), lambda b, i: (b, i, 0)),
        out_shape=jax.ShapeDtypeStruct((bsz, t, width), BF16),
        scratch_shapes=[pltpu.VMEM((HG_HEADS, HG_DIM, HG_DIM), F32)],
        compiler_params=pltpu.CompilerParams(dimension_semantics=("arbitrary", "arbitrary"),
                                             vmem_limit_bytes=VMEM_LIMIT),
        name="hgrn2",
    )(hg3, lb, ng)


def _dsa_kernel(qt_ref, iqt_ref, iwt_ref, ik_ref, c_ref, ct_ref, wuvt_ref, o_ref,
                keys_ref, cut_ref, qall_ref, iqall_ref, m_ref, l_ref, acc_ref, *, topk, pos_bits):
    n = pl.program_id(1)
    t0 = n * Q_ROWS
    nch = (t0 + Q_ROWS + KEY_CHUNK - 1) // KEY_CHUNK
    kf = float(topk)
    key_i = lax.broadcasted_iota(I32, (KEY_CHUNK, Q_ROWS), 0)
    tq = t0 + lax.broadcasted_iota(I32, (KEY_CHUNK, Q_ROWS), 1)
    pair = 2 * Q_ROWS

    for h in range(ATT_HEADS):
        qall_ref[:, h * Q_ROWS:(h + 1) * Q_ROWS] = qt_ref[h * KV_LATENT:(h + 1) * KV_LATENT, :]
    for h in range(IDX_HEADS):
        iqall_ref[:, h * Q_ROWS:(h + 1) * Q_ROWS] = iqt_ref[h * LANES:(h + 1) * LANES, :]
    head_w = [iwt_ref[IDX_DIM + h:IDX_DIM + h + 1, :] for h in range(IDX_HEADS)]

    def index_body(j, carry):
        k0 = pl.multiple_of(j * KEY_CHUNK, KEY_CHUNK)
        ikc = ik_ref[pl.ds(k0, KEY_CHUNK), :]
        isc = jnp.zeros((KEY_CHUNK, Q_ROWS), F32)
        for hp in range(IDX_HEADS // 2):
            z = _dot(ikc, iqall_ref[:, hp * pair:(hp + 1) * pair])
            for hh in range(2):
                isc = isc + head_w[2 * hp + hh] * jnp.maximum(z[:, hh * Q_ROWS:(hh + 1) * Q_ROWS], 0.0)
        isc = jnp.where(isc == 0.0, 0.0, isc)
        bits = pltpu.bitcast(isc, I32)
        key = jnp.bitwise_xor(bits, jnp.bitwise_and(jnp.right_shift(bits, 31), 0x7FFFFFFF))
        key = jnp.where(k0 + key_i <= tq, key, INT_MIN)
        keys_ref[pl.ds(k0, KEY_CHUNK), :] = key
        return carry

    lax.fori_loop(0, nch, index_body, 0)

    def count(pred):
        def body(j, acc):
            k0 = pl.multiple_of(j * KEY_CHUNK, KEY_CHUNK)
            hit = jnp.where(pred(keys_ref[pl.ds(k0, KEY_CHUNK), :], k0), 1.0, 0.0)
            return acc + jnp.sum(hit.reshape(KEY_CHUNK // COUNT_ROWS, COUNT_ROWS, Q_ROWS), axis=0)
        acc = lax.fori_loop(0, nch, body, jnp.zeros((COUNT_ROWS, Q_ROWS), F32))
        return jnp.sum(acc, axis=0, keepdims=True)

    def count_ge(cand):
        return count(lambda k, k0: k >= cand)

    thr0 = jnp.where(count_ge(jnp.zeros((1, Q_ROWS), I32)) >= kf, 0, INT_MIN).astype(I32)

    def bit_body(i, thr):
        cand = thr + lax.shift_left(jnp.int32(1), 30 - i)
        return jnp.where(count_ge(cand) >= kf, cand, thr)

    thr = lax.fori_loop(0, 31, bit_body, thr0)
    live = thr > INT_MIN
    cut_ref[...] = jnp.broadcast_to(jnp.where(live, jnp.int32(2 ** 30), -1), cut_ref.shape)
    n_ge = count_ge(thr)
    tie = jnp.max(jnp.where(jnp.logical_and(live, n_ge > kf), 1.0, 0.0))

    @pl.when(tie > 0.0)
    def _():
        need = kf - count(lambda k, k0: k > thr)

        def cut_body(i, cut):
            cand = cut + lax.shift_left(jnp.int32(1), pos_bits - 1 - i)
            below = count(lambda k, k0: jnp.logical_and(k == thr, k0 + key_i < cand))
            return jnp.where(below < need, cand, cut)

        cut = lax.fori_loop(0, pos_bits, cut_body, jnp.zeros((1, Q_ROWS), I32))
        cut_ref[...] = jnp.broadcast_to(jnp.where(live, cut, -1), cut_ref.shape)

    m_ref[...] = jnp.full(m_ref.shape, NEG_BIG, F32)
    l_ref[...] = jnp.zeros(l_ref.shape, F32)
    acc_ref[...] = jnp.zeros(acc_ref.shape, F32)
    cut_b = cut_ref[0:1, :]

    def attn_body(j, carry):
        k0 = pl.multiple_of(j * KEY_CHUNK, KEY_CHUNK)
        key = keys_ref[pl.ds(k0, KEY_CHUNK), :]
        pos = k0 + key_i
        sel = jnp.logical_or(key > thr, jnp.logical_and(key == thr, pos <= cut_b))
        rel = jnp.where(sel, (pos - t0).astype(F32), NEG_BIG)
        cc = c_ref[pl.ds(k0, KEY_CHUNK), :]
        cct = ct_ref[:, pl.ds(k0, KEY_CHUNK)]
        for hp in range(ATT_HEADS // 2):
            cs = slice(hp * pair, (hp + 1) * pair)
            s2 = _dot(cc, qall_ref[:, cs])
            ps, alphas = [], []
            for hh in range(2):
                h = 2 * hp + hh
                slope = LOG2_E * 2.0 ** (-(8.0 / ATT_HEADS) * (h + 1))
                s = s2[:, hh * Q_ROWS:(hh + 1) * Q_ROWS] + slope * rel
                m_old = m_ref[h:h + 1, :]
                m_new = jnp.maximum(m_old, jnp.max(s, axis=0, keepdims=True))
                alpha = jnp.exp2(m_old - m_new)
                p = jnp.exp2(s - m_new)
                l_ref[h:h + 1, :] = alpha * l_ref[h:h + 1, :] + jnp.sum(p, axis=0, keepdims=True)
                m_ref[h:h + 1, :] = m_new
                ps.append(p.astype(BF16))
                alphas.append(alpha)
            acc_ref[:, cs] = (acc_ref[:, cs] * jnp.concatenate(alphas, axis=1)
                              + _dot(cct, jnp.concatenate(ps, axis=1)))
        return carry

    lax.fori_loop(0, nch, attn_body, 0)

    yt = jnp.zeros((ATT_HEADS * ATT_HEAD_DIM, Q_ROWS), F32)
    for h in range(ATT_HEADS):
        o_lat = (acc_ref[:, h * Q_ROWS:(h + 1) * Q_ROWS] / l_ref[h:h + 1, :]).astype(BF16)
        yt = yt + _dot(wuvt_ref[h], o_lat)
    o_ref[...] = yt.T.astype(o_ref.dtype)


def _dsa(qt, iqt, iwt, ik, c, ct, wuvt_p, *, bsz, t):
    n = bsz * t
    nq = t // Q_ROWS
    topk = min(TOPK_MAX, t // 4)
    assert t % KEY_CHUNK == 0
    width = ATT_HEADS * ATT_HEAD_DIM
    qcols = lambda h: pl.BlockSpec((h, Q_ROWS), lambda b, i: (0, b * nq + i))
    return pl.pallas_call(
        functools.partial(_dsa_kernel, topk=topk, pos_bits=(t - 1).bit_length()),
        grid=(bsz, nq),
        in_specs=[qcols(ATT_HEADS * KV_LATENT), qcols(IDX_HEADS * LANES), qcols(LANES),
                  pl.BlockSpec((t, LANES), lambda b, i: (b, 0)),
                  pl.BlockSpec((t, KV_LATENT), lambda b, i: (b, 0)),
                  pl.BlockSpec((KV_LATENT, t), lambda b, i: (0, b)),
                  pl.BlockSpec(wuvt_p.shape, lambda b, i: (0, 0, 0))],
        out_specs=pl.BlockSpec((Q_ROWS, width), lambda b, i: (b * nq + i, 0)),
        out_shape=jax.ShapeDtypeStruct((n, width), BF16),
        scratch_shapes=[pltpu.VMEM((t, Q_ROWS), I32),
                        pltpu.VMEM((8, Q_ROWS), I32),
                        pltpu.VMEM((KV_LATENT, ATT_HEADS * Q_ROWS), BF16),
                        pltpu.VMEM((LANES, IDX_HEADS * Q_ROWS), BF16),
                        pltpu.VMEM((8, Q_ROWS), F32),
                        pltpu.VMEM((8, Q_ROWS), F32),
                        pltpu.VMEM((KV_LATENT, ATT_HEADS * Q_ROWS), F32)],
        compiler_params=pltpu.CompilerParams(dimension_semantics=("arbitrary", "arbitrary"),
                                             vmem_limit_bytes=VMEM_LIMIT),
        name="dsa",
    )(qt, iqt, iwt, ik, c, ct, wuvt_p)


def _mix_router_kernel(x_ref, ya_ref, yr_ref, wo_ref, g_ref, b_ref, wr_ref, br_ref,
                       x1_ref, x1b_ref, post_ref, gatet_ref, cnt_ref, carry_ref,
                       *, alpha, steps_per_group, tm):
    i = pl.program_id(0)

    @pl.when(i % steps_per_group == 0)
    def _():
        carry_ref[...] = jnp.zeros_like(carry_ref)

    half = ya_ref.shape[1]
    mix = _dot(ya_ref[...], wo_ref[0:half, :]) + _dot(yr_ref[...], wo_ref[half:, :])
    x1 = _layer_norm(alpha * x_ref[...] + mix, g_ref[...], b_ref[...])
    x1_ref[...] = x1
    x1b_ref[...] = x1.astype(BF16)

    logits = jnp.dot(x1, wr_ref[...], preferred_element_type=F32,
                     precision=lax.Precision.HIGHEST) + br_ref[...]
    lane = lax.broadcasted_iota(I32, logits.shape, 1)
    lane_f = lane.astype(F32)
    work = jnp.where(lane < N_EXPERTS, logits, -jnp.inf)
    sel_f = jnp.zeros(logits.shape, F32)
    dense = jnp.zeros(logits.shape, F32)
    denom = jnp.zeros((tm, 1), F32)
    top = None
    for _ in range(TOP_K):
        mx = jnp.max(work, axis=-1, keepdims=True)
        first = jnp.min(jnp.where(work == mx, lane_f, float(LANES)), axis=-1, keepdims=True)
        hit = lane_f == first
        if top is None:
            top = mx
        e = jnp.exp(mx - top)
        dense = dense + jnp.where(hit, e, 0.0)
        denom = denom + e
        sel_f = sel_f + jnp.where(hit, 1.0, 0.0)
        work = jnp.where(hit, -jnp.inf, work)
    gate = dense / denom
    gatet_ref[...] = gate.T[0:N_EXPERTS, :]

    sel = sel_f > 0.0
    r2 = lax.broadcasted_iota(I32, (tm, tm), 0)
    c2 = lax.broadcasted_iota(I32, (tm, tm), 1)
    before = jnp.where(c2 < r2, 1.0, 0.0).astype(BF16)
    rank = _dot(before, sel_f.astype(BF16)) + carry_ref[...]
    pos = jnp.where(sel, rank, -1.0)
    post_ref[...] = pos.T[0:N_EXPERTS, :]
    carry_ref[...] = carry_ref[...] + jnp.sum(sel_f, axis=0, keepdims=True)
    cnt_ref[0] = jnp.sum(sel_f.reshape(tm // LANES, LANES, LANES), axis=1)


def _mix_router(x2, ya, yr, wo, g, b, wr_p, br_p, *, alpha, tm, group):
    n, d = x2.shape
    half = ya.shape[1]
    rows = lambda w: pl.BlockSpec((tm, w), lambda i: (i, 0))
    full = lambda shape: pl.BlockSpec(shape, lambda i: (0,) * len(shape))
    return pl.pallas_call(
        functools.partial(_mix_router_kernel, alpha=alpha, steps_per_group=group // tm, tm=tm),
        grid=(n // tm,),
        in_specs=[rows(d), rows(half), rows(half), full(wo.shape), full(g.shape), full(b.shape),
                  full(wr_p.shape), full(br_p.shape)],
        out_specs=[rows(d), rows(d),
                   pl.BlockSpec((N_EXPERTS, tm), lambda i: (0, i)),
                   pl.BlockSpec((N_EXPERTS, tm), lambda i: (0, i)),
                   pl.BlockSpec((1, tm // LANES, LANES), lambda i: (i, 0, 0))],
        out_shape=[jax.ShapeDtypeStruct((n, d), F32),
                   jax.ShapeDtypeStruct((n, d), BF16),
                   jax.ShapeDtypeStruct((N_EXPERTS, n), F32),
                   jax.ShapeDtypeStruct((N_EXPERTS, n), F32),
                   jax.ShapeDtypeStruct((n // tm, tm // LANES, LANES), F32)],
        scratch_shapes=[pltpu.VMEM((1, LANES), F32)],
        compiler_params=pltpu.CompilerParams(dimension_semantics=("arbitrary",),
                                             vmem_limit_bytes=VMEM_LIMIT),
        name="mix_router",
    )(x2, ya, yr, wo, g, b, wr_p, br_p)


def _moe_kernel(grp_ref, exp_ref, r0_ref, w0_ref, narrow_ref, first_ref, valid_ref,
                x_ref, post_ref, gatet_ref, wup_ref, bup_ref, wdn_ref, bdn_ref,
                y_ref, *, d_ff, group, win):
    i = pl.program_id(0)

    @pl.when(first_ref[i] == 1)
    def _():
        y_ref[...] = jnp.zeros_like(y_ref)

    def run_item(w0, width):
        r0 = r0_ref[i].astype(F32)
        slot = lax.broadcasted_iota(I32, (MOE_ROWS, width), 0).astype(F32)
        hit = (post_ref[0, :, pl.ds(w0, width)] - r0) == slot
        onehot = jnp.where(hit, 1.0, 0.0).astype(BF16)
        gslot = jnp.sum(jnp.where(hit, gatet_ref[0, :, pl.ds(w0, width)], 0.0),
                        axis=-1, keepdims=True)
        xg = _dot(onehot, x_ref[pl.ds(w0, width), :]).astype(BF16)
        h = _dot(xg, wup_ref[0]) + bup_ref[0]
        glu = jnp.minimum(h[:, :d_ff], SWIGLU_LIMIT)
        lin = jnp.clip(h[:, d_ff:], -SWIGLU_LIMIT, SWIGLU_LIMIT)
        a = glu * _sigmoid(SWIGLU_ALPHA * glu) * (lin + 1.0)
        out = ((_dot(a.astype(BF16), wdn_ref[0]) + bdn_ref[0]) * gslot).astype(BF16)
        y_ref[pl.ds(w0, width), :] += _dot_tn(onehot, out)

    is_valid = valid_ref[i] == 1
    if win < group:
        @pl.when(jnp.logical_and(is_valid, narrow_ref[i] == 1))
        def _():
            run_item(pl.multiple_of(w0_ref[i], LANES), win)

        @pl.when(jnp.logical_and(is_valid, narrow_ref[i] == 0))
        def _():
            run_item(0, group)
    else:
        @pl.when(is_valid)
        def _():
            run_item(0, group)


def _moe(items, x1b, post3, gatet3, wup, bup, wdn, bdn, *, group, win, max_items):
    n, d = x1b.shape
    d_ff = wdn.shape[1]
    grp_map = lambda i, grp, exp, *_: (grp[i], 0)
    exp_map3 = lambda i, grp, exp, *_: (exp[i], 0, 0)
    row_map3 = lambda i, grp, exp, *_: (exp[i], 0, grp[i])
    grid_spec = pltpu.PrefetchScalarGridSpec(
        num_scalar_prefetch=7,
        grid=(max_items,),
        in_specs=[pl.BlockSpec((group, d), grp_map),
                  pl.BlockSpec((1, 1, group), row_map3),
                  pl.BlockSpec((1, 1, group), row_map3),
                  pl.BlockSpec((1, d, 2 * d_ff), exp_map3),
                  pl.BlockSpec((1, 1, 2 * d_ff), exp_map3),
                  pl.BlockSpec((1, d_ff, d), exp_map3),
                  pl.BlockSpec((1, 1, d), exp_map3)],
        out_specs=pl.BlockSpec((group, d), grp_map),
    )
    return pl.pallas_call(
        functools.partial(_moe_kernel, d_ff=d_ff, group=group, win=win),
        grid_spec=grid_spec,
        out_shape=jax.ShapeDtypeStruct((n, d), F32),
        compiler_params=pltpu.CompilerParams(dimension_semantics=("arbitrary",),
                                             vmem_limit_bytes=VMEM_LIMIT),
        name="moe",
    )(*items, x1b, post3, gatet3, wup, bup, wdn, bdn)


def _moe_items(cnt, *, group, win, max_items):
    n_chunks = cnt.shape[0]
    wpg = group // LANES
    n_grp = n_chunks // wpg
    c = cnt[:, :N_EXPERTS].astype(I32).reshape(n_grp, wpg, N_EXPERTS)
    cum = jnp.cumsum(c, axis=1)
    tot = cum[:, -1, :]
    nblk = (tot + MOE_ROWS - 1) // MOE_ROWS
    ends = jnp.cumsum(nblk.reshape(-1))
    starts = ends - nblk.reshape(-1)
    n_items = ends[-1]
    it = jnp.arange(max_items, dtype=I32)
    valid = it < n_items
    pair = jnp.sum(ends[None, :] <= jnp.minimum(it, n_items - 1)[:, None], axis=1).astype(I32)
    grp = pair // N_EXPERTS
    exp = pair % N_EXPERTS
    r0 = jnp.where(valid, (it - starts[pair]) * MOE_ROWS, 0).astype(I32)
    cum_pe = cum[grp, :, exp]
    tot_pe = tot[grp, exp]
    r_last = jnp.minimum(r0 + MOE_ROWS, tot_pe) - 1
    c0 = jnp.minimum(jnp.sum(cum_pe <= r0[:, None], axis=1), wpg - 1).astype(I32)
    c1 = jnp.minimum(jnp.sum(cum_pe <= r_last[:, None], axis=1), wpg - 1).astype(I32)
    w0 = jnp.clip(c0 * LANES, 0, max(group - win, 0)).astype(I32)
    narrow = ((c1 + 1) * LANES <= w0 + win).astype(I32)
    prev_grp = jnp.concatenate([jnp.full((1,), -1, I32), grp[:-1]])
    first = jnp.logical_and(valid, grp != prev_grp).astype(I32)
    return grp, exp, r0, w0, narrow, first, valid.astype(I32)


def _final_kernel(x1_ref, y_ref, p_ref, g1_ref, b1_ref, wp_ref, wg_ref, g2_ref, b2_ref, o_ref, *, alpha):
    x2 = _layer_norm(alpha * x1_ref[...] + y_ref[...], g1_ref[...], b1_ref[...])
    gate = _sigmoid(_dot(x2.astype(BF16), wg_ref[...]))
    ple = _dot(p_ref[...].astype(BF16), wp_ref[...]) * gate
    o_ref[...] = _layer_norm(alpha * x2 + ple, g2_ref[...], b2_ref[...])


def _final(x1, y, p2, g1, b1, wp, wg, g2, b2, *, alpha, tm):
    n, d = x1.shape
    rows = lambda w: pl.BlockSpec((tm, w), lambda i: (i, 0))
    full = lambda shape: pl.BlockSpec(shape, lambda i: (0,) * len(shape))
    return pl.pallas_call(
        functools.partial(_final_kernel, alpha=alpha),
        grid=(n // tm,),
        in_specs=[rows(d), rows(d), rows(p2.shape[1]), full(g1.shape), full(b1.shape),
                  full(wp.shape), full(wg.shape), full(g2.shape), full(b2.shape)],
        out_specs=rows(d),
        out_shape=jax.ShapeDtypeStruct((n, d), F32),
        compiler_params=pltpu.CompilerParams(dimension_semantics=("arbitrary",),
                                             vmem_limit_bytes=VMEM_LIMIT),
        name="final",
    )(x1, y, p2, g1, b1, wp, wg, g2, b2)


def _pack_w_in(w_in):
    d = w_in.shape[0]
    sizes = (ATT_HEADS * ATT_HEAD_DIM, KV_LATENT, IDX_HEADS * IDX_DIM, IDX_DIM, IDX_HEADS,
             HG_HEADS * HG_DIM, HG_HEADS * HG_DIM, HG_HEADS * HG_DIM, HG_HEADS * HG_DIM)
    offs = [0]
    for s in sizes:
        offs.append(offs[-1] + s)
    sec = [w_in[:, offs[k]:offs[k + 1]] for k in range(len(sizes))]
    iq = sec[2].reshape(d, IDX_HEADS, IDX_DIM)
    iq = jnp.pad(iq, ((0, 0), (0, 0), (0, LANES - IDX_DIM))).reshape(d, IDX_HEADS * LANES)
    ikw = jnp.pad(jnp.concatenate([sec[3], sec[4]], axis=1), ((0, 0), (0, LANES - IDX_DIM - IDX_HEADS)))
    w_n = jnp.concatenate([sec[1], ikw, sec[5], sec[6], sec[7], sec[8]], axis=1).astype(BF16)
    w_t = jnp.concatenate([sec[0], iq, ikw, sec[1]], axis=1).T.astype(BF16)
    return w_n, w_t


def _block_diag_uk_t(w_uk):
    eye = jnp.eye(ATT_HEADS, dtype=w_uk.dtype)
    bd = jnp.einsum("rhd,hg->hrgd", w_uk, eye)
    return bd.reshape(ATT_HEADS * KV_LATENT, ATT_HEADS * ATT_HEAD_DIM).astype(BF16)


def _padded_uv_t(w_uv):
    eye = jnp.eye(ATT_HEADS, dtype=w_uv.dtype)
    return jnp.einsum("rhd,hg->hgdr", w_uv, eye).reshape(
        ATT_HEADS, ATT_HEADS * ATT_HEAD_DIM, KV_LATENT).astype(BF16)


def _layer(x, p_l, w_in, kv_g, ik_g, ik_b, w_uk, w_uv, lb, hg_ng, w_o, ln_mix_g, ln_mix_b,
           w_router, b_router, w_up, b_up, w_down, b_down, ln_ffn_g, ln_ffn_b,
           w_ple_proj, w_ple_gate, ln_ple_g, ln_ple_b, *, alpha, tm, hg_tb, group):
    bsz, t, d = x.shape
    n = bsz * t
    x2 = x.reshape(n, d)
    row = lambda v: v.reshape(1, -1).astype(F32)
    pad_lane = lambda v: jnp.pad(row(v), ((0, 0), (0, LANES - v.shape[-1])))

    w_n, w_t = _pack_w_in(w_in)
    qt, iqt, iwt, c, ct, ik, hg = _inproj(x2, w_n, w_t, _block_diag_uk_t(w_uk), row(kv_g),
                                          kv_g.reshape(-1, 1).astype(F32), pad_lane(ik_g), pad_lane(ik_b), tm=tm)
    y_rec = _hgrn2(hg.reshape(bsz, t, -1), row(lb), row(hg_ng), tb=hg_tb)
    y_att = _dsa(qt, iqt, iwt, ik, c, ct, _padded_uv_t(w_uv), bsz=bsz, t=t)

    wr_p = jnp.pad(w_router.astype(F32), ((0, 0), (0, LANES - N_EXPERTS)))
    x1, x1b, post, gatet, cnt = _mix_router(
        x2, y_att, y_rec.reshape(n, -1), w_o.astype(BF16), row(ln_mix_g), row(ln_mix_b),
        wr_p, pad_lane(b_router), alpha=alpha, tm=tm, group=group)

    max_items = (n // group) * (-(-group * TOP_K // MOE_ROWS) + N_EXPERTS)
    win = min(MOE_WIN, group)
    items = _moe_items(cnt.reshape(n // LANES, LANES), group=group, win=win, max_items=max_items)
    d_ff = w_down.shape[1]
    y = _moe(items, x1b, post.reshape(N_EXPERTS, 1, n), gatet.reshape(N_EXPERTS, 1, n),
             w_up.astype(BF16), b_up.reshape(N_EXPERTS, 1, 2 * d_ff).astype(F32),
             w_down.astype(BF16), b_down.reshape(N_EXPERTS, 1, d).astype(F32),
             group=group, win=win, max_items=max_items)

    out = _final(x1, y, p_l.reshape(n, -1), row(ln_ffn_g), row(ln_ffn_b), w_ple_proj.astype(BF16),
                 w_ple_gate.astype(BF16), row(ln_ple_g), row(ln_ple_b), alpha=alpha, tm=tm)
    return out.reshape(bsz, t, d)


def kernel(x, p, w_in, kv_norm_g, idx_k_norm_g, idx_k_norm_b, w_uk, w_uv, hg_lb_logits, hg_norm_g, w_o,
           ln_mix_g, ln_mix_b, w_router, b_router, w_up, b_up, w_down, b_down, ln_ffn_g, ln_ffn_b,
           w_ple_proj, w_ple_gate, ln_ple_g, ln_ple_b):
    depth = w_in.shape[0]
    alpha = (2.0 * depth) ** 0.25
    lower_bounds = jnp.cumsum(jax.nn.softmax(hg_lb_logits.astype(F32), axis=0), axis=0)
    n = x.shape[0] * x.shape[1]
    tm = min(512, n)
    hg_tb = min(512, x.shape[1])
    group = min(2048, n)
    for l in range(depth):
        x = _layer(x, p[l], w_in[l], kv_norm_g[l], idx_k_norm_g[l], idx_k_norm_b[l], w_uk[l], w_uv[l],
                   lower_bounds[l], hg_norm_g[l], w_o[l], ln_mix_g[l], ln_mix_b[l], w_router[l],
                   b_router[l], w_up[l], b_up[l], w_down[l], b_down[l], ln_ffn_g[l], ln_ffn_b[l],
                   w_ple_proj[l], w_ple_gate[l], ln_ple_g[l], ln_ple_b[l],
                   alpha=alpha, tm=tm, hg_tb=hg_tb, group=group)
    return x
```

```python
import functools

import jax
import jax.numpy as jnp
from jax import lax
from jax.experimental import pallas as pl
from jax.experimental.pallas import tpu as pltpu
from jax.experimental.pallas import tpu_sc as plsc

F32 = jnp.float32
BF16 = jnp.bfloat16
I32 = jnp.int32

ATT_HEADS = 8
ATT_HEAD_DIM = 64
KV_LATENT = 256
IDX_HEADS = 8
IDX_DIM = 64
TOPK_MAX = 256
HG_HEADS = 4
HG_DIM = 128
HG_CHUNK = 32
N_EXPERTS = 32
TOP_K = 4
SWIGLU_LIMIT = 7.0
SWIGLU_ALPHA = 1.702
LN_EPS = 1e-5
RMS_EPS = 1e-6

LANES = 128
Q_ROWS = 128
KEY_CHUNK = 512
COUNT_ROWS = 128
MOE_ROWS = 256
SC_CORES = 2
SC_SUBCORES = 16
SC_WORKERS = SC_CORES * SC_SUBCORES
SC_ROWS = 64
VMEM_LIMIT = 56 * 1024 * 1024

INT_MIN = -(2 ** 31)
NEG_BIG = -1e30
LOG2_E = 1.4426950408889634

_N_KV = 0
_N_IKW = _N_KV + KV_LATENT
_N_HG = _N_IKW + LANES
_N_END = _N_HG + 4 * HG_HEADS * HG_DIM
_T_AQ = 0
_T_IQ = _T_AQ + ATT_HEADS * ATT_HEAD_DIM
_T_IKW = _T_IQ + IDX_HEADS * LANES
_T_KV = _T_IKW + LANES
_T_END = _T_KV + KV_LATENT


def _dot(a, b):
    return jnp.dot(a, b, preferred_element_type=F32)


def _dot_nt(a, b):
    return lax.dot_general(a, b, (((1,), (1,)), ((), ())), preferred_element_type=F32)


def _dot_tn(a, b):
    return lax.dot_general(a, b, (((0,), (0,)), ((), ())), preferred_element_type=F32)


def _layer_norm(z, g, b):
    mu = jnp.mean(z, axis=-1, keepdims=True)
    d = z - mu
    var = jnp.mean(d * d, axis=-1, keepdims=True)
    return d * lax.rsqrt(var + LN_EPS) * g + b


def _sigmoid(x):
    return 1.0 / (1.0 + jnp.exp(-x))


def _inproj_kernel(x_ref, wn_ref, wt_ref, wukt_ref, kvg_ref, kvgc_ref, ikg_ref, ikb_ref,
                   qt_ref, iqt_ref, iwt_ref, c_ref, ct_ref, ik_ref, hg_ref):
    xb = x_ref[...].astype(BF16)

    aqt = _dot_nt(wt_ref[_T_AQ:_T_IQ, :], xb).astype(BF16)
    qt_ref[...] = (_dot(wukt_ref[...], aqt) * (LOG2_E * ATT_HEAD_DIM ** -0.5)).astype(BF16)
    iqt_ref[...] = (_dot_nt(wt_ref[_T_IQ:_T_IKW, :], xb) * (IDX_DIM ** -0.5)).astype(BF16)
    iwt_ref[...] = _dot_nt(wt_ref[_T_IKW:_T_KV, :], xb) * (IDX_HEADS ** -0.5)
    act = _dot_nt(wt_ref[_T_KV:_T_END, :], xb)
    ct = act * lax.rsqrt(jnp.mean(act * act, axis=0, keepdims=True) + RMS_EPS) * kvgc_ref[...]
    ct_ref[...] = ct.astype(BF16)

    ac = _dot(xb, wn_ref[:, _N_KV:_N_IKW])
    c = ac * lax.rsqrt(jnp.mean(ac * ac, axis=-1, keepdims=True) + RMS_EPS) * kvg_ref[...]
    c_ref[...] = c.astype(BF16)

    ikw = _dot(xb, wn_ref[:, _N_IKW:_N_HG])
    lane = lax.broadcasted_iota(I32, ikw.shape, 1)
    is_k = lane < IDX_DIM
    mu = jnp.sum(jnp.where(is_k, ikw, 0.0), axis=-1, keepdims=True) * (1.0 / IDX_DIM)
    d = jnp.where(is_k, ikw - mu, 0.0)
    var = jnp.sum(d * d, axis=-1, keepdims=True) * (1.0 / IDX_DIM)
    ik = d * lax.rsqrt(var + LN_EPS) * ikg_ref[...] + ikb_ref[...]
    ik_ref[...] = ik.astype(BF16)

    hg_ref[...] = _dot(xb, wn_ref[:, _N_HG:_N_END])


def _inproj(x2, w_n, w_t, wuk_t, kvg, kvg_col, ikg, ikb, *, tm):
    n, d = x2.shape
    grid = (n // tm,)
    full = lambda shape: pl.BlockSpec(shape, lambda i: (0,) * len(shape))
    rows = lambda w: pl.BlockSpec((tm, w), lambda i: (i, 0))
    cols = lambda h: pl.BlockSpec((h, tm), lambda i: (0, i))
    return pl.pallas_call(
        _inproj_kernel,
        grid=grid,
        in_specs=[rows(d), full(w_n.shape), full(w_t.shape), full(wuk_t.shape), full(kvg.shape),
                  full(kvg_col.shape), full(ikg.shape), full(ikb.shape)],
        out_specs=[cols(ATT_HEADS * KV_LATENT), cols(IDX_HEADS * LANES), cols(LANES),
                   rows(KV_LATENT), cols(KV_LATENT), rows(LANES), rows(4 * HG_HEADS * HG_DIM)],
        out_shape=[jax.ShapeDtypeStruct((ATT_HEADS * KV_LATENT, n), BF16),
                   jax.ShapeDtypeStruct((IDX_HEADS * LANES, n), BF16),
                   jax.ShapeDtypeStruct((LANES, n), F32),
                   jax.ShapeDtypeStruct((n, KV_LATENT), BF16),
                   jax.ShapeDtypeStruct((KV_LATENT, n), BF16),
                   jax.ShapeDtypeStruct((n, LANES), BF16),
                   jax.ShapeDtypeStruct((n, 4 * HG_HEADS * HG_DIM), F32)],
        compiler_params=pltpu.CompilerParams(dimension_semantics=("arbitrary",),
                                             vmem_limit_bytes=VMEM_LIMIT),
        name="inproj",
    )(x2, w_n, w_t, wuk_t, kvg, kvg_col, ikg, ikb)


def _hgrn2_kernel(hg_ref, lb_ref, ng_ref, o_ref, state_ref, *, tb):
    @pl.when(pl.program_id(1) == 0)
    def _():
        state_ref[...] = jnp.zeros_like(state_ref)

    nchunk = tb // HG_CHUNK
    width = HG_HEADS * HG_DIM
    row = lax.broadcasted_iota(I32, (tb, HG_DIM), 0)
    rin = jnp.bitwise_and(row, HG_CHUNK - 1)
    r2 = lax.broadcasted_iota(I32, (tb, tb), 0)
    c2 = lax.broadcasted_iota(I32, (tb, tb), 1)
    intra = jnp.logical_and(r2 // HG_CHUNK == c2 // HG_CHUNK, c2 <= r2)

    for h in range(HG_HEADS):
        sl = slice(h * HG_DIM, (h + 1) * HG_DIM)
        gq = hg_ref[0, :, h * HG_DIM:(h + 1) * HG_DIM]
        gf = hg_ref[0, :, width + h * HG_DIM:width + (h + 1) * HG_DIM]
        gi = hg_ref[0, :, 2 * width + h * HG_DIM:2 * width + (h + 1) * HG_DIM]
        gg = hg_ref[0, :, 3 * width + h * HG_DIM:3 * width + (h + 1) * HG_DIM]
        lb = lb_ref[:, sl]
        forget = lb + (1.0 - lb) * _sigmoid(gf)
        logf = jnp.log(forget)
        b = logf
        s = 1
        while s < HG_CHUNK:
            b = b + jnp.where(rin >= s, pltpu.roll(b, s, axis=0), 0.0)
            s *= 2
        tot = jnp.sum(logf.reshape(nchunk, HG_CHUNK, HG_DIM), axis=1, keepdims=True)
        bl = jnp.broadcast_to(tot, (nchunk, HG_CHUNK, HG_DIM)).reshape(tb, HG_DIM)
        kk = 1.0 - forget
        q_dec = (gq * _sigmoid(gq) * jnp.exp(b)).astype(BF16)
        k_inv = (kk * jnp.exp(-b)).astype(BF16)
        k_end = (kk * jnp.exp(bl - b)).astype(BF16)
        vb = gi.astype(BF16)
        scores = jnp.where(intra, _dot_nt(q_dec, k_inv), 0.0).astype(BF16)
        o = _dot(scores, vb)
        decay = jnp.exp(tot)
        st = state_ref[h]
        inter = []
        for n in range(nchunk):
            rs = slice(n * HG_CHUNK, (n + 1) * HG_CHUNK)
            inter.append(_dot_nt(q_dec[rs], st.astype(BF16)))
            st = st * decay[n] + _dot_tn(vb[rs], k_end[rs])
        state_ref[h] = st
        o = o + jnp.concatenate(inter, axis=0)
        o = o * lax.rsqrt(jnp.mean(o * o, axis=-1, keepdims=True) + RMS_EPS)
        o = o * ng_ref[:, sl] * (gg * _sigmoid(gg))
        o_ref[0, :, sl] = o.astype(o_ref.dtype)


def _hgrn2(hg3, lb, ng, *, tb):
    bsz, t, w4 = hg3.shape
    width = HG_HEADS * HG_DIM
    return pl.pallas_call(
        functools.partial(_hgrn2_kernel, tb=tb),
        grid=(bsz, t // tb),
        in_specs=[pl.BlockSpec((1, tb, w4), lambda b, i: (b, i, 0)),
                  pl.BlockSpec((1, width), lambda b, i: (0, 0)),
                  pl.BlockSpec((1, width), lambda b, i: (0, 0))],
        out_specs=pl.BlockSpec((1, tb, width), lambda b, i: (b, i, 0)),
        out_shape=jax.ShapeDtypeStruct((bsz, t, width), BF16),
        scratch_shapes=[pltpu.VMEM((HG_HEADS, HG_DIM, HG_DIM), F32)],
        compiler_params=pltpu.CompilerParams(dimension_semantics=("arbitrary", "arbitrary"),
                                             vmem_limit_bytes=VMEM_LIMIT),
        name="hgrn2",
    )(hg3, lb, ng)


def _dsa_kernel(qt_ref, iqt_ref, iwt_ref, ik_ref, c_ref, ct_ref, wuvt_ref, o_ref,
                keys_ref, cut_ref, qall_ref, iqall_ref, m_ref, l_ref, acc_ref, *, topk, pos_bits):
    n = pl.program_id(1)
    t0 = n * Q_ROWS
    nch = (t0 + Q_ROWS + KEY_CHUNK - 1) // KEY_CHUNK
    kf = float(topk)
    key_i = lax.broadcasted_iota(I32, (KEY_CHUNK, Q_ROWS), 0)
    tq = t0 + lax.broadcasted_iota(I32, (KEY_CHUNK, Q_ROWS), 1)
    pair = 2 * Q_ROWS

    for h in range(ATT_HEADS):
        qall_ref[:, h * Q_ROWS:(h + 1) * Q_ROWS] = qt_ref[h * KV_LATENT:(h + 1) * KV_LATENT, :]
    for h in range(IDX_HEADS):
        iqall_ref[:, h * Q_ROWS:(h + 1) * Q_ROWS] = iqt_ref[h * LANES:(h + 1) * LANES, :]
    head_w = [iwt_ref[IDX_DIM + h:IDX_DIM + h + 1, :] for h in range(IDX_HEADS)]

    def index_body(j, carry):
        k0 = pl.multiple_of(j * KEY_CHUNK, KEY_CHUNK)
        ikc = ik_ref[pl.ds(k0, KEY_CHUNK), :]
        isc = jnp.zeros((KEY_CHUNK, Q_ROWS), F32)
        for hp in range(IDX_HEADS // 2):
            z = _dot(ikc, iqall_ref[:, hp * pair:(hp + 1) * pair])
            for hh in range(2):
                isc = isc + head_w[2 * hp + hh] * jnp.maximum(z[:, hh * Q_ROWS:(hh + 1) * Q_ROWS], 0.0)
        isc = jnp.where(isc == 0.0, 0.0, isc)
        bits = pltpu.bitcast(isc, I32)
        key = jnp.bitwise_xor(bits, jnp.bitwise_and(jnp.right_shift(bits, 31), 0x7FFFFFFF))
        key = jnp.where(k0 + key_i <= tq, key, INT_MIN)
        keys_ref[pl.ds(k0, KEY_CHUNK), :] = key
        return carry

    lax.fori_loop(0, nch, index_body, 0)

    def count(pred):
        def body(j, acc):
            k0 = pl.multiple_of(j * KEY_CHUNK, KEY_CHUNK)
            hit = jnp.where(pred(keys_ref[pl.ds(k0, KEY_CHUNK), :], k0), 1.0, 0.0)
            return acc + jnp.sum(hit.reshape(KEY_CHUNK // COUNT_ROWS, COUNT_ROWS, Q_ROWS), axis=0)
        acc = lax.fori_loop(0, nch, body, jnp.zeros((COUNT_ROWS, Q_ROWS), F32))
        return jnp.sum(acc, axis=0, keepdims=True)

    def count_ge(cand):
        return count(lambda k, k0: k >= cand)

    thr0 = jnp.where(count_ge(jnp.zeros((1, Q_ROWS), I32)) >= kf, 0, INT_MIN).astype(I32)

    def bit_body(i, thr):
        cand = thr + lax.shift_left(jnp.int32(1), 30 - i)
        return jnp.where(count_ge(cand) >= kf, cand, thr)

    thr = lax.fori_loop(0, 31, bit_body, thr0)
    live = thr > INT_MIN
    cut_ref[...] = jnp.broadcast_to(jnp.where(live, jnp.int32(2 ** 30), -1), cut_ref.shape)
    n_ge = count_ge(thr)
    tie = jnp.max(jnp.where(jnp.logical_and(live, n_ge > kf), 1.0, 0.0))

    @pl.when(tie > 0.0)
    def _():
        need = kf - count(lambda k, k0: k > thr)

        def cut_body(i, cut):
            cand = cut + lax.shift_left(jnp.int32(1), pos_bits - 1 - i)
            below = count(lambda k, k0: jnp.logical_and(k == thr, k0 + key_i < cand))
            return jnp.where(below < need, cand, cut)

        cut = lax.fori_loop(0, pos_bits, cut_body, jnp.zeros((1, Q_ROWS), I32))
        cut_ref[...] = jnp.broadcast_to(jnp.where(live, cut, -1), cut_ref.shape)

    m_ref[...] = jnp.full(m_ref.shape, NEG_BIG, F32)
    l_ref[...] = jnp.zeros(l_ref.shape, F32)
    acc_ref[...] = jnp.zeros(acc_ref.shape, F32)
    cut_b = cut_ref[0:1, :]

    def attn_body(j, carry):
        k0 = pl.multiple_of(j * KEY_CHUNK, KEY_CHUNK)
        key = keys_ref[pl.ds(k0, KEY_CHUNK), :]
        pos = k0 + key_i
        sel = jnp.logical_or(key > thr, jnp.logical_and(key == thr, pos <= cut_b))
        rel = jnp.where(sel, (pos - t0).astype(F32), NEG_BIG)
        cc = c_ref[pl.ds(k0, KEY_CHUNK), :]
        cct = ct_ref[:, pl.ds(k0, KEY_CHUNK)]
        for hp in range(ATT_HEADS // 2):
            cs = slice(hp * pair, (hp + 1) * pair)
            s2 = _dot(cc, qall_ref[:, cs])
            ps, alphas = [], []
            for hh in range(2):
                h = 2 * hp + hh
                slope = LOG2_E * 2.0 ** (-(8.0 / ATT_HEADS) * (h + 1))
                s = s2[:, hh * Q_ROWS:(hh + 1) * Q_ROWS] + slope * rel
                m_old = m_ref[h:h + 1, :]
                m_new = jnp.maximum(m_old, jnp.max(s, axis=0, keepdims=True))
                alpha = jnp.exp2(m_old - m_new)
                p = jnp.exp2(s - m_new)
                l_ref[h:h + 1, :] = alpha * l_ref[h:h + 1, :] + jnp.sum(p, axis=0, keepdims=True)
                m_ref[h:h + 1, :] = m_new
                ps.append(p.astype(BF16))
                alphas.append(alpha)
            acc_ref[:, cs] = (acc_ref[:, cs] * jnp.concatenate(alphas, axis=1)
                              + _dot(cct, jnp.concatenate(ps, axis=1)))
        return carry

    lax.fori_loop(0, nch, attn_body, 0)

    yt = []
    for h in range(ATT_HEADS):
        o_lat = (acc_ref[:, h * Q_ROWS:(h + 1) * Q_ROWS] / l_ref[h:h + 1, :]).astype(BF16)
        yt.append(_dot(wuvt_ref[h], o_lat))
    o_ref[...] = jnp.concatenate(yt, axis=0).T.astype(o_ref.dtype)


def _dsa(qt, iqt, iwt, ik, c, ct, wuvt_p, *, bsz, t):
    n = bsz * t
    nq = t // Q_ROWS
    topk = min(TOPK_MAX, t // 4)
    assert t % KEY_CHUNK == 0
    width = ATT_HEADS * ATT_HEAD_DIM
    qcols = lambda h: pl.BlockSpec((h, Q_ROWS), lambda b, i: (0, b * nq + i))
    return pl.pallas_call(
        functools.partial(_dsa_kernel, topk=topk, pos_bits=(t - 1).bit_length()),
        grid=(bsz, nq),
        in_specs=[qcols(ATT_HEADS * KV_LATENT), qcols(IDX_HEADS * LANES), qcols(LANES),
                  pl.BlockSpec((t, LANES), lambda b, i: (b, 0)),
                  pl.BlockSpec((t, KV_LATENT), lambda b, i: (b, 0)),
                  pl.BlockSpec((KV_LATENT, t), lambda b, i: (0, b)),
                  pl.BlockSpec(wuvt_p.shape, lambda b, i: (0, 0, 0))],
        out_specs=pl.BlockSpec((Q_ROWS, width), lambda b, i: (b * nq + i, 0)),
        out_shape=jax.ShapeDtypeStruct((n, width), BF16),
        scratch_shapes=[pltpu.VMEM((t, Q_ROWS), I32),
                        pltpu.VMEM((8, Q_ROWS), I32),
                        pltpu.VMEM((KV_LATENT, ATT_HEADS * Q_ROWS), BF16),
                        pltpu.VMEM((LANES, IDX_HEADS * Q_ROWS), BF16),
                        pltpu.VMEM((8, Q_ROWS), F32),
                        pltpu.VMEM((8, Q_ROWS), F32),
                        pltpu.VMEM((KV_LATENT, ATT_HEADS * Q_ROWS), F32)],
        compiler_params=pltpu.CompilerParams(dimension_semantics=("arbitrary", "arbitrary"),
                                             vmem_limit_bytes=VMEM_LIMIT),
        name="dsa",
    )(qt, iqt, iwt, ik, c, ct, wuvt_p)


def _mix_router_kernel(x_ref, ya_ref, yr_ref, wo_ref, g_ref, b_ref, wr_ref, br_ref,
                       x1_ref, x1w_ref, route_ref, tot_ref, carry_ref, *, alpha, tm):
    @pl.when(pl.program_id(0) == 0)
    def _():
        carry_ref[...] = jnp.zeros_like(carry_ref)

    half = ya_ref.shape[1]
    mix = _dot(ya_ref[...], wo_ref[0:half, :]) + _dot(yr_ref[...], wo_ref[half:, :])
    x1 = _layer_norm(alpha * x_ref[...] + mix, g_ref[...], b_ref[...])
    x1_ref[...] = x1
    x1w_ref[...] = _pack_words(x1)

    logits = jnp.dot(x1, wr_ref[...], preferred_element_type=F32,
                     precision=lax.Precision.HIGHEST) + br_ref[...]
    lane = lax.broadcasted_iota(I32, logits.shape, 1)
    lane_f = lane.astype(F32)
    work = jnp.where(lane < N_EXPERTS, logits, -jnp.inf)
    sel_f = jnp.zeros(logits.shape, F32)
    denom = jnp.zeros((tm, 1), F32)
    hits, experts, weights = [], [], []
    for _ in range(TOP_K):
        mx = jnp.max(work, axis=-1, keepdims=True)
        first = jnp.min(jnp.where(work == mx, lane_f, float(LANES)), axis=-1, keepdims=True)
        hit = lane_f == first
        e = jnp.exp(mx - (weights[0][1] if weights else mx))
        hits.append(hit)
        experts.append(first)
        weights.append((e, mx))
        denom = denom + e
        sel_f = sel_f + jnp.where(hit, 1.0, 0.0)
        work = jnp.where(hit, -jnp.inf, work)

    r2 = lax.broadcasted_iota(I32, (tm, tm), 0)
    c2 = lax.broadcasted_iota(I32, (tm, tm), 1)
    before = jnp.where(c2 < r2, 1.0, 0.0).astype(BF16)
    rank = _dot(before, sel_f.astype(BF16)) + carry_ref[...]
    carry = carry_ref[...] + jnp.sum(sel_f, axis=0, keepdims=True)
    carry_ref[...] = carry
    tot_ref[...] = jnp.broadcast_to(carry, tot_ref.shape)

    route = jnp.zeros(logits.shape, F32)
    for k in range(TOP_K):
        rank_k = jnp.sum(jnp.where(hits[k], rank, 0.0), axis=-1, keepdims=True)
        route = jnp.where(lane == k, experts[k], route)
        route = jnp.where(lane == TOP_K + k, rank_k, route)
        route = jnp.where(lane == 2 * TOP_K + k, weights[k][0] / denom, route)
    route_ref[...] = route


def _mix_router(x2, ya, yr, wo, g, b, wr_p, br_p, *, alpha, tm):
    n, d = x2.shape
    half = ya.shape[1]
    rows = lambda w: pl.BlockSpec((tm, w), lambda i: (i, 0))
    full = lambda shape: pl.BlockSpec(shape, lambda i: (0,) * len(shape))
    return pl.pallas_call(
        functools.partial(_mix_router_kernel, alpha=alpha, tm=tm),
        grid=(n // tm,),
        in_specs=[rows(d), rows(half), rows(half), full(wo.shape), full(g.shape), full(b.shape),
                  full(wr_p.shape), full(br_p.shape)],
        out_specs=[rows(d), rows(d // 2), rows(LANES), full((8, LANES))],
        out_shape=[jax.ShapeDtypeStruct((n, d), F32),
                   jax.ShapeDtypeStruct((n, d // 2), I32),
                   jax.ShapeDtypeStruct((n, LANES), F32),
                   jax.ShapeDtypeStruct((8, LANES), F32)],
        scratch_shapes=[pltpu.VMEM((1, LANES), F32)],
        compiler_params=pltpu.CompilerParams(dimension_semantics=("arbitrary",),
                                             vmem_limit_bytes=VMEM_LIMIT),
        name="mix_router",
    )(x2, ya, yr, wo, g, b, wr_p, br_p)


def _sc_gather_rows(table, idx):
    n_idx = idx.shape[0]
    width = table.shape[1]
    per_worker = n_idx // SC_WORKERS
    assert per_worker * SC_WORKERS == n_idx and per_worker % SC_ROWS == 0
    mesh = plsc.VectorSubcoreMesh(core_axis_name="c", subcore_axis_name="s",
                                  num_cores=SC_CORES, num_subcores=SC_SUBCORES)

    @functools.partial(
        pl.kernel, mesh=mesh,
        out_type=jax.ShapeDtypeStruct((n_idx, width), table.dtype),
        scratch_types=[pltpu.VMEM((SC_ROWS,), I32),
                       pltpu.VMEM((SC_ROWS, width), table.dtype),
                       pltpu.SemaphoreType.DMA])
    def gather(table_hbm, idx_hbm, out_hbm, idx_v, rows_v, sem):
        worker = lax.axis_index("s") * SC_CORES + lax.axis_index("c")

        @pl.loop(0, per_worker // SC_ROWS)
        def _(step):
            base = pl.multiple_of(worker * per_worker + step * SC_ROWS, SC_ROWS)
            pltpu.sync_copy(idx_hbm.at[pl.ds(base, SC_ROWS)], idx_v)
            pltpu.async_copy(table_hbm.at[idx_v], rows_v, sem).wait()
            pltpu.sync_copy(rows_v, out_hbm.at[pl.ds(base, SC_ROWS)])

    return gather(table, idx)


def _pack_words(v):
    k = v.shape[1] // 2
    hi = pltpu.bitcast(v[:, :k].astype(BF16).astype(F32), I32)
    lo = pltpu.bitcast(v[:, k:].astype(BF16).astype(F32), I32)
    return jnp.bitwise_or(hi, lax.shift_right_logical(lo, jnp.int32(16)))


def _unpack_words(w):
    hi = pltpu.bitcast(jnp.bitwise_and(w, jnp.int32(-65536)), F32)
    lo = pltpu.bitcast(jnp.left_shift(w, 16), F32)
    return jnp.concatenate([hi, lo], axis=1).astype(BF16)


def _expert_kernel(exp_ref, blk_ref, x_ref, wup_ref, bup_ref, wdn_ref, bdn_ref, o_ref, *, d_ff):
    del exp_ref, blk_ref
    h = _dot(_unpack_words(x_ref[...]), wup_ref[0]) + bup_ref[0]
    glu = jnp.minimum(h[:, :d_ff], SWIGLU_LIMIT)
    lin = jnp.clip(h[:, d_ff:], -SWIGLU_LIMIT, SWIGLU_LIMIT)
    a = glu * _sigmoid(SWIGLU_ALPHA * glu) * (lin + 1.0)
    o_ref[...] = _pack_words(_dot(a.astype(BF16), wdn_ref[0]) + bdn_ref[0])


def _experts(blk_expert, blk_index, xs, wup, bup, wdn, bdn):
    cap, words = xs.shape
    d = 2 * words
    d_ff = wdn.shape[1]
    row_map = lambda i, exp, blk: (blk[i], 0)
    exp_map3 = lambda i, exp, blk: (exp[i], 0, 0)
    grid_spec = pltpu.PrefetchScalarGridSpec(
        num_scalar_prefetch=2,
        grid=(cap // MOE_ROWS,),
        in_specs=[pl.BlockSpec((MOE_ROWS, words), row_map),
                  pl.BlockSpec((1, d, 2 * d_ff), exp_map3),
                  pl.BlockSpec((1, 1, 2 * d_ff), exp_map3),
                  pl.BlockSpec((1, d_ff, d), exp_map3),
                  pl.BlockSpec((1, 1, d), exp_map3)],
        out_specs=pl.BlockSpec((MOE_ROWS, words), row_map),
    )
    return pl.pallas_call(
        functools.partial(_expert_kernel, d_ff=d_ff),
        grid_spec=grid_spec,
        out_shape=jax.ShapeDtypeStruct((cap, words), I32),
        compiler_params=pltpu.CompilerParams(dimension_semantics=("arbitrary",),
                                             vmem_limit_bytes=VMEM_LIMIT),
        name="experts",
    )(blk_expert, blk_index, xs, wup, bup, wdn, bdn)


def _route_tables(route, totals, *, cap):
    n = route.shape[0]
    expert = route[:, 0:TOP_K].astype(I32)
    rank = route[:, TOP_K:2 * TOP_K].astype(I32)
    counts = totals[0, :N_EXPERTS].astype(I32)
    padded = (counts + MOE_ROWS - 1) // MOE_ROWS * MOE_ROWS
    pad_end = jnp.cumsum(padded)
    pad_start = pad_end - padded
    dest = (pad_start[expert] + rank).T.reshape(-1)
    tok = jnp.tile(jnp.arange(n, dtype=I32), TOP_K)
    slot_tok = jnp.zeros((cap,), I32).at[dest].set(tok)
    n_blk = cap // MOE_ROWS
    used = pad_end[-1] // MOE_ROWS
    blk = jnp.minimum(jnp.arange(n_blk, dtype=I32), used - 1)
    blk_expert = jnp.minimum(jnp.sum(pad_end[None, :] <= (blk * MOE_ROWS)[:, None], axis=1),
                             N_EXPERTS - 1).astype(I32)
    return dest, slot_tok, blk_expert, blk


def _final_kernel(x1_ref, ya_ref, yb_ref, yc_ref, yd_ref, route_ref, p_ref, g1_ref, b1_ref,
                  wp_ref, wg_ref, g2_ref, b2_ref, o_ref, *, alpha):
    ffn = jnp.zeros(x1_ref.shape, F32)
    for k, yk_ref in enumerate((ya_ref, yb_ref, yc_ref, yd_ref)):
        gate_k = route_ref[:, 2 * TOP_K + k:2 * TOP_K + k + 1]
        ffn = ffn + gate_k * _unpack_words(yk_ref[...]).astype(F32)
    x2 = _layer_norm(alpha * x1_ref[...] + ffn, g1_ref[...], b1_ref[...])
    gate = _sigmoid(_dot(x2.astype(BF16), wg_ref[...]))
    ple = _dot(p_ref[...].astype(BF16), wp_ref[...]) * gate
    o_ref[...] = _layer_norm(alpha * x2 + ple, g2_ref[...], b2_ref[...])


def _final(x1, y4w, route, p2, g1, b1, wp, wg, g2, b2, *, alpha, tm):
    n, d = x1.shape
    assert TOP_K == 4
    rows = lambda w: pl.BlockSpec((tm, w), lambda i: (i, 0))
    full = lambda shape: pl.BlockSpec(shape, lambda i: (0,) * len(shape))
    pick = lambda k: pl.BlockSpec((tm, d // 2), lambda i: (k * (n // tm) + i, 0))
    return pl.pallas_call(
        functools.partial(_final_kernel, alpha=alpha),
        grid=(n // tm,),
        in_specs=[rows(d), pick(0), pick(1), pick(2), pick(3), rows(LANES), rows(p2.shape[1]),
                  full(g1.shape), full(b1.shape),
                  full(wp.shape), full(wg.shape), full(g2.shape), full(b2.shape)],
        out_specs=rows(d),
        out_shape=jax.ShapeDtypeStruct((n, d), F32),
        compiler_params=pltpu.CompilerParams(dimension_semantics=("arbitrary",),
                                             vmem_limit_bytes=VMEM_LIMIT),
        name="final",
    )(x1, y4w, y4w, y4w, y4w, route, p2, g1, b1, wp, wg, g2, b2)


def _pack_w_in(w_in):
    d = w_in.shape[0]
    sizes = (ATT_HEADS * ATT_HEAD_DIM, KV_LATENT, IDX_HEADS * IDX_DIM, IDX_DIM, IDX_HEADS,
             HG_HEADS * HG_DIM, HG_HEADS * HG_DIM, HG_HEADS * HG_DIM, HG_HEADS * HG_DIM)
    offs = [0]
    for s in sizes:
        offs.append(offs[-1] + s)
    sec = [w_in[:, offs[k]:offs[k + 1]] for k in range(len(sizes))]
    iq = sec[2].reshape(d, IDX_HEADS, IDX_DIM)
    iq = jnp.pad(iq, ((0, 0), (0, 0), (0, LANES - IDX_DIM))).reshape(d, IDX_HEADS * LANES)
    ikw = jnp.pad(jnp.concatenate([sec[3], sec[4]], axis=1), ((0, 0), (0, LANES - IDX_DIM - IDX_HEADS)))
    w_n = jnp.concatenate([sec[1], ikw, sec[5], sec[6], sec[7], sec[8]], axis=1).astype(BF16)
    w_t = jnp.concatenate([sec[0], iq, ikw, sec[1]], axis=1).T.astype(BF16)
    return w_n, w_t


def _block_diag_uk_t(w_uk):
    eye = jnp.eye(ATT_HEADS, dtype=w_uk.dtype)
    bd = jnp.einsum("rhd,hg->hrgd", w_uk, eye)
    return bd.reshape(ATT_HEADS * KV_LATENT, ATT_HEADS * ATT_HEAD_DIM).astype(BF16)


def _uv_t(w_uv):
    return jnp.transpose(w_uv, (1, 2, 0)).astype(BF16)


def _layer(x, p_l, w_in, kv_g, ik_g, ik_b, w_uk, w_uv, lb, hg_ng, w_o, ln_mix_g, ln_mix_b,
           w_router, b_router, w_up, b_up, w_down, b_down, ln_ffn_g, ln_ffn_b,
           w_ple_proj, w_ple_gate, ln_ple_g, ln_ple_b, *, alpha, tm, hg_tb):
    bsz, t, d = x.shape
    n = bsz * t
    x2 = x.reshape(n, d)
    row = lambda v: v.reshape(1, -1).astype(F32)
    pad_lane = lambda v: jnp.pad(row(v), ((0, 0), (0, LANES - v.shape[-1])))

    w_n, w_t = _pack_w_in(w_in)
    qt, iqt, iwt, c, ct, ik, hg = _inproj(x2, w_n, w_t, _block_diag_uk_t(w_uk), row(kv_g),
                                          kv_g.reshape(-1, 1).astype(F32), pad_lane(ik_g), pad_lane(ik_b), tm=tm)
    y_rec = _hgrn2(hg.reshape(bsz, t, -1), row(lb), row(hg_ng), tb=hg_tb)
    y_att = _dsa(qt, iqt, iwt, ik, c, ct, _uv_t(w_uv), bsz=bsz, t=t)

    wr_p = jnp.pad(w_router.astype(F32), ((0, 0), (0, LANES - N_EXPERTS)))
    x1, x1w, route, totals = _mix_router(
        x2, y_att, y_rec.reshape(n, -1), w_o.astype(BF16), row(ln_mix_g), row(ln_mix_b),
        wr_p, pad_lane(b_router), alpha=alpha, tm=tm)

    step = SC_WORKERS * SC_ROWS
    cap = -(-(n * TOP_K + N_EXPERTS * MOE_ROWS) // step) * step
    dest, slot_tok, blk_expert, blk_index = _route_tables(route, totals, cap=cap)
    d_ff = w_down.shape[1]
    xs = _sc_gather_rows(x1w, slot_tok)
    outw = _experts(blk_expert, blk_index, xs,
                    w_up.astype(BF16), b_up.reshape(N_EXPERTS, 1, 2 * d_ff).astype(F32),
                    w_down.astype(BF16), b_down.reshape(N_EXPERTS, 1, d).astype(F32))
    y4w = _sc_gather_rows(outw, dest)

    out = _final(x1, y4w, route, p_l.reshape(n, -1), row(ln_ffn_g), row(ln_ffn_b),
                 w_ple_proj.astype(BF16), w_ple_gate.astype(BF16), row(ln_ple_g), row(ln_ple_b),
                 alpha=alpha, tm=tm)
    return out.reshape(bsz, t, d)


def kernel(x, p, w_in, kv_norm_g, idx_k_norm_g, idx_k_norm_b, w_uk, w_uv, hg_lb_logits, hg_norm_g, w_o,
           ln_mix_g, ln_mix_b, w_router, b_router, w_up, b_up, w_down, b_down, ln_ffn_g, ln_ffn_b,
           w_ple_proj, w_ple_gate, ln_ple_g, ln_ple_b):
    depth = w_in.shape[0]
    alpha = (2.0 * depth) ** 0.25
    lower_bounds = jnp.cumsum(jax.nn.softmax(hg_lb_logits.astype(F32), axis=0), axis=0)
    n = x.shape[0] * x.shape[1]
    tm = min(512, n)
    hg_tb = min(512, x.shape[1])
    for l in range(depth):
        x = _layer(x, p[l], w_in[l], kv_norm_g[l], idx_k_norm_g[l], idx_k_norm_b[l], w_uk[l], w_uv[l],
                   lower_bounds[l], hg_norm_g[l], w_o[l], ln_mix_g[l], ln_mix_b[l], w_router[l],
                   b_router[l], w_up[l], b_up[l], w_down[l], b_down[l], ln_ffn_g[l], ln_ffn_b[l],
                   w_ple_proj[l], w_ple_gate[l], ln_ple_g[l], ln_ple_b[l],
                   alpha=alpha, tm=tm, hg_tb=hg_tb)
    return x
```

```python
import functools

import jax
import jax.numpy as jnp
from jax import lax
from jax.experimental import pallas as pl
from jax.experimental.pallas import tpu as pltpu
from jax.experimental.pallas import tpu_sc as plsc

F32 = jnp.float32
BF16 = jnp.bfloat16
I32 = jnp.int32

ATT_HEADS = 8
ATT_HEAD_DIM = 64
KV_LATENT = 256
IDX_HEADS = 8
IDX_DIM = 64
TOPK_MAX = 256
HG_HEADS = 4
HG_DIM = 128
HG_CHUNK = 32
N_EXPERTS = 32
TOP_K = 4
SWIGLU_LIMIT = 7.0
SWIGLU_ALPHA = 1.702
LN_EPS = 1e-5
RMS_EPS = 1e-6

LANES = 128
Q_ROWS = 128
KEY_CHUNK = 512
COUNT_ROWS = 128
MOE_ROWS = 256
SC_CORES = 2
SC_SUBCORES = 16
SC_WORKERS = SC_CORES * SC_SUBCORES
SC_ROWS = 64
SC_SCATTER_ROWS = 128
VMEM_LIMIT = 56 * 1024 * 1024

INT_MIN = -(2 ** 31)
NEG_BIG = -1e30
LOG2_E = 1.4426950408889634

_N_KV = 0
_N_IKW = _N_KV + KV_LATENT
_N_HG = _N_IKW + LANES
_N_END = _N_HG + 4 * HG_HEADS * HG_DIM
_T_AQ = 0
_T_IQ = _T_AQ + ATT_HEADS * ATT_HEAD_DIM
_T_IKW = _T_IQ + IDX_HEADS * LANES
_T_KV = _T_IKW + LANES
_T_END = _T_KV + KV_LATENT


def _dot(a, b):
    return jnp.dot(a, b, preferred_element_type=F32)


def _dot_nt(a, b):
    return lax.dot_general(a, b, (((1,), (1,)), ((), ())), preferred_element_type=F32)


def _dot_tn(a, b):
    return lax.dot_general(a, b, (((0,), (0,)), ((), ())), preferred_element_type=F32)


def _layer_norm(z, g, b):
    mu = jnp.mean(z, axis=-1, keepdims=True)
    d = z - mu
    var = jnp.mean(d * d, axis=-1, keepdims=True)
    return d * lax.rsqrt(var + LN_EPS) * g + b


def _sigmoid(x):
    return 1.0 / (1.0 + jnp.exp(-x))


def _inproj_kernel(x_ref, wn_ref, wt_ref, wukt_ref, kvg_ref, kvgc_ref, ikg_ref, ikb_ref,
                   qt_ref, iqt_ref, iwt_ref, c_ref, ct_ref, ik_ref, hg_ref):
    xb = x_ref[...].astype(BF16)

    aqt = _dot_nt(wt_ref[_T_AQ:_T_IQ, :], xb).astype(BF16)
    qt_ref[...] = (_dot(wukt_ref[...], aqt) * (LOG2_E * ATT_HEAD_DIM ** -0.5)).astype(BF16)
    iqt_ref[...] = (_dot_nt(wt_ref[_T_IQ:_T_IKW, :], xb) * (IDX_DIM ** -0.5)).astype(BF16)
    iwt_ref[...] = _dot_nt(wt_ref[_T_IKW:_T_KV, :], xb) * (IDX_HEADS ** -0.5)
    act = _dot_nt(wt_ref[_T_KV:_T_END, :], xb)
    ct = act * lax.rsqrt(jnp.mean(act * act, axis=0, keepdims=True) + RMS_EPS) * kvgc_ref[...]
    ct_ref[...] = ct.astype(BF16)

    ac = _dot(xb, wn_ref[:, _N_KV:_N_IKW])
    c = ac * lax.rsqrt(jnp.mean(ac * ac, axis=-1, keepdims=True) + RMS_EPS) * kvg_ref[...]
    c_ref[...] = c.astype(BF16)

    ikw = _dot(xb, wn_ref[:, _N_IKW:_N_HG])
    lane = lax.broadcasted_iota(I32, ikw.shape, 1)
    is_k = lane < IDX_DIM
    mu = jnp.sum(jnp.where(is_k, ikw, 0.0), axis=-1, keepdims=True) * (1.0 / IDX_DIM)
    d = jnp.where(is_k, ikw - mu, 0.0)
    var = jnp.sum(d * d, axis=-1, keepdims=True) * (1.0 / IDX_DIM)
    ik = d * lax.rsqrt(var + LN_EPS) * ikg_ref[...] + ikb_ref[...]
    ik_ref[...] = ik.astype(BF16)

    hg_ref[...] = _dot(xb, wn_ref[:, _N_HG:_N_END])


def _inproj(x2, w_n, w_t, wuk_t, kvg, kvg_col, ikg, ikb, *, tm):
    n, d = x2.shape
    grid = (n // tm,)
    full = lambda shape: pl.BlockSpec(shape, lambda i: (0,) * len(shape))
    rows = lambda w: pl.BlockSpec((tm, w), lambda i: (i, 0))
    cols = lambda h: pl.BlockSpec((h, tm), lambda i: (0, i))
    return pl.pallas_call(
        _inproj_kernel,
        grid=grid,
        in_specs=[rows(d), full(w_n.shape), full(w_t.shape), full(wuk_t.shape), full(kvg.shape),
                  full(kvg_col.shape), full(ikg.shape), full(ikb.shape)],
        out_specs=[cols(ATT_HEADS * KV_LATENT), cols(IDX_HEADS * LANES), cols(LANES),
                   rows(KV_LATENT), cols(KV_LATENT), rows(LANES), rows(4 * HG_HEADS * HG_DIM)],
        out_shape=[jax.ShapeDtypeStruct((ATT_HEADS * KV_LATENT, n), BF16),
                   jax.ShapeDtypeStruct((IDX_HEADS * LANES, n), BF16),
                   jax.ShapeDtypeStruct((LANES, n), F32),
                   jax.ShapeDtypeStruct((n, KV_LATENT), BF16),
                   jax.ShapeDtypeStruct((KV_LATENT, n), BF16),
                   jax.ShapeDtypeStruct((n, LANES), BF16),
                   jax.ShapeDtypeStruct((n, 4 * HG_HEADS * HG_DIM), F32)],
        compiler_params=pltpu.CompilerParams(dimension_semantics=("arbitrary",),
                                             vmem_limit_bytes=VMEM_LIMIT),
        name="inproj",
    )(x2, w_n, w_t, wuk_t, kvg, kvg_col, ikg, ikb)


def _hgrn2_kernel(hg_ref, lb_ref, ng_ref, o_ref, state_ref, *, tb):
    @pl.when(pl.program_id(1) == 0)
    def _():
        state_ref[...] = jnp.zeros_like(state_ref)

    nchunk = tb // HG_CHUNK
    width = HG_HEADS * HG_DIM
    row = lax.broadcasted_iota(I32, (tb, HG_DIM), 0)
    rin = jnp.bitwise_and(row, HG_CHUNK - 1)
    r2 = lax.broadcasted_iota(I32, (tb, tb), 0)
    c2 = lax.broadcasted_iota(I32, (tb, tb), 1)
    intra = jnp.logical_and(r2 // HG_CHUNK == c2 // HG_CHUNK, c2 <= r2)

    for h in range(HG_HEADS):
        sl = slice(h * HG_DIM, (h + 1) * HG_DIM)
        gq = hg_ref[0, :, h * HG_DIM:(h + 1) * HG_DIM]
        gf = hg_ref[0, :, width + h * HG_DIM:width + (h + 1) * HG_DIM]
        gi = hg_ref[0, :, 2 * width + h * HG_DIM:2 * width + (h + 1) * HG_DIM]
        gg = hg_ref[0, :, 3 * width + h * HG_DIM:3 * width + (h + 1) * HG_DIM]
        lb = lb_ref[:, sl]
        forget = lb + (1.0 - lb) * _sigmoid(gf)
        logf = jnp.log(forget)
        b = logf
        s = 1
        while s < HG_CHUNK:
            b = b + jnp.where(rin >= s, pltpu.roll(b, s, axis=0), 0.0)
            s *= 2
        tot = jnp.sum(logf.reshape(nchunk, HG_CHUNK, HG_DIM), axis=1, keepdims=True)
        bl = jnp.broadcast_to(tot, (nchunk, HG_CHUNK, HG_DIM)).reshape(tb, HG_DIM)
        kk = 1.0 - forget
        q_dec = (gq * _sigmoid(gq) * jnp.exp(b)).astype(BF16)
        k_inv = (kk * jnp.exp(-b)).astype(BF16)
        k_end = (kk * jnp.exp(bl - b)).astype(BF16)
        vb = gi.astype(BF16)
        scores = jnp.where(intra, _dot_nt(q_dec, k_inv), 0.0).astype(BF16)
        o = _dot(scores, vb)
        decay = jnp.exp(tot)
        st = state_ref[h]
        inter = []
        for n in range(nchunk):
            rs = slice(n * HG_CHUNK, (n + 1) * HG_CHUNK)
            inter.append(_dot_nt(q_dec[rs], st.astype(BF16)))
            st = st * decay[n] + _dot_tn(vb[rs], k_end[rs])
        state_ref[h] = st
        o = o + jnp.concatenate(inter, axis=0)
        o = o * lax.rsqrt(jnp.mean(o * o, axis=-1, keepdims=True) + RMS_EPS)
        o = o * ng_ref[:, sl] * (gg * _sigmoid(gg))
        o_ref[0, :, sl] = o.astype(o_ref.dtype)


def _hgrn2(hg3, lb, ng, *, tb):
    bsz, t, w4 = hg3.shape
    width = HG_HEADS * HG_DIM
    return pl.pallas_call(
        functools.partial(_hgrn2_kernel, tb=tb),
        grid=(bsz, t // tb),
        in_specs=[pl.BlockSpec((1, tb, w4), lambda b, i: (b, i, 0)),
                  pl.BlockSpec((1, width), lambda b, i: (0, 0)),
                  pl.BlockSpec((1, width), lambda b, i: (0, 0))],
        out_specs=pl.BlockSpec((1, tb, width), lambda b, i: (b, i, 0)),
        out_shape=jax.ShapeDtypeStruct((bsz, t, width), BF16),
        scratch_shapes=[pltpu.VMEM((HG_HEADS, HG_DIM, HG_DIM), F32)],
        compiler_params=pltpu.CompilerParams(dimension_semantics=("arbitrary", "arbitrary"),
                                             vmem_limit_bytes=VMEM_LIMIT),
        name="hgrn2",
    )(hg3, lb, ng)


def _dsa_kernel(qt_ref, iqt_ref, iwt_ref, ik_ref, c_ref, ct_ref, wuvt_ref, o_ref,
                keys_ref, cut_ref, qall_ref, iqall_ref, m_ref, l_ref, acc_ref, *, topk, pos_bits):
    n = pl.program_id(1)
    t0 = n * Q_ROWS
    nch = (t0 + Q_ROWS + KEY_CHUNK - 1) // KEY_CHUNK
    kf = float(topk)
    key_i = lax.broadcasted_iota(I32, (KEY_CHUNK, Q_ROWS), 0)
    tq = t0 + lax.broadcasted_iota(I32, (KEY_CHUNK, Q_ROWS), 1)
    pair = 2 * Q_ROWS

    for h in range(ATT_HEADS):
        qall_ref[:, h * Q_ROWS:(h + 1) * Q_ROWS] = qt_ref[h * KV_LATENT:(h + 1) * KV_LATENT, :]
    for h in range(IDX_HEADS):
        iqall_ref[:, h * Q_ROWS:(h + 1) * Q_ROWS] = iqt_ref[h * LANES:(h + 1) * LANES, :]
    head_w = [iwt_ref[IDX_DIM + h:IDX_DIM + h + 1, :] for h in range(IDX_HEADS)]

    def index_body(j, carry):
        k0 = pl.multiple_of(j * KEY_CHUNK, KEY_CHUNK)
        ikc = ik_ref[pl.ds(k0, KEY_CHUNK), :]
        isc = jnp.zeros((KEY_CHUNK, Q_ROWS), F32)
        for hp in range(IDX_HEADS // 2):
            z = _dot(ikc, iqall_ref[:, hp * pair:(hp + 1) * pair])
            for hh in range(2):
                isc = isc + head_w[2 * hp + hh] * jnp.maximum(z[:, hh * Q_ROWS:(hh + 1) * Q_ROWS], 0.0)
        isc = jnp.where(isc == 0.0, 0.0, isc)
        bits = pltpu.bitcast(isc, I32)
        key = jnp.bitwise_xor(bits, jnp.bitwise_and(jnp.right_shift(bits, 31), 0x7FFFFFFF))
        key = jnp.where(k0 + key_i <= tq, key, INT_MIN)
        keys_ref[pl.ds(k0, KEY_CHUNK), :] = key
        return carry

    lax.fori_loop(0, nch, index_body, 0)

    def count(pred):
        def body(j, acc):
            k0 = pl.multiple_of(j * KEY_CHUNK, KEY_CHUNK)
            hit = jnp.where(pred(keys_ref[pl.ds(k0, KEY_CHUNK), :], k0), 1.0, 0.0)
            return acc + jnp.sum(hit.reshape(KEY_CHUNK // COUNT_ROWS, COUNT_ROWS, Q_ROWS), axis=0)
        acc = lax.fori_loop(0, nch, body, jnp.zeros((COUNT_ROWS, Q_ROWS), F32))
        return jnp.sum(acc, axis=0, keepdims=True)

    def count_ge(cand):
        return count(lambda k, k0: k >= cand)

    thr0 = jnp.where(count_ge(jnp.zeros((1, Q_ROWS), I32)) >= kf, 0, INT_MIN).astype(I32)

    def bit_body(i, thr):
        cand = thr + lax.shift_left(jnp.int32(1), 30 - i)
        return jnp.where(count_ge(cand) >= kf, cand, thr)

    thr = lax.fori_loop(0, 31, bit_body, thr0)
    live = thr > INT_MIN
    cut_ref[...] = jnp.broadcast_to(jnp.where(live, jnp.int32(2 ** 30), -1), cut_ref.shape)
    n_ge = count_ge(thr)
    tie = jnp.max(jnp.where(jnp.logical_and(live, n_ge > kf), 1.0, 0.0))

    @pl.when(tie > 0.0)
    def _():
        need = kf - count(lambda k, k0: k > thr)

        def cut_body(i, cut):
            cand = cut + lax.shift_left(jnp.int32(1), pos_bits - 1 - i)
            below = count(lambda k, k0: jnp.logical_and(k == thr, k0 + key_i < cand))
            return jnp.where(below < need, cand, cut)

        cut = lax.fori_loop(0, pos_bits, cut_body, jnp.zeros((1, Q_ROWS), I32))
        cut_ref[...] = jnp.broadcast_to(jnp.where(live, cut, -1), cut_ref.shape)

    m_ref[...] = jnp.full(m_ref.shape, NEG_BIG, F32)
    l_ref[...] = jnp.zeros(l_ref.shape, F32)
    acc_ref[...] = jnp.zeros(acc_ref.shape, F32)
    cut_b = cut_ref[0:1, :]

    def attn_body(j, carry):
        k0 = pl.multiple_of(j * KEY_CHUNK, KEY_CHUNK)
        key = keys_ref[pl.ds(k0, KEY_CHUNK), :]
        pos = k0 + key_i
        sel = jnp.logical_or(key > thr, jnp.logical_and(key == thr, pos <= cut_b))
        rel = jnp.where(sel, (pos - t0).astype(F32), NEG_BIG)
        cc = c_ref[pl.ds(k0, KEY_CHUNK), :]
        cct = ct_ref[:, pl.ds(k0, KEY_CHUNK)]
        for hp in range(ATT_HEADS // 2):
            cs = slice(hp * pair, (hp + 1) * pair)
            s2 = _dot(cc, qall_ref[:, cs])
            ps, alphas = [], []
            for hh in range(2):
                h = 2 * hp + hh
                slope = LOG2_E * 2.0 ** (-(8.0 / ATT_HEADS) * (h + 1))
                s = s2[:, hh * Q_ROWS:(hh + 1) * Q_ROWS] + slope * rel
                m_old = m_ref[h:h + 1, :]
                m_new = jnp.maximum(m_old, jnp.max(s, axis=0, keepdims=True))
                alpha = jnp.exp2(m_old - m_new)
                p = jnp.exp2(s - m_new)
                l_ref[h:h + 1, :] = alpha * l_ref[h:h + 1, :] + jnp.sum(p, axis=0, keepdims=True)
                m_ref[h:h + 1, :] = m_new
                ps.append(p.astype(BF16))
                alphas.append(alpha)
            acc_ref[:, cs] = (acc_ref[:, cs] * jnp.concatenate(alphas, axis=1)
                              + _dot(cct, jnp.concatenate(ps, axis=1)))
        return carry

    lax.fori_loop(0, nch, attn_body, 0)

    yt = []
    for h in range(ATT_HEADS):
        o_lat = (acc_ref[:, h * Q_ROWS:(h + 1) * Q_ROWS] / l_ref[h:h + 1, :]).astype(BF16)
        yt.append(_dot(wuvt_ref[h], o_lat))
    o_ref[...] = jnp.concatenate(yt, axis=0).T.astype(o_ref.dtype)


def _dsa(qt, iqt, iwt, ik, c, ct, wuvt_p, *, bsz, t):
    n = bsz * t
    nq = t // Q_ROWS
    topk = min(TOPK_MAX, t // 4)
    assert t % KEY_CHUNK == 0
    width = ATT_HEADS * ATT_HEAD_DIM
    qcols = lambda h: pl.BlockSpec((h, Q_ROWS), lambda b, i: (0, b * nq + i))
    return pl.pallas_call(
        functools.partial(_dsa_kernel, topk=topk, pos_bits=(t - 1).bit_length()),
        grid=(bsz, nq),
        in_specs=[qcols(ATT_HEADS * KV_LATENT), qcols(IDX_HEADS * LANES), qcols(LANES),
                  pl.BlockSpec((t, LANES), lambda b, i: (b, 0)),
                  pl.BlockSpec((t, KV_LATENT), lambda b, i: (b, 0)),
                  pl.BlockSpec((KV_LATENT, t), lambda b, i: (0, b)),
                  pl.BlockSpec(wuvt_p.shape, lambda b, i: (0, 0, 0))],
        out_specs=pl.BlockSpec((Q_ROWS, width), lambda b, i: (b * nq + i, 0)),
        out_shape=jax.ShapeDtypeStruct((n, width), BF16),
        scratch_shapes=[pltpu.VMEM((t, Q_ROWS), I32),
                        pltpu.VMEM((8, Q_ROWS), I32),
                        pltpu.VMEM((KV_LATENT, ATT_HEADS * Q_ROWS), BF16),
                        pltpu.VMEM((LANES, IDX_HEADS * Q_ROWS), BF16),
                        pltpu.VMEM((8, Q_ROWS), F32),
                        pltpu.VMEM((8, Q_ROWS), F32),
                        pltpu.VMEM((KV_LATENT, ATT_HEADS * Q_ROWS), F32)],
        compiler_params=pltpu.CompilerParams(dimension_semantics=("arbitrary", "arbitrary"),
                                             vmem_limit_bytes=VMEM_LIMIT),
        name="dsa",
    )(qt, iqt, iwt, ik, c, ct, wuvt_p)


def _mix_router_kernel(x_ref, ya_ref, yr_ref, wo_ref, g_ref, b_ref, wr_ref, br_ref,
                       x1_ref, x1w_ref, route_ref, tot_ref, carry_ref, *, alpha, tm):
    @pl.when(pl.program_id(0) == 0)
    def _():
        carry_ref[...] = jnp.zeros_like(carry_ref)

    half = ya_ref.shape[1]
    mix = _dot(ya_ref[...], wo_ref[0:half, :]) + _dot(yr_ref[...], wo_ref[half:, :])
    x1 = _layer_norm(alpha * x_ref[...] + mix, g_ref[...], b_ref[...])
    x1_ref[...] = x1
    x1w_ref[...] = _pack_words(x1)

    logits = jnp.dot(x1, wr_ref[...], preferred_element_type=F32,
                     precision=lax.Precision.HIGHEST) + br_ref[...]
    lane = lax.broadcasted_iota(I32, logits.shape, 1)
    lane_f = lane.astype(F32)
    work = jnp.where(lane < N_EXPERTS, logits, -jnp.inf)
    sel_f = jnp.zeros(logits.shape, F32)
    denom = jnp.zeros((tm, 1), F32)
    hits, experts, weights = [], [], []
    for _ in range(TOP_K):
        mx = jnp.max(work, axis=-1, keepdims=True)
        first = jnp.min(jnp.where(work == mx, lane_f, float(LANES)), axis=-1, keepdims=True)
        hit = lane_f == first
        e = jnp.exp(mx - (weights[0][1] if weights else mx))
        hits.append(hit)
        experts.append(first)
        weights.append((e, mx))
        denom = denom + e
        sel_f = sel_f + jnp.where(hit, 1.0, 0.0)
        work = jnp.where(hit, -jnp.inf, work)

    r2 = lax.broadcasted_iota(I32, (tm, tm), 0)
    c2 = lax.broadcasted_iota(I32, (tm, tm), 1)
    before = jnp.where(c2 < r2, 1.0, 0.0).astype(BF16)
    rank = _dot(before, sel_f.astype(BF16)) + carry_ref[...]
    carry = carry_ref[...] + jnp.sum(sel_f, axis=0, keepdims=True)
    carry_ref[...] = carry
    tot_ref[...] = jnp.broadcast_to(carry, tot_ref.shape)

    route = jnp.zeros(logits.shape, F32)
    for k in range(TOP_K):
        rank_k = jnp.sum(jnp.where(hits[k], rank, 0.0), axis=-1, keepdims=True)
        route = jnp.where(lane == k, experts[k], route)
        route = jnp.where(lane == TOP_K + k, rank_k, route)
        route = jnp.where(lane == 2 * TOP_K + k, weights[k][0] / denom, route)
    route_ref[...] = route


def _mix_router(x2, ya, yr, wo, g, b, wr_p, br_p, *, alpha, tm):
    n, d = x2.shape
    half = ya.shape[1]
    rows = lambda w: pl.BlockSpec((tm, w), lambda i: (i, 0))
    full = lambda shape: pl.BlockSpec(shape, lambda i: (0,) * len(shape))
    return pl.pallas_call(
        functools.partial(_mix_router_kernel, alpha=alpha, tm=tm),
        grid=(n // tm,),
        in_specs=[rows(d), rows(half), rows(half), full(wo.shape), full(g.shape), full(b.shape),
                  full(wr_p.shape), full(br_p.shape)],
        out_specs=[rows(d), rows(d // 2), rows(LANES), full((8, LANES))],
        out_shape=[jax.ShapeDtypeStruct((n, d), F32),
                   jax.ShapeDtypeStruct((n, d // 2), I32),
                   jax.ShapeDtypeStruct((n, LANES), F32),
                   jax.ShapeDtypeStruct((8, LANES), F32)],
        scratch_shapes=[pltpu.VMEM((1, LANES), F32)],
        compiler_params=pltpu.CompilerParams(dimension_semantics=("arbitrary",),
                                             vmem_limit_bytes=VMEM_LIMIT),
        name="mix_router",
    )(x2, ya, yr, wo, g, b, wr_p, br_p)


def _sc_gather_rows(table, idx):
    n_idx = idx.shape[0]
    width = table.shape[1]
    per_worker = n_idx // SC_WORKERS
    assert per_worker * SC_WORKERS == n_idx and per_worker % SC_ROWS == 0
    mesh = plsc.VectorSubcoreMesh(core_axis_name="c", subcore_axis_name="s",
                                  num_cores=SC_CORES, num_subcores=SC_SUBCORES)

    @functools.partial(
        pl.kernel, mesh=mesh,
        out_type=jax.ShapeDtypeStruct((n_idx, width), table.dtype),
        scratch_types=[pltpu.VMEM((SC_ROWS,), I32),
                       pltpu.VMEM((SC_ROWS, width), table.dtype),
                       pltpu.SemaphoreType.DMA])
    def gather(table_hbm, idx_hbm, out_hbm, idx_v, rows_v, sem):
        worker = lax.axis_index("s") * SC_CORES + lax.axis_index("c")

        @pl.loop(0, per_worker // SC_ROWS)
        def _(step):
            base = pl.multiple_of(worker * per_worker + step * SC_ROWS, SC_ROWS)
            pltpu.sync_copy(idx_hbm.at[pl.ds(base, SC_ROWS)], idx_v)
            pltpu.async_copy(table_hbm.at[idx_v], rows_v, sem).wait()
            pltpu.sync_copy(rows_v, out_hbm.at[pl.ds(base, SC_ROWS)])

    return gather(table, idx)


def _sc_scatter_rows(rows, dest3, cap):
    n, width = rows.shape
    chunks, picks, r = dest3.shape
    assert r == SC_SCATTER_ROWS and chunks * r == n and chunks % SC_WORKERS == 0
    per_worker = chunks // SC_WORKERS
    mesh = plsc.VectorSubcoreMesh(core_axis_name="c", subcore_axis_name="s",
                                  num_cores=SC_CORES, num_subcores=SC_SUBCORES)

    @functools.partial(
        pl.kernel, mesh=mesh,
        out_type=jax.ShapeDtypeStruct((cap, width), rows.dtype),
        scratch_types=[pltpu.VMEM((picks, r), I32),
                       pltpu.VMEM((r, width), rows.dtype),
                       pltpu.SemaphoreType.DMA])
    def scatter(rows_hbm, dest_hbm, out_hbm, idx_v, rows_v, sem):
        worker = lax.axis_index("s") * SC_CORES + lax.axis_index("c")

        @pl.loop(0, per_worker)
        def _(step):
            chunk = worker * per_worker + step
            base = pl.multiple_of(chunk * r, r)
            pltpu.sync_copy(rows_hbm.at[pl.ds(base, r)], rows_v)
            pltpu.sync_copy(dest_hbm.at[chunk], idx_v)
            for k in range(picks):
                pltpu.async_copy(rows_v, out_hbm.at[idx_v.at[k]], sem).wait()

    return scatter(rows, dest3)


def _pack_words(v):
    k = v.shape[1] // 2
    hi = pltpu.bitcast(v[:, :k].astype(BF16).astype(F32), I32)
    lo = pltpu.bitcast(v[:, k:].astype(BF16).astype(F32), I32)
    return jnp.bitwise_or(hi, lax.shift_right_logical(lo, jnp.int32(16)))


def _unpack_words(w):
    hi = pltpu.bitcast(jnp.bitwise_and(w, jnp.int32(-65536)), F32)
    lo = pltpu.bitcast(jnp.left_shift(w, 16), F32)
    return jnp.concatenate([hi, lo], axis=1).astype(BF16)


def _expert_kernel(exp_ref, blk_ref, new_ref, x_ref, wup_ref, bup_ref, wdn_ref, bdn_ref, o_ref,
                   wup_s, wdn_s, *, d_ff):
    del exp_ref, blk_ref

    @pl.when(new_ref[pl.program_id(0)] == 1)
    def _():
        wup_s[...] = wup_ref[0].astype(BF16)
        wdn_s[...] = wdn_ref[0].astype(BF16)

    h = _dot(_unpack_words(x_ref[...]), wup_s[...]) + bup_ref[0]
    glu = jnp.minimum(h[:, :d_ff], SWIGLU_LIMIT)
    lin = jnp.clip(h[:, d_ff:], -SWIGLU_LIMIT, SWIGLU_LIMIT)
    a = glu * _sigmoid(SWIGLU_ALPHA * glu) * (lin + 1.0)
    o_ref[...] = _pack_words(_dot(a.astype(BF16), wdn_s[...]) + bdn_ref[0])


def _experts(blk_expert, blk_index, blk_new, xs, wup, bup, wdn, bdn):
    cap, words = xs.shape
    d = 2 * words
    d_ff = wdn.shape[1]
    row_map = lambda i, exp, blk, new: (blk[i], 0)
    exp_map3 = lambda i, exp, blk, new: (exp[i], 0, 0)
    grid_spec = pltpu.PrefetchScalarGridSpec(
        num_scalar_prefetch=3,
        grid=(cap // MOE_ROWS,),
        in_specs=[pl.BlockSpec((MOE_ROWS, words), row_map),
                  pl.BlockSpec((1, d, 2 * d_ff), exp_map3),
                  pl.BlockSpec((1, 1, 2 * d_ff), exp_map3),
                  pl.BlockSpec((1, d_ff, d), exp_map3),
                  pl.BlockSpec((1, 1, d), exp_map3)],
        out_specs=pl.BlockSpec((MOE_ROWS, words), row_map),
        scratch_shapes=[pltpu.VMEM((d, 2 * d_ff), BF16), pltpu.VMEM((d_ff, d), BF16)],
    )
    return pl.pallas_call(
        functools.partial(_expert_kernel, d_ff=d_ff),
        grid_spec=grid_spec,
        out_shape=jax.ShapeDtypeStruct((cap, words), I32),
        compiler_params=pltpu.CompilerParams(dimension_semantics=("arbitrary",),
                                             vmem_limit_bytes=VMEM_LIMIT),
        name="experts",
    )(blk_expert, blk_index, blk_new, xs, wup, bup, wdn, bdn)


def _route_tables(route, totals, *, cap):
    n = route.shape[0]
    expert = route[:, 0:TOP_K].astype(I32)
    rank = route[:, TOP_K:2 * TOP_K].astype(I32)
    counts = totals[0, :N_EXPERTS].astype(I32)
    padded = (counts + MOE_ROWS - 1) // MOE_ROWS * MOE_ROWS
    pad_end = jnp.cumsum(padded)
    pad_start = pad_end - padded
    dest = pad_start[expert] + rank
    dest_pick_major = dest.T.reshape(-1)
    dest_chunks = dest.reshape(n // SC_SCATTER_ROWS, SC_SCATTER_ROWS, TOP_K).transpose(0, 2, 1)
    n_blk = cap // MOE_ROWS
    used = pad_end[-1] // MOE_ROWS
    blk = jnp.minimum(jnp.arange(n_blk, dtype=I32), used - 1)
    blk_expert = jnp.minimum(jnp.sum(pad_end[None, :] <= (blk * MOE_ROWS)[:, None], axis=1),
                             N_EXPERTS - 1).astype(I32)
    blk_new = jnp.concatenate([jnp.ones((1,), I32), (blk_expert[1:] != blk_expert[:-1]).astype(I32)])
    return dest_pick_major, dest_chunks, blk_expert, blk, blk_new


def _final_kernel(x1_ref, ya_ref, yb_ref, yc_ref, yd_ref, route_ref, p_ref, g1_ref, b1_ref,
                  wp_ref, wg_ref, g2_ref, b2_ref, o_ref, *, alpha):
    ffn = jnp.zeros(x1_ref.shape, F32)
    for k, yk_ref in enumerate((ya_ref, yb_ref, yc_ref, yd_ref)):
        gate_k = route_ref[:, 2 * TOP_K + k:2 * TOP_K + k + 1]
        ffn = ffn + gate_k * _unpack_words(yk_ref[...]).astype(F32)
    x2 = _layer_norm(alpha * x1_ref[...] + ffn, g1_ref[...], b1_ref[...])
    gate = _sigmoid(_dot(x2.astype(BF16), wg_ref[...]))
    ple = _dot(p_ref[...].astype(BF16), wp_ref[...]) * gate
    o_ref[...] = _layer_norm(alpha * x2 + ple, g2_ref[...], b2_ref[...])


def _final(x1, y4w, route, p2, g1, b1, wp, wg, g2, b2, *, alpha, tm):
    n, d = x1.shape
    assert TOP_K == 4
    rows = lambda w: pl.BlockSpec((tm, w), lambda i: (i, 0))
    full = lambda shape: pl.BlockSpec(shape, lambda i: (0,) * len(shape))
    pick = lambda k: pl.BlockSpec((tm, d // 2), lambda i: (k * (n // tm) + i, 0))
    return pl.pallas_call(
        functools.partial(_final_kernel, alpha=alpha),
        grid=(n // tm,),
        in_specs=[rows(d), pick(0), pick(1), pick(2), pick(3), rows(LANES), rows(p2.shape[1]),
                  full(g1.shape), full(b1.shape),
                  full(wp.shape), full(wg.shape), full(g2.shape), full(b2.shape)],
        out_specs=rows(d),
        out_shape=jax.ShapeDtypeStruct((n, d), F32),
        compiler_params=pltpu.CompilerParams(dimension_semantics=("arbitrary",),
                                             vmem_limit_bytes=VMEM_LIMIT),
        name="final",
    )(x1, y4w, y4w, y4w, y4w, route, p2, g1, b1, wp, wg, g2, b2)


def _pack_w_in(w_in):
    d = w_in.shape[0]
    sizes = (ATT_HEADS * ATT_HEAD_DIM, KV_LATENT, IDX_HEADS * IDX_DIM, IDX_DIM, IDX_HEADS,
             HG_HEADS * HG_DIM, HG_HEADS * HG_DIM, HG_HEADS * HG_DIM, HG_HEADS * HG_DIM)
    offs = [0]
    for s in sizes:
        offs.append(offs[-1] + s)
    sec = [w_in[:, offs[k]:offs[k + 1]] for k in range(len(sizes))]
    iq = sec[2].reshape(d, IDX_HEADS, IDX_DIM)
    iq = jnp.pad(iq, ((0, 0), (0, 0), (0, LANES - IDX_DIM))).reshape(d, IDX_HEADS * LANES)
    ikw = jnp.pad(jnp.concatenate([sec[3], sec[4]], axis=1), ((0, 0), (0, LANES - IDX_DIM - IDX_HEADS)))
    w_n = jnp.concatenate([sec[1], ikw, sec[5], sec[6], sec[7], sec[8]], axis=1).astype(BF16)
    w_t = jnp.concatenate([sec[0], iq, ikw, sec[1]], axis=1).T.astype(BF16)
    return w_n, w_t


def _block_diag_uk_t(w_uk):
    eye = jnp.eye(ATT_HEADS, dtype=w_uk.dtype)
    bd = jnp.einsum("rhd,hg->hrgd", w_uk, eye)
    return bd.reshape(ATT_HEADS * KV_LATENT, ATT_HEADS * ATT_HEAD_DIM).astype(BF16)


def _uv_t(w_uv):
    return jnp.transpose(w_uv, (1, 2, 0)).astype(BF16)


def _layer(x, p_l, w_in, kv_g, ik_g, ik_b, w_uk, w_uv, lb, hg_ng, w_o, ln_mix_g, ln_mix_b,
           w_router, b_router, w_up, b_up, w_down, b_down, ln_ffn_g, ln_ffn_b,
           w_ple_proj, w_ple_gate, ln_ple_g, ln_ple_b, *, alpha, tm, hg_tb):
    bsz, t, d = x.shape
    n = bsz * t
    x2 = x.reshape(n, d)
    row = lambda v: v.reshape(1, -1).astype(F32)
    pad_lane = lambda v: jnp.pad(row(v), ((0, 0), (0, LANES - v.shape[-1])))

    w_n, w_t = _pack_w_in(w_in)
    qt, iqt, iwt, c, ct, ik, hg = _inproj(x2, w_n, w_t, _block_diag_uk_t(w_uk), row(kv_g),
                                          kv_g.reshape(-1, 1).astype(F32), pad_lane(ik_g), pad_lane(ik_b), tm=tm)
    y_rec = _hgrn2(hg.reshape(bsz, t, -1), row(lb), row(hg_ng), tb=hg_tb)
    y_att = _dsa(qt, iqt, iwt, ik, c, ct, _uv_t(w_uv), bsz=bsz, t=t)

    wr_p = jnp.pad(w_router.astype(F32), ((0, 0), (0, LANES - N_EXPERTS)))
    x1, x1w, route, totals = _mix_router(
        x2, y_att, y_rec.reshape(n, -1), w_o.astype(BF16), row(ln_mix_g), row(ln_mix_b),
        wr_p, pad_lane(b_router), alpha=alpha, tm=tm)

    step = SC_WORKERS * SC_ROWS
    cap = -(-(n * TOP_K + N_EXPERTS * MOE_ROWS) // step) * step
    dest, dest_chunks, blk_expert, blk_index, blk_new = _route_tables(route, totals, cap=cap)
    d_ff = w_down.shape[1]
    xs = _sc_scatter_rows(x1w, dest_chunks, cap)
    outw = _experts(blk_expert, blk_index, blk_new, xs,
                    w_up.astype(F32), b_up.reshape(N_EXPERTS, 1, 2 * d_ff).astype(F32),
                    w_down.astype(F32), b_down.reshape(N_EXPERTS, 1, d).astype(F32))
    y4w = _sc_gather_rows(outw, dest)

    out = _final(x1, y4w, route, p_l.reshape(n, -1), row(ln_ffn_g), row(ln_ffn_b),
                 w_ple_proj.astype(BF16), w_ple_gate.astype(BF16), row(ln_ple_g), row(ln_ple_b),
                 alpha=alpha, tm=tm)
    return out.reshape(bsz, t, d)


def kernel(x, p, w_in, kv_norm_g, idx_k_norm_g, idx_k_norm_b, w_uk, w_uv, hg_lb_logits, hg_norm_g, w_o,
           ln_mix_g, ln_mix_b, w_router, b_router, w_up, b_up, w_down, b_down, ln_ffn_g, ln_ffn_b,
           w_ple_proj, w_ple_gate, ln_ple_g, ln_ple_b):
    depth = w_in.shape[0]
    alpha = (2.0 * depth) ** 0.25
    lower_bounds = jnp.cumsum(jax.nn.softmax(hg_lb_logits.astype(F32), axis=0), axis=0)
    n = x.shape[0] * x.shape[1]
    tm = min(512, n)
    hg_tb = min(512, x.shape[1])
    for l in range(depth):
        x = _layer(x, p[l], w_in[l], kv_norm_g[l], idx_k_norm_g[l], idx_k_norm_b[l], w_uk[l], w_uv[l],
                   lower_bounds[l], hg_norm_g[l], w_o[l], ln_mix_g[l], ln_mix_b[l], w_router[l],
                   b_router[l], w_up[l], b_up[l], w_down[l], b_down[l], ln_ffn_g[l], ln_ffn_b[l],
                   w_ple_proj[l], w_ple_gate[l], ln_ple_g[l], ln_ple_b[l],
                   alpha=alpha, tm=tm, hg_tb=hg_tb)
    return x
```

```python
import functools

import jax
import jax.numpy as jnp
from jax import lax
from jax.experimental import pallas as pl
from jax.experimental.pallas import tpu as pltpu
from jax.experimental.pallas import tpu_sc as plsc

F32 = jnp.float32
BF16 = jnp.bfloat16
I32 = jnp.int32

ATT_HEADS = 8
ATT_HEAD_DIM = 64
KV_LATENT = 256
IDX_HEADS = 8
IDX_DIM = 64
TOPK_MAX = 256
HG_HEADS = 4
HG_DIM = 128
HG_CHUNK = 32
N_EXPERTS = 32
TOP_K = 4
SWIGLU_LIMIT = 7.0
SWIGLU_ALPHA = 1.702
LN_EPS = 1e-5
RMS_EPS = 1e-6

LANES = 128
Q_ROWS = 128
KEY_CHUNK = 512
COUNT_ROWS = 128
MOE_ROWS = 512
SC_CORES = 2
SC_SUBCORES = 16
SC_WORKERS = SC_CORES * SC_SUBCORES
SC_ROWS = 64
SC_SCATTER_ROWS = 128
VMEM_LIMIT = 56 * 1024 * 1024

INT_MIN = -(2 ** 31)
NEG_BIG = -1e30
LOG2_E = 1.4426950408889634

_N_KV = 0
_N_IKW = _N_KV + KV_LATENT
_N_HG = _N_IKW + LANES
_N_END = _N_HG + 4 * HG_HEADS * HG_DIM
_T_AQ = 0
_T_IQ = _T_AQ + ATT_HEADS * ATT_HEAD_DIM
_T_IKW = _T_IQ + IDX_HEADS * LANES
_T_KV = _T_IKW + LANES
_T_END = _T_KV + KV_LATENT


def _dot(a, b):
    return jnp.dot(a, b, preferred_element_type=F32)


def _dot_nt(a, b):
    return lax.dot_general(a, b, (((1,), (1,)), ((), ())), preferred_element_type=F32)


def _dot_tn(a, b):
    return lax.dot_general(a, b, (((0,), (0,)), ((), ())), preferred_element_type=F32)


def _layer_norm(z, g, b):
    mu = jnp.mean(z, axis=-1, keepdims=True)
    d = z - mu
    var = jnp.mean(d * d, axis=-1, keepdims=True)
    return d * lax.rsqrt(var + LN_EPS) * g + b


def _sigmoid(x):
    return 1.0 / (1.0 + jnp.exp(-x))


def _inproj_kernel(x_ref, wn_ref, wt_ref, wukt_ref, kvg_ref, kvgc_ref, ikg_ref, ikb_ref,
                   qt_ref, iqt_ref, iwt_ref, c_ref, ct_ref, ik_ref, hg_ref):
    xb = x_ref[...].astype(BF16)

    aqt = _dot_nt(wt_ref[_T_AQ:_T_IQ, :], xb).astype(BF16)
    qt_ref[...] = (_dot(wukt_ref[...], aqt) * (LOG2_E * ATT_HEAD_DIM ** -0.5)).astype(BF16)
    iqt_ref[...] = (_dot_nt(wt_ref[_T_IQ:_T_IKW, :], xb) * (IDX_DIM ** -0.5)).astype(BF16)
    iwt_ref[...] = _dot_nt(wt_ref[_T_IKW:_T_KV, :], xb) * (IDX_HEADS ** -0.5)
    act = _dot_nt(wt_ref[_T_KV:_T_END, :], xb)
    ct = act * lax.rsqrt(jnp.mean(act * act, axis=0, keepdims=True) + RMS_EPS) * kvgc_ref[...]
    ct_ref[...] = ct.astype(BF16)

    ac = _dot(xb, wn_ref[:, _N_KV:_N_IKW])
    c = ac * lax.rsqrt(jnp.mean(ac * ac, axis=-1, keepdims=True) + RMS_EPS) * kvg_ref[...]
    c_ref[...] = c.astype(BF16)

    ikw = _dot(xb, wn_ref[:, _N_IKW:_N_HG])
    lane = lax.broadcasted_iota(I32, ikw.shape, 1)
    is_k = lane < IDX_DIM
    mu = jnp.sum(jnp.where(is_k, ikw, 0.0), axis=-1, keepdims=True) * (1.0 / IDX_DIM)
    d = jnp.where(is_k, ikw - mu, 0.0)
    var = jnp.sum(d * d, axis=-1, keepdims=True) * (1.0 / IDX_DIM)
    ik = d * lax.rsqrt(var + LN_EPS) * ikg_ref[...] + ikb_ref[...]
    ik_ref[...] = ik.astype(BF16)

    hg_ref[...] = _dot(xb, wn_ref[:, _N_HG:_N_END])


def _inproj(x2, w_n, w_t, wuk_t, kvg, kvg_col, ikg, ikb, *, tm):
    n, d = x2.shape
    grid = (n // tm,)
    full = lambda shape: pl.BlockSpec(shape, lambda i: (0,) * len(shape))
    rows = lambda w: pl.BlockSpec((tm, w), lambda i: (i, 0))
    cols = lambda h: pl.BlockSpec((h, tm), lambda i: (0, i))
    return pl.pallas_call(
        _inproj_kernel,
        grid=grid,
        in_specs=[rows(d), full(w_n.shape), full(w_t.shape), full(wuk_t.shape), full(kvg.shape),
                  full(kvg_col.shape), full(ikg.shape), full(ikb.shape)],
        out_specs=[cols(ATT_HEADS * KV_LATENT), cols(IDX_HEADS * LANES), cols(LANES),
                   rows(KV_LATENT), cols(KV_LATENT), rows(LANES), rows(4 * HG_HEADS * HG_DIM)],
        out_shape=[jax.ShapeDtypeStruct((ATT_HEADS * KV_LATENT, n), BF16),
                   jax.ShapeDtypeStruct((IDX_HEADS * LANES, n), BF16),
                   jax.ShapeDtypeStruct((LANES, n), F32),
                   jax.ShapeDtypeStruct((n, KV_LATENT), BF16),
                   jax.ShapeDtypeStruct((KV_LATENT, n), BF16),
                   jax.ShapeDtypeStruct((n, LANES), BF16),
                   jax.ShapeDtypeStruct((n, 4 * HG_HEADS * HG_DIM), F32)],
        compiler_params=pltpu.CompilerParams(dimension_semantics=("arbitrary",),
                                             vmem_limit_bytes=VMEM_LIMIT),
        name="inproj",
    )(x2, w_n, w_t, wuk_t, kvg, kvg_col, ikg, ikb)


def _hgrn2_kernel(hg_ref, lb_ref, ng_ref, o_ref, state_ref, *, tb):
    @pl.when(pl.program_id(1) == 0)
    def _():
        state_ref[...] = jnp.zeros_like(state_ref)

    nchunk = tb // HG_CHUNK
    width = HG_HEADS * HG_DIM
    row = lax.broadcasted_iota(I32, (tb, HG_DIM), 0)
    rin = jnp.bitwise_and(row, HG_CHUNK - 1)
    r2 = lax.broadcasted_iota(I32, (tb, tb), 0)
    c2 = lax.broadcasted_iota(I32, (tb, tb), 1)
    intra = jnp.logical_and(r2 // HG_CHUNK == c2 // HG_CHUNK, c2 <= r2)

    for h in range(HG_HEADS):
        sl = slice(h * HG_DIM, (h + 1) * HG_DIM)
        gq = hg_ref[0, :, h * HG_DIM:(h + 1) * HG_DIM]
        gf = hg_ref[0, :, width + h * HG_DIM:width + (h + 1) * HG_DIM]
        gi = hg_ref[0, :, 2 * width + h * HG_DIM:2 * width + (h + 1) * HG_DIM]
        gg = hg_ref[0, :, 3 * width + h * HG_DIM:3 * width + (h + 1) * HG_DIM]
        lb = lb_ref[:, sl]
        forget = lb + (1.0 - lb) * _sigmoid(gf)
        logf = jnp.log(forget)
        b = logf
        s = 1
        while s < HG_CHUNK:
            b = b + jnp.where(rin >= s, pltpu.roll(b, s, axis=0), 0.0)
            s *= 2
        tot = jnp.sum(logf.reshape(nchunk, HG_CHUNK, HG_DIM), axis=1, keepdims=True)
        bl = jnp.broadcast_to(tot, (nchunk, HG_CHUNK, HG_DIM)).reshape(tb, HG_DIM)
        kk = 1.0 - forget
        q_dec = (gq * _sigmoid(gq) * jnp.exp(b)).astype(BF16)
        k_inv = (kk * jnp.exp(-b)).astype(BF16)
        k_end = (kk * jnp.exp(bl - b)).astype(BF16)
        vb = gi.astype(BF16)
        scores = jnp.where(intra, _dot_nt(q_dec, k_inv), 0.0).astype(BF16)
        o = _dot(scores, vb)
        decay = jnp.exp(tot)
        chunk_rows = [slice(n * HG_CHUNK, (n + 1) * HG_CHUNK) for n in range(nchunk)]
        incs = [_dot_tn(vb[rs], k_end[rs]) for rs in chunk_rows]
        st = state_ref[h]
        starts = []
        for n in range(nchunk):
            starts.append(st.astype(BF16))
            st = st * decay[n] + incs[n]
        state_ref[h] = st
        inter = [_dot_nt(q_dec[rs], starts[n]) for n, rs in enumerate(chunk_rows)]
        o = o + jnp.concatenate(inter, axis=0)
        o = o * lax.rsqrt(jnp.mean(o * o, axis=-1, keepdims=True) + RMS_EPS)
        o = o * ng_ref[:, sl] * (gg * _sigmoid(gg))
        o_ref[0, :, sl] = o.astype(o_ref.dtype)


def _hgrn2(hg3, lb, ng, *, tb):
    bsz, t, w4 = hg3.shape
    width = HG_HEADS * HG_DIM
    return pl.pallas_call(
        functools.partial(_hgrn2_kernel, tb=tb),
        grid=(bsz, t // tb),
        in_specs=[pl.BlockSpec((1, tb, w4), lambda b, i: (b, i, 0)),
                  pl.BlockSpec((1, width), lambda b, i: (0, 0)),
                  pl.BlockSpec((1, width), lambda b, i: (0, 0))],
        out_specs=pl.BlockSpec((1, tb, width), lambda b, i: (b, i, 0)),
        out_shape=jax.ShapeDtypeStruct((bsz, t, width), BF16),
        scratch_shapes=[pltpu.VMEM((HG_HEADS, HG_DIM, HG_DIM), F32)],
        compiler_params=pltpu.CompilerParams(dimension_semantics=("arbitrary", "arbitrary"),
                                             vmem_limit_bytes=VMEM_LIMIT),
        name="hgrn2",
    )(hg3, lb, ng)


def _dsa_kernel(qt_ref, iqt_ref, iwt_ref, ik_ref, c_ref, ct_ref, wuvt_ref, o_ref,
                keys_ref, cut_ref, qall_ref, iqall_ref, m_ref, l_ref, acc_ref, *, topk, pos_bits):
    n = pl.program_id(1)
    t0 = n * Q_ROWS
    nch = (t0 + Q_ROWS + KEY_CHUNK - 1) // KEY_CHUNK
    kf = float(topk)
    key_i = lax.broadcasted_iota(I32, (KEY_CHUNK, Q_ROWS), 0)
    tq = t0 + lax.broadcasted_iota(I32, (KEY_CHUNK, Q_ROWS), 1)
    pair = 2 * Q_ROWS

    for h in range(ATT_HEADS):
        qall_ref[:, h * Q_ROWS:(h + 1) * Q_ROWS] = qt_ref[h * KV_LATENT:(h + 1) * KV_LATENT, :]
    for h in range(IDX_HEADS):
        iqall_ref[:, h * Q_ROWS:(h + 1) * Q_ROWS] = iqt_ref[h * LANES:(h + 1) * LANES, :]
    head_w = [iwt_ref[IDX_DIM + h:IDX_DIM + h + 1, :] for h in range(IDX_HEADS)]

    def index_body(j, carry):
        k0 = pl.multiple_of(j * KEY_CHUNK, KEY_CHUNK)
        ikc = ik_ref[pl.ds(k0, KEY_CHUNK), :]
        isc = jnp.zeros((KEY_CHUNK, Q_ROWS), F32)
        for hp in range(IDX_HEADS // 2):
            z = _dot(ikc, iqall_ref[:, hp * pair:(hp + 1) * pair])
            for hh in range(2):
                isc = isc + head_w[2 * hp + hh] * jnp.maximum(z[:, hh * Q_ROWS:(hh + 1) * Q_ROWS], 0.0)
        isc = jnp.where(isc == 0.0, 0.0, isc)
        bits = pltpu.bitcast(isc, I32)
        key = jnp.bitwise_xor(bits, jnp.bitwise_and(jnp.right_shift(bits, 31), 0x7FFFFFFF))
        key = jnp.where(k0 + key_i <= tq, key, INT_MIN)
        keys_ref[pl.ds(k0, KEY_CHUNK), :] = key
        return carry

    lax.fori_loop(0, nch, index_body, 0)

    def count(pred):
        def body(j, acc):
            k0 = pl.multiple_of(j * KEY_CHUNK, KEY_CHUNK)
            hit = jnp.where(pred(keys_ref[pl.ds(k0, KEY_CHUNK), :], k0), 1.0, 0.0)
            return acc + jnp.sum(hit.reshape(KEY_CHUNK // COUNT_ROWS, COUNT_ROWS, Q_ROWS), axis=0)
        acc = lax.fori_loop(0, nch, body, jnp.zeros((COUNT_ROWS, Q_ROWS), F32))
        return jnp.sum(acc, axis=0, keepdims=True)

    def count_ge(cand):
        return count(lambda k, k0: k >= cand)

    thr0 = jnp.where(count_ge(jnp.zeros((1, Q_ROWS), I32)) >= kf, 0, INT_MIN).astype(I32)

    def bit_body(i, thr):
        cand = thr + lax.shift_left(jnp.int32(1), 30 - i)
        return jnp.where(count_ge(cand) >= kf, cand, thr)

    thr = lax.fori_loop(0, 31, bit_body, thr0)
    live = thr > INT_MIN
    cut_ref[...] = jnp.broadcast_to(jnp.where(live, jnp.int32(2 ** 30), -1), cut_ref.shape)
    n_ge = count_ge(thr)
    tie = jnp.max(jnp.where(jnp.logical_and(live, n_ge > kf), 1.0, 0.0))

    @pl.when(tie > 0.0)
    def _():
        need = kf - count(lambda k, k0: k > thr)

        def cut_body(i, cut):
            cand = cut + lax.shift_left(jnp.int32(1), pos_bits - 1 - i)
            below = count(lambda k, k0: jnp.logical_and(k == thr, k0 + key_i < cand))
            return jnp.where(below < need, cand, cut)

        cut = lax.fori_loop(0, pos_bits, cut_body, jnp.zeros((1, Q_ROWS), I32))
        cut_ref[...] = jnp.broadcast_to(jnp.where(live, cut, -1), cut_ref.shape)

    m_ref[...] = jnp.full(m_ref.shape, NEG_BIG, F32)
    l_ref[...] = jnp.zeros(l_ref.shape, F32)
    acc_ref[...] = jnp.zeros(acc_ref.shape, F32)
    cut_b = cut_ref[0:1, :]

    def attn_body(j, carry):
        k0 = pl.multiple_of(j * KEY_CHUNK, KEY_CHUNK)
        key = keys_ref[pl.ds(k0, KEY_CHUNK), :]
        pos = k0 + key_i
        sel = jnp.logical_or(key > thr, jnp.logical_and(key == thr, pos <= cut_b))
        rel = jnp.where(sel, (pos - t0).astype(F32), NEG_BIG)
        cc = c_ref[pl.ds(k0, KEY_CHUNK), :]
        cct = ct_ref[:, pl.ds(k0, KEY_CHUNK)]
        for hp in range(ATT_HEADS // 2):
            cs = slice(hp * pair, (hp + 1) * pair)
            s2 = _dot(cc, qall_ref[:, cs])
            ps, alphas = [], []
            for hh in range(2):
                h = 2 * hp + hh
                slope = LOG2_E * 2.0 ** (-(8.0 / ATT_HEADS) * (h + 1))
                s = s2[:, hh * Q_ROWS:(hh + 1) * Q_ROWS] + slope * rel
                m_old = m_ref[h:h + 1, :]
                m_new = jnp.maximum(m_old, jnp.max(s, axis=0, keepdims=True))
                alpha = jnp.exp2(m_old - m_new)
                p = jnp.exp2(s - m_new)
                l_ref[h:h + 1, :] = alpha * l_ref[h:h + 1, :] + jnp.sum(p, axis=0, keepdims=True)
                m_ref[h:h + 1, :] = m_new
                ps.append(p.astype(BF16))
                alphas.append(alpha)
            acc_ref[:, cs] = (acc_ref[:, cs] * jnp.concatenate(alphas, axis=1)
                              + _dot(cct, jnp.concatenate(ps, axis=1)))
        return carry

    lax.fori_loop(0, nch, attn_body, 0)

    yt = []
    for h in range(ATT_HEADS):
        o_lat = (acc_ref[:, h * Q_ROWS:(h + 1) * Q_ROWS] / l_ref[h:h + 1, :]).astype(BF16)
        yt.append(_dot(wuvt_ref[h], o_lat))
    o_ref[...] = jnp.concatenate(yt, axis=0).T.astype(o_ref.dtype)


def _dsa(qt, iqt, iwt, ik, c, ct, wuvt_p, *, bsz, t):
    n = bsz * t
    nq = t // Q_ROWS
    topk = min(TOPK_MAX, t // 4)
    assert t % KEY_CHUNK == 0
    width = ATT_HEADS * ATT_HEAD_DIM
    qcols = lambda h: pl.BlockSpec((h, Q_ROWS), lambda b, i: (0, b * nq + i))
    return pl.pallas_call(
        functools.partial(_dsa_kernel, topk=topk, pos_bits=(t - 1).bit_length()),
        grid=(bsz, nq),
        in_specs=[qcols(ATT_HEADS * KV_LATENT), qcols(IDX_HEADS * LANES), qcols(LANES),
                  pl.BlockSpec((t, LANES), lambda b, i: (b, 0)),
                  pl.BlockSpec((t, KV_LATENT), lambda b, i: (b, 0)),
                  pl.BlockSpec((KV_LATENT, t), lambda b, i: (0, b)),
                  pl.BlockSpec(wuvt_p.shape, lambda b, i: (0, 0, 0))],
        out_specs=pl.BlockSpec((Q_ROWS, width), lambda b, i: (b * nq + i, 0)),
        out_shape=jax.ShapeDtypeStruct((n, width), BF16),
        scratch_shapes=[pltpu.VMEM((t, Q_ROWS), I32),
                        pltpu.VMEM((8, Q_ROWS), I32),
                        pltpu.VMEM((KV_LATENT, ATT_HEADS * Q_ROWS), BF16),
                        pltpu.VMEM((LANES, IDX_HEADS * Q_ROWS), BF16),
                        pltpu.VMEM((8, Q_ROWS), F32),
                        pltpu.VMEM((8, Q_ROWS), F32),
                        pltpu.VMEM((KV_LATENT, ATT_HEADS * Q_ROWS), F32)],
        compiler_params=pltpu.CompilerParams(dimension_semantics=("arbitrary", "arbitrary"),
                                             vmem_limit_bytes=VMEM_LIMIT),
        name="dsa",
    )(qt, iqt, iwt, ik, c, ct, wuvt_p)


def _mix_router_kernel(x_ref, ya_ref, yr_ref, wo_ref, g_ref, b_ref, wr_ref, br_ref,
                       x1_ref, x1w_ref, route_ref, tot_ref, carry_ref, *, alpha, tm):
    @pl.when(pl.program_id(0) == 0)
    def _():
        carry_ref[...] = jnp.zeros_like(carry_ref)

    half = ya_ref.shape[1]
    tr = tm
    lane = lax.broadcasted_iota(I32, (tr, LANES), 1)
    lane_f = lane.astype(F32)
    r2 = lax.broadcasted_iota(I32, (tr, tr), 0)
    c2 = lax.broadcasted_iota(I32, (tr, tr), 1)
    before = jnp.where(c2 < r2, 1.0, 0.0).astype(BF16)
    carry = carry_ref[...]

    for part in range(tm // tr):
        rows = slice(part * tr, (part + 1) * tr)
        mix = _dot(ya_ref[rows, :], wo_ref[0:half, :]) + _dot(yr_ref[rows, :], wo_ref[half:, :])
        x1 = _layer_norm(alpha * x_ref[rows, :] + mix, g_ref[...], b_ref[...])
        x1_ref[rows, :] = x1
        x1w_ref[rows, :] = _pack_words(x1)

        logits = jnp.dot(x1, wr_ref[...], preferred_element_type=F32,
                         precision=lax.Precision.HIGHEST) + br_ref[...]
        work = jnp.where(lane < N_EXPERTS, logits, -jnp.inf)
        sel_f = jnp.zeros(logits.shape, F32)
        denom = jnp.zeros((tr, 1), F32)
        hits, experts, weights = [], [], []
        for _ in range(TOP_K):
            mx = jnp.max(work, axis=-1, keepdims=True)
            first = jnp.min(jnp.where(work == mx, lane_f, float(LANES)), axis=-1, keepdims=True)
            hit = lane_f == first
            e = jnp.exp(mx - (weights[0][1] if weights else mx))
            hits.append(hit)
            experts.append(first)
            weights.append((e, mx))
            denom = denom + e
            sel_f = sel_f + jnp.where(hit, 1.0, 0.0)
            work = jnp.where(hit, -jnp.inf, work)

        rank = _dot(before, sel_f.astype(BF16)) + carry
        carry = carry + jnp.sum(sel_f, axis=0, keepdims=True)

        route = jnp.zeros(logits.shape, F32)
        for k in range(TOP_K):
            rank_k = jnp.sum(jnp.where(hits[k], rank, 0.0), axis=-1, keepdims=True)
            route = jnp.where(lane == k, experts[k], route)
            route = jnp.where(lane == TOP_K + k, rank_k, route)
            route = jnp.where(lane == 2 * TOP_K + k, weights[k][0] / denom, route)
        route_ref[rows, :] = route

    carry_ref[...] = carry
    tot_ref[...] = jnp.broadcast_to(carry, tot_ref.shape)


def _mix_router(x2, ya, yr, wo, g, b, wr_p, br_p, *, alpha, tm):
    n, d = x2.shape
    half = ya.shape[1]
    rows = lambda w: pl.BlockSpec((tm, w), lambda i: (i, 0))
    full = lambda shape: pl.BlockSpec(shape, lambda i: (0,) * len(shape))
    return pl.pallas_call(
        functools.partial(_mix_router_kernel, alpha=alpha, tm=tm),
        grid=(n // tm,),
        in_specs=[rows(d), rows(half), rows(half), full(wo.shape), full(g.shape), full(b.shape),
                  full(wr_p.shape), full(br_p.shape)],
        out_specs=[rows(d), rows(d // 2), rows(LANES), full((8, LANES))],
        out_shape=[jax.ShapeDtypeStruct((n, d), F32),
                   jax.ShapeDtypeStruct((n, d // 2), I32),
                   jax.ShapeDtypeStruct((n, LANES), F32),
                   jax.ShapeDtypeStruct((8, LANES), F32)],
        scratch_shapes=[pltpu.VMEM((1, LANES), F32)],
        compiler_params=pltpu.CompilerParams(dimension_semantics=("arbitrary",),
                                             vmem_limit_bytes=VMEM_LIMIT),
        name="mix_router",
    )(x2, ya, yr, wo, g, b, wr_p, br_p)


def _sc_gather_rows(table, idx):
    n_idx = idx.shape[0]
    width = table.shape[1]
    per_worker = n_idx // SC_WORKERS
    assert per_worker * SC_WORKERS == n_idx and per_worker % SC_ROWS == 0
    mesh = plsc.VectorSubcoreMesh(core_axis_name="c", subcore_axis_name="s",
                                  num_cores=SC_CORES, num_subcores=SC_SUBCORES)

    @functools.partial(
        pl.kernel, mesh=mesh,
        out_type=jax.ShapeDtypeStruct((n_idx, width), table.dtype),
        scratch_types=[pltpu.VMEM((SC_ROWS,), I32),
                       pltpu.VMEM((SC_ROWS, width), table.dtype),
                       pltpu.SemaphoreType.DMA])
    def gather(table_hbm, idx_hbm, out_hbm, idx_v, rows_v, sem):
        worker = lax.axis_index("s") * SC_CORES + lax.axis_index("c")

        @pl.loop(0, per_worker // SC_ROWS)
        def _(step):
            base = pl.multiple_of(worker * per_worker + step * SC_ROWS, SC_ROWS)
            pltpu.sync_copy(idx_hbm.at[pl.ds(base, SC_ROWS)], idx_v)
            pltpu.async_copy(table_hbm.at[idx_v], rows_v, sem).wait()
            pltpu.sync_copy(rows_v, out_hbm.at[pl.ds(base, SC_ROWS)])

    return gather(table, idx)


def _sc_scatter_rows(rows, dest3, cap):
    n, width = rows.shape
    chunks, picks, r = dest3.shape
    assert r == SC_SCATTER_ROWS and chunks * r == n and chunks % SC_WORKERS == 0
    per_worker = chunks // SC_WORKERS
    mesh = plsc.VectorSubcoreMesh(core_axis_name="c", subcore_axis_name="s",
                                  num_cores=SC_CORES, num_subcores=SC_SUBCORES)

    @functools.partial(
        pl.kernel, mesh=mesh,
        out_type=jax.ShapeDtypeStruct((cap, width), rows.dtype),
        scratch_types=[pltpu.VMEM((picks, r), I32),
                       pltpu.VMEM((r, width), rows.dtype),
                       pltpu.SemaphoreType.DMA])
    def scatter(rows_hbm, dest_hbm, out_hbm, idx_v, rows_v, sem):
        worker = lax.axis_index("s") * SC_CORES + lax.axis_index("c")

        @pl.loop(0, per_worker)
        def _(step):
            chunk = worker * per_worker + step
            base = pl.multiple_of(chunk * r, r)
            pltpu.sync_copy(rows_hbm.at[pl.ds(base, r)], rows_v)
            pltpu.sync_copy(dest_hbm.at[chunk], idx_v)
            for k in range(picks):
                pltpu.async_copy(rows_v, out_hbm.at[idx_v.at[k]], sem).wait()

    return scatter(rows, dest3)


def _pack_words(v):
    k = v.shape[1] // 2
    hi = pltpu.bitcast(v[:, :k].astype(BF16).astype(F32), I32)
    lo = pltpu.bitcast(v[:, k:].astype(BF16).astype(F32), I32)
    return jnp.bitwise_or(hi, lax.shift_right_logical(lo, jnp.int32(16)))


def _unpack_words(w):
    hi = pltpu.bitcast(jnp.bitwise_and(w, jnp.int32(-65536)), F32)
    lo = pltpu.bitcast(jnp.left_shift(w, 16), F32)
    return jnp.concatenate([hi, lo], axis=1).astype(BF16)


def _expert_kernel(exp_ref, blk_ref, new_ref, x_ref, wup_ref, bup_ref, wdn_ref, bdn_ref, o_ref,
                   wup_s, wdn_s, *, d_ff):
    del exp_ref, blk_ref

    @pl.when(new_ref[pl.program_id(0)] == 1)
    def _():
        wup_s[...] = wup_ref[0].astype(BF16)
        wdn_s[...] = wdn_ref[0].astype(BF16)

    half = MOE_ROWS // 2
    for r in range(2):
        rows = slice(r * half, (r + 1) * half)
        h = _dot(_unpack_words(x_ref[rows, :]), wup_s[...]) + bup_ref[0]
        glu = jnp.minimum(h[:, :d_ff], SWIGLU_LIMIT)
        lin = jnp.clip(h[:, d_ff:], -SWIGLU_LIMIT, SWIGLU_LIMIT)
        a = glu * _sigmoid(SWIGLU_ALPHA * glu) * (lin + 1.0)
        o_ref[rows, :] = _pack_words(_dot(a.astype(BF16), wdn_s[...]) + bdn_ref[0])


def _experts(blk_expert, blk_index, blk_new, xs, wup, bup, wdn, bdn):
    cap, words = xs.shape
    d = 2 * words
    d_ff = wdn.shape[1]
    row_map = lambda i, exp, blk, new: (blk[i], 0)
    exp_map3 = lambda i, exp, blk, new: (exp[i], 0, 0)
    grid_spec = pltpu.PrefetchScalarGridSpec(
        num_scalar_prefetch=3,
        grid=(cap // MOE_ROWS,),
        in_specs=[pl.BlockSpec((MOE_ROWS, words), row_map),
                  pl.BlockSpec((1, d, 2 * d_ff), exp_map3),
                  pl.BlockSpec((1, 1, 2 * d_ff), exp_map3),
                  pl.BlockSpec((1, d_ff, d), exp_map3),
                  pl.BlockSpec((1, 1, d), exp_map3)],
        out_specs=pl.BlockSpec((MOE_ROWS, words), row_map),
        scratch_shapes=[pltpu.VMEM((d, 2 * d_ff), BF16), pltpu.VMEM((d_ff, d), BF16)],
    )
    return pl.pallas_call(
        functools.partial(_expert_kernel, d_ff=d_ff),
        grid_spec=grid_spec,
        out_shape=jax.ShapeDtypeStruct((cap, words), I32),
        compiler_params=pltpu.CompilerParams(dimension_semantics=("arbitrary",),
                                             vmem_limit_bytes=VMEM_LIMIT),
        name="experts",
    )(blk_expert, blk_index, blk_new, xs, wup, bup, wdn, bdn)


def _route_tables(route, totals, *, cap):
    n = route.shape[0]
    expert = route[:, 0:TOP_K].astype(I32)
    rank = route[:, TOP_K:2 * TOP_K].astype(I32)
    counts = totals[0, :N_EXPERTS].astype(I32)
    padded = (counts + MOE_ROWS - 1) // MOE_ROWS * MOE_ROWS
    pad_end = jnp.cumsum(padded)
    pad_start = pad_end - padded
    dest = pad_start[expert] + rank
    dest_pick_major = dest.T.reshape(-1)
    dest_chunks = dest.reshape(n // SC_SCATTER_ROWS, SC_SCATTER_ROWS, TOP_K).transpose(0, 2, 1)
    n_blk = cap // MOE_ROWS
    used = pad_end[-1] // MOE_ROWS
    blk = jnp.minimum(jnp.arange(n_blk, dtype=I32), used - 1)
    blk_expert = jnp.minimum(jnp.sum(pad_end[None, :] <= (blk * MOE_ROWS)[:, None], axis=1),
                             N_EXPERTS - 1).astype(I32)
    blk_new = jnp.concatenate([jnp.ones((1,), I32), (blk_expert[1:] != blk_expert[:-1]).astype(I32)])
    return dest_pick_major, dest_chunks, blk_expert, blk, blk_new


def _final_kernel(x1_ref, ya_ref, yb_ref, yc_ref, yd_ref, route_ref, p_ref, g1_ref, b1_ref,
                  wp_ref, wg_ref, g2_ref, b2_ref, o_ref, *, alpha):
    ffn = jnp.zeros(x1_ref.shape, F32)
    for k, yk_ref in enumerate((ya_ref, yb_ref, yc_ref, yd_ref)):
        gate_k = route_ref[:, 2 * TOP_K + k:2 * TOP_K + k + 1]
        ffn = ffn + gate_k * _unpack_words(yk_ref[...]).astype(F32)
    x2 = _layer_norm(alpha * x1_ref[...] + ffn, g1_ref[...], b1_ref[...])
    gate = _sigmoid(_dot(x2.astype(BF16), wg_ref[...]))
    ple = _dot(p_ref[...].astype(BF16), wp_ref[...]) * gate
    o_ref[...] = _layer_norm(alpha * x2 + ple, g2_ref[...], b2_ref[...])


def _final(x1, y4w, route, p2, g1, b1, wp, wg, g2, b2, *, alpha, tm):
    n, d = x1.shape
    assert TOP_K == 4
    rows = lambda w: pl.BlockSpec((tm, w), lambda i: (i, 0))
    full = lambda shape: pl.BlockSpec(shape, lambda i: (0,) * len(shape))
    pick = lambda k: pl.BlockSpec((tm, d // 2), lambda i: (k * (n // tm) + i, 0))
    return pl.pallas_call(
        functools.partial(_final_kernel, alpha=alpha),
        grid=(n // tm,),
        in_specs=[rows(d), pick(0), pick(1), pick(2), pick(3), rows(LANES), rows(p2.shape[1]),
                  full(g1.shape), full(b1.shape),
                  full(wp.shape), full(wg.shape), full(g2.shape), full(b2.shape)],
        out_specs=rows(d),
        out_shape=jax.ShapeDtypeStruct((n, d), F32),
        compiler_params=pltpu.CompilerParams(dimension_semantics=("arbitrary",),
                                             vmem_limit_bytes=VMEM_LIMIT),
        name="final",
    )(x1, y4w, y4w, y4w, y4w, route, p2, g1, b1, wp, wg, g2, b2)


def _pack_w_in(w_in):
    d = w_in.shape[0]
    sizes = (ATT_HEADS * ATT_HEAD_DIM, KV_LATENT, IDX_HEADS * IDX_DIM, IDX_DIM, IDX_HEADS,
             HG_HEADS * HG_DIM, HG_HEADS * HG_DIM, HG_HEADS * HG_DIM, HG_HEADS * HG_DIM)
    offs = [0]
    for s in sizes:
        offs.append(offs[-1] + s)
    sec = [w_in[:, offs[k]:offs[k + 1]] for k in range(len(sizes))]
    iq = sec[2].reshape(d, IDX_HEADS, IDX_DIM)
    iq = jnp.pad(iq, ((0, 0), (0, 0), (0, LANES - IDX_DIM))).reshape(d, IDX_HEADS * LANES)
    ikw = jnp.pad(jnp.concatenate([sec[3], sec[4]], axis=1), ((0, 0), (0, LANES - IDX_DIM - IDX_HEADS)))
    w_n = jnp.concatenate([sec[1], ikw, sec[5], sec[6], sec[7], sec[8]], axis=1).astype(BF16)
    w_t = jnp.concatenate([sec[0], iq, ikw, sec[1]], axis=1).T.astype(BF16)
    return w_n, w_t


def _block_diag_uk_t(w_uk):
    eye = jnp.eye(ATT_HEADS, dtype=w_uk.dtype)
    bd = jnp.einsum("rhd,hg->hrgd", w_uk, eye)
    return bd.reshape(ATT_HEADS * KV_LATENT, ATT_HEADS * ATT_HEAD_DIM).astype(BF16)


def _uv_t(w_uv):
    return jnp.transpose(w_uv, (1, 2, 0)).astype(BF16)


def _layer(x, p_l, w_in, kv_g, ik_g, ik_b, w_uk, w_uv, lb, hg_ng, w_o, ln_mix_g, ln_mix_b,
           w_router, b_router, w_up, b_up, w_down, b_down, ln_ffn_g, ln_ffn_b,
           w_ple_proj, w_ple_gate, ln_ple_g, ln_ple_b, *, alpha, tm, hg_tb):
    bsz, t, d = x.shape
    n = bsz * t
    x2 = x.reshape(n, d)
    row = lambda v: v.reshape(1, -1).astype(F32)
    pad_lane = lambda v: jnp.pad(row(v), ((0, 0), (0, LANES - v.shape[-1])))

    w_n, w_t = _pack_w_in(w_in)
    qt, iqt, iwt, c, ct, ik, hg = _inproj(x2, w_n, w_t, _block_diag_uk_t(w_uk), row(kv_g),
                                          kv_g.reshape(-1, 1).astype(F32), pad_lane(ik_g), pad_lane(ik_b), tm=tm)
    y_rec = _hgrn2(hg.reshape(bsz, t, -1), row(lb), row(hg_ng), tb=hg_tb)
    y_att = _dsa(qt, iqt, iwt, ik, c, ct, _uv_t(w_uv), bsz=bsz, t=t)

    wr_p = jnp.pad(w_router.astype(F32), ((0, 0), (0, LANES - N_EXPERTS)))
    x1, x1w, route, totals = _mix_router(
        x2, y_att, y_rec.reshape(n, -1), w_o.astype(BF16), row(ln_mix_g), row(ln_mix_b),
        wr_p, pad_lane(b_router), alpha=alpha, tm=tm)

    step = SC_WORKERS * SC_ROWS
    cap = -(-(n * TOP_K + N_EXPERTS * MOE_ROWS) // step) * step
    dest, dest_chunks, blk_expert, blk_index, blk_new = _route_tables(route, totals, cap=cap)
    d_ff = w_down.shape[1]
    xs = _sc_scatter_rows(x1w, dest_chunks, cap)
    outw = _experts(blk_expert, blk_index, blk_new, xs,
                    w_up.astype(F32), b_up.reshape(N_EXPERTS, 1, 2 * d_ff).astype(F32),
                    w_down.astype(F32), b_down.reshape(N_EXPERTS, 1, d).astype(F32))
    y4w = _sc_gather_rows(outw, dest)

    out = _final(x1, y4w, route, p_l.reshape(n, -1), row(ln_ffn_g), row(ln_ffn_b),
                 w_ple_proj.astype(BF16), w_ple_gate.astype(BF16), row(ln_ple_g), row(ln_ple_b),
                 alpha=alpha, tm=tm)
    return out.reshape(bsz, t, d)


def kernel(x, p, w_in, kv_norm_g, idx_k_norm_g, idx_k_norm_b, w_uk, w_uv, hg_lb_logits, hg_norm_g, w_o,
           ln_mix_g, ln_mix_b, w_router, b_router, w_up, b_up, w_down, b_down, ln_ffn_g, ln_ffn_b,
           w_ple_proj, w_ple_gate, ln_ple_g, ln_ple_b):
    depth = w_in.shape[0]
    alpha = (2.0 * depth) ** 0.25
    lower_bounds = jnp.cumsum(jax.nn.softmax(hg_lb_logits.astype(F32), axis=0), axis=0)
    n = x.shape[0] * x.shape[1]
    tm = min(512, n)
    hg_tb = min(512, x.shape[1])
    for l in range(depth):
        x = _layer(x, p[l], w_in[l], kv_norm_g[l], idx_k_norm_g[l], idx_k_norm_b[l], w_uk[l], w_uv[l],
                   lower_bounds[l], hg_norm_g[l], w_o[l], ln_mix_g[l], ln_mix_b[l], w_router[l],
                   b_router[l], w_up[l], b_up[l], w_down[l], b_down[l], ln_ffn_g[l], ln_ffn_b[l],
                   w_ple_proj[l], w_ple_gate[l], ln_ple_g[l], ln_ple_b[l],
                   alpha=alpha, tm=tm, hg_tb=hg_tb)
    return x
```

```python
import functools

import jax
import jax.numpy as jnp
from jax import lax
from jax.experimental import pallas as pl
from jax.experimental.pallas import tpu as pltpu
from jax.experimental.pallas import tpu_sc as plsc

F32 = jnp.float32
BF16 = jnp.bfloat16
I32 = jnp.int32

ATT_HEADS = 8
ATT_HEAD_DIM = 64
KV_LATENT = 256
IDX_HEADS = 8
IDX_DIM = 64
TOPK_MAX = 256
HG_HEADS = 4
HG_DIM = 128
HG_CHUNK = 32
N_EXPERTS = 32
TOP_K = 4
SWIGLU_LIMIT = 7.0
SWIGLU_ALPHA = 1.702
LN_EPS = 1e-5
RMS_EPS = 1e-6

LANES = 128
Q_ROWS = 128
KEY_CHUNK = 512
COUNT_ROWS = 128
MOE_ROWS = 512
SC_CORES = 2
SC_SUBCORES = 16
SC_WORKERS = SC_CORES * SC_SUBCORES
SC_ROWS = 128
SC_SCATTER_ROWS = 128
COMBINE_PARTS = 2
VMEM_LIMIT = 56 * 1024 * 1024

INT_MIN = -(2 ** 31)
NEG_BIG = -1e30
LOG2_E = 1.4426950408889634

_N_KV = 0
_N_IKW = _N_KV + KV_LATENT
_N_HG = _N_IKW + LANES
_N_END = _N_HG + 4 * HG_HEADS * HG_DIM
_T_AQ = 0
_T_IQ = _T_AQ + ATT_HEADS * ATT_HEAD_DIM
_T_IKW = _T_IQ + IDX_HEADS * LANES
_T_KV = _T_IKW + LANES
_T_END = _T_KV + KV_LATENT


def _dot(a, b):
    return jnp.dot(a, b, preferred_element_type=F32)


def _dot_nt(a, b):
    return lax.dot_general(a, b, (((1,), (1,)), ((), ())), preferred_element_type=F32)


def _dot_tn(a, b):
    return lax.dot_general(a, b, (((0,), (0,)), ((), ())), preferred_element_type=F32)


def _layer_norm(z, g, b):
    mu = jnp.mean(z, axis=-1, keepdims=True)
    d = z - mu
    var = jnp.mean(d * d, axis=-1, keepdims=True)
    return d * lax.rsqrt(var + LN_EPS) * g + b


def _sigmoid(x):
    return 1.0 / (1.0 + jnp.exp(-x))


def _inproj_kernel(x_ref, wn_ref, wt_ref, wukt_ref, kvg_ref, kvgc_ref, ikg_ref, ikb_ref,
                   qt_ref, iqt_ref, iwt_ref, c_ref, ct_ref, ik_ref, hg_ref):
    xb = x_ref[...].astype(BF16)

    aqt = _dot_nt(wt_ref[_T_AQ:_T_IQ, :], xb).astype(BF16)
    qt_ref[...] = (_dot(wukt_ref[...], aqt) * (LOG2_E * ATT_HEAD_DIM ** -0.5)).astype(BF16)
    iqt_ref[...] = (_dot_nt(wt_ref[_T_IQ:_T_IKW, :], xb) * (IDX_DIM ** -0.5)).astype(BF16)
    iwt_ref[...] = _dot_nt(wt_ref[_T_IKW:_T_KV, :], xb) * (IDX_HEADS ** -0.5)
    act = _dot_nt(wt_ref[_T_KV:_T_END, :], xb)
    ct = act * lax.rsqrt(jnp.mean(act * act, axis=0, keepdims=True) + RMS_EPS) * kvgc_ref[...]
    ct_ref[...] = ct.astype(BF16)

    ac = _dot(xb, wn_ref[:, _N_KV:_N_IKW])
    c = ac * lax.rsqrt(jnp.mean(ac * ac, axis=-1, keepdims=True) + RMS_EPS) * kvg_ref[...]
    c_ref[...] = c.astype(BF16)

    ikw = _dot(xb, wn_ref[:, _N_IKW:_N_HG])
    lane = lax.broadcasted_iota(I32, ikw.shape, 1)
    is_k = lane < IDX_DIM
    mu = jnp.sum(jnp.where(is_k, ikw, 0.0), axis=-1, keepdims=True) * (1.0 / IDX_DIM)
    d = jnp.where(is_k, ikw - mu, 0.0)
    var = jnp.sum(d * d, axis=-1, keepdims=True) * (1.0 / IDX_DIM)
    ik = d * lax.rsqrt(var + LN_EPS) * ikg_ref[...] + ikb_ref[...]
    ik_ref[...] = ik.astype(BF16)

    hg_ref[...] = _dot(xb, wn_ref[:, _N_HG:_N_END])


def _inproj(x2, w_n, w_t, wuk_t, kvg, kvg_col, ikg, ikb, *, tm):
    n, d = x2.shape
    grid = (n // tm,)
    full = lambda shape: pl.BlockSpec(shape, lambda i: (0,) * len(shape))
    rows = lambda w: pl.BlockSpec((tm, w), lambda i: (i, 0))
    cols = lambda h: pl.BlockSpec((h, tm), lambda i: (0, i))
    return pl.pallas_call(
        _inproj_kernel,
        grid=grid,
        in_specs=[rows(d), full(w_n.shape), full(w_t.shape), full(wuk_t.shape), full(kvg.shape),
                  full(kvg_col.shape), full(ikg.shape), full(ikb.shape)],
        out_specs=[cols(ATT_HEADS * KV_LATENT), cols(IDX_HEADS * LANES), cols(LANES),
                   rows(KV_LATENT), cols(KV_LATENT), rows(LANES), rows(4 * HG_HEADS * HG_DIM)],
        out_shape=[jax.ShapeDtypeStruct((ATT_HEADS * KV_LATENT, n), BF16),
                   jax.ShapeDtypeStruct((IDX_HEADS * LANES, n), BF16),
                   jax.ShapeDtypeStruct((LANES, n), F32),
                   jax.ShapeDtypeStruct((n, KV_LATENT), BF16),
                   jax.ShapeDtypeStruct((KV_LATENT, n), BF16),
                   jax.ShapeDtypeStruct((n, LANES), BF16),
                   jax.ShapeDtypeStruct((n, 4 * HG_HEADS * HG_DIM), F32)],
        compiler_params=pltpu.CompilerParams(dimension_semantics=("arbitrary",),
                                             vmem_limit_bytes=VMEM_LIMIT),
        name="inproj",
    )(x2, w_n, w_t, wuk_t, kvg, kvg_col, ikg, ikb)


def _hgrn2_kernel(hg_ref, lb_ref, ng_ref, o_ref, state_ref, *, tb):
    @pl.when(pl.program_id(1) == 0)
    def _():
        state_ref[...] = jnp.zeros_like(state_ref)

    nchunk = tb // HG_CHUNK
    width = HG_HEADS * HG_DIM
    row = lax.broadcasted_iota(I32, (tb, HG_DIM), 0)
    rin = jnp.bitwise_and(row, HG_CHUNK - 1)
    r2 = lax.broadcasted_iota(I32, (tb, tb), 0)
    c2 = lax.broadcasted_iota(I32, (tb, tb), 1)
    intra = jnp.logical_and(r2 // HG_CHUNK == c2 // HG_CHUNK, c2 <= r2)

    for h in range(HG_HEADS):
        sl = slice(h * HG_DIM, (h + 1) * HG_DIM)
        gq = hg_ref[0, :, h * HG_DIM:(h + 1) * HG_DIM]
        gf = hg_ref[0, :, width + h * HG_DIM:width + (h + 1) * HG_DIM]
        gi = hg_ref[0, :, 2 * width + h * HG_DIM:2 * width + (h + 1) * HG_DIM]
        gg = hg_ref[0, :, 3 * width + h * HG_DIM:3 * width + (h + 1) * HG_DIM]
        lb = lb_ref[:, sl]
        forget = lb + (1.0 - lb) * _sigmoid(gf)
        logf = jnp.log(forget)
        b = logf
        s = 1
        while s < HG_CHUNK:
            b = b + jnp.where(rin >= s, pltpu.roll(b, s, axis=0), 0.0)
            s *= 2
        tot = jnp.sum(logf.reshape(nchunk, HG_CHUNK, HG_DIM), axis=1, keepdims=True)
        bl = jnp.broadcast_to(tot, (nchunk, HG_CHUNK, HG_DIM)).reshape(tb, HG_DIM)
        kk = 1.0 - forget
        q_dec = (gq * _sigmoid(gq) * jnp.exp(b)).astype(BF16)
        k_inv = (kk * jnp.exp(-b)).astype(BF16)
        k_end = (kk * jnp.exp(bl - b)).astype(BF16)
        vb = gi.astype(BF16)
        scores = jnp.where(intra, _dot_nt(q_dec, k_inv), 0.0).astype(BF16)
        o = _dot(scores, vb)
        decay = jnp.exp(tot)
        chunk_rows = [slice(n * HG_CHUNK, (n + 1) * HG_CHUNK) for n in range(nchunk)]
        incs = [_dot_tn(vb[rs], k_end[rs]) for rs in chunk_rows]
        st = state_ref[h]
        starts = []
        for n in range(nchunk):
            starts.append(st.astype(BF16))
            st = st * decay[n] + incs[n]
        state_ref[h] = st
        inter = [_dot_nt(q_dec[rs], starts[n]) for n, rs in enumerate(chunk_rows)]
        o = o + jnp.concatenate(inter, axis=0)
        o = o * lax.rsqrt(jnp.mean(o * o, axis=-1, keepdims=True) + RMS_EPS)
        o = o * ng_ref[:, sl] * (gg * _sigmoid(gg))
        o_ref[0, :, sl] = o.astype(o_ref.dtype)


def _hgrn2(hg3, lb, ng, *, tb):
    bsz, t, w4 = hg3.shape
    width = HG_HEADS * HG_DIM
    return pl.pallas_call(
        functools.partial(_hgrn2_kernel, tb=tb),
        grid=(bsz, t // tb),
        in_specs=[pl.BlockSpec((1, tb, w4), lambda b, i: (b, i, 0)),
                  pl.BlockSpec((1, width), lambda b, i: (0, 0)),
                  pl.BlockSpec((1, width), lambda b, i: (0, 0))],
        out_specs=pl.BlockSpec((1, tb, width), lambda b, i: (b, i, 0)),
        out_shape=jax.ShapeDtypeStruct((bsz, t, width), BF16),
        scratch_shapes=[pltpu.VMEM((HG_HEADS, HG_DIM, HG_DIM), F32)],
        compiler_params=pltpu.CompilerParams(dimension_semantics=("arbitrary", "arbitrary"),
                                             vmem_limit_bytes=VMEM_LIMIT),
        name="hgrn2",
    )(hg3, lb, ng)


def _dsa_kernel(qt_ref, iqt_ref, iwt_ref, ik_ref, c_ref, ct_ref, wuvt_ref, o_ref,
                keys_ref, cut_ref, qall_ref, iqall_ref, m_ref, l_ref, acc_ref, *, topk, pos_bits):
    n = pl.program_id(1)
    t0 = n * Q_ROWS
    nch = (t0 + Q_ROWS + KEY_CHUNK - 1) // KEY_CHUNK
    kf = float(topk)
    key_i = lax.broadcasted_iota(I32, (KEY_CHUNK, Q_ROWS), 0)
    tq = t0 + lax.broadcasted_iota(I32, (KEY_CHUNK, Q_ROWS), 1)
    pair = 2 * Q_ROWS

    for h in range(ATT_HEADS):
        qall_ref[:, h * Q_ROWS:(h + 1) * Q_ROWS] = qt_ref[h * KV_LATENT:(h + 1) * KV_LATENT, :]
    for h in range(IDX_HEADS):
        iqall_ref[:, h * Q_ROWS:(h + 1) * Q_ROWS] = iqt_ref[h * LANES:(h + 1) * LANES, :]
    head_w = [iwt_ref[IDX_DIM + h:IDX_DIM + h + 1, :] for h in range(IDX_HEADS)]

    def index_body(j, carry):
        k0 = pl.multiple_of(j * KEY_CHUNK, KEY_CHUNK)
        ikc = ik_ref[pl.ds(k0, KEY_CHUNK), :]
        isc = jnp.zeros((KEY_CHUNK, Q_ROWS), F32)
        for hp in range(IDX_HEADS // 2):
            z = _dot(ikc, iqall_ref[:, hp * pair:(hp + 1) * pair])
            for hh in range(2):
                isc = isc + head_w[2 * hp + hh] * jnp.maximum(z[:, hh * Q_ROWS:(hh + 1) * Q_ROWS], 0.0)
        isc = jnp.where(isc == 0.0, 0.0, isc)
        bits = pltpu.bitcast(isc, I32)
        key = jnp.bitwise_xor(bits, jnp.bitwise_and(jnp.right_shift(bits, 31), 0x7FFFFFFF))
        key = jnp.where(k0 + key_i <= tq, key, INT_MIN)
        keys_ref[pl.ds(k0, KEY_CHUNK), :] = key
        return carry

    lax.fori_loop(0, nch, index_body, 0)

    def count(pred):
        def body(j, acc):
            k0 = pl.multiple_of(j * KEY_CHUNK, KEY_CHUNK)
            hit = jnp.where(pred(keys_ref[pl.ds(k0, KEY_CHUNK), :], k0), 1.0, 0.0)
            return acc + jnp.sum(hit.reshape(KEY_CHUNK // COUNT_ROWS, COUNT_ROWS, Q_ROWS), axis=0)
        acc = lax.fori_loop(0, nch, body, jnp.zeros((COUNT_ROWS, Q_ROWS), F32))
        return jnp.sum(acc, axis=0, keepdims=True)

    def count_ge(cand):
        return count(lambda k, k0: k >= cand)

    thr0 = jnp.where(count_ge(jnp.zeros((1, Q_ROWS), I32)) >= kf, 0, INT_MIN).astype(I32)

    def bit_body(i, thr):
        cand = thr + lax.shift_left(jnp.int32(1), 30 - i)
        return jnp.where(count_ge(cand) >= kf, cand, thr)

    thr = lax.fori_loop(0, 31, bit_body, thr0)
    live = thr > INT_MIN
    cut_ref[...] = jnp.broadcast_to(jnp.where(live, jnp.int32(2 ** 30), -1), cut_ref.shape)
    n_ge = count_ge(thr)
    tie = jnp.max(jnp.where(jnp.logical_and(live, n_ge > kf), 1.0, 0.0))

    @pl.when(tie > 0.0)
    def _():
        need = kf - count(lambda k, k0: k > thr)

        def cut_body(i, cut):
            cand = cut + lax.shift_left(jnp.int32(1), pos_bits - 1 - i)
            below = count(lambda k, k0: jnp.logical_and(k == thr, k0 + key_i < cand))
            return jnp.where(below < need, cand, cut)

        cut = lax.fori_loop(0, pos_bits, cut_body, jnp.zeros((1, Q_ROWS), I32))
        cut_ref[...] = jnp.broadcast_to(jnp.where(live, cut, -1), cut_ref.shape)

    m_ref[...] = jnp.full(m_ref.shape, NEG_BIG, F32)
    l_ref[...] = jnp.zeros(l_ref.shape, F32)
    acc_ref[...] = jnp.zeros(acc_ref.shape, F32)
    cut_b = cut_ref[0:1, :]

    def attn_body(j, carry):
        k0 = pl.multiple_of(j * KEY_CHUNK, KEY_CHUNK)
        key = keys_ref[pl.ds(k0, KEY_CHUNK), :]
        pos = k0 + key_i
        sel = jnp.logical_or(key > thr, jnp.logical_and(key == thr, pos <= cut_b))
        rel = jnp.where(sel, (pos - t0).astype(F32), NEG_BIG)
        cc = c_ref[pl.ds(k0, KEY_CHUNK), :]
        cct = ct_ref[:, pl.ds(k0, KEY_CHUNK)]
        for hp in range(ATT_HEADS // 2):
            cs = slice(hp * pair, (hp + 1) * pair)
            s2 = _dot(cc, qall_ref[:, cs])
            ps, alphas = [], []
            for hh in range(2):
                h = 2 * hp + hh
                slope = LOG2_E * 2.0 ** (-(8.0 / ATT_HEADS) * (h + 1))
                s = s2[:, hh * Q_ROWS:(hh + 1) * Q_ROWS] + slope * rel
                m_old = m_ref[h:h + 1, :]
                m_new = jnp.maximum(m_old, jnp.max(s, axis=0, keepdims=True))
                alpha = jnp.exp2(m_old - m_new)
                p = jnp.exp2(s - m_new)
                l_ref[h:h + 1, :] = alpha * l_ref[h:h + 1, :] + jnp.sum(p, axis=0, keepdims=True)
                m_ref[h:h + 1, :] = m_new
                ps.append(p.astype(BF16))
                alphas.append(alpha)
            acc_ref[:, cs] = (acc_ref[:, cs] * jnp.concatenate(alphas, axis=1)
                              + _dot(cct, jnp.concatenate(ps, axis=1)))
        return carry

    lax.fori_loop(0, nch, attn_body, 0)

    yt = []
    for h in range(ATT_HEADS):
        o_lat = (acc_ref[:, h * Q_ROWS:(h + 1) * Q_ROWS] / l_ref[h:h + 1, :]).astype(BF16)
        yt.append(_dot(wuvt_ref[h], o_lat))
    o_ref[...] = jnp.concatenate(yt, axis=0).T.astype(o_ref.dtype)


def _dsa(qt, iqt, iwt, ik, c, ct, wuvt_p, *, bsz, t):
    n = bsz * t
    nq = t // Q_ROWS
    topk = min(TOPK_MAX, t // 4)
    assert t % KEY_CHUNK == 0
    width = ATT_HEADS * ATT_HEAD_DIM
    qcols = lambda h: pl.BlockSpec((h, Q_ROWS), lambda b, i: (0, b * nq + i))
    return pl.pallas_call(
        functools.partial(_dsa_kernel, topk=topk, pos_bits=(t - 1).bit_length()),
        grid=(bsz, nq),
        in_specs=[qcols(ATT_HEADS * KV_LATENT), qcols(IDX_HEADS * LANES), qcols(LANES),
                  pl.BlockSpec((t, LANES), lambda b, i: (b, 0)),
                  pl.BlockSpec((t, KV_LATENT), lambda b, i: (b, 0)),
                  pl.BlockSpec((KV_LATENT, t), lambda b, i: (0, b)),
                  pl.BlockSpec(wuvt_p.shape, lambda b, i: (0, 0, 0))],
        out_specs=pl.BlockSpec((Q_ROWS, width), lambda b, i: (b * nq + i, 0)),
        out_shape=jax.ShapeDtypeStruct((n, width), BF16),
        scratch_shapes=[pltpu.VMEM((t, Q_ROWS), I32),
                        pltpu.VMEM((8, Q_ROWS), I32),
                        pltpu.VMEM((KV_LATENT, ATT_HEADS * Q_ROWS), BF16),
                        pltpu.VMEM((LANES, IDX_HEADS * Q_ROWS), BF16),
                        pltpu.VMEM((8, Q_ROWS), F32),
                        pltpu.VMEM((8, Q_ROWS), F32),
                        pltpu.VMEM((KV_LATENT, ATT_HEADS * Q_ROWS), F32)],
        compiler_params=pltpu.CompilerParams(dimension_semantics=("arbitrary", "arbitrary"),
                                             vmem_limit_bytes=VMEM_LIMIT),
        name="dsa",
    )(qt, iqt, iwt, ik, c, ct, wuvt_p)


def _mix_router_kernel(x_ref, ya_ref, yr_ref, wo_ref, g_ref, b_ref, wr_ref, br_ref,
                       x1_ref, x1w_ref, route_ref, tot_ref, carry_ref, *, alpha, tm):
    @pl.when(pl.program_id(0) == 0)
    def _():
        carry_ref[...] = jnp.zeros_like(carry_ref)

    half = ya_ref.shape[1]
    tr = tm
    lane = lax.broadcasted_iota(I32, (tr, LANES), 1)
    lane_f = lane.astype(F32)
    r2 = lax.broadcasted_iota(I32, (tr, tr), 0)
    c2 = lax.broadcasted_iota(I32, (tr, tr), 1)
    before = jnp.where(c2 < r2, 1.0, 0.0).astype(BF16)
    carry = carry_ref[...]

    for part in range(tm // tr):
        rows = slice(part * tr, (part + 1) * tr)
        mix = _dot(ya_ref[rows, :], wo_ref[0:half, :]) + _dot(yr_ref[rows, :], wo_ref[half:, :])
        x1 = _layer_norm(alpha * x_ref[rows, :] + mix, g_ref[...], b_ref[...])
        x1_ref[rows, :] = x1
        x1w_ref[rows, :] = _pack_words(x1)

        logits = jnp.dot(x1, wr_ref[...], preferred_element_type=F32,
                         precision=lax.Precision.HIGHEST) + br_ref[...]
        work = jnp.where(lane < N_EXPERTS, logits, -jnp.inf)
        sel_f = jnp.zeros(logits.shape, F32)
        denom = jnp.zeros((tr, 1), F32)
        hits, experts, weights = [], [], []
        for _ in range(TOP_K):
            mx = jnp.max(work, axis=-1, keepdims=True)
            first = jnp.min(jnp.where(work == mx, lane_f, float(LANES)), axis=-1, keepdims=True)
            hit = lane_f == first
            e = jnp.exp(mx - (weights[0][1] if weights else mx))
            hits.append(hit)
            experts.append(first)
            weights.append((e, mx))
            denom = denom + e
            sel_f = sel_f + jnp.where(hit, 1.0, 0.0)
            work = jnp.where(hit, -jnp.inf, work)

        rank = _dot(before, sel_f.astype(BF16)) + carry
        carry = carry + jnp.sum(sel_f, axis=0, keepdims=True)

        route = jnp.zeros(logits.shape, F32)
        for k in range(TOP_K):
            rank_k = jnp.sum(jnp.where(hits[k], rank, 0.0), axis=-1, keepdims=True)
            route = jnp.where(lane == k, experts[k], route)
            route = jnp.where(lane == TOP_K + k, rank_k, route)
            route = jnp.where(lane == 2 * TOP_K + k, weights[k][0] / denom, route)
        route_ref[rows, :] = route

    carry_ref[...] = carry
    tot_ref[...] = jnp.broadcast_to(carry, tot_ref.shape)


def _mix_router(x2, ya, yr, wo, g, b, wr_p, br_p, *, alpha, tm):
    n, d = x2.shape
    half = ya.shape[1]
    rows = lambda w: pl.BlockSpec((tm, w), lambda i: (i, 0))
    full = lambda shape: pl.BlockSpec(shape, lambda i: (0,) * len(shape))
    return pl.pallas_call(
        functools.partial(_mix_router_kernel, alpha=alpha, tm=tm),
        grid=(n // tm,),
        in_specs=[rows(d), rows(half), rows(half), full(wo.shape), full(g.shape), full(b.shape),
                  full(wr_p.shape), full(br_p.shape)],
        out_specs=[rows(d), rows(d // 2), rows(LANES), full((8, LANES))],
        out_shape=[jax.ShapeDtypeStruct((n, d), F32),
                   jax.ShapeDtypeStruct((n, d // 2), I32),
                   jax.ShapeDtypeStruct((n, LANES), F32),
                   jax.ShapeDtypeStruct((8, LANES), F32)],
        scratch_shapes=[pltpu.VMEM((1, LANES), F32)],
        compiler_params=pltpu.CompilerParams(dimension_semantics=("arbitrary",),
                                             vmem_limit_bytes=VMEM_LIMIT),
        name="mix_router",
    )(x2, ya, yr, wo, g, b, wr_p, br_p)


def _sc_gather_rows(table, idx):
    n_idx = idx.shape[0]
    width = table.shape[1]
    per_worker = n_idx // SC_WORKERS
    assert per_worker * SC_WORKERS == n_idx and per_worker % SC_ROWS == 0
    mesh = plsc.VectorSubcoreMesh(core_axis_name="c", subcore_axis_name="s",
                                  num_cores=SC_CORES, num_subcores=SC_SUBCORES)

    @functools.partial(
        pl.kernel, mesh=mesh,
        out_type=jax.ShapeDtypeStruct((n_idx, width), table.dtype),
        scratch_types=[pltpu.VMEM((SC_ROWS,), I32),
                       pltpu.VMEM((SC_ROWS, width), table.dtype),
                       pltpu.SemaphoreType.DMA])
    def gather(table_hbm, idx_hbm, out_hbm, idx_v, rows_v, sem):
        worker = lax.axis_index("s") * SC_CORES + lax.axis_index("c")

        @pl.loop(0, per_worker // SC_ROWS)
        def _(step):
            base = pl.multiple_of(worker * per_worker + step * SC_ROWS, SC_ROWS)
            pltpu.sync_copy(idx_hbm.at[pl.ds(base, SC_ROWS)], idx_v)
            pltpu.async_copy(table_hbm.at[idx_v], rows_v, sem).wait()
            pltpu.sync_copy(rows_v, out_hbm.at[pl.ds(base, SC_ROWS)])

    return gather(table, idx)


def _sc_scatter_rows(rows, dest3, cap):
    n, width = rows.shape
    chunks, picks, r = dest3.shape
    assert r == SC_SCATTER_ROWS and chunks * r == n and chunks % SC_WORKERS == 0
    per_worker = chunks // SC_WORKERS
    mesh = plsc.VectorSubcoreMesh(core_axis_name="c", subcore_axis_name="s",
                                  num_cores=SC_CORES, num_subcores=SC_SUBCORES)

    @functools.partial(
        pl.kernel, mesh=mesh,
        out_type=jax.ShapeDtypeStruct((cap, width), rows.dtype),
        scratch_types=[pltpu.VMEM((picks, r), I32),
                       pltpu.VMEM((r, width), rows.dtype),
                       pltpu.SemaphoreType.DMA])
    def scatter(rows_hbm, dest_hbm, out_hbm, idx_v, rows_v, sem):
        worker = lax.axis_index("s") * SC_CORES + lax.axis_index("c")

        @pl.loop(0, per_worker)
        def _(step):
            chunk = worker * per_worker + step
            base = pl.multiple_of(chunk * r, r)
            pltpu.sync_copy(rows_hbm.at[pl.ds(base, r)], rows_v)
            pltpu.sync_copy(dest_hbm.at[chunk], idx_v)
            for k in range(picks):
                pltpu.async_copy(rows_v, out_hbm.at[idx_v.at[k]], sem).wait()

    return scatter(rows, dest3)


def _pack_words(v):
    k = v.shape[1] // 2
    hi = pltpu.bitcast(v[:, :k].astype(BF16).astype(F32), I32)
    lo = pltpu.bitcast(v[:, k:].astype(BF16).astype(F32), I32)
    return jnp.bitwise_or(hi, lax.shift_right_logical(lo, jnp.int32(16)))


def _unpack_words(w, dtype=BF16):
    hi = pltpu.bitcast(jnp.bitwise_and(w, jnp.int32(-65536)), F32)
    lo = pltpu.bitcast(jnp.left_shift(w, 16), F32)
    return jnp.concatenate([hi, lo], axis=1).astype(dtype)


def _expert_kernel(exp_ref, blk_ref, new_ref, x_ref, wup_ref, bup_ref, wdn_ref, bdn_ref, o_ref,
                   wup_s, wdn_s, *, d_ff):
    del exp_ref, blk_ref

    @pl.when(new_ref[pl.program_id(0)] == 1)
    def _():
        wup_s[...] = wup_ref[0].astype(BF16)
        wdn_s[...] = wdn_ref[0].astype(BF16)

    half = MOE_ROWS // 2
    for r in range(2):
        rows = slice(r * half, (r + 1) * half)
        h = _dot(_unpack_words(x_ref[rows, :]), wup_s[...]) + bup_ref[0]
        glu = jnp.minimum(h[:, :d_ff], SWIGLU_LIMIT)
        lin = jnp.clip(h[:, d_ff:], -SWIGLU_LIMIT, SWIGLU_LIMIT)
        a = glu * _sigmoid(SWIGLU_ALPHA * glu) * (lin + 1.0)
        o_ref[rows, :] = _pack_words(_dot(a.astype(BF16), wdn_s[...]) + bdn_ref[0])


def _experts(blk_expert, blk_index, blk_new, xs, wup, bup, wdn, bdn):
    cap, words = xs.shape
    d = 2 * words
    d_ff = wdn.shape[1]
    row_map = lambda i, exp, blk, new: (blk[i], 0)
    exp_map3 = lambda i, exp, blk, new: (exp[i], 0, 0)
    grid_spec = pltpu.PrefetchScalarGridSpec(
        num_scalar_prefetch=3,
        grid=(cap // MOE_ROWS,),
        in_specs=[pl.BlockSpec((MOE_ROWS, words), row_map),
                  pl.BlockSpec((1, d, 2 * d_ff), exp_map3),
                  pl.BlockSpec((1, 1, 2 * d_ff), exp_map3),
                  pl.BlockSpec((1, d_ff, d), exp_map3),
                  pl.BlockSpec((1, 1, d), exp_map3)],
        out_specs=pl.BlockSpec((MOE_ROWS, words), row_map),
        scratch_shapes=[pltpu.VMEM((d, 2 * d_ff), BF16), pltpu.VMEM((d_ff, d), BF16)],
    )
    return pl.pallas_call(
        functools.partial(_expert_kernel, d_ff=d_ff),
        grid_spec=grid_spec,
        out_shape=jax.ShapeDtypeStruct((cap, words), I32),
        compiler_params=pltpu.CompilerParams(dimension_semantics=("arbitrary",),
                                             vmem_limit_bytes=VMEM_LIMIT),
        name="experts",
    )(blk_expert, blk_index, blk_new, xs, wup, bup, wdn, bdn)


def _route_tables(route, totals, *, cap):
    n = route.shape[0]
    expert = route[:, 0:TOP_K].astype(I32)
    rank = route[:, TOP_K:2 * TOP_K].astype(I32)
    counts = totals[0, :N_EXPERTS].astype(I32)
    padded = (counts + MOE_ROWS - 1) // MOE_ROWS * MOE_ROWS
    pad_end = jnp.cumsum(padded)
    pad_start = pad_end - padded
    dest = pad_start[expert] + rank
    dest_chunks = dest.reshape(n // SC_SCATTER_ROWS, SC_SCATTER_ROWS, TOP_K).transpose(0, 2, 1)
    n_blk = cap // MOE_ROWS
    used = pad_end[-1] // MOE_ROWS
    blk = jnp.minimum(jnp.arange(n_blk, dtype=I32), used - 1)
    blk_expert = jnp.minimum(jnp.sum(pad_end[None, :] <= (blk * MOE_ROWS)[:, None], axis=1),
                             N_EXPERTS - 1).astype(I32)
    blk_new = jnp.concatenate([jnp.ones((1,), I32), (blk_expert[1:] != blk_expert[:-1]).astype(I32)])
    return dest, dest_chunks, blk_expert, blk, blk_new


def _final_kernel(x1_ref, ya_ref, yb_ref, yc_ref, yd_ref, route_ref, p_ref, g1_ref, b1_ref,
                  wp_ref, wg_ref, g2_ref, b2_ref, *rest, alpha):
    o_ref = rest[-1]
    ffn = jnp.zeros(x1_ref.shape, F32)
    for k, yk_ref in enumerate((ya_ref, yb_ref, yc_ref, yd_ref)):
        gate_k = route_ref[:, 2 * TOP_K + k:2 * TOP_K + k + 1]
        ffn = ffn + gate_k * _unpack_words(yk_ref[...], F32)
    x2 = _layer_norm(alpha * x1_ref[...] + ffn, g1_ref[...], b1_ref[...])
    gate = _sigmoid(_dot(x2.astype(BF16), wg_ref[...]))
    ple = _dot(p_ref[...].astype(BF16), wp_ref[...]) * gate
    o_ref[...] = _layer_norm(alpha * x2 + ple, g2_ref[...], b2_ref[...])


def _final(x1, y4w, route, p2, g1, b1, wp, wg, g2, b2, *, alpha, tm, part, parts, prev=None):
    n, d = x1.shape
    assert TOP_K == 4
    steps = n // parts // tm
    off = part * steps
    rows = lambda w: pl.BlockSpec((tm, w), lambda i: (off + i, 0))
    full = lambda shape: pl.BlockSpec(shape, lambda i: (0,) * len(shape))
    pick = lambda k: pl.BlockSpec((tm, d // 2), lambda i: (k * steps + i, 0))
    in_specs = [rows(d), pick(0), pick(1), pick(2), pick(3), rows(LANES), rows(p2.shape[1]),
                full(g1.shape), full(b1.shape),
                full(wp.shape), full(wg.shape), full(g2.shape), full(b2.shape)]
    args = [x1, y4w, y4w, y4w, y4w, route, p2, g1, b1, wp, wg, g2, b2]
    aliases = {}
    if prev is not None:
        in_specs.append(pl.BlockSpec(memory_space=pl.ANY))
        args.append(prev)
        aliases = {len(args) - 1: 0}
    return pl.pallas_call(
        functools.partial(_final_kernel, alpha=alpha),
        grid=(steps,),
        in_specs=in_specs,
        out_specs=rows(d),
        out_shape=jax.ShapeDtypeStruct((n, d), F32),
        input_output_aliases=aliases,
        compiler_params=pltpu.CompilerParams(dimension_semantics=("arbitrary",),
                                             vmem_limit_bytes=VMEM_LIMIT),
        name="final",
    )(*args)


def _pack_w_in(w_in):
    d = w_in.shape[0]
    sizes = (ATT_HEADS * ATT_HEAD_DIM, KV_LATENT, IDX_HEADS * IDX_DIM, IDX_DIM, IDX_HEADS,
             HG_HEADS * HG_DIM, HG_HEADS * HG_DIM, HG_HEADS * HG_DIM, HG_HEADS * HG_DIM)
    offs = [0]
    for s in sizes:
        offs.append(offs[-1] + s)
    sec = [w_in[:, offs[k]:offs[k + 1]] for k in range(len(sizes))]
    iq = sec[2].reshape(d, IDX_HEADS, IDX_DIM)
    iq = jnp.pad(iq, ((0, 0), (0, 0), (0, LANES - IDX_DIM))).reshape(d, IDX_HEADS * LANES)
    ikw = jnp.pad(jnp.concatenate([sec[3], sec[4]], axis=1), ((0, 0), (0, LANES - IDX_DIM - IDX_HEADS)))
    w_n = jnp.concatenate([sec[1], ikw, sec[5], sec[6], sec[7], sec[8]], axis=1).astype(BF16)
    w_t = jnp.concatenate([sec[0], iq, ikw, sec[1]], axis=1).T.astype(BF16)
    return w_n, w_t


def _block_diag_uk_t(w_uk):
    eye = jnp.eye(ATT_HEADS, dtype=w_uk.dtype)
    bd = jnp.einsum("rhd,hg->hrgd", w_uk, eye)
    return bd.reshape(ATT_HEADS * KV_LATENT, ATT_HEADS * ATT_HEAD_DIM).astype(BF16)


def _uv_t(w_uv):
    return jnp.transpose(w_uv, (1, 2, 0)).astype(BF16)


def _layer(x, p_l, w_in, kv_g, ik_g, ik_b, w_uk, w_uv, lb, hg_ng, w_o, ln_mix_g, ln_mix_b,
           w_router, b_router, w_up, b_up, w_down, b_down, ln_ffn_g, ln_ffn_b,
           w_ple_proj, w_ple_gate, ln_ple_g, ln_ple_b, *, alpha, tm, hg_tb):
    bsz, t, d = x.shape
    n = bsz * t
    x2 = x.reshape(n, d)
    row = lambda v: v.reshape(1, -1).astype(F32)
    pad_lane = lambda v: jnp.pad(row(v), ((0, 0), (0, LANES - v.shape[-1])))

    w_n, w_t = _pack_w_in(w_in)
    qt, iqt, iwt, c, ct, ik, hg = _inproj(x2, w_n, w_t, _block_diag_uk_t(w_uk), row(kv_g),
                                          kv_g.reshape(-1, 1).astype(F32), pad_lane(ik_g), pad_lane(ik_b), tm=tm)
    y_rec = _hgrn2(hg.reshape(bsz, t, -1), row(lb), row(hg_ng), tb=hg_tb)
    y_att = _dsa(qt, iqt, iwt, ik, c, ct, _uv_t(w_uv), bsz=bsz, t=t)

    wr_p = jnp.pad(w_router.astype(F32), ((0, 0), (0, LANES - N_EXPERTS)))
    x1, x1w, route, totals = _mix_router(
        x2, y_att, y_rec.reshape(n, -1), w_o.astype(BF16), row(ln_mix_g), row(ln_mix_b),
        wr_p, pad_lane(b_router), alpha=alpha, tm=tm)

    step = SC_WORKERS * SC_ROWS
    cap = -(-(n * TOP_K + N_EXPERTS * MOE_ROWS) // step) * step
    dest, dest_chunks, blk_expert, blk_index, blk_new = _route_tables(route, totals, cap=cap)
    d_ff = w_down.shape[1]
    xs = _sc_scatter_rows(x1w, dest_chunks, cap)
    outw = _experts(blk_expert, blk_index, blk_new, xs,
                    w_up.astype(F32), b_up.reshape(N_EXPERTS, 1, 2 * d_ff).astype(F32),
                    w_down.astype(F32), b_down.reshape(N_EXPERTS, 1, d).astype(F32))

    parts = COMBINE_PARTS if n % (COMBINE_PARTS * max(tm, SC_WORKERS * SC_ROWS // TOP_K)) == 0 else 1
    out = None
    for part in range(parts):
        rows_p = slice(part * (n // parts), (part + 1) * (n // parts))
        y4w = _sc_gather_rows(outw, dest[rows_p].T.reshape(-1))
        out = _final(x1, y4w, route, p_l.reshape(n, -1), row(ln_ffn_g), row(ln_ffn_b),
                     w_ple_proj.astype(BF16), w_ple_gate.astype(BF16), row(ln_ple_g), row(ln_ple_b),
                     alpha=alpha, tm=tm, part=part, parts=parts, prev=out)
    return out.reshape(bsz, t, d)


def kernel(x, p, w_in, kv_norm_g, idx_k_norm_g, idx_k_norm_b, w_uk, w_uv, hg_lb_logits, hg_norm_g, w_o,
           ln_mix_g, ln_mix_b, w_router, b_router, w_up, b_up, w_down, b_down, ln_ffn_g, ln_ffn_b,
           w_ple_proj, w_ple_gate, ln_ple_g, ln_ple_b):
    depth = w_in.shape[0]
    alpha = (2.0 * depth) ** 0.25
    lower_bounds = jnp.cumsum(jax.nn.softmax(hg_lb_logits.astype(F32), axis=0), axis=0)
    n = x.shape[0] * x.shape[1]
    tm = min(512, n)
    hg_tb = min(512, x.shape[1])
    for l in range(depth):
        x = _layer(x, p[l], w_in[l], kv_norm_g[l], idx_k_norm_g[l], idx_k_norm_b[l], w_uk[l], w_uv[l],
                   lower_bounds[l], hg_norm_g[l], w_o[l], ln_mix_g[l], ln_mix_b[l], w_router[l],
                   b_router[l], w_up[l], b_up[l], w_down[l], b_down[l], ln_ffn_g[l], ln_ffn_b[l],
                   w_ple_proj[l], w_ple_gate[l], ln_ple_g[l], ln_ple_b[l],
                   alpha=alpha, tm=tm, hg_tb=hg_tb)
    return x
```

```python
import functools

import jax
import jax.numpy as jnp
from jax import lax
from jax.experimental import pallas as pl
from jax.experimental.pallas import tpu as pltpu
from jax.experimental.pallas import tpu_sc as plsc

F32 = jnp.float32
BF16 = jnp.bfloat16
I32 = jnp.int32

ATT_HEADS = 8
ATT_HEAD_DIM = 64
KV_LATENT = 256
IDX_HEADS = 8
IDX_DIM = 64
TOPK_MAX = 256
HG_HEADS = 4
HG_DIM = 128
HG_CHUNK = 32
N_EXPERTS = 32
TOP_K = 4
SWIGLU_LIMIT = 7.0
SWIGLU_ALPHA = 1.702
LN_EPS = 1e-5
RMS_EPS = 1e-6

LANES = 128
Q_ROWS = 128
KEY_CHUNK = 512
COUNT_ROWS = 128
MOE_ROWS = 512
SC_CORES = 2
SC_SUBCORES = 16
SC_WORKERS = SC_CORES * SC_SUBCORES
SC_ROWS = 128
SC_SCATTER_ROWS = 128
ROUTER_ROWS = 512
COMBINE_PARTS = 2
VMEM_LIMIT = 56 * 1024 * 1024

INT_MIN = -(2 ** 31)
NEG_BIG = -1e30
LOG2_E = 1.4426950408889634

_N_KV = 0
_N_IKW = _N_KV + KV_LATENT
_N_HG = _N_IKW + LANES
_N_END = _N_HG + 4 * HG_HEADS * HG_DIM
_T_AQ = 0
_T_IQ = _T_AQ + ATT_HEADS * ATT_HEAD_DIM
_T_IKW = _T_IQ + IDX_HEADS * LANES
_T_KV = _T_IKW + LANES
_T_END = _T_KV + KV_LATENT


def _dot(a, b):
    return jnp.dot(a, b, preferred_element_type=F32)


def _dot_nt(a, b):
    return lax.dot_general(a, b, (((1,), (1,)), ((), ())), preferred_element_type=F32)


def _dot_tn(a, b):
    return lax.dot_general(a, b, (((0,), (0,)), ((), ())), preferred_element_type=F32)


def _layer_norm(z, g, b):
    mu = jnp.mean(z, axis=-1, keepdims=True)
    d = z - mu
    var = jnp.mean(d * d, axis=-1, keepdims=True)
    return d * lax.rsqrt(var + LN_EPS) * g + b


def _sigmoid(x):
    return 1.0 / (1.0 + jnp.exp(-x))


def _inproj_kernel(x_ref, wn_ref, wt_ref, wukt_ref, kvg_ref, kvgc_ref, ikg_ref, ikb_ref,
                   qt_ref, iqt_ref, iwt_ref, c_ref, ct_ref, ik_ref, hg_ref):
    xb = x_ref[...].astype(BF16)

    aqt = _dot_nt(wt_ref[_T_AQ:_T_IQ, :], xb).astype(BF16)
    qt_ref[...] = (_dot(wukt_ref[...], aqt) * (LOG2_E * ATT_HEAD_DIM ** -0.5)).astype(BF16)
    iqt_ref[...] = (_dot_nt(wt_ref[_T_IQ:_T_IKW, :], xb) * (IDX_DIM ** -0.5)).astype(BF16)
    iwt_ref[...] = _dot_nt(wt_ref[_T_IKW:_T_KV, :], xb) * (IDX_HEADS ** -0.5)
    act = _dot_nt(wt_ref[_T_KV:_T_END, :], xb)
    ct = act * lax.rsqrt(jnp.mean(act * act, axis=0, keepdims=True) + RMS_EPS) * kvgc_ref[...]
    ct_ref[...] = ct.astype(BF16)

    ac = _dot(xb, wn_ref[:, _N_KV:_N_IKW])
    c = ac * lax.rsqrt(jnp.mean(ac * ac, axis=-1, keepdims=True) + RMS_EPS) * kvg_ref[...]
    c_ref[...] = c.astype(BF16)

    ikw = _dot(xb, wn_ref[:, _N_IKW:_N_HG])
    lane = lax.broadcasted_iota(I32, ikw.shape, 1)
    is_k = lane < IDX_DIM
    mu = jnp.sum(jnp.where(is_k, ikw, 0.0), axis=-1, keepdims=True) * (1.0 / IDX_DIM)
    d = jnp.where(is_k, ikw - mu, 0.0)
    var = jnp.sum(d * d, axis=-1, keepdims=True) * (1.0 / IDX_DIM)
    ik = d * lax.rsqrt(var + LN_EPS) * ikg_ref[...] + ikb_ref[...]
    ik_ref[...] = ik.astype(BF16)

    hg_ref[...] = _dot(xb, wn_ref[:, _N_HG:_N_END])


def _inproj(x2, w_n, w_t, wuk_t, kvg, kvg_col, ikg, ikb, *, tm):
    n, d = x2.shape
    grid = (n // tm,)
    full = lambda shape: pl.BlockSpec(shape, lambda i: (0,) * len(shape))
    rows = lambda w: pl.BlockSpec((tm, w), lambda i: (i, 0))
    cols = lambda h: pl.BlockSpec((h, tm), lambda i: (0, i))
    return pl.pallas_call(
        _inproj_kernel,
        grid=grid,
        in_specs=[rows(d), full(w_n.shape), full(w_t.shape), full(wuk_t.shape), full(kvg.shape),
                  full(kvg_col.shape), full(ikg.shape), full(ikb.shape)],
        out_specs=[cols(ATT_HEADS * KV_LATENT), cols(IDX_HEADS * LANES), cols(LANES),
                   rows(KV_LATENT), cols(KV_LATENT), rows(LANES), rows(4 * HG_HEADS * HG_DIM)],
        out_shape=[jax.ShapeDtypeStruct((ATT_HEADS * KV_LATENT, n), BF16),
                   jax.ShapeDtypeStruct((IDX_HEADS * LANES, n), BF16),
                   jax.ShapeDtypeStruct((LANES, n), F32),
                   jax.ShapeDtypeStruct((n, KV_LATENT), BF16),
                   jax.ShapeDtypeStruct((KV_LATENT, n), BF16),
                   jax.ShapeDtypeStruct((n, LANES), BF16),
                   jax.ShapeDtypeStruct((n, 4 * HG_HEADS * HG_DIM), F32)],
        compiler_params=pltpu.CompilerParams(dimension_semantics=("arbitrary",),
                                             vmem_limit_bytes=VMEM_LIMIT),
        name="inproj",
    )(x2, w_n, w_t, wuk_t, kvg, kvg_col, ikg, ikb)


def _hgrn2_kernel(hg_ref, lb_ref, ng_ref, o_ref, state_ref, *, tb):
    @pl.when(pl.program_id(1) == 0)
    def _():
        state_ref[...] = jnp.zeros_like(state_ref)

    nchunk = tb // HG_CHUNK
    width = HG_HEADS * HG_DIM
    row = lax.broadcasted_iota(I32, (tb, HG_DIM), 0)
    rin = jnp.bitwise_and(row, HG_CHUNK - 1)
    r2 = lax.broadcasted_iota(I32, (tb, tb), 0)
    c2 = lax.broadcasted_iota(I32, (tb, tb), 1)
    intra = jnp.logical_and(r2 // HG_CHUNK == c2 // HG_CHUNK, c2 <= r2)

    for h in range(HG_HEADS):
        sl = slice(h * HG_DIM, (h + 1) * HG_DIM)
        gq = hg_ref[0, :, h * HG_DIM:(h + 1) * HG_DIM]
        gf = hg_ref[0, :, width + h * HG_DIM:width + (h + 1) * HG_DIM]
        gi = hg_ref[0, :, 2 * width + h * HG_DIM:2 * width + (h + 1) * HG_DIM]
        gg = hg_ref[0, :, 3 * width + h * HG_DIM:3 * width + (h + 1) * HG_DIM]
        lb = lb_ref[:, sl]
        forget = lb + (1.0 - lb) * _sigmoid(gf)
        logf = jnp.log(forget)
        b = logf
        s = 1
        while s < HG_CHUNK:
            b = b + jnp.where(rin >= s, pltpu.roll(b, s, axis=0), 0.0)
            s *= 2
        tot = jnp.sum(logf.reshape(nchunk, HG_CHUNK, HG_DIM), axis=1, keepdims=True)
        bl = jnp.broadcast_to(tot, (nchunk, HG_CHUNK, HG_DIM)).reshape(tb, HG_DIM)
        kk = 1.0 - forget
        q_dec = (gq * _sigmoid(gq) * jnp.exp(b)).astype(BF16)
        k_inv = (kk * jnp.exp(-b)).astype(BF16)
        k_end = (kk * jnp.exp(bl - b)).astype(BF16)
        vb = gi.astype(BF16)
        scores = jnp.where(intra, _dot_nt(q_dec, k_inv), 0.0).astype(BF16)
        o = _dot(scores, vb)
        decay = jnp.exp(tot)
        chunk_rows = [slice(n * HG_CHUNK, (n + 1) * HG_CHUNK) for n in range(nchunk)]
        incs = [_dot_tn(vb[rs], k_end[rs]) for rs in chunk_rows]
        st = state_ref[h]
        starts = []
        for n in range(nchunk):
            starts.append(st.astype(BF16))
            st = st * decay[n] + incs[n]
        state_ref[h] = st
        inter = [_dot_nt(q_dec[rs], starts[n]) for n, rs in enumerate(chunk_rows)]
        o = o + jnp.concatenate(inter, axis=0)
        o = o * lax.rsqrt(jnp.mean(o * o, axis=-1, keepdims=True) + RMS_EPS)
        o = o * ng_ref[:, sl] * (gg * _sigmoid(gg))
        o_ref[0, :, sl] = o.astype(o_ref.dtype)


def _hgrn2(hg3, lb, ng, *, tb):
    bsz, t, w4 = hg3.shape
    width = HG_HEADS * HG_DIM
    return pl.pallas_call(
        functools.partial(_hgrn2_kernel, tb=tb),
        grid=(bsz, t // tb),
        in_specs=[pl.BlockSpec((1, tb, w4), lambda b, i: (b, i, 0)),
                  pl.BlockSpec((1, width), lambda b, i: (0, 0)),
                  pl.BlockSpec((1, width), lambda b, i: (0, 0))],
        out_specs=pl.BlockSpec((1, tb, width), lambda b, i: (b, i, 0)),
        out_shape=jax.ShapeDtypeStruct((bsz, t, width), BF16),
        scratch_shapes=[pltpu.VMEM((HG_HEADS, HG_DIM, HG_DIM), F32)],
        compiler_params=pltpu.CompilerParams(dimension_semantics=("arbitrary", "arbitrary"),
                                             vmem_limit_bytes=VMEM_LIMIT),
        name="hgrn2",
    )(hg3, lb, ng)


def _dsa_kernel(qt_ref, iqt_ref, iwt_ref, ik_ref, c_ref, ct_ref, wuvt_ref, o_ref,
                keys_ref, cut_ref, qall_ref, iqall_ref, m_ref, l_ref, acc_ref, *, topk, pos_bits):
    n = pl.program_id(1)
    t0 = n * Q_ROWS
    nch = (t0 + Q_ROWS + KEY_CHUNK - 1) // KEY_CHUNK
    kf = float(topk)
    key_i = lax.broadcasted_iota(I32, (KEY_CHUNK, Q_ROWS), 0)
    tq = t0 + lax.broadcasted_iota(I32, (KEY_CHUNK, Q_ROWS), 1)
    pair = 2 * Q_ROWS

    for h in range(ATT_HEADS):
        qall_ref[:, h * Q_ROWS:(h + 1) * Q_ROWS] = qt_ref[h * KV_LATENT:(h + 1) * KV_LATENT, :]
    for h in range(IDX_HEADS):
        iqall_ref[:, h * Q_ROWS:(h + 1) * Q_ROWS] = iqt_ref[h * LANES:(h + 1) * LANES, :]
    head_w = [iwt_ref[IDX_DIM + h:IDX_DIM + h + 1, :] for h in range(IDX_HEADS)]

    def index_body(j, carry):
        k0 = pl.multiple_of(j * KEY_CHUNK, KEY_CHUNK)
        ikc = ik_ref[pl.ds(k0, KEY_CHUNK), :]
        isc = jnp.zeros((KEY_CHUNK, Q_ROWS), F32)
        for hp in range(IDX_HEADS // 2):
            z = _dot(ikc, iqall_ref[:, hp * pair:(hp + 1) * pair])
            for hh in range(2):
                isc = isc + head_w[2 * hp + hh] * jnp.maximum(z[:, hh * Q_ROWS:(hh + 1) * Q_ROWS], 0.0)
        isc = jnp.where(isc == 0.0, 0.0, isc)
        bits = pltpu.bitcast(isc, I32)
        key = jnp.bitwise_xor(bits, jnp.bitwise_and(jnp.right_shift(bits, 31), 0x7FFFFFFF))
        key = jnp.where(k0 + key_i <= tq, key, INT_MIN)
        keys_ref[pl.ds(k0, KEY_CHUNK), :] = key
        return carry

    lax.fori_loop(0, nch, index_body, 0)

    def count(pred):
        def body(j, acc):
            k0 = pl.multiple_of(j * KEY_CHUNK, KEY_CHUNK)
            hit = jnp.where(pred(keys_ref[pl.ds(k0, KEY_CHUNK), :], k0), 1.0, 0.0)
            return acc + jnp.sum(hit.reshape(KEY_CHUNK // COUNT_ROWS, COUNT_ROWS, Q_ROWS), axis=0)
        acc = lax.fori_loop(0, nch, body, jnp.zeros((COUNT_ROWS, Q_ROWS), F32))
        return jnp.sum(acc, axis=0, keepdims=True)

    def count_ge(cand):
        return count(lambda k, k0: k >= cand)

    thr0 = jnp.where(count_ge(jnp.zeros((1, Q_ROWS), I32)) >= kf, 0, INT_MIN).astype(I32)

    def bit_body(i, thr):
        cand = thr + lax.shift_left(jnp.int32(1), 30 - i)
        return jnp.where(count_ge(cand) >= kf, cand, thr)

    thr = lax.fori_loop(0, 31, bit_body, thr0)
    live = thr > INT_MIN
    cut_ref[...] = jnp.broadcast_to(jnp.where(live, jnp.int32(2 ** 30), -1), cut_ref.shape)
    n_ge = count_ge(thr)
    tie = jnp.max(jnp.where(jnp.logical_and(live, n_ge > kf), 1.0, 0.0))

    @pl.when(tie > 0.0)
    def _():
        need = kf - count(lambda k, k0: k > thr)

        def cut_body(i, cut):
            cand = cut + lax.shift_left(jnp.int32(1), pos_bits - 1 - i)
            below = count(lambda k, k0: jnp.logical_and(k == thr, k0 + key_i < cand))
            return jnp.where(below < need, cand, cut)

        cut = lax.fori_loop(0, pos_bits, cut_body, jnp.zeros((1, Q_ROWS), I32))
        cut_ref[...] = jnp.broadcast_to(jnp.where(live, cut, -1), cut_ref.shape)

    m_ref[...] = jnp.full(m_ref.shape, NEG_BIG, F32)
    l_ref[...] = jnp.zeros(l_ref.shape, F32)
    acc_ref[...] = jnp.zeros(acc_ref.shape, F32)
    cut_b = cut_ref[0:1, :]

    def attn_body(j, carry):
        k0 = pl.multiple_of(j * KEY_CHUNK, KEY_CHUNK)
        key = keys_ref[pl.ds(k0, KEY_CHUNK), :]
        pos = k0 + key_i
        sel = jnp.logical_or(key > thr, jnp.logical_and(key == thr, pos <= cut_b))
        rel = jnp.where(sel, (pos - t0).astype(F32), NEG_BIG)
        cc = c_ref[pl.ds(k0, KEY_CHUNK), :]
        cct = ct_ref[:, pl.ds(k0, KEY_CHUNK)]
        for hp in range(ATT_HEADS // 2):
            cs = slice(hp * pair, (hp + 1) * pair)
            s2 = _dot(cc, qall_ref[:, cs])
            ps, alphas = [], []
            for hh in range(2):
                h = 2 * hp + hh
                slope = LOG2_E * 2.0 ** (-(8.0 / ATT_HEADS) * (h + 1))
                s = s2[:, hh * Q_ROWS:(hh + 1) * Q_ROWS] + slope * rel
                m_old = m_ref[h:h + 1, :]
                m_new = jnp.maximum(m_old, jnp.max(s, axis=0, keepdims=True))
                alpha = jnp.exp2(m_old - m_new)
                p = jnp.exp2(s - m_new)
                l_ref[h:h + 1, :] = alpha * l_ref[h:h + 1, :] + jnp.sum(p, axis=0, keepdims=True)
                m_ref[h:h + 1, :] = m_new
                ps.append(p.astype(BF16))
                alphas.append(alpha)
            acc_ref[:, cs] = (acc_ref[:, cs] * jnp.concatenate(alphas, axis=1)
                              + _dot(cct, jnp.concatenate(ps, axis=1)))
        return carry

    lax.fori_loop(0, nch, attn_body, 0)

    yt = []
    for h in range(ATT_HEADS):
        o_lat = (acc_ref[:, h * Q_ROWS:(h + 1) * Q_ROWS] / l_ref[h:h + 1, :]).astype(BF16)
        yt.append(_dot(wuvt_ref[h], o_lat))
    o_ref[...] = jnp.concatenate(yt, axis=0).T.astype(o_ref.dtype)


def _dsa(qt, iqt, iwt, ik, c, ct, wuvt_p, *, bsz, t):
    n = bsz * t
    nq = t // Q_ROWS
    topk = min(TOPK_MAX, t // 4)
    assert t % KEY_CHUNK == 0
    width = ATT_HEADS * ATT_HEAD_DIM
    qcols = lambda h: pl.BlockSpec((h, Q_ROWS), lambda b, i: (0, b * nq + i))
    return pl.pallas_call(
        functools.partial(_dsa_kernel, topk=topk, pos_bits=(t - 1).bit_length()),
        grid=(bsz, nq),
        in_specs=[qcols(ATT_HEADS * KV_LATENT), qcols(IDX_HEADS * LANES), qcols(LANES),
                  pl.BlockSpec((t, LANES), lambda b, i: (b, 0)),
                  pl.BlockSpec((t, KV_LATENT), lambda b, i: (b, 0)),
                  pl.BlockSpec((KV_LATENT, t), lambda b, i: (0, b)),
                  pl.BlockSpec(wuvt_p.shape, lambda b, i: (0, 0, 0))],
        out_specs=pl.BlockSpec((Q_ROWS, width), lambda b, i: (b * nq + i, 0)),
        out_shape=jax.ShapeDtypeStruct((n, width), BF16),
        scratch_shapes=[pltpu.VMEM((t, Q_ROWS), I32),
                        pltpu.VMEM((8, Q_ROWS), I32),
                        pltpu.VMEM((KV_LATENT, ATT_HEADS * Q_ROWS), BF16),
                        pltpu.VMEM((LANES, IDX_HEADS * Q_ROWS), BF16),
                        pltpu.VMEM((8, Q_ROWS), F32),
                        pltpu.VMEM((8, Q_ROWS), F32),
                        pltpu.VMEM((KV_LATENT, ATT_HEADS * Q_ROWS), F32)],
        compiler_params=pltpu.CompilerParams(dimension_semantics=("arbitrary", "arbitrary"),
                                             vmem_limit_bytes=VMEM_LIMIT),
        name="dsa",
    )(qt, iqt, iwt, ik, c, ct, wuvt_p)


def _mix_router_kernel(x_ref, ya_ref, yr_ref, wo_ref, g_ref, b_ref, wr_ref, br_ref,
                       x1_ref, x1w_ref, route_ref, tot_ref, carry_ref, *, alpha, tm):
    @pl.when(pl.program_id(0) == 0)
    def _():
        carry_ref[...] = jnp.zeros_like(carry_ref)

    half = ya_ref.shape[1]
    tr = min(tm, ROUTER_ROWS)
    lane = lax.broadcasted_iota(I32, (tr, LANES), 1)
    lane_f = lane.astype(F32)
    r2 = lax.broadcasted_iota(I32, (tr, tr), 0)
    c2 = lax.broadcasted_iota(I32, (tr, tr), 1)
    before = jnp.where(c2 < r2, 1.0, 0.0).astype(BF16)
    carry = carry_ref[...]

    for part in range(tm // tr):
        rows = slice(part * tr, (part + 1) * tr)
        mix = _dot(ya_ref[rows, :], wo_ref[0:half, :]) + _dot(yr_ref[rows, :], wo_ref[half:, :])
        x1 = _layer_norm(alpha * x_ref[rows, :] + mix, g_ref[...], b_ref[...])
        x1_ref[rows, :] = x1
        x1w_ref[rows, :] = _pack_words(x1)

        x_hi = x1.astype(BF16)
        x_lo = (x1 - x_hi.astype(F32)).astype(BF16)
        hh_hl = _dot(x_hi, wr_ref[...])
        logits = (hh_hl[:, :LANES] + hh_hl[:, LANES:] + _dot(x_lo, wr_ref[:, :LANES])) + br_ref[...]
        work = jnp.where(lane < N_EXPERTS, logits, -jnp.inf)
        sel_f = jnp.zeros(logits.shape, F32)
        denom = jnp.zeros((tr, 1), F32)
        hits, experts, weights = [], [], []
        for _ in range(TOP_K):
            mx = jnp.max(work, axis=-1, keepdims=True)
            first = jnp.min(jnp.where(work == mx, lane_f, float(LANES)), axis=-1, keepdims=True)
            hit = lane_f == first
            e = jnp.exp(mx - (weights[0][1] if weights else mx))
            hits.append(hit)
            experts.append(first)
            weights.append((e, mx))
            denom = denom + e
            sel_f = sel_f + jnp.where(hit, 1.0, 0.0)
            work = jnp.where(hit, -jnp.inf, work)

        rank = _dot(before, sel_f.astype(BF16)) + carry
        carry = carry + jnp.sum(sel_f, axis=0, keepdims=True)

        route = jnp.zeros(logits.shape, F32)
        for k in range(TOP_K):
            rank_k = jnp.sum(jnp.where(hits[k], rank, 0.0), axis=-1, keepdims=True)
            route = jnp.where(lane == k, experts[k], route)
            route = jnp.where(lane == TOP_K + k, rank_k, route)
            route = jnp.where(lane == 2 * TOP_K + k, weights[k][0] / denom, route)
        route_ref[rows, :] = route

    carry_ref[...] = carry
    tot_ref[...] = jnp.broadcast_to(carry, tot_ref.shape)


def _mix_router(x2, ya, yr, wo, g, b, wr_p, br_p, *, alpha, tm):
    n, d = x2.shape
    half = ya.shape[1]
    rows = lambda w: pl.BlockSpec((tm, w), lambda i: (i, 0))
    full = lambda shape: pl.BlockSpec(shape, lambda i: (0,) * len(shape))
    return pl.pallas_call(
        functools.partial(_mix_router_kernel, alpha=alpha, tm=tm),
        grid=(n // tm,),
        in_specs=[rows(d), rows(half), rows(half), full(wo.shape), full(g.shape), full(b.shape),
                  full(wr_p.shape), full(br_p.shape)],
        out_specs=[rows(d), rows(d // 2), rows(LANES), full((8, LANES))],
        out_shape=[jax.ShapeDtypeStruct((n, d), F32),
                   jax.ShapeDtypeStruct((n, d // 2), I32),
                   jax.ShapeDtypeStruct((n, LANES), F32),
                   jax.ShapeDtypeStruct((8, LANES), F32)],
        scratch_shapes=[pltpu.VMEM((1, LANES), F32)],
        compiler_params=pltpu.CompilerParams(dimension_semantics=("arbitrary",),
                                             vmem_limit_bytes=VMEM_LIMIT),
        name="mix_router",
    )(x2, ya, yr, wo, g, b, wr_p, br_p)


def _sc_gather_rows(table, idx):
    n_idx = idx.shape[0]
    width = table.shape[1]
    per_worker = n_idx // SC_WORKERS
    assert per_worker * SC_WORKERS == n_idx and per_worker % SC_ROWS == 0
    mesh = plsc.VectorSubcoreMesh(core_axis_name="c", subcore_axis_name="s",
                                  num_cores=SC_CORES, num_subcores=SC_SUBCORES)

    @functools.partial(
        pl.kernel, mesh=mesh,
        out_type=jax.ShapeDtypeStruct((n_idx, width), table.dtype),
        scratch_types=[pltpu.VMEM((SC_ROWS,), I32),
                       pltpu.VMEM((SC_ROWS, width), table.dtype),
                       pltpu.SemaphoreType.DMA])
    def gather(table_hbm, idx_hbm, out_hbm, idx_v, rows_v, sem):
        worker = lax.axis_index("s") * SC_CORES + lax.axis_index("c")

        @pl.loop(0, per_worker // SC_ROWS)
        def _(step):
            base = pl.multiple_of(worker * per_worker + step * SC_ROWS, SC_ROWS)
            pltpu.sync_copy(idx_hbm.at[pl.ds(base, SC_ROWS)], idx_v)
            pltpu.async_copy(table_hbm.at[idx_v], rows_v, sem).wait()
            pltpu.sync_copy(rows_v, out_hbm.at[pl.ds(base, SC_ROWS)])

    return gather(table, idx)


def _sc_scatter_rows(rows, dest3, cap):
    n, width = rows.shape
    chunks, picks, r = dest3.shape
    assert r == SC_SCATTER_ROWS and chunks * r == n and chunks % SC_WORKERS == 0
    per_worker = chunks // SC_WORKERS
    mesh = plsc.VectorSubcoreMesh(core_axis_name="c", subcore_axis_name="s",
                                  num_cores=SC_CORES, num_subcores=SC_SUBCORES)

    @functools.partial(
        pl.kernel, mesh=mesh,
        out_type=jax.ShapeDtypeStruct((cap, width), rows.dtype),
        scratch_types=[pltpu.VMEM((picks, r), I32),
                       pltpu.VMEM((r, width), rows.dtype),
                       pltpu.SemaphoreType.DMA])
    def scatter(rows_hbm, dest_hbm, out_hbm, idx_v, rows_v, sem):
        worker = lax.axis_index("s") * SC_CORES + lax.axis_index("c")

        @pl.loop(0, per_worker)
        def _(step):
            chunk = worker * per_worker + step
            base = pl.multiple_of(chunk * r, r)
            pltpu.sync_copy(rows_hbm.at[pl.ds(base, r)], rows_v)
            pltpu.sync_copy(dest_hbm.at[chunk], idx_v)
            for k in range(picks):
                pltpu.async_copy(rows_v, out_hbm.at[idx_v.at[k]], sem).wait()

    return scatter(rows, dest3)


def _pack_words(v):
    k = v.shape[1] // 2
    hi = pltpu.bitcast(v[:, :k].astype(BF16).astype(F32), I32)
    lo = pltpu.bitcast(v[:, k:].astype(BF16).astype(F32), I32)
    return jnp.bitwise_or(hi, lax.shift_right_logical(lo, jnp.int32(16)))


def _unpack_words(w, dtype=BF16):
    hi = pltpu.bitcast(jnp.bitwise_and(w, jnp.int32(-65536)), F32)
    lo = pltpu.bitcast(jnp.left_shift(w, 16), F32)
    return jnp.concatenate([hi, lo], axis=1).astype(dtype)


def _expert_kernel(exp_ref, blk_ref, new_ref, x_ref, wup_ref, bup_ref, wdn_ref, bdn_ref, o_ref,
                   wup_s, wdn_s, *, d_ff):
    del exp_ref, blk_ref

    @pl.when(new_ref[pl.program_id(0)] == 1)
    def _():
        wup_s[...] = wup_ref[0].astype(BF16)
        wdn_s[...] = wdn_ref[0].astype(BF16)

    half = MOE_ROWS // 2
    for r in range(2):
        rows = slice(r * half, (r + 1) * half)
        h = _dot(_unpack_words(x_ref[rows, :]), wup_s[...]) + bup_ref[0]
        glu = jnp.minimum(h[:, :d_ff], SWIGLU_LIMIT)
        lin = jnp.clip(h[:, d_ff:], -SWIGLU_LIMIT, SWIGLU_LIMIT)
        a = glu * _sigmoid(SWIGLU_ALPHA * glu) * (lin + 1.0)
        o_ref[rows, :] = _pack_words(_dot(a.astype(BF16), wdn_s[...]) + bdn_ref[0])


def _experts(blk_expert, blk_index, blk_new, xs, wup, bup, wdn, bdn):
    cap, words = xs.shape
    d = 2 * words
    d_ff = wdn.shape[1]
    row_map = lambda i, exp, blk, new: (blk[i], 0)
    exp_map3 = lambda i, exp, blk, new: (exp[i], 0, 0)
    grid_spec = pltpu.PrefetchScalarGridSpec(
        num_scalar_prefetch=3,
        grid=(cap // MOE_ROWS,),
        in_specs=[pl.BlockSpec((MOE_ROWS, words), row_map),
                  pl.BlockSpec((1, d, 2 * d_ff), exp_map3),
                  pl.BlockSpec((1, 1, 2 * d_ff), exp_map3),
                  pl.BlockSpec((1, d_ff, d), exp_map3),
                  pl.BlockSpec((1, 1, d), exp_map3)],
        out_specs=pl.BlockSpec((MOE_ROWS, words), row_map),
        scratch_shapes=[pltpu.VMEM((d, 2 * d_ff), BF16), pltpu.VMEM((d_ff, d), BF16)],
    )
    return pl.pallas_call(
        functools.partial(_expert_kernel, d_ff=d_ff),
        grid_spec=grid_spec,
        out_shape=jax.ShapeDtypeStruct((cap, words), I32),
        compiler_params=pltpu.CompilerParams(dimension_semantics=("arbitrary",),
                                             vmem_limit_bytes=VMEM_LIMIT),
        name="experts",
    )(blk_expert, blk_index, blk_new, xs, wup, bup, wdn, bdn)


def _route_tables(route, totals, *, cap):
    n = route.shape[0]
    expert = route[:, 0:TOP_K].astype(I32)
    rank = route[:, TOP_K:2 * TOP_K].astype(I32)
    counts = totals[0, :N_EXPERTS].astype(I32)
    padded = (counts + MOE_ROWS - 1) // MOE_ROWS * MOE_ROWS
    pad_end = jnp.cumsum(padded)
    pad_start = pad_end - padded
    dest = pad_start[expert] + rank
    dest_chunks = dest.reshape(n // SC_SCATTER_ROWS, SC_SCATTER_ROWS, TOP_K).transpose(0, 2, 1)
    n_blk = cap // MOE_ROWS
    used = pad_end[-1] // MOE_ROWS
    blk = jnp.minimum(jnp.arange(n_blk, dtype=I32), used - 1)
    blk_expert = jnp.minimum(jnp.sum(pad_end[None, :] <= (blk * MOE_ROWS)[:, None], axis=1),
                             N_EXPERTS - 1).astype(I32)
    blk_new = jnp.concatenate([jnp.ones((1,), I32), (blk_expert[1:] != blk_expert[:-1]).astype(I32)])
    return dest, dest_chunks, blk_expert, blk, blk_new


def _final_kernel(x1_ref, ya_ref, yb_ref, yc_ref, yd_ref, route_ref, p_ref, g1_ref, b1_ref,
                  wp_ref, wg_ref, g2_ref, b2_ref, *rest, alpha):
    o_ref = rest[-1]
    ffn = jnp.zeros(x1_ref.shape, F32)
    for k, yk_ref in enumerate((ya_ref, yb_ref, yc_ref, yd_ref)):
        gate_k = route_ref[:, 2 * TOP_K + k:2 * TOP_K + k + 1]
        ffn = ffn + gate_k * _unpack_words(yk_ref[...], F32)
    x2 = _layer_norm(alpha * x1_ref[...] + ffn, g1_ref[...], b1_ref[...])
    gate = _sigmoid(_dot(x2.astype(BF16), wg_ref[...]))
    ple = _dot(p_ref[...].astype(BF16), wp_ref[...]) * gate
    o_ref[...] = _layer_norm(alpha * x2 + ple, g2_ref[...], b2_ref[...])


def _final(x1, y4w, route, p2, g1, b1, wp, wg, g2, b2, *, alpha, tm, part, parts, prev=None):
    n, d = x1.shape
    assert TOP_K == 4
    steps = n // parts // tm
    off = part * steps
    rows = lambda w: pl.BlockSpec((tm, w), lambda i: (off + i, 0))
    full = lambda shape: pl.BlockSpec(shape, lambda i: (0,) * len(shape))
    pick = lambda k: pl.BlockSpec((tm, d // 2), lambda i: (k * steps + i, 0))
    in_specs = [rows(d), pick(0), pick(1), pick(2), pick(3), rows(LANES), rows(p2.shape[1]),
                full(g1.shape), full(b1.shape),
                full(wp.shape), full(wg.shape), full(g2.shape), full(b2.shape)]
    args = [x1, y4w, y4w, y4w, y4w, route, p2, g1, b1, wp, wg, g2, b2]
    aliases = {}
    if prev is not None:
        in_specs.append(pl.BlockSpec(memory_space=pl.ANY))
        args.append(prev)
        aliases = {len(args) - 1: 0}
    return pl.pallas_call(
        functools.partial(_final_kernel, alpha=alpha),
        grid=(steps,),
        in_specs=in_specs,
        out_specs=rows(d),
        out_shape=jax.ShapeDtypeStruct((n, d), F32),
        input_output_aliases=aliases,
        compiler_params=pltpu.CompilerParams(dimension_semantics=("arbitrary",),
                                             vmem_limit_bytes=VMEM_LIMIT),
        name="final",
    )(*args)


def _pack_w_in(w_in):
    d = w_in.shape[0]
    sizes = (ATT_HEADS * ATT_HEAD_DIM, KV_LATENT, IDX_HEADS * IDX_DIM, IDX_DIM, IDX_HEADS,
             HG_HEADS * HG_DIM, HG_HEADS * HG_DIM, HG_HEADS * HG_DIM, HG_HEADS * HG_DIM)
    offs = [0]
    for s in sizes:
        offs.append(offs[-1] + s)
    sec = [w_in[:, offs[k]:offs[k + 1]] for k in range(len(sizes))]
    iq = sec[2].reshape(d, IDX_HEADS, IDX_DIM)
    iq = jnp.pad(iq, ((0, 0), (0, 0), (0, LANES - IDX_DIM))).reshape(d, IDX_HEADS * LANES)
    ikw = jnp.pad(jnp.concatenate([sec[3], sec[4]], axis=1), ((0, 0), (0, LANES - IDX_DIM - IDX_HEADS)))
    w_n = jnp.concatenate([sec[1], ikw, sec[5], sec[6], sec[7], sec[8]], axis=1).astype(BF16)
    w_t = jnp.concatenate([sec[0], iq, ikw, sec[1]], axis=1).T.astype(BF16)
    return w_n, w_t


def _block_diag_uk_t(w_uk):
    eye = jnp.eye(ATT_HEADS, dtype=w_uk.dtype)
    bd = jnp.einsum("rhd,hg->hrgd", w_uk, eye)
    return bd.reshape(ATT_HEADS * KV_LATENT, ATT_HEADS * ATT_HEAD_DIM).astype(BF16)


def _uv_t(w_uv):
    return jnp.transpose(w_uv, (1, 2, 0)).astype(BF16)


def _layer(x, p_l, w_in, kv_g, ik_g, ik_b, w_uk, w_uv, lb, hg_ng, w_o, ln_mix_g, ln_mix_b,
           w_router, b_router, w_up, b_up, w_down, b_down, ln_ffn_g, ln_ffn_b,
           w_ple_proj, w_ple_gate, ln_ple_g, ln_ple_b, *, alpha, tm, hg_tb):
    bsz, t, d = x.shape
    n = bsz * t
    x2 = x.reshape(n, d)
    row = lambda v: v.reshape(1, -1).astype(F32)
    pad_lane = lambda v: jnp.pad(row(v), ((0, 0), (0, LANES - v.shape[-1])))

    w_n, w_t = _pack_w_in(w_in)
    qt, iqt, iwt, c, ct, ik, hg = _inproj(x2, w_n, w_t, _block_diag_uk_t(w_uk), row(kv_g),
                                          kv_g.reshape(-1, 1).astype(F32), pad_lane(ik_g), pad_lane(ik_b), tm=tm)
    y_rec = _hgrn2(hg.reshape(bsz, t, -1), row(lb), row(hg_ng), tb=hg_tb)
    y_att = _dsa(qt, iqt, iwt, ik, c, ct, _uv_t(w_uv), bsz=bsz, t=t)

    wr_f = jnp.pad(w_router.astype(F32), ((0, 0), (0, LANES - N_EXPERTS)))
    wr_hi = wr_f.astype(BF16)
    wr_p = jnp.concatenate([wr_hi, (wr_f - wr_hi.astype(F32)).astype(BF16)], axis=1)
    x1, x1w, route, totals = _mix_router(
        x2, y_att, y_rec.reshape(n, -1), w_o.astype(BF16), row(ln_mix_g), row(ln_mix_b),
        wr_p, pad_lane(b_router), alpha=alpha, tm=tm)

    step = SC_WORKERS * SC_ROWS
    cap = -(-(n * TOP_K + N_EXPERTS * MOE_ROWS) // step) * step
    dest, dest_chunks, blk_expert, blk_index, blk_new = _route_tables(route, totals, cap=cap)
    d_ff = w_down.shape[1]
    xs = _sc_scatter_rows(x1w, dest_chunks, cap)
    outw = _experts(blk_expert, blk_index, blk_new, xs,
                    w_up.astype(F32), b_up.reshape(N_EXPERTS, 1, 2 * d_ff).astype(F32),
                    w_down.astype(F32), b_down.reshape(N_EXPERTS, 1, d).astype(F32))

    parts = COMBINE_PARTS if n % (COMBINE_PARTS * max(tm, SC_WORKERS * SC_ROWS // TOP_K)) == 0 else 1
    out = None
    for part in range(parts):
        rows_p = slice(part * (n // parts), (part + 1) * (n // parts))
        y4w = _sc_gather_rows(outw, dest[rows_p].T.reshape(-1))
        out = _final(x1, y4w, route, p_l.reshape(n, -1), row(ln_ffn_g), row(ln_ffn_b),
                     w_ple_proj.astype(BF16), w_ple_gate.astype(BF16), row(ln_ple_g), row(ln_ple_b),
                     alpha=alpha, tm=tm, part=part, parts=parts, prev=out)
    return out.reshape(bsz, t, d)


def kernel(x, p, w_in, kv_norm_g, idx_k_norm_g, idx_k_norm_b, w_uk, w_uv, hg_lb_logits, hg_norm_g, w_o,
           ln_mix_g, ln_mix_b, w_router, b_router, w_up, b_up, w_down, b_down, ln_ffn_g, ln_ffn_b,
           w_ple_proj, w_ple_gate, ln_ple_g, ln_ple_b):
    depth = w_in.shape[0]
    alpha = (2.0 * depth) ** 0.25
    lower_bounds = jnp.cumsum(jax.nn.softmax(hg_lb_logits.astype(F32), axis=0), axis=0)
    n = x.shape[0] * x.shape[1]
    tm = min(512, n)
    hg_tb = min(512, x.shape[1])
    for l in range(depth):
        x = _layer(x, p[l], w_in[l], kv_norm_g[l], idx_k_norm_g[l], idx_k_norm_b[l], w_uk[l], w_uv[l],
                   lower_bounds[l], hg_norm_g[l], w_o[l], ln_mix_g[l], ln_mix_b[l], w_router[l],
                   b_router[l], w_up[l], b_up[l], w_down[l], b_down[l], ln_ffn_g[l], ln_ffn_b[l],
                   w_ple_proj[l], w_ple_gate[l], ln_ple_g[l], ln_ple_b[l],
                   alpha=alpha, tm=tm, hg_tb=hg_tb)
    return x
```

```python
import functools

import jax
import jax.numpy as jnp
from jax import lax
from jax.experimental import pallas as pl
from jax.experimental.pallas import tpu as pltpu
from jax.experimental.pallas import tpu_sc as plsc

F32 = jnp.float32
BF16 = jnp.bfloat16
I32 = jnp.int32

ATT_HEADS = 8
ATT_HEAD_DIM = 64
KV_LATENT = 256
IDX_HEADS = 8
IDX_DIM = 64
TOPK_MAX = 256
HG_HEADS = 4
HG_DIM = 128
HG_CHUNK = 32
N_EXPERTS = 32
TOP_K = 4
SWIGLU_LIMIT = 7.0
SWIGLU_ALPHA = 1.702
LN_EPS = 1e-5
RMS_EPS = 1e-6

LANES = 128
Q_ROWS = 128
KEY_CHUNK = 512
COUNT_ROWS = 128
SEARCH_BITS = 20
EXTRACT_MAX = 4
MOE_ROWS = 512
SC_CORES = 2
SC_SUBCORES = 16
SC_WORKERS = SC_CORES * SC_SUBCORES
SC_ROWS = 128
SC_SCATTER_ROWS = 128
ROUTER_ROWS = 512
COMBINE_PARTS = 2
VMEM_LIMIT = 56 * 1024 * 1024

INT_MIN = -(2 ** 31)
NEG_BIG = -1e30
LOG2_E = 1.4426950408889634

_N_KV = 0
_N_IKW = _N_KV + KV_LATENT
_N_HG = _N_IKW + LANES
_N_END = _N_HG + 4 * HG_HEADS * HG_DIM
_T_AQ = 0
_T_IQ = _T_AQ + ATT_HEADS * ATT_HEAD_DIM
_T_IKW = _T_IQ + IDX_HEADS * LANES
_T_KV = _T_IKW + LANES
_T_END = _T_KV + KV_LATENT


def _dot(a, b):
    return jnp.dot(a, b, preferred_element_type=F32)


def _dot_nt(a, b):
    return lax.dot_general(a, b, (((1,), (1,)), ((), ())), preferred_element_type=F32)


def _dot_tn(a, b):
    return lax.dot_general(a, b, (((0,), (0,)), ((), ())), preferred_element_type=F32)


def _layer_norm(z, g, b):
    mu = jnp.mean(z, axis=-1, keepdims=True)
    d = z - mu
    var = jnp.mean(d * d, axis=-1, keepdims=True)
    return d * lax.rsqrt(var + LN_EPS) * g + b


def _sigmoid(x):
    return 1.0 / (1.0 + jnp.exp(-x))


def _inproj_kernel(x_ref, wn_ref, wt_ref, wukt_ref, kvg_ref, kvgc_ref, ikg_ref, ikb_ref,
                   qt_ref, iqt_ref, iwt_ref, c_ref, ct_ref, ik_ref, hg_ref):
    xb = x_ref[...].astype(BF16)

    aqt = _dot_nt(wt_ref[_T_AQ:_T_IQ, :], xb).astype(BF16)
    qt_ref[...] = (_dot(wukt_ref[...], aqt) * (LOG2_E * ATT_HEAD_DIM ** -0.5)).astype(BF16)
    iqt_ref[...] = (_dot_nt(wt_ref[_T_IQ:_T_IKW, :], xb) * (IDX_DIM ** -0.5)).astype(BF16)
    iwt_ref[...] = _dot_nt(wt_ref[_T_IKW:_T_KV, :], xb) * (IDX_HEADS ** -0.5)
    act = _dot_nt(wt_ref[_T_KV:_T_END, :], xb)
    ct = act * lax.rsqrt(jnp.mean(act * act, axis=0, keepdims=True) + RMS_EPS) * kvgc_ref[...]
    ct_ref[...] = ct.astype(BF16)

    ac = _dot(xb, wn_ref[:, _N_KV:_N_IKW])
    c = ac * lax.rsqrt(jnp.mean(ac * ac, axis=-1, keepdims=True) + RMS_EPS) * kvg_ref[...]
    c_ref[...] = c.astype(BF16)

    ikw = _dot(xb, wn_ref[:, _N_IKW:_N_HG])
    lane = lax.broadcasted_iota(I32, ikw.shape, 1)
    is_k = lane < IDX_DIM
    mu = jnp.sum(jnp.where(is_k, ikw, 0.0), axis=-1, keepdims=True) * (1.0 / IDX_DIM)
    d = jnp.where(is_k, ikw - mu, 0.0)
    var = jnp.sum(d * d, axis=-1, keepdims=True) * (1.0 / IDX_DIM)
    ik = d * lax.rsqrt(var + LN_EPS) * ikg_ref[...] + ikb_ref[...]
    ik_ref[...] = ik.astype(BF16)

    hg_ref[...] = _dot(xb, wn_ref[:, _N_HG:_N_END])


def _inproj(x2, w_n, w_t, wuk_t, kvg, kvg_col, ikg, ikb, *, tm):
    n, d = x2.shape
    grid = (n // tm,)
    full = lambda shape: pl.BlockSpec(shape, lambda i: (0,) * len(shape))
    rows = lambda w: pl.BlockSpec((tm, w), lambda i: (i, 0))
    cols = lambda h: pl.BlockSpec((h, tm), lambda i: (0, i))
    return pl.pallas_call(
        _inproj_kernel,
        grid=grid,
        in_specs=[rows(d), full(w_n.shape), full(w_t.shape), full(wuk_t.shape), full(kvg.shape),
                  full(kvg_col.shape), full(ikg.shape), full(ikb.shape)],
        out_specs=[cols(ATT_HEADS * KV_LATENT), cols(IDX_HEADS * LANES), cols(LANES),
                   rows(KV_LATENT), cols(KV_LATENT), rows(LANES), rows(4 * HG_HEADS * HG_DIM)],
        out_shape=[jax.ShapeDtypeStruct((ATT_HEADS * KV_LATENT, n), BF16),
                   jax.ShapeDtypeStruct((IDX_HEADS * LANES, n), BF16),
                   jax.ShapeDtypeStruct((LANES, n), F32),
                   jax.ShapeDtypeStruct((n, KV_LATENT), BF16),
                   jax.ShapeDtypeStruct((KV_LATENT, n), BF16),
                   jax.ShapeDtypeStruct((n, LANES), BF16),
                   jax.ShapeDtypeStruct((n, 4 * HG_HEADS * HG_DIM), F32)],
        compiler_params=pltpu.CompilerParams(dimension_semantics=("arbitrary",),
                                             vmem_limit_bytes=VMEM_LIMIT),
        name="inproj",
    )(x2, w_n, w_t, wuk_t, kvg, kvg_col, ikg, ikb)


def _hgrn2_kernel(hg_ref, lb_ref, ng_ref, o_ref, state_ref, *, tb):
    @pl.when(pl.program_id(1) == 0)
    def _():
        state_ref[...] = jnp.zeros_like(state_ref)

    nchunk = tb // HG_CHUNK
    width = HG_HEADS * HG_DIM
    row = lax.broadcasted_iota(I32, (tb, HG_DIM), 0)
    rin = jnp.bitwise_and(row, HG_CHUNK - 1)
    r2 = lax.broadcasted_iota(I32, (tb, tb), 0)
    c2 = lax.broadcasted_iota(I32, (tb, tb), 1)
    intra = jnp.logical_and(r2 // HG_CHUNK == c2 // HG_CHUNK, c2 <= r2)

    for h in range(HG_HEADS):
        sl = slice(h * HG_DIM, (h + 1) * HG_DIM)
        gq = hg_ref[0, :, h * HG_DIM:(h + 1) * HG_DIM]
        gf = hg_ref[0, :, width + h * HG_DIM:width + (h + 1) * HG_DIM]
        gi = hg_ref[0, :, 2 * width + h * HG_DIM:2 * width + (h + 1) * HG_DIM]
        gg = hg_ref[0, :, 3 * width + h * HG_DIM:3 * width + (h + 1) * HG_DIM]
        lb = lb_ref[:, sl]
        forget = lb + (1.0 - lb) * _sigmoid(gf)
        logf = jnp.log(forget)
        b = logf
        s = 1
        while s < HG_CHUNK:
            b = b + jnp.where(rin >= s, pltpu.roll(b, s, axis=0), 0.0)
            s *= 2
        tot = jnp.sum(logf.reshape(nchunk, HG_CHUNK, HG_DIM), axis=1, keepdims=True)
        bl = jnp.broadcast_to(tot, (nchunk, HG_CHUNK, HG_DIM)).reshape(tb, HG_DIM)
        kk = 1.0 - forget
        q_dec = (gq * _sigmoid(gq) * jnp.exp(b)).astype(BF16)
        k_inv = (kk * jnp.exp(-b)).astype(BF16)
        k_end = (kk * jnp.exp(bl - b)).astype(BF16)
        vb = gi.astype(BF16)
        scores = jnp.where(intra, _dot_nt(q_dec, k_inv), 0.0).astype(BF16)
        o = _dot(scores, vb)
        decay = jnp.exp(tot)
        chunk_rows = [slice(n * HG_CHUNK, (n + 1) * HG_CHUNK) for n in range(nchunk)]
        incs = [_dot_tn(vb[rs], k_end[rs]) for rs in chunk_rows]
        st = state_ref[h]
        starts = []
        for n in range(nchunk):
            starts.append(st.astype(BF16))
            st = st * decay[n] + incs[n]
        state_ref[h] = st
        inter = [_dot_nt(q_dec[rs], starts[n]) for n, rs in enumerate(chunk_rows)]
        o = o + jnp.concatenate(inter, axis=0)
        o = o * lax.rsqrt(jnp.mean(o * o, axis=-1, keepdims=True) + RMS_EPS)
        o = o * ng_ref[:, sl] * (gg * _sigmoid(gg))
        o_ref[0, :, sl] = o.astype(o_ref.dtype)


def _hgrn2(hg3, lb, ng, *, tb):
    bsz, t, w4 = hg3.shape
    width = HG_HEADS * HG_DIM
    return pl.pallas_call(
        functools.partial(_hgrn2_kernel, tb=tb),
        grid=(bsz, t // tb),
        in_specs=[pl.BlockSpec((1, tb, w4), lambda b, i: (b, i, 0)),
                  pl.BlockSpec((1, width), lambda b, i: (0, 0)),
                  pl.BlockSpec((1, width), lambda b, i: (0, 0))],
        out_specs=pl.BlockSpec((1, tb, width), lambda b, i: (b, i, 0)),
        out_shape=jax.ShapeDtypeStruct((bsz, t, width), BF16),
        scratch_shapes=[pltpu.VMEM((HG_HEADS, HG_DIM, HG_DIM), F32)],
        compiler_params=pltpu.CompilerParams(dimension_semantics=("arbitrary", "arbitrary"),
                                             vmem_limit_bytes=VMEM_LIMIT),
        name="hgrn2",
    )(hg3, lb, ng)


def _dsa_kernel(qt_ref, iqt_ref, iwt_ref, ik_ref, c_ref, ct_ref, wuvt_ref, o_ref,
                keys_ref, cut_ref, thr_ref, redo_ref, qall_ref, iqall_ref, m_ref, l_ref, acc_ref,
                *, topk, pos_bits):
    n = pl.program_id(1)
    t0 = n * Q_ROWS
    nch = (t0 + Q_ROWS + KEY_CHUNK - 1) // KEY_CHUNK
    kf = float(topk)
    key_i = lax.broadcasted_iota(I32, (KEY_CHUNK, Q_ROWS), 0)
    tq = t0 + lax.broadcasted_iota(I32, (KEY_CHUNK, Q_ROWS), 1)
    pair = 2 * Q_ROWS

    for h in range(ATT_HEADS):
        qall_ref[:, h * Q_ROWS:(h + 1) * Q_ROWS] = qt_ref[h * KV_LATENT:(h + 1) * KV_LATENT, :]
    for h in range(IDX_HEADS):
        iqall_ref[:, h * Q_ROWS:(h + 1) * Q_ROWS] = iqt_ref[h * LANES:(h + 1) * LANES, :]
    head_w = [iwt_ref[IDX_DIM + h:IDX_DIM + h + 1, :] for h in range(IDX_HEADS)]

    def index_body(j, carry):
        k0 = pl.multiple_of(j * KEY_CHUNK, KEY_CHUNK)
        ikc = ik_ref[pl.ds(k0, KEY_CHUNK), :]
        isc = jnp.zeros((KEY_CHUNK, Q_ROWS), F32)
        for hp in range(IDX_HEADS // 2):
            z = _dot(ikc, iqall_ref[:, hp * pair:(hp + 1) * pair])
            for hh in range(2):
                isc = isc + head_w[2 * hp + hh] * jnp.maximum(z[:, hh * Q_ROWS:(hh + 1) * Q_ROWS], 0.0)
        isc = jnp.where(isc == 0.0, 0.0, isc)
        bits = pltpu.bitcast(isc, I32)
        key = jnp.bitwise_xor(bits, jnp.bitwise_and(jnp.right_shift(bits, 31), 0x7FFFFFFF))
        key = jnp.where(k0 + key_i <= tq, key, INT_MIN)
        keys_ref[pl.ds(k0, KEY_CHUNK), :] = key
        return carry

    lax.fori_loop(0, nch, index_body, 0)

    def count(pred):
        def body(j, acc):
            k0 = pl.multiple_of(j * KEY_CHUNK, KEY_CHUNK)
            hit = jnp.where(pred(keys_ref[pl.ds(k0, KEY_CHUNK), :], k0), 1.0, 0.0)
            return acc + jnp.sum(hit.reshape(KEY_CHUNK // COUNT_ROWS, COUNT_ROWS, Q_ROWS), axis=0)
        acc = lax.fori_loop(0, nch, body, jnp.zeros((COUNT_ROWS, Q_ROWS), F32))
        return jnp.sum(acc, axis=0, keepdims=True)

    def count_ge(cand):
        return count(lambda k, k0: k >= cand)

    def masked_max(bound):
        def body(j, acc):
            k0 = pl.multiple_of(j * KEY_CHUNK, KEY_CHUNK)
            key = keys_ref[pl.ds(k0, KEY_CHUNK), :]
            kept = jnp.where(key <= bound, key, INT_MIN)
            return jnp.maximum(acc, jnp.max(kept.reshape(KEY_CHUNK // COUNT_ROWS, COUNT_ROWS, Q_ROWS), axis=0))
        acc = lax.fori_loop(0, nch, body, jnp.full((COUNT_ROWS, Q_ROWS), INT_MIN, I32))
        return jnp.max(acc, axis=0, keepdims=True)

    n_all = (nch * KEY_CHUNK).astype(F32)
    n0 = count_ge(jnp.zeros((1, Q_ROWS), I32))
    pos0 = n0 >= kf
    state0 = (jnp.where(pos0, 0, INT_MIN).astype(I32), jnp.where(pos0, n0, n_all), jnp.where(pos0, 0.0, n0))

    def bracket_body(i, state):
        thr, n_lo, n_hi = state
        cand = thr + lax.shift_left(jnp.int32(1), 30 - i)
        cnt = count_ge(cand)
        ge = cnt >= kf
        return jnp.where(ge, cand, thr), jnp.where(ge, cnt, n_lo), jnp.where(ge, n_hi, cnt)

    thr1, n_lo, n_hi = lax.fori_loop(0, SEARCH_BITS, bracket_body, state0)
    low_bits = 31 - SEARCH_BITS
    has_topk = tq[0:1, :] + 1 >= topk
    in_bracket = n_lo - n_hi
    exact_cut = n_lo == kf
    few = jnp.logical_or(jnp.logical_not(has_topk), jnp.logical_or(exact_cut, in_bracket <= float(EXTRACT_MAX)))
    thr_ref[...] = jnp.broadcast_to(thr1, thr_ref.shape)
    redo_ref[0] = jnp.where(jnp.min(jnp.where(few, 1.0, 0.0)) > 0.0, 0, 1)

    @pl.when(redo_ref[0] == 0)
    def _():
        need = kf - n_hi
        bound = thr1 + (2 ** low_bits - 1)
        found = jnp.zeros((1, Q_ROWS), F32)
        thr_x = thr1
        for j in range(1, EXTRACT_MAX + 1):
            cur = masked_max(bound)
            inside = cur >= thr1
            found = found + jnp.where(inside, 1.0, 0.0)
            thr_x = jnp.where(jnp.logical_and(inside, need == float(j)), cur, thr_x)
            bound = jnp.where(inside, cur - 1, INT_MIN)
        use_x = jnp.logical_and(has_topk, jnp.logical_not(exact_cut))
        clash = jnp.logical_and(use_x, found != in_bracket)
        thr_ref[...] = jnp.broadcast_to(jnp.where(use_x, thr_x, thr1), thr_ref.shape)
        redo_ref[0] = jnp.where(jnp.max(jnp.where(clash, 1.0, 0.0)) > 0.0, 1, 0)

    @pl.when(redo_ref[0] == 1)
    def _():
        def bit_body(i, thr):
            cand = thr + lax.shift_left(jnp.int32(1), low_bits - 1 - i)
            return jnp.where(count_ge(cand) >= kf, cand, thr)

        thr_ref[...] = jnp.broadcast_to(lax.fori_loop(0, low_bits, bit_body, thr1), thr_ref.shape)

    thr = thr_ref[0:1, :]
    live = thr > INT_MIN
    cut_ref[...] = jnp.broadcast_to(jnp.where(live, jnp.int32(2 ** 30), -1), cut_ref.shape)
    n_ge = count_ge(thr)
    tie = jnp.max(jnp.where(jnp.logical_and(live, n_ge > kf), 1.0, 0.0))

    @pl.when(tie > 0.0)
    def _():
        need = kf - count(lambda k, k0: k > thr)

        def cut_body(i, cut):
            cand = cut + lax.shift_left(jnp.int32(1), pos_bits - 1 - i)
            below = count(lambda k, k0: jnp.logical_and(k == thr, k0 + key_i < cand))
            return jnp.where(below < need, cand, cut)

        cut = lax.fori_loop(0, pos_bits, cut_body, jnp.zeros((1, Q_ROWS), I32))
        cut_ref[...] = jnp.broadcast_to(jnp.where(live, cut, -1), cut_ref.shape)

    m_ref[...] = jnp.full(m_ref.shape, NEG_BIG, F32)
    l_ref[...] = jnp.zeros(l_ref.shape, F32)
    acc_ref[...] = jnp.zeros(acc_ref.shape, F32)
    cut_b = cut_ref[0:1, :]

    def attn_body(j, carry):
        k0 = pl.multiple_of(j * KEY_CHUNK, KEY_CHUNK)
        key = keys_ref[pl.ds(k0, KEY_CHUNK), :]
        pos = k0 + key_i
        sel = jnp.logical_or(key > thr, jnp.logical_and(key == thr, pos <= cut_b))
        rel = jnp.where(sel, (pos - t0).astype(F32), NEG_BIG)
        cc = c_ref[pl.ds(k0, KEY_CHUNK), :]
        cct = ct_ref[:, pl.ds(k0, KEY_CHUNK)]
        for hp in range(ATT_HEADS // 2):
            cs = slice(hp * pair, (hp + 1) * pair)
            s2 = _dot(cc, qall_ref[:, cs])
            ps, alphas = [], []
            for hh in range(2):
                h = 2 * hp + hh
                slope = LOG2_E * 2.0 ** (-(8.0 / ATT_HEADS) * (h + 1))
                s = s2[:, hh * Q_ROWS:(hh + 1) * Q_ROWS] + slope * rel
                m_old = m_ref[h:h + 1, :]
                m_new = jnp.maximum(m_old, jnp.max(s, axis=0, keepdims=True))
                alpha = jnp.exp2(m_old - m_new)
                p = jnp.exp2(s - m_new)
                l_ref[h:h + 1, :] = alpha * l_ref[h:h + 1, :] + jnp.sum(p, axis=0, keepdims=True)
                m_ref[h:h + 1, :] = m_new
                ps.append(p.astype(BF16))
                alphas.append(alpha)
            acc_ref[:, cs] = (acc_ref[:, cs] * jnp.concatenate(alphas, axis=1)
                              + _dot(cct, jnp.concatenate(ps, axis=1)))
        return carry

    lax.fori_loop(0, nch, attn_body, 0)

    yt = []
    for h in range(ATT_HEADS):
        o_lat = (acc_ref[:, h * Q_ROWS:(h + 1) * Q_ROWS] / l_ref[h:h + 1, :]).astype(BF16)
        yt.append(_dot(wuvt_ref[h], o_lat))
    o_ref[...] = jnp.concatenate(yt, axis=0).T.astype(o_ref.dtype)


def _dsa(qt, iqt, iwt, ik, c, ct, wuvt_p, *, bsz, t):
    n = bsz * t
    nq = t // Q_ROWS
    topk = min(TOPK_MAX, t // 4)
    assert t % KEY_CHUNK == 0
    width = ATT_HEADS * ATT_HEAD_DIM
    qcols = lambda h: pl.BlockSpec((h, Q_ROWS), lambda b, i: (0, b * nq + i))
    return pl.pallas_call(
        functools.partial(_dsa_kernel, topk=topk, pos_bits=(t - 1).bit_length()),
        grid=(bsz, nq),
        in_specs=[qcols(ATT_HEADS * KV_LATENT), qcols(IDX_HEADS * LANES), qcols(LANES),
                  pl.BlockSpec((t, LANES), lambda b, i: (b, 0)),
                  pl.BlockSpec((t, KV_LATENT), lambda b, i: (b, 0)),
                  pl.BlockSpec((KV_LATENT, t), lambda b, i: (0, b)),
                  pl.BlockSpec(wuvt_p.shape, lambda b, i: (0, 0, 0))],
        out_specs=pl.BlockSpec((Q_ROWS, width), lambda b, i: (b * nq + i, 0)),
        out_shape=jax.ShapeDtypeStruct((n, width), BF16),
        scratch_shapes=[pltpu.VMEM((t, Q_ROWS), I32),
                        pltpu.VMEM((8, Q_ROWS), I32),
                        pltpu.VMEM((8, Q_ROWS), I32),
                        pltpu.SMEM((1,), I32),
                        pltpu.VMEM((KV_LATENT, ATT_HEADS * Q_ROWS), BF16),
                        pltpu.VMEM((LANES, IDX_HEADS * Q_ROWS), BF16),
                        pltpu.VMEM((8, Q_ROWS), F32),
                        pltpu.VMEM((8, Q_ROWS), F32),
                        pltpu.VMEM((KV_LATENT, ATT_HEADS * Q_ROWS), F32)],
        compiler_params=pltpu.CompilerParams(dimension_semantics=("arbitrary", "arbitrary"),
                                             vmem_limit_bytes=VMEM_LIMIT),
        name="dsa",
    )(qt, iqt, iwt, ik, c, ct, wuvt_p)


def _mix_router_kernel(x_ref, ya_ref, yr_ref, wo_ref, g_ref, b_ref, wr_ref, br_ref,
                       x1_ref, x1w_ref, route_ref, tot_ref, carry_ref, *, alpha, tm):
    @pl.when(pl.program_id(0) == 0)
    def _():
        carry_ref[...] = jnp.zeros_like(carry_ref)

    half = ya_ref.shape[1]
    tr = min(tm, ROUTER_ROWS)
    lane = lax.broadcasted_iota(I32, (tr, LANES), 1)
    lane_f = lane.astype(F32)
    r2 = lax.broadcasted_iota(I32, (tr, tr), 0)
    c2 = lax.broadcasted_iota(I32, (tr, tr), 1)
    before = jnp.where(c2 < r2, 1.0, 0.0).astype(BF16)
    carry = carry_ref[...]

    for part in range(tm // tr):
        rows = slice(part * tr, (part + 1) * tr)
        mix = _dot(ya_ref[rows, :], wo_ref[0:half, :]) + _dot(yr_ref[rows, :], wo_ref[half:, :])
        x1 = _layer_norm(alpha * x_ref[rows, :] + mix, g_ref[...], b_ref[...])
        x1_ref[rows, :] = x1
        x1w_ref[rows, :] = _pack_words(x1)

        x_hi = x1.astype(BF16)
        x_lo = (x1 - x_hi.astype(F32)).astype(BF16)
        hh_hl = _dot(x_hi, wr_ref[...])
        logits = (hh_hl[:, :LANES] + hh_hl[:, LANES:] + _dot(x_lo, wr_ref[:, :LANES])) + br_ref[...]
        work = jnp.where(lane < N_EXPERTS, logits, -jnp.inf)
        sel_f = jnp.zeros(logits.shape, F32)
        denom = jnp.zeros((tr, 1), F32)
        hits, experts, weights = [], [], []
        for _ in range(TOP_K):
            mx = jnp.max(work, axis=-1, keepdims=True)
            first = jnp.min(jnp.where(work == mx, lane_f, float(LANES)), axis=-1, keepdims=True)
            hit = lane_f == first
            e = jnp.exp(mx - (weights[0][1] if weights else mx))
            hits.append(hit)
            experts.append(first)
            weights.append((e, mx))
            denom = denom + e
            sel_f = sel_f + jnp.where(hit, 1.0, 0.0)
            work = jnp.where(hit, -jnp.inf, work)

        rank = _dot(before, sel_f.astype(BF16)) + carry
        carry = carry + jnp.sum(sel_f, axis=0, keepdims=True)

        route = jnp.zeros(logits.shape, F32)
        for k in range(TOP_K):
            rank_k = jnp.sum(jnp.where(hits[k], rank, 0.0), axis=-1, keepdims=True)
            route = jnp.where(lane == k, experts[k], route)
            route = jnp.where(lane == TOP_K + k, rank_k, route)
            route = jnp.where(lane == 2 * TOP_K + k, weights[k][0] / denom, route)
        route_ref[rows, :] = route

    carry_ref[...] = carry
    tot_ref[...] = jnp.broadcast_to(carry, tot_ref.shape)


def _mix_router(x2, ya, yr, wo, g, b, wr_p, br_p, *, alpha, tm):
    n, d = x2.shape
    half = ya.shape[1]
    rows = lambda w: pl.BlockSpec((tm, w), lambda i: (i, 0))
    full = lambda shape: pl.BlockSpec(shape, lambda i: (0,) * len(shape))
    return pl.pallas_call(
        functools.partial(_mix_router_kernel, alpha=alpha, tm=tm),
        grid=(n // tm,),
        in_specs=[rows(d), rows(half), rows(half), full(wo.shape), full(g.shape), full(b.shape),
                  full(wr_p.shape), full(br_p.shape)],
        out_specs=[rows(d), rows(d // 2), rows(LANES), full((8, LANES))],
        out_shape=[jax.ShapeDtypeStruct((n, d), F32),
                   jax.ShapeDtypeStruct((n, d // 2), I32),
                   jax.ShapeDtypeStruct((n, LANES), F32),
                   jax.ShapeDtypeStruct((8, LANES), F32)],
        scratch_shapes=[pltpu.VMEM((1, LANES), F32)],
        compiler_params=pltpu.CompilerParams(dimension_semantics=("arbitrary",),
                                             vmem_limit_bytes=VMEM_LIMIT),
        name="mix_router",
    )(x2, ya, yr, wo, g, b, wr_p, br_p)


def _sc_gather_rows(table, idx):
    n_idx = idx.shape[0]
    width = table.shape[1]
    per_worker = n_idx // SC_WORKERS
    assert per_worker * SC_WORKERS == n_idx and per_worker % SC_ROWS == 0
    mesh = plsc.VectorSubcoreMesh(core_axis_name="c", subcore_axis_name="s",
                                  num_cores=SC_CORES, num_subcores=SC_SUBCORES)

    @functools.partial(
        pl.kernel, mesh=mesh,
        out_type=jax.ShapeDtypeStruct((n_idx, width), table.dtype),
        scratch_types=[pltpu.VMEM((SC_ROWS,), I32),
                       pltpu.VMEM((SC_ROWS, width), table.dtype),
                       pltpu.SemaphoreType.DMA])
    def gather(table_hbm, idx_hbm, out_hbm, idx_v, rows_v, sem):
        worker = lax.axis_index("s") * SC_CORES + lax.axis_index("c")

        @pl.loop(0, per_worker // SC_ROWS)
        def _(step):
            base = pl.multiple_of(worker * per_worker + step * SC_ROWS, SC_ROWS)
            pltpu.sync_copy(idx_hbm.at[pl.ds(base, SC_ROWS)], idx_v)
            pltpu.async_copy(table_hbm.at[idx_v], rows_v, sem).wait()
            pltpu.sync_copy(rows_v, out_hbm.at[pl.ds(base, SC_ROWS)])

    return gather(table, idx)


def _sc_scatter_rows(rows, dest3, cap):
    n, width = rows.shape
    chunks, picks, r = dest3.shape
    assert r == SC_SCATTER_ROWS and chunks * r == n and chunks % SC_WORKERS == 0
    per_worker = chunks // SC_WORKERS
    mesh = plsc.VectorSubcoreMesh(core_axis_name="c", subcore_axis_name="s",
                                  num_cores=SC_CORES, num_subcores=SC_SUBCORES)

    @functools.partial(
        pl.kernel, mesh=mesh,
        out_type=jax.ShapeDtypeStruct((cap, width), rows.dtype),
        scratch_types=[pltpu.VMEM((picks, r), I32),
                       pltpu.VMEM((r, width), rows.dtype),
                       pltpu.SemaphoreType.DMA])
    def scatter(rows_hbm, dest_hbm, out_hbm, idx_v, rows_v, sem):
        worker = lax.axis_index("s") * SC_CORES + lax.axis_index("c")

        @pl.loop(0, per_worker)
        def _(step):
            chunk = worker * per_worker + step
            base = pl.multiple_of(chunk * r, r)
            pltpu.sync_copy(rows_hbm.at[pl.ds(base, r)], rows_v)
            pltpu.sync_copy(dest_hbm.at[chunk], idx_v)
            for k in range(picks):
                pltpu.async_copy(rows_v, out_hbm.at[idx_v.at[k]], sem).wait()

    return scatter(rows, dest3)


def _pack_words(v):
    k = v.shape[1] // 2
    hi = pltpu.bitcast(v[:, :k].astype(BF16).astype(F32), I32)
    lo = pltpu.bitcast(v[:, k:].astype(BF16).astype(F32), I32)
    return jnp.bitwise_or(hi, lax.shift_right_logical(lo, jnp.int32(16)))


def _unpack_words(w, dtype=BF16):
    hi = pltpu.bitcast(jnp.bitwise_and(w, jnp.int32(-65536)), F32)
    lo = pltpu.bitcast(jnp.left_shift(w, 16), F32)
    return jnp.concatenate([hi, lo], axis=1).astype(dtype)


def _expert_kernel(exp_ref, blk_ref, new_ref, x_ref, wup_ref, bup_ref, wdn_ref, bdn_ref, o_ref,
                   wup_s, wdn_s, *, d_ff):
    del exp_ref, blk_ref

    @pl.when(new_ref[pl.program_id(0)] == 1)
    def _():
        wup_s[...] = wup_ref[0].astype(BF16)
        wdn_s[...] = wdn_ref[0].astype(BF16)

    half = MOE_ROWS // 2
    for r in range(2):
        rows = slice(r * half, (r + 1) * half)
        h = _dot(_unpack_words(x_ref[rows, :]), wup_s[...]) + bup_ref[0]
        glu = jnp.minimum(h[:, :d_ff], SWIGLU_LIMIT)
        lin = jnp.clip(h[:, d_ff:], -SWIGLU_LIMIT, SWIGLU_LIMIT)
        a = glu * _sigmoid(SWIGLU_ALPHA * glu) * (lin + 1.0)
        o_ref[rows, :] = _pack_words(_dot(a.astype(BF16), wdn_s[...]) + bdn_ref[0])


def _experts(blk_expert, blk_index, blk_new, xs, wup, bup, wdn, bdn):
    cap, words = xs.shape
    d = 2 * words
    d_ff = wdn.shape[1]
    row_map = lambda i, exp, blk, new: (blk[i], 0)
    exp_map3 = lambda i, exp, blk, new: (exp[i], 0, 0)
    grid_spec = pltpu.PrefetchScalarGridSpec(
        num_scalar_prefetch=3,
        grid=(cap // MOE_ROWS,),
        in_specs=[pl.BlockSpec((MOE_ROWS, words), row_map),
                  pl.BlockSpec((1, d, 2 * d_ff), exp_map3),
                  pl.BlockSpec((1, 1, 2 * d_ff), exp_map3),
                  pl.BlockSpec((1, d_ff, d), exp_map3),
                  pl.BlockSpec((1, 1, d), exp_map3)],
        out_specs=pl.BlockSpec((MOE_ROWS, words), row_map),
        scratch_shapes=[pltpu.VMEM((d, 2 * d_ff), BF16), pltpu.VMEM((d_ff, d), BF16)],
    )
    return pl.pallas_call(
        functools.partial(_expert_kernel, d_ff=d_ff),
        grid_spec=grid_spec,
        out_shape=jax.ShapeDtypeStruct((cap, words), I32),
        compiler_params=pltpu.CompilerParams(dimension_semantics=("arbitrary",),
                                             vmem_limit_bytes=VMEM_LIMIT),
        name="experts",
    )(blk_expert, blk_index, blk_new, xs, wup, bup, wdn, bdn)


def _route_tables(route, totals, *, cap):
    n = route.shape[0]
    expert = route[:, 0:TOP_K].astype(I32)
    rank = route[:, TOP_K:2 * TOP_K].astype(I32)
    counts = totals[0, :N_EXPERTS].astype(I32)
    padded = (counts + MOE_ROWS - 1) // MOE_ROWS * MOE_ROWS
    pad_end = jnp.cumsum(padded)
    pad_start = pad_end - padded
    dest = pad_start[expert] + rank
    dest_chunks = dest.reshape(n // SC_SCATTER_ROWS, SC_SCATTER_ROWS, TOP_K).transpose(0, 2, 1)
    n_blk = cap // MOE_ROWS
    used = pad_end[-1] // MOE_ROWS
    blk = jnp.minimum(jnp.arange(n_blk, dtype=I32), used - 1)
    blk_expert = jnp.minimum(jnp.sum(pad_end[None, :] <= (blk * MOE_ROWS)[:, None], axis=1),
                             N_EXPERTS - 1).astype(I32)
    blk_new = jnp.concatenate([jnp.ones((1,), I32), (blk_expert[1:] != blk_expert[:-1]).astype(I32)])
    return dest, dest_chunks, blk_expert, blk, blk_new


def _final_kernel(x1_ref, ya_ref, yb_ref, yc_ref, yd_ref, route_ref, p_ref, g1_ref, b1_ref,
                  wp_ref, wg_ref, g2_ref, b2_ref, *rest, alpha):
    o_ref = rest[-1]
    ffn = jnp.zeros(x1_ref.shape, F32)
    for k, yk_ref in enumerate((ya_ref, yb_ref, yc_ref, yd_ref)):
        gate_k = route_ref[:, 2 * TOP_K + k:2 * TOP_K + k + 1]
        ffn = ffn + gate_k * _unpack_words(yk_ref[...], F32)
    x2 = _layer_norm(alpha * x1_ref[...] + ffn, g1_ref[...], b1_ref[...])
    gate = _sigmoid(_dot(x2.astype(BF16), wg_ref[...]))
    ple = _dot(p_ref[...].astype(BF16), wp_ref[...]) * gate
    o_ref[...] = _layer_norm(alpha * x2 + ple, g2_ref[...], b2_ref[...])


def _final(x1, y4w, route, p2, g1, b1, wp, wg, g2, b2, *, alpha, tm, part, parts, prev=None):
    n, d = x1.shape
    assert TOP_K == 4
    steps = n // parts // tm
    off = part * steps
    rows = lambda w: pl.BlockSpec((tm, w), lambda i: (off + i, 0))
    full = lambda shape: pl.BlockSpec(shape, lambda i: (0,) * len(shape))
    pick = lambda k: pl.BlockSpec((tm, d // 2), lambda i: (k * steps + i, 0))
    in_specs = [rows(d), pick(0), pick(1), pick(2), pick(3), rows(LANES), rows(p2.shape[1]),
                full(g1.shape), full(b1.shape),
                full(wp.shape), full(wg.shape), full(g2.shape), full(b2.shape)]
    args = [x1, y4w, y4w, y4w, y4w, route, p2, g1, b1, wp, wg, g2, b2]
    aliases = {}
    if prev is not None:
        in_specs.append(pl.BlockSpec(memory_space=pl.ANY))
        args.append(prev)
        aliases = {len(args) - 1: 0}
    return pl.pallas_call(
        functools.partial(_final_kernel, alpha=alpha),
        grid=(steps,),
        in_specs=in_specs,
        out_specs=rows(d),
        out_shape=jax.ShapeDtypeStruct((n, d), F32),
        input_output_aliases=aliases,
        compiler_params=pltpu.CompilerParams(dimension_semantics=("arbitrary",),
                                             vmem_limit_bytes=VMEM_LIMIT),
        name="final",
    )(*args)


def _pack_w_in(w_in):
    d = w_in.shape[0]
    sizes = (ATT_HEADS * ATT_HEAD_DIM, KV_LATENT, IDX_HEADS * IDX_DIM, IDX_DIM, IDX_HEADS,
             HG_HEADS * HG_DIM, HG_HEADS * HG_DIM, HG_HEADS * HG_DIM, HG_HEADS * HG_DIM)
    offs = [0]
    for s in sizes:
        offs.append(offs[-1] + s)
    sec = [w_in[:, offs[k]:offs[k + 1]] for k in range(len(sizes))]
    iq = sec[2].reshape(d, IDX_HEADS, IDX_DIM)
    iq = jnp.pad(iq, ((0, 0), (0, 0), (0, LANES - IDX_DIM))).reshape(d, IDX_HEADS * LANES)
    ikw = jnp.pad(jnp.concatenate([sec[3], sec[4]], axis=1), ((0, 0), (0, LANES - IDX_DIM - IDX_HEADS)))
    w_n = jnp.concatenate([sec[1], ikw, sec[5], sec[6], sec[7], sec[8]], axis=1).astype(BF16)
    w_t = jnp.concatenate([sec[0], iq, ikw, sec[1]], axis=1).T.astype(BF16)
    return w_n, w_t


def _block_diag_uk_t(w_uk):
    eye = jnp.eye(ATT_HEADS, dtype=w_uk.dtype)
    bd = jnp.einsum("rhd,hg->hrgd", w_uk, eye)
    return bd.reshape(ATT_HEADS * KV_LATENT, ATT_HEADS * ATT_HEAD_DIM).astype(BF16)


def _uv_t(w_uv):
    return jnp.transpose(w_uv, (1, 2, 0)).astype(BF16)


def _layer(x, p_l, w_in, kv_g, ik_g, ik_b, w_uk, w_uv, lb, hg_ng, w_o, ln_mix_g, ln_mix_b,
           w_router, b_router, w_up, b_up, w_down, b_down, ln_ffn_g, ln_ffn_b,
           w_ple_proj, w_ple_gate, ln_ple_g, ln_ple_b, *, alpha, tm, hg_tb):
    bsz, t, d = x.shape
    n = bsz * t
    x2 = x.reshape(n, d)
    row = lambda v: v.reshape(1, -1).astype(F32)
    pad_lane = lambda v: jnp.pad(row(v), ((0, 0), (0, LANES - v.shape[-1])))

    w_n, w_t = _pack_w_in(w_in)
    qt, iqt, iwt, c, ct, ik, hg = _inproj(x2, w_n, w_t, _block_diag_uk_t(w_uk), row(kv_g),
                                          kv_g.reshape(-1, 1).astype(F32), pad_lane(ik_g), pad_lane(ik_b), tm=tm)
    y_rec = _hgrn2(hg.reshape(bsz, t, -1), row(lb), row(hg_ng), tb=hg_tb)
    y_att = _dsa(qt, iqt, iwt, ik, c, ct, _uv_t(w_uv), bsz=bsz, t=t)

    wr_f = jnp.pad(w_router.astype(F32), ((0, 0), (0, LANES - N_EXPERTS)))
    wr_hi = wr_f.astype(BF16)
    wr_p = jnp.concatenate([wr_hi, (wr_f - wr_hi.astype(F32)).astype(BF16)], axis=1)
    x1, x1w, route, totals = _mix_router(
        x2, y_att, y_rec.reshape(n, -1), w_o.astype(BF16), row(ln_mix_g), row(ln_mix_b),
        wr_p, pad_lane(b_router), alpha=alpha, tm=tm)

    step = SC_WORKERS * SC_ROWS
    cap = -(-(n * TOP_K + N_EXPERTS * MOE_ROWS) // step) * step
    dest, dest_chunks, blk_expert, blk_index, blk_new = _route_tables(route, totals, cap=cap)
    d_ff = w_down.shape[1]
    xs = _sc_scatter_rows(x1w, dest_chunks, cap)
    outw = _experts(blk_expert, blk_index, blk_new, xs,
                    w_up.astype(F32), b_up.reshape(N_EXPERTS, 1, 2 * d_ff).astype(F32),
                    w_down.astype(F32), b_down.reshape(N_EXPERTS, 1, d).astype(F32))

    parts = COMBINE_PARTS if n % (COMBINE_PARTS * max(tm, SC_WORKERS * SC_ROWS // TOP_K)) == 0 else 1
    out = None
    for part in range(parts):
        rows_p = slice(part * (n // parts), (part + 1) * (n // parts))
        y4w = _sc_gather_rows(outw, dest[rows_p].T.reshape(-1))
        out = _final(x1, y4w, route, p_l.reshape(n, -1), row(ln_ffn_g), row(ln_ffn_b),
                     w_ple_proj.astype(BF16), w_ple_gate.astype(BF16), row(ln_ple_g), row(ln_ple_b),
                     alpha=alpha, tm=tm, part=part, parts=parts, prev=out)
    return out.reshape(bsz, t, d)


def kernel(x, p, w_in, kv_norm_g, idx_k_norm_g, idx_k_norm_b, w_uk, w_uv, hg_lb_logits, hg_norm_g, w_o,
           ln_mix_g, ln_mix_b, w_router, b_router, w_up, b_up, w_down, b_down, ln_ffn_g, ln_ffn_b,
           w_ple_proj, w_ple_gate, ln_ple_g, ln_ple_b):
    depth = w_in.shape[0]
    alpha = (2.0 * depth) ** 0.25
    lower_bounds = jnp.cumsum(jax.nn.softmax(hg_lb_logits.astype(F32), axis=0), axis=0)
    n = x.shape[0] * x.shape[1]
    tm = min(512, n)
    hg_tb = min(512, x.shape[1])
    for l in range(depth):
        x = _layer(x, p[l], w_in[l], kv_norm_g[l], idx_k_norm_g[l], idx_k_norm_b[l], w_uk[l], w_uv[l],
                   lower_bounds[l], hg_norm_g[l], w_o[l], ln_mix_g[l], ln_mix_b[l], w_router[l],
                   b_router[l], w_up[l], b_up[l], w_down[l], b_down[l], ln_ffn_g[l], ln_ffn_b[l],
                   w_ple_proj[l], w_ple_gate[l], ln_ple_g[l], ln_ple_b[l],
                   alpha=alpha, tm=tm, hg_tb=hg_tb)
    return x
```

```python
import functools

import jax
import jax.numpy as jnp
from jax import lax
from jax.experimental import pallas as pl
from jax.experimental.pallas import tpu as pltpu
from jax.experimental.pallas import tpu_sc as plsc

F32 = jnp.float32
BF16 = jnp.bfloat16
I32 = jnp.int32

ATT_HEADS = 8
ATT_HEAD_DIM = 64
KV_LATENT = 256
IDX_HEADS = 8
IDX_DIM = 64
TOPK_MAX = 256
HG_HEADS = 4
HG_DIM = 128
HG_CHUNK = 32
N_EXPERTS = 32
TOP_K = 4
SWIGLU_LIMIT = 7.0
SWIGLU_ALPHA = 1.702
LN_EPS = 1e-5
RMS_EPS = 1e-6

LANES = 128
Q_ROWS = 128
KEY_CHUNK = 512
COUNT_ROWS = 128
SEARCH_BITS = 20
EXTRACT_MAX = 4
MOE_ROWS = 512
SC_CORES = 2
SC_SUBCORES = 16
SC_WORKERS = SC_CORES * SC_SUBCORES
SC_ROWS = 128
SC_SCATTER_ROWS = 128
ROUTER_ROWS = 512
COMBINE_PARTS = 2
VMEM_LIMIT = 56 * 1024 * 1024

INT_MIN = -(2 ** 31)
NEG_BIG = -1e30
LOG2_E = 1.4426950408889634

_N_KV = 0
_N_IKW = _N_KV + KV_LATENT
_N_HG = _N_IKW + LANES
_N_END = _N_HG + 4 * HG_HEADS * HG_DIM
_T_AQ = 0
_T_IQ = _T_AQ + ATT_HEADS * ATT_HEAD_DIM
_T_IKW = _T_IQ + IDX_HEADS * LANES
_T_KV = _T_IKW + LANES
_T_END = _T_KV + KV_LATENT


def _dot(a, b):
    return jnp.dot(a, b, preferred_element_type=F32)


def _dot_nt(a, b):
    return lax.dot_general(a, b, (((1,), (1,)), ((), ())), preferred_element_type=F32)


def _dot_tn(a, b):
    return lax.dot_general(a, b, (((0,), (0,)), ((), ())), preferred_element_type=F32)


def _layer_norm(z, g, b):
    mu = jnp.mean(z, axis=-1, keepdims=True)
    d = z - mu
    var = jnp.mean(d * d, axis=-1, keepdims=True)
    return d * lax.rsqrt(var + LN_EPS) * g + b


def _sigmoid(x):
    return 1.0 / (1.0 + jnp.exp(-x))


def _inproj_kernel(x_ref, wn_ref, wt_ref, wukt_ref, kvg_ref, kvgc_ref, ikg_ref, ikb_ref,
                   qt_ref, iqt_ref, iwt_ref, c_ref, ct_ref, ik_ref, hg_ref):
    xb = x_ref[...].astype(BF16)

    aqt = _dot_nt(wt_ref[_T_AQ:_T_IQ, :], xb).astype(BF16)
    iqt_ref[...] = (_dot_nt(wt_ref[_T_IQ:_T_IKW, :], xb) * (IDX_DIM ** -0.5)).astype(BF16)
    iwt_ref[...] = _dot_nt(wt_ref[_T_IKW:_T_KV, :], xb) * (IDX_HEADS ** -0.5)
    act = _dot_nt(wt_ref[_T_KV:_T_END, :], xb)
    ct = act * lax.rsqrt(jnp.mean(act * act, axis=0, keepdims=True) + RMS_EPS) * kvgc_ref[...]
    ct_ref[...] = ct.astype(BF16)
    qt_ref[...] = (_dot(wukt_ref[...], aqt) * (LOG2_E * ATT_HEAD_DIM ** -0.5)).astype(BF16)

    ac = _dot(xb, wn_ref[:, _N_KV:_N_IKW])
    c = ac * lax.rsqrt(jnp.mean(ac * ac, axis=-1, keepdims=True) + RMS_EPS) * kvg_ref[...]
    c_ref[...] = c.astype(BF16)

    ikw = _dot(xb, wn_ref[:, _N_IKW:_N_HG])
    lane = lax.broadcasted_iota(I32, ikw.shape, 1)
    is_k = lane < IDX_DIM
    mu = jnp.sum(jnp.where(is_k, ikw, 0.0), axis=-1, keepdims=True) * (1.0 / IDX_DIM)
    d = jnp.where(is_k, ikw - mu, 0.0)
    var = jnp.sum(d * d, axis=-1, keepdims=True) * (1.0 / IDX_DIM)
    ik = d * lax.rsqrt(var + LN_EPS) * ikg_ref[...] + ikb_ref[...]
    ik_ref[...] = ik.astype(BF16)

    hg_ref[...] = _dot(xb, wn_ref[:, _N_HG:_N_END])


def _inproj(x2, w_n, w_t, wuk_t, kvg, kvg_col, ikg, ikb, *, tm):
    n, d = x2.shape
    grid = (n // tm,)
    full = lambda shape: pl.BlockSpec(shape, lambda i: (0,) * len(shape))
    rows = lambda w: pl.BlockSpec((tm, w), lambda i: (i, 0))
    cols = lambda h: pl.BlockSpec((h, tm), lambda i: (0, i))
    return pl.pallas_call(
        _inproj_kernel,
        grid=grid,
        in_specs=[rows(d), full(w_n.shape), full(w_t.shape), full(wuk_t.shape), full(kvg.shape),
                  full(kvg_col.shape), full(ikg.shape), full(ikb.shape)],
        out_specs=[cols(ATT_HEADS * KV_LATENT), cols(IDX_HEADS * LANES), cols(LANES),
                   rows(KV_LATENT), cols(KV_LATENT), rows(LANES), rows(4 * HG_HEADS * HG_DIM)],
        out_shape=[jax.ShapeDtypeStruct((ATT_HEADS * KV_LATENT, n), BF16),
                   jax.ShapeDtypeStruct((IDX_HEADS * LANES, n), BF16),
                   jax.ShapeDtypeStruct((LANES, n), F32),
                   jax.ShapeDtypeStruct((n, KV_LATENT), BF16),
                   jax.ShapeDtypeStruct((KV_LATENT, n), BF16),
                   jax.ShapeDtypeStruct((n, LANES), BF16),
                   jax.ShapeDtypeStruct((n, 4 * HG_HEADS * HG_DIM), F32)],
        compiler_params=pltpu.CompilerParams(dimension_semantics=("arbitrary",),
                                             vmem_limit_bytes=VMEM_LIMIT),
        name="inproj",
    )(x2, w_n, w_t, wuk_t, kvg, kvg_col, ikg, ikb)


def _hgrn2_kernel(hg_ref, lb_ref, ng_ref, o_ref, state_ref, *, tb):
    @pl.when(pl.program_id(1) == 0)
    def _():
        state_ref[...] = jnp.zeros_like(state_ref)

    nchunk = tb // HG_CHUNK
    width = HG_HEADS * HG_DIM
    row = lax.broadcasted_iota(I32, (tb, HG_DIM), 0)
    rin = jnp.bitwise_and(row, HG_CHUNK - 1)
    r2 = lax.broadcasted_iota(I32, (tb, tb), 0)
    c2 = lax.broadcasted_iota(I32, (tb, tb), 1)
    intra = jnp.logical_and(r2 // HG_CHUNK == c2 // HG_CHUNK, c2 <= r2)

    for h in range(HG_HEADS):
        sl = slice(h * HG_DIM, (h + 1) * HG_DIM)
        gq = hg_ref[0, :, h * HG_DIM:(h + 1) * HG_DIM]
        gf = hg_ref[0, :, width + h * HG_DIM:width + (h + 1) * HG_DIM]
        gi = hg_ref[0, :, 2 * width + h * HG_DIM:2 * width + (h + 1) * HG_DIM]
        gg = hg_ref[0, :, 3 * width + h * HG_DIM:3 * width + (h + 1) * HG_DIM]
        lb = lb_ref[:, sl]
        forget = lb + (1.0 - lb) * _sigmoid(gf)
        logf = jnp.log(forget)
        b = logf
        s = 1
        while s < HG_CHUNK:
            b = b + jnp.where(rin >= s, pltpu.roll(b, s, axis=0), 0.0)
            s *= 2
        tot = jnp.sum(logf.reshape(nchunk, HG_CHUNK, HG_DIM), axis=1, keepdims=True)
        bl = jnp.broadcast_to(tot, (nchunk, HG_CHUNK, HG_DIM)).reshape(tb, HG_DIM)
        kk = 1.0 - forget
        q_dec = (gq * _sigmoid(gq) * jnp.exp(b)).astype(BF16)
        k_inv = (kk * jnp.exp(-b)).astype(BF16)
        k_end = (kk * jnp.exp(bl - b)).astype(BF16)
        vb = gi.astype(BF16)
        scores = jnp.where(intra, _dot_nt(q_dec, k_inv), 0.0).astype(BF16)
        o = _dot(scores, vb)
        decay = jnp.exp(tot)
        chunk_rows = [slice(n * HG_CHUNK, (n + 1) * HG_CHUNK) for n in range(nchunk)]
        incs = [_dot_tn(vb[rs], k_end[rs]) for rs in chunk_rows]
        st = state_ref[h]
        starts = []
        for n in range(nchunk):
            starts.append(st.astype(BF16))
            st = st * decay[n] + incs[n]
        state_ref[h] = st
        inter = [_dot_nt(q_dec[rs], starts[n]) for n, rs in enumerate(chunk_rows)]
        o = o + jnp.concatenate(inter, axis=0)
        o = o * lax.rsqrt(jnp.mean(o * o, axis=-1, keepdims=True) + RMS_EPS)
        o = o * ng_ref[:, sl] * (gg * _sigmoid(gg))
        o_ref[0, :, sl] = o.astype(o_ref.dtype)


def _hgrn2(hg3, lb, ng, *, tb):
    bsz, t, w4 = hg3.shape
    width = HG_HEADS * HG_DIM
    return pl.pallas_call(
        functools.partial(_hgrn2_kernel, tb=tb),
        grid=(bsz, t // tb),
        in_specs=[pl.BlockSpec((1, tb, w4), lambda b, i: (b, i, 0)),
                  pl.BlockSpec((1, width), lambda b, i: (0, 0)),
                  pl.BlockSpec((1, width), lambda b, i: (0, 0))],
        out_specs=pl.BlockSpec((1, tb, width), lambda b, i: (b, i, 0)),
        out_shape=jax.ShapeDtypeStruct((bsz, t, width), BF16),
        scratch_shapes=[pltpu.VMEM((HG_HEADS, HG_DIM, HG_DIM), F32)],
        compiler_params=pltpu.CompilerParams(dimension_semantics=("arbitrary", "arbitrary"),
                                             vmem_limit_bytes=VMEM_LIMIT),
        name="hgrn2",
    )(hg3, lb, ng)


def _dsa_kernel(qt_ref, iqt_ref, iwt_ref, ik_ref, c_ref, ct_ref, wuvt_ref, o_ref,
                keys_ref, cut_ref, thr_ref, redo_ref, qall_ref, iqall_ref, m_ref, l_ref, acc_ref,
                *, topk, pos_bits):
    n = pl.program_id(1)
    t0 = n * Q_ROWS
    nch = (t0 + Q_ROWS + KEY_CHUNK - 1) // KEY_CHUNK
    kf = float(topk)
    key_i = lax.broadcasted_iota(I32, (KEY_CHUNK, Q_ROWS), 0)
    tq = t0 + lax.broadcasted_iota(I32, (KEY_CHUNK, Q_ROWS), 1)
    pair = 2 * Q_ROWS

    for h in range(ATT_HEADS):
        qall_ref[:, h * Q_ROWS:(h + 1) * Q_ROWS] = qt_ref[h * KV_LATENT:(h + 1) * KV_LATENT, :]
    for h in range(IDX_HEADS):
        iqall_ref[:, h * Q_ROWS:(h + 1) * Q_ROWS] = iqt_ref[h * LANES:(h + 1) * LANES, :]
    head_w = [iwt_ref[IDX_DIM + h:IDX_DIM + h + 1, :] for h in range(IDX_HEADS)]

    def index_body(j, carry):
        k0 = pl.multiple_of(j * KEY_CHUNK, KEY_CHUNK)
        ikc = ik_ref[pl.ds(k0, KEY_CHUNK), :]
        isc = jnp.zeros((KEY_CHUNK, Q_ROWS), F32)
        for hp in range(IDX_HEADS // 2):
            z = _dot(ikc, iqall_ref[:, hp * pair:(hp + 1) * pair])
            for hh in range(2):
                isc = isc + head_w[2 * hp + hh] * jnp.maximum(z[:, hh * Q_ROWS:(hh + 1) * Q_ROWS], 0.0)
        isc = jnp.where(isc == 0.0, 0.0, isc)
        bits = pltpu.bitcast(isc, I32)
        key = jnp.bitwise_xor(bits, jnp.bitwise_and(jnp.right_shift(bits, 31), 0x7FFFFFFF))
        key = jnp.where(k0 + key_i <= tq, key, INT_MIN)
        keys_ref[pl.ds(k0, KEY_CHUNK), :] = key
        return carry

    lax.fori_loop(0, nch, index_body, 0)

    def count(pred):
        def body(j, acc):
            k0 = pl.multiple_of(j * KEY_CHUNK, KEY_CHUNK)
            hit = jnp.where(pred(keys_ref[pl.ds(k0, KEY_CHUNK), :], k0), 1.0, 0.0)
            return acc + jnp.sum(hit.reshape(KEY_CHUNK // COUNT_ROWS, COUNT_ROWS, Q_ROWS), axis=0)
        acc = lax.fori_loop(0, nch, body, jnp.zeros((COUNT_ROWS, Q_ROWS), F32))
        return jnp.sum(acc, axis=0, keepdims=True)

    def count_ge(cand):
        return count(lambda k, k0: k >= cand)

    def masked_max(bound):
        def body(j, acc):
            k0 = pl.multiple_of(j * KEY_CHUNK, KEY_CHUNK)
            key = keys_ref[pl.ds(k0, KEY_CHUNK), :]
            kept = jnp.where(key <= bound, key, INT_MIN)
            return jnp.maximum(acc, jnp.max(kept.reshape(KEY_CHUNK // COUNT_ROWS, COUNT_ROWS, Q_ROWS), axis=0))
        acc = lax.fori_loop(0, nch, body, jnp.full((COUNT_ROWS, Q_ROWS), INT_MIN, I32))
        return jnp.max(acc, axis=0, keepdims=True)

    n_all = (nch * KEY_CHUNK).astype(F32)
    n0 = count_ge(jnp.zeros((1, Q_ROWS), I32))
    pos0 = n0 >= kf
    state0 = (jnp.where(pos0, 0, INT_MIN).astype(I32), jnp.where(pos0, n0, n_all), jnp.where(pos0, 0.0, n0))

    def bracket_body(i, state):
        thr, n_lo, n_hi = state
        cand = thr + lax.shift_left(jnp.int32(1), 30 - i)
        cnt = count_ge(cand)
        ge = cnt >= kf
        return jnp.where(ge, cand, thr), jnp.where(ge, cnt, n_lo), jnp.where(ge, n_hi, cnt)

    thr1, n_lo, n_hi = lax.fori_loop(0, SEARCH_BITS, bracket_body, state0)
    low_bits = 31 - SEARCH_BITS
    has_topk = tq[0:1, :] + 1 >= topk
    in_bracket = n_lo - n_hi
    exact_cut = n_lo == kf
    few = jnp.logical_or(jnp.logical_not(has_topk), jnp.logical_or(exact_cut, in_bracket <= float(EXTRACT_MAX)))
    thr_ref[...] = jnp.broadcast_to(thr1, thr_ref.shape)
    redo_ref[0] = jnp.where(jnp.min(jnp.where(few, 1.0, 0.0)) > 0.0, 0, 1)

    @pl.when(redo_ref[0] == 0)
    def _():
        need = kf - n_hi
        bound = thr1 + (2 ** low_bits - 1)
        found = jnp.zeros((1, Q_ROWS), F32)
        thr_x = thr1
        for j in range(1, EXTRACT_MAX + 1):
            cur = masked_max(bound)
            inside = cur >= thr1
            found = found + jnp.where(inside, 1.0, 0.0)
            thr_x = jnp.where(jnp.logical_and(inside, need == float(j)), cur, thr_x)
            bound = jnp.where(inside, cur - 1, INT_MIN)
        use_x = jnp.logical_and(has_topk, jnp.logical_not(exact_cut))
        clash = jnp.logical_and(use_x, found != in_bracket)
        thr_ref[...] = jnp.broadcast_to(jnp.where(use_x, thr_x, thr1), thr_ref.shape)
        redo_ref[0] = jnp.where(jnp.max(jnp.where(clash, 1.0, 0.0)) > 0.0, 1, 0)

    @pl.when(redo_ref[0] == 1)
    def _():
        def bit_body(i, thr):
            cand = thr + lax.shift_left(jnp.int32(1), low_bits - 1 - i)
            return jnp.where(count_ge(cand) >= kf, cand, thr)

        thr_ref[...] = jnp.broadcast_to(lax.fori_loop(0, low_bits, bit_body, thr1), thr_ref.shape)

    thr = thr_ref[0:1, :]
    live = thr > INT_MIN
    cut_ref[...] = jnp.broadcast_to(jnp.where(live, jnp.int32(2 ** 30), -1), cut_ref.shape)
    n_ge = count_ge(thr)
    tie = jnp.max(jnp.where(jnp.logical_and(live, n_ge > kf), 1.0, 0.0))

    @pl.when(tie > 0.0)
    def _():
        need = kf - count(lambda k, k0: k > thr)

        def cut_body(i, cut):
            cand = cut + lax.shift_left(jnp.int32(1), pos_bits - 1 - i)
            below = count(lambda k, k0: jnp.logical_and(k == thr, k0 + key_i < cand))
            return jnp.where(below < need, cand, cut)

        cut = lax.fori_loop(0, pos_bits, cut_body, jnp.zeros((1, Q_ROWS), I32))
        cut_ref[...] = jnp.broadcast_to(jnp.where(live, cut, -1), cut_ref.shape)

    m_ref[...] = jnp.full(m_ref.shape, NEG_BIG, F32)
    l_ref[...] = jnp.zeros(l_ref.shape, F32)
    acc_ref[...] = jnp.zeros(acc_ref.shape, F32)
    cut_b = cut_ref[0:1, :]

    def attn_body(j, carry):
        k0 = pl.multiple_of(j * KEY_CHUNK, KEY_CHUNK)
        key = keys_ref[pl.ds(k0, KEY_CHUNK), :]
        pos = k0 + key_i
        sel = jnp.logical_or(key > thr, jnp.logical_and(key == thr, pos <= cut_b))
        rel = jnp.where(sel, (pos - t0).astype(F32), NEG_BIG)
        cc = c_ref[pl.ds(k0, KEY_CHUNK), :]
        cct = ct_ref[:, pl.ds(k0, KEY_CHUNK)]
        n_pairs = ATT_HEADS // 2
        scores = [_dot(cc, qall_ref[:, g * pair:(g + 1) * pair]) for g in range(n_pairs)]
        for hp in range(n_pairs):
            cs = slice(hp * pair, (hp + 1) * pair)
            s2 = scores[hp]
            ps, alphas = [], []
            for hh in range(2):
                h = 2 * hp + hh
                slope = LOG2_E * 2.0 ** (-(8.0 / ATT_HEADS) * (h + 1))
                s = s2[:, hh * Q_ROWS:(hh + 1) * Q_ROWS] + slope * rel
                m_old = m_ref[h:h + 1, :]
                m_new = jnp.maximum(m_old, jnp.max(s, axis=0, keepdims=True))
                alpha = jnp.exp2(m_old - m_new)
                p = jnp.exp2(s - m_new)
                l_ref[h:h + 1, :] = alpha * l_ref[h:h + 1, :] + jnp.sum(p, axis=0, keepdims=True)
                m_ref[h:h + 1, :] = m_new
                ps.append(p.astype(BF16))
                alphas.append(alpha)
            acc_ref[:, cs] = (acc_ref[:, cs] * jnp.concatenate(alphas, axis=1)
                              + _dot(cct, jnp.concatenate(ps, axis=1)))
        return carry

    lax.fori_loop(0, nch, attn_body, 0)

    yt = []
    for h in range(ATT_HEADS):
        o_lat = (acc_ref[:, h * Q_ROWS:(h + 1) * Q_ROWS] / l_ref[h:h + 1, :]).astype(BF16)
        yt.append(_dot(wuvt_ref[h], o_lat))
    o_ref[...] = jnp.concatenate(yt, axis=0).T.astype(o_ref.dtype)


def _dsa(qt, iqt, iwt, ik, c, ct, wuvt_p, *, bsz, t):
    n = bsz * t
    nq = t // Q_ROWS
    topk = min(TOPK_MAX, t // 4)
    assert t % KEY_CHUNK == 0
    width = ATT_HEADS * ATT_HEAD_DIM
    qcols = lambda h: pl.BlockSpec((h, Q_ROWS), lambda b, i: (0, b * nq + i))
    return pl.pallas_call(
        functools.partial(_dsa_kernel, topk=topk, pos_bits=(t - 1).bit_length()),
        grid=(bsz, nq),
        in_specs=[qcols(ATT_HEADS * KV_LATENT), qcols(IDX_HEADS * LANES), qcols(LANES),
                  pl.BlockSpec((t, LANES), lambda b, i: (b, 0)),
                  pl.BlockSpec((t, KV_LATENT), lambda b, i: (b, 0)),
                  pl.BlockSpec((KV_LATENT, t), lambda b, i: (0, b)),
                  pl.BlockSpec(wuvt_p.shape, lambda b, i: (0, 0, 0))],
        out_specs=pl.BlockSpec((Q_ROWS, width), lambda b, i: (b * nq + i, 0)),
        out_shape=jax.ShapeDtypeStruct((n, width), BF16),
        scratch_shapes=[pltpu.VMEM((t, Q_ROWS), I32),
                        pltpu.VMEM((8, Q_ROWS), I32),
                        pltpu.VMEM((8, Q_ROWS), I32),
                        pltpu.SMEM((1,), I32),
                        pltpu.VMEM((KV_LATENT, ATT_HEADS * Q_ROWS), BF16),
                        pltpu.VMEM((LANES, IDX_HEADS * Q_ROWS), BF16),
                        pltpu.VMEM((8, Q_ROWS), F32),
                        pltpu.VMEM((8, Q_ROWS), F32),
                        pltpu.VMEM((KV_LATENT, ATT_HEADS * Q_ROWS), F32)],
        compiler_params=pltpu.CompilerParams(dimension_semantics=("arbitrary", "arbitrary"),
                                             vmem_limit_bytes=VMEM_LIMIT),
        name="dsa",
    )(qt, iqt, iwt, ik, c, ct, wuvt_p)


def _mix_router_kernel(x_ref, ya_ref, yr_ref, wo_ref, g_ref, b_ref, wr_ref, br_ref,
                       x1_ref, x1w_ref, route_ref, tot_ref, carry_ref, *, alpha, tm):
    @pl.when(pl.program_id(0) == 0)
    def _():
        carry_ref[...] = jnp.zeros_like(carry_ref)

    half = ya_ref.shape[1]
    tr = min(tm, ROUTER_ROWS)
    lane = lax.broadcasted_iota(I32, (tr, LANES), 1)
    lane_f = lane.astype(F32)
    r2 = lax.broadcasted_iota(I32, (tr, tr), 0)
    c2 = lax.broadcasted_iota(I32, (tr, tr), 1)
    before = jnp.where(c2 < r2, 1.0, 0.0).astype(BF16)
    carry = carry_ref[...]

    for part in range(tm // tr):
        rows = slice(part * tr, (part + 1) * tr)
        mix = _dot(ya_ref[rows, :], wo_ref[0:half, :]) + _dot(yr_ref[rows, :], wo_ref[half:, :])
        x1 = _layer_norm(alpha * x_ref[rows, :] + mix, g_ref[...], b_ref[...])
        x1_ref[rows, :] = x1
        x1w_ref[rows, :] = _pack_words(x1)

        x_hi = x1.astype(BF16)
        x_lo = (x1 - x_hi.astype(F32)).astype(BF16)
        hh_hl = _dot(x_hi, wr_ref[...])
        logits = (hh_hl[:, :LANES] + hh_hl[:, LANES:] + _dot(x_lo, wr_ref[:, :LANES])) + br_ref[...]
        work = jnp.where(lane < N_EXPERTS, logits, -jnp.inf)
        sel_f = jnp.zeros(logits.shape, F32)
        denom = jnp.zeros((tr, 1), F32)
        hits, experts, weights = [], [], []
        for _ in range(TOP_K):
            mx = jnp.max(work, axis=-1, keepdims=True)
            first = jnp.min(jnp.where(work == mx, lane_f, float(LANES)), axis=-1, keepdims=True)
            hit = lane_f == first
            e = jnp.exp(mx - (weights[0][1] if weights else mx))
            hits.append(hit)
            experts.append(first)
            weights.append((e, mx))
            denom = denom + e
            sel_f = sel_f + jnp.where(hit, 1.0, 0.0)
            work = jnp.where(hit, -jnp.inf, work)

        rank = _dot(before, sel_f.astype(BF16)) + carry
        carry = carry + jnp.sum(sel_f, axis=0, keepdims=True)

        route = jnp.zeros(logits.shape, F32)
        for k in range(TOP_K):
            rank_k = jnp.sum(jnp.where(hits[k], rank, 0.0), axis=-1, keepdims=True)
            route = jnp.where(lane == k, experts[k], route)
            route = jnp.where(lane == TOP_K + k, rank_k, route)
            route = jnp.where(lane == 2 * TOP_K + k, weights[k][0] / denom, route)
        route_ref[rows, :] = route

    carry_ref[...] = carry
    tot_ref[...] = jnp.broadcast_to(carry, tot_ref.shape)


def _mix_router(x2, ya, yr, wo, g, b, wr_p, br_p, *, alpha, tm):
    n, d = x2.shape
    half = ya.shape[1]
    rows = lambda w: pl.BlockSpec((tm, w), lambda i: (i, 0))
    full = lambda shape: pl.BlockSpec(shape, lambda i: (0,) * len(shape))
    return pl.pallas_call(
        functools.partial(_mix_router_kernel, alpha=alpha, tm=tm),
        grid=(n // tm,),
        in_specs=[rows(d), rows(half), rows(half), full(wo.shape), full(g.shape), full(b.shape),
                  full(wr_p.shape), full(br_p.shape)],
        out_specs=[rows(d), rows(d // 2), rows(LANES), full((8, LANES))],
        out_shape=[jax.ShapeDtypeStruct((n, d), F32),
                   jax.ShapeDtypeStruct((n, d // 2), I32),
                   jax.ShapeDtypeStruct((n, LANES), F32),
                   jax.ShapeDtypeStruct((8, LANES), F32)],
        scratch_shapes=[pltpu.VMEM((1, LANES), F32)],
        compiler_params=pltpu.CompilerParams(dimension_semantics=("arbitrary",),
                                             vmem_limit_bytes=VMEM_LIMIT),
        name="mix_router",
    )(x2, ya, yr, wo, g, b, wr_p, br_p)


def _sc_gather_rows(table, idx):
    n_idx = idx.shape[0]
    width = table.shape[1]
    per_worker = n_idx // SC_WORKERS
    assert per_worker * SC_WORKERS == n_idx and per_worker % SC_ROWS == 0
    mesh = plsc.VectorSubcoreMesh(core_axis_name="c", subcore_axis_name="s",
                                  num_cores=SC_CORES, num_subcores=SC_SUBCORES)

    @functools.partial(
        pl.kernel, mesh=mesh,
        out_type=jax.ShapeDtypeStruct((n_idx, width), table.dtype),
        scratch_types=[pltpu.VMEM((SC_ROWS,), I32),
                       pltpu.VMEM((SC_ROWS, width), table.dtype),
                       pltpu.SemaphoreType.DMA])
    def gather(table_hbm, idx_hbm, out_hbm, idx_v, rows_v, sem):
        worker = lax.axis_index("s") * SC_CORES + lax.axis_index("c")

        @pl.loop(0, per_worker // SC_ROWS)
        def _(step):
            base = pl.multiple_of(worker * per_worker + step * SC_ROWS, SC_ROWS)
            pltpu.sync_copy(idx_hbm.at[pl.ds(base, SC_ROWS)], idx_v)
            pltpu.async_copy(table_hbm.at[idx_v], rows_v, sem).wait()
            pltpu.sync_copy(rows_v, out_hbm.at[pl.ds(base, SC_ROWS)])

    return gather(table, idx)


def _sc_scatter_rows(rows, dest3, cap):
    n, width = rows.shape
    chunks, picks, r = dest3.shape
    assert r == SC_SCATTER_ROWS and chunks * r == n and chunks % SC_WORKERS == 0
    per_worker = chunks // SC_WORKERS
    mesh = plsc.VectorSubcoreMesh(core_axis_name="c", subcore_axis_name="s",
                                  num_cores=SC_CORES, num_subcores=SC_SUBCORES)

    @functools.partial(
        pl.kernel, mesh=mesh,
        out_type=jax.ShapeDtypeStruct((cap, width), rows.dtype),
        scratch_types=[pltpu.VMEM((picks, r), I32),
                       pltpu.VMEM((r, width), rows.dtype),
                       pltpu.SemaphoreType.DMA])
    def scatter(rows_hbm, dest_hbm, out_hbm, idx_v, rows_v, sem):
        worker = lax.axis_index("s") * SC_CORES + lax.axis_index("c")

        @pl.loop(0, per_worker)
        def _(step):
            chunk = worker * per_worker + step
            base = pl.multiple_of(chunk * r, r)
            pltpu.sync_copy(rows_hbm.at[pl.ds(base, r)], rows_v)
            pltpu.sync_copy(dest_hbm.at[chunk], idx_v)
            for k in range(picks):
                pltpu.async_copy(rows_v, out_hbm.at[idx_v.at[k]], sem).wait()

    return scatter(rows, dest3)


def _pack_words(v):
    k = v.shape[1] // 2
    hi = pltpu.bitcast(v[:, :k].astype(BF16).astype(F32), I32)
    lo = pltpu.bitcast(v[:, k:].astype(BF16).astype(F32), I32)
    return jnp.bitwise_or(hi, lax.shift_right_logical(lo, jnp.int32(16)))


def _unpack_words(w, dtype=BF16):
    hi = pltpu.bitcast(jnp.bitwise_and(w, jnp.int32(-65536)), F32)
    lo = pltpu.bitcast(jnp.left_shift(w, 16), F32)
    return jnp.concatenate([hi, lo], axis=1).astype(dtype)


def _expert_kernel(exp_ref, blk_ref, new_ref, x_ref, wup_ref, bup_ref, wdn_ref, bdn_ref, o_ref,
                   wup_s, wdn_s, *, d_ff):
    del exp_ref, blk_ref

    @pl.when(new_ref[pl.program_id(0)] == 1)
    def _():
        wup_s[...] = wup_ref[0].astype(BF16)
        wdn_s[...] = wdn_ref[0].astype(BF16)

    half = MOE_ROWS // 2
    halves = [slice(r * half, (r + 1) * half) for r in range(2)]
    ups = [_dot(_unpack_words(x_ref[rows, :]), wup_s[...]) for rows in halves]
    for rows, up in zip(halves, ups):
        h = up + bup_ref[0]
        glu = jnp.minimum(h[:, :d_ff], SWIGLU_LIMIT)
        lin = jnp.clip(h[:, d_ff:], -SWIGLU_LIMIT, SWIGLU_LIMIT)
        a = glu * _sigmoid(SWIGLU_ALPHA * glu) * (lin + 1.0)
        o_ref[rows, :] = _pack_words(_dot(a.astype(BF16), wdn_s[...]) + bdn_ref[0])


def _experts(blk_expert, blk_index, blk_new, xs, wup, bup, wdn, bdn):
    cap, words = xs.shape
    d = 2 * words
    d_ff = wdn.shape[1]
    row_map = lambda i, exp, blk, new: (blk[i], 0)
    exp_map3 = lambda i, exp, blk, new: (exp[i], 0, 0)
    grid_spec = pltpu.PrefetchScalarGridSpec(
        num_scalar_prefetch=3,
        grid=(cap // MOE_ROWS,),
        in_specs=[pl.BlockSpec((MOE_ROWS, words), row_map),
                  pl.BlockSpec((1, d, 2 * d_ff), exp_map3),
                  pl.BlockSpec((1, 1, 2 * d_ff), exp_map3),
                  pl.BlockSpec((1, d_ff, d), exp_map3),
                  pl.BlockSpec((1, 1, d), exp_map3)],
        out_specs=pl.BlockSpec((MOE_ROWS, words), row_map),
        scratch_shapes=[pltpu.VMEM((d, 2 * d_ff), BF16), pltpu.VMEM((d_ff, d), BF16)],
    )
    return pl.pallas_call(
        functools.partial(_expert_kernel, d_ff=d_ff),
        grid_spec=grid_spec,
        out_shape=jax.ShapeDtypeStruct((cap, words), I32),
        compiler_params=pltpu.CompilerParams(dimension_semantics=("arbitrary",),
                                             vmem_limit_bytes=VMEM_LIMIT),
        name="experts",
    )(blk_expert, blk_index, blk_new, xs, wup, bup, wdn, bdn)


def _route_tables(route, totals, *, cap):
    n = route.shape[0]
    expert = route[:, 0:TOP_K].astype(I32)
    rank = route[:, TOP_K:2 * TOP_K].astype(I32)
    counts = totals[0, :N_EXPERTS].astype(I32)
    padded = (counts + MOE_ROWS - 1) // MOE_ROWS * MOE_ROWS
    pad_end = jnp.cumsum(padded)
    pad_start = pad_end - padded
    dest = pad_start[expert] + rank
    dest_chunks = dest.reshape(n // SC_SCATTER_ROWS, SC_SCATTER_ROWS, TOP_K).transpose(0, 2, 1)
    n_blk = cap // MOE_ROWS
    used = pad_end[-1] // MOE_ROWS
    blk = jnp.minimum(jnp.arange(n_blk, dtype=I32), used - 1)
    blk_expert = jnp.minimum(jnp.sum(pad_end[None, :] <= (blk * MOE_ROWS)[:, None], axis=1),
                             N_EXPERTS - 1).astype(I32)
    blk_new = jnp.concatenate([jnp.ones((1,), I32), (blk_expert[1:] != blk_expert[:-1]).astype(I32)])
    return dest, dest_chunks, blk_expert, blk, blk_new


def _final_kernel(x1_ref, ya_ref, yb_ref, yc_ref, yd_ref, route_ref, p_ref, g1_ref, b1_ref,
                  wp_ref, wg_ref, g2_ref, b2_ref, *rest, alpha):
    o_ref = rest[-1]
    ple_lin = _dot(p_ref[...].astype(BF16), wp_ref[...])
    ffn = jnp.zeros(x1_ref.shape, F32)
    for k, yk_ref in enumerate((ya_ref, yb_ref, yc_ref, yd_ref)):
        gate_k = route_ref[:, 2 * TOP_K + k:2 * TOP_K + k + 1]
        ffn = ffn + gate_k * _unpack_words(yk_ref[...], F32)
    x2 = _layer_norm(alpha * x1_ref[...] + ffn, g1_ref[...], b1_ref[...])
    gate = _sigmoid(_dot(x2.astype(BF16), wg_ref[...]))
    o_ref[...] = _layer_norm(alpha * x2 + ple_lin * gate, g2_ref[...], b2_ref[...])


def _final(x1, y4w, route, p2, g1, b1, wp, wg, g2, b2, *, alpha, tm, part, parts, prev=None):
    n, d = x1.shape
    assert TOP_K == 4
    steps = n // parts // tm
    off = part * steps
    rows = lambda w: pl.BlockSpec((tm, w), lambda i: (off + i, 0))
    full = lambda shape: pl.BlockSpec(shape, lambda i: (0,) * len(shape))
    pick = lambda k: pl.BlockSpec((tm, d // 2), lambda i: (k * steps + i, 0))
    in_specs = [rows(d), pick(0), pick(1), pick(2), pick(3), rows(LANES), rows(p2.shape[1]),
                full(g1.shape), full(b1.shape),
                full(wp.shape), full(wg.shape), full(g2.shape), full(b2.shape)]
    args = [x1, y4w, y4w, y4w, y4w, route, p2, g1, b1, wp, wg, g2, b2]
    aliases = {}
    if prev is not None:
        in_specs.append(pl.BlockSpec(memory_space=pl.ANY))
        args.append(prev)
        aliases = {len(args) - 1: 0}
    return pl.pallas_call(
        functools.partial(_final_kernel, alpha=alpha),
        grid=(steps,),
        in_specs=in_specs,
        out_specs=rows(d),
        out_shape=jax.ShapeDtypeStruct((n, d), F32),
        input_output_aliases=aliases,
        compiler_params=pltpu.CompilerParams(dimension_semantics=("arbitrary",),
                                             vmem_limit_bytes=VMEM_LIMIT),
        name="final",
    )(*args)


def _pack_w_in(w_in):
    d = w_in.shape[0]
    sizes = (ATT_HEADS * ATT_HEAD_DIM, KV_LATENT, IDX_HEADS * IDX_DIM, IDX_DIM, IDX_HEADS,
             HG_HEADS * HG_DIM, HG_HEADS * HG_DIM, HG_HEADS * HG_DIM, HG_HEADS * HG_DIM)
    offs = [0]
    for s in sizes:
        offs.append(offs[-1] + s)
    sec = [w_in[:, offs[k]:offs[k + 1]] for k in range(len(sizes))]
    iq = sec[2].reshape(d, IDX_HEADS, IDX_DIM)
    iq = jnp.pad(iq, ((0, 0), (0, 0), (0, LANES - IDX_DIM))).reshape(d, IDX_HEADS * LANES)
    ikw = jnp.pad(jnp.concatenate([sec[3], sec[4]], axis=1), ((0, 0), (0, LANES - IDX_DIM - IDX_HEADS)))
    w_n = jnp.concatenate([sec[1], ikw, sec[5], sec[6], sec[7], sec[8]], axis=1).astype(BF16)
    w_t = jnp.concatenate([sec[0], iq, ikw, sec[1]], axis=1).T.astype(BF16)
    return w_n, w_t


def _block_diag_uk_t(w_uk):
    eye = jnp.eye(ATT_HEADS, dtype=w_uk.dtype)
    bd = jnp.einsum("rhd,hg->hrgd", w_uk, eye)
    return bd.reshape(ATT_HEADS * KV_LATENT, ATT_HEADS * ATT_HEAD_DIM).astype(BF16)


def _uv_t(w_uv):
    return jnp.transpose(w_uv, (1, 2, 0)).astype(BF16)


def _layer(x, p_l, w_in, kv_g, ik_g, ik_b, w_uk, w_uv, lb, hg_ng, w_o, ln_mix_g, ln_mix_b,
           w_router, b_router, w_up, b_up, w_down, b_down, ln_ffn_g, ln_ffn_b,
           w_ple_proj, w_ple_gate, ln_ple_g, ln_ple_b, *, alpha, tm, hg_tb):
    bsz, t, d = x.shape
    n = bsz * t
    x2 = x.reshape(n, d)
    row = lambda v: v.reshape(1, -1).astype(F32)
    pad_lane = lambda v: jnp.pad(row(v), ((0, 0), (0, LANES - v.shape[-1])))

    w_n, w_t = _pack_w_in(w_in)
    qt, iqt, iwt, c, ct, ik, hg = _inproj(x2, w_n, w_t, _block_diag_uk_t(w_uk), row(kv_g),
                                          kv_g.reshape(-1, 1).astype(F32), pad_lane(ik_g), pad_lane(ik_b), tm=tm)
    y_rec = _hgrn2(hg.reshape(bsz, t, -1), row(lb), row(hg_ng), tb=hg_tb)
    y_att = _dsa(qt, iqt, iwt, ik, c, ct, _uv_t(w_uv), bsz=bsz, t=t)

    wr_f = jnp.pad(w_router.astype(F32), ((0, 0), (0, LANES - N_EXPERTS)))
    wr_hi = wr_f.astype(BF16)
    wr_p = jnp.concatenate([wr_hi, (wr_f - wr_hi.astype(F32)).astype(BF16)], axis=1)
    x1, x1w, route, totals = _mix_router(
        x2, y_att, y_rec.reshape(n, -1), w_o.astype(BF16), row(ln_mix_g), row(ln_mix_b),
        wr_p, pad_lane(b_router), alpha=alpha, tm=tm)

    step = SC_WORKERS * SC_ROWS
    cap = -(-(n * TOP_K + N_EXPERTS * MOE_ROWS) // step) * step
    dest, dest_chunks, blk_expert, blk_index, blk_new = _route_tables(route, totals, cap=cap)
    d_ff = w_down.shape[1]
    xs = _sc_scatter_rows(x1w, dest_chunks, cap)
    outw = _experts(blk_expert, blk_index, blk_new, xs,
                    w_up.astype(F32), b_up.reshape(N_EXPERTS, 1, 2 * d_ff).astype(F32),
                    w_down.astype(F32), b_down.reshape(N_EXPERTS, 1, d).astype(F32))

    parts = COMBINE_PARTS if n % (COMBINE_PARTS * max(tm, SC_WORKERS * SC_ROWS // TOP_K)) == 0 else 1
    out = None
    for part in range(parts):
        rows_p = slice(part * (n // parts), (part + 1) * (n // parts))
        y4w = _sc_gather_rows(outw, dest[rows_p].T.reshape(-1))
        out = _final(x1, y4w, route, p_l.reshape(n, -1), row(ln_ffn_g), row(ln_ffn_b),
                     w_ple_proj.astype(BF16), w_ple_gate.astype(BF16), row(ln_ple_g), row(ln_ple_b),
                     alpha=alpha, tm=tm, part=part, parts=parts, prev=out)
    return out.reshape(bsz, t, d)


def kernel(x, p, w_in, kv_norm_g, idx_k_norm_g, idx_k_norm_b, w_uk, w_uv, hg_lb_logits, hg_norm_g, w_o,
           ln_mix_g, ln_mix_b, w_router, b_router, w_up, b_up, w_down, b_down, ln_ffn_g, ln_ffn_b,
           w_ple_proj, w_ple_gate, ln_ple_g, ln_ple_b):
    depth = w_in.shape[0]
    alpha = (2.0 * depth) ** 0.25
    lower_bounds = jnp.cumsum(jax.nn.softmax(hg_lb_logits.astype(F32), axis=0), axis=0)
    n = x.shape[0] * x.shape[1]
    tm = min(512, n)
    hg_tb = min(512, x.shape[1])
    for l in range(depth):
        x = _layer(x, p[l], w_in[l], kv_norm_g[l], idx_k_norm_g[l], idx_k_norm_b[l], w_uk[l], w_uv[l],
                   lower_bounds[l], hg_norm_g[l], w_o[l], ln_mix_g[l], ln_mix_b[l], w_router[l],
                   b_router[l], w_up[l], b_up[l], w_down[l], b_down[l], ln_ffn_g[l], ln_ffn_b[l],
                   w_ple_proj[l], w_ple_gate[l], ln_ple_g[l], ln_ple_b[l],
                   alpha=alpha, tm=tm, hg_tb=hg_tb)
    return x
```

```python
import functools

import jax
import jax.numpy as jnp
from jax import lax
from jax.experimental import pallas as pl
from jax.experimental.pallas import tpu as pltpu
from jax.experimental.pallas import tpu_sc as plsc

F32 = jnp.float32
BF16 = jnp.bfloat16
I32 = jnp.int32

ATT_HEADS = 8
ATT_HEAD_DIM = 64
KV_LATENT = 256
IDX_HEADS = 8
IDX_DIM = 64
TOPK_MAX = 256
HG_HEADS = 4
HG_DIM = 128
HG_CHUNK = 32
N_EXPERTS = 32
TOP_K = 4
SWIGLU_LIMIT = 7.0
SWIGLU_ALPHA = 1.702
LN_EPS = 1e-5
RMS_EPS = 1e-6

LANES = 128
Q_ROWS = 128
KEY_CHUNK = 512
COUNT_ROWS = 128
SEARCH_BITS = 20
EXTRACT_MAX = 4
MOE_ROWS = 512
SC_CORES = 2
SC_SUBCORES = 16
SC_WORKERS = SC_CORES * SC_SUBCORES
SC_ROWS = 128
SC_SCATTER_ROWS = 128
COMBINE_PARTS = 4
VMEM_LIMIT = 56 * 1024 * 1024

INT_MIN = -(2 ** 31)
NEG_BIG = -1e30
LOG2_E = 1.4426950408889634

_N_KV = 0
_N_IKW = _N_KV + KV_LATENT
_N_HG = _N_IKW + LANES
_N_END = _N_HG + 4 * HG_HEADS * HG_DIM
_T_AQ = 0
_T_IQ = _T_AQ + ATT_HEADS * ATT_HEAD_DIM
_T_IKW = _T_IQ + IDX_HEADS * LANES
_T_KV = _T_IKW + LANES
_T_END = _T_KV + KV_LATENT


def _dot(a, b):
    return jnp.dot(a, b, preferred_element_type=F32)


def _dot_nt(a, b):
    return lax.dot_general(a, b, (((1,), (1,)), ((), ())), preferred_element_type=F32)


def _dot_tn(a, b):
    return lax.dot_general(a, b, (((0,), (0,)), ((), ())), preferred_element_type=F32)


def _layer_norm(z, g, b):
    mu = jnp.mean(z, axis=-1, keepdims=True)
    d = z - mu
    var = jnp.mean(d * d, axis=-1, keepdims=True)
    return d * lax.rsqrt(var + LN_EPS) * g + b


def _sigmoid(x):
    return 1.0 / (1.0 + jnp.exp(-x))


def _inproj_kernel(x_ref, wn_ref, wt_ref, wukt_ref, kvg_ref, kvgc_ref, ikg_ref, ikb_ref,
                   qt_ref, iqt_ref, iwt_ref, c_ref, ct_ref, ik_ref, hg_ref):
    xb = x_ref[...].astype(BF16)

    aqt = _dot_nt(wt_ref[_T_AQ:_T_IQ, :], xb).astype(BF16)
    iqt_ref[...] = (_dot_nt(wt_ref[_T_IQ:_T_IKW, :], xb) * (IDX_DIM ** -0.5)).astype(BF16)
    iwt_ref[...] = _dot_nt(wt_ref[_T_IKW:_T_KV, :], xb) * (IDX_HEADS ** -0.5)
    act = _dot_nt(wt_ref[_T_KV:_T_END, :], xb)
    ct = act * lax.rsqrt(jnp.mean(act * act, axis=0, keepdims=True) + RMS_EPS) * kvgc_ref[...]
    ct_ref[...] = ct.astype(BF16)
    qt_ref[...] = (_dot(wukt_ref[...], aqt) * (LOG2_E * ATT_HEAD_DIM ** -0.5)).astype(BF16)

    ac = _dot(xb, wn_ref[:, _N_KV:_N_IKW])
    c = ac * lax.rsqrt(jnp.mean(ac * ac, axis=-1, keepdims=True) + RMS_EPS) * kvg_ref[...]
    c_ref[...] = c.astype(BF16)

    ikw = _dot(xb, wn_ref[:, _N_IKW:_N_HG])
    lane = lax.broadcasted_iota(I32, ikw.shape, 1)
    is_k = lane < IDX_DIM
    mu = jnp.sum(jnp.where(is_k, ikw, 0.0), axis=-1, keepdims=True) * (1.0 / IDX_DIM)
    d = jnp.where(is_k, ikw - mu, 0.0)
    var = jnp.sum(d * d, axis=-1, keepdims=True) * (1.0 / IDX_DIM)
    ik = d * lax.rsqrt(var + LN_EPS) * ikg_ref[...] + ikb_ref[...]
    ik_ref[...] = ik.astype(BF16)

    hg_ref[...] = _dot(xb, wn_ref[:, _N_HG:_N_END])


def _inproj(x2, w_n, w_t, wuk_t, kvg, kvg_col, ikg, ikb, *, tm):
    n, d = x2.shape
    grid = (n // tm,)
    full = lambda shape: pl.BlockSpec(shape, lambda i: (0,) * len(shape))
    rows = lambda w: pl.BlockSpec((tm, w), lambda i: (i, 0))
    cols = lambda h: pl.BlockSpec((h, tm), lambda i: (0, i))
    return pl.pallas_call(
        _inproj_kernel,
        grid=grid,
        in_specs=[rows(d), full(w_n.shape), full(w_t.shape), full(wuk_t.shape), full(kvg.shape),
                  full(kvg_col.shape), full(ikg.shape), full(ikb.shape)],
        out_specs=[cols(ATT_HEADS * KV_LATENT), cols(IDX_HEADS * LANES), cols(LANES),
                   rows(KV_LATENT), cols(KV_LATENT), rows(LANES), rows(4 * HG_HEADS * HG_DIM)],
        out_shape=[jax.ShapeDtypeStruct((ATT_HEADS * KV_LATENT, n), BF16),
                   jax.ShapeDtypeStruct((IDX_HEADS * LANES, n), BF16),
                   jax.ShapeDtypeStruct((LANES, n), F32),
                   jax.ShapeDtypeStruct((n, KV_LATENT), BF16),
                   jax.ShapeDtypeStruct((KV_LATENT, n), BF16),
                   jax.ShapeDtypeStruct((n, LANES), BF16),
                   jax.ShapeDtypeStruct((n, 4 * HG_HEADS * HG_DIM), F32)],
        compiler_params=pltpu.CompilerParams(dimension_semantics=("arbitrary",),
                                             vmem_limit_bytes=VMEM_LIMIT),
        name="inproj",
    )(x2, w_n, w_t, wuk_t, kvg, kvg_col, ikg, ikb)


def _hgrn2_kernel(hg_ref, lb_ref, ng_ref, o_ref, state_ref, *, tb):
    @pl.when(pl.program_id(1) == 0)
    def _():
        state_ref[...] = jnp.zeros_like(state_ref)

    nchunk = tb // HG_CHUNK
    width = HG_HEADS * HG_DIM
    row = lax.broadcasted_iota(I32, (tb, HG_DIM), 0)
    rin = jnp.bitwise_and(row, HG_CHUNK - 1)
    r2 = lax.broadcasted_iota(I32, (tb, tb), 0)
    c2 = lax.broadcasted_iota(I32, (tb, tb), 1)
    intra = jnp.logical_and(r2 // HG_CHUNK == c2 // HG_CHUNK, c2 <= r2)

    for h in range(HG_HEADS):
        sl = slice(h * HG_DIM, (h + 1) * HG_DIM)
        gq = hg_ref[0, :, h * HG_DIM:(h + 1) * HG_DIM]
        gf = hg_ref[0, :, width + h * HG_DIM:width + (h + 1) * HG_DIM]
        gi = hg_ref[0, :, 2 * width + h * HG_DIM:2 * width + (h + 1) * HG_DIM]
        gg = hg_ref[0, :, 3 * width + h * HG_DIM:3 * width + (h + 1) * HG_DIM]
        lb = lb_ref[:, sl]
        forget = lb + (1.0 - lb) * _sigmoid(gf)
        logf = jnp.log(forget)
        b = logf
        s = 1
        while s < HG_CHUNK:
            b = b + jnp.where(rin >= s, pltpu.roll(b, s, axis=0), 0.0)
            s *= 2
        tot = jnp.sum(logf.reshape(nchunk, HG_CHUNK, HG_DIM), axis=1, keepdims=True)
        bl = jnp.broadcast_to(tot, (nchunk, HG_CHUNK, HG_DIM)).reshape(tb, HG_DIM)
        kk = 1.0 - forget
        q_dec = (gq * _sigmoid(gq) * jnp.exp(b)).astype(BF16)
        k_inv = (kk * jnp.exp(-b)).astype(BF16)
        k_end = (kk * jnp.exp(bl - b)).astype(BF16)
        vb = gi.astype(BF16)
        scores = jnp.where(intra, _dot_nt(q_dec, k_inv), 0.0).astype(BF16)
        o = _dot(scores, vb)
        decay = jnp.exp(tot)
        chunk_rows = [slice(n * HG_CHUNK, (n + 1) * HG_CHUNK) for n in range(nchunk)]
        incs = [_dot_tn(vb[rs], k_end[rs]) for rs in chunk_rows]
        st = state_ref[h]
        starts = []
        for n in range(nchunk):
            starts.append(st.astype(BF16))
            st = st * decay[n] + incs[n]
        state_ref[h] = st
        inter = [_dot_nt(q_dec[rs], starts[n]) for n, rs in enumerate(chunk_rows)]
        o = o + jnp.concatenate(inter, axis=0)
        o = o * lax.rsqrt(jnp.mean(o * o, axis=-1, keepdims=True) + RMS_EPS)
        o = o * ng_ref[:, sl] * (gg * _sigmoid(gg))
        o_ref[0, :, sl] = o.astype(o_ref.dtype)


def _hgrn2(hg3, lb, ng, *, tb):
    bsz, t, w4 = hg3.shape
    width = HG_HEADS * HG_DIM
    return pl.pallas_call(
        functools.partial(_hgrn2_kernel, tb=tb),
        grid=(bsz, t // tb),
        in_specs=[pl.BlockSpec((1, tb, w4), lambda b, i: (b, i, 0)),
                  pl.BlockSpec((1, width), lambda b, i: (0, 0)),
                  pl.BlockSpec((1, width), lambda b, i: (0, 0))],
        out_specs=pl.BlockSpec((1, tb, width), lambda b, i: (b, i, 0)),
        out_shape=jax.ShapeDtypeStruct((bsz, t, width), BF16),
        scratch_shapes=[pltpu.VMEM((HG_HEADS, HG_DIM, HG_DIM), F32)],
        compiler_params=pltpu.CompilerParams(dimension_semantics=("arbitrary", "arbitrary"),
                                             vmem_limit_bytes=VMEM_LIMIT),
        name="hgrn2",
    )(hg3, lb, ng)


def _dsa_kernel(qt_ref, iqt_ref, iwt_ref, ik_ref, c_ref, ct_ref, wuvt_ref, o_ref,
                keys_ref, cut_ref, thr_ref, redo_ref, qall_ref, iqall_ref, m_ref, l_ref, acc_ref,
                *, topk, pos_bits):
    n = pl.program_id(1)
    t0 = n * Q_ROWS
    nch = (t0 + Q_ROWS + KEY_CHUNK - 1) // KEY_CHUNK
    kf = float(topk)
    key_i = lax.broadcasted_iota(I32, (KEY_CHUNK, Q_ROWS), 0)
    tq = t0 + lax.broadcasted_iota(I32, (KEY_CHUNK, Q_ROWS), 1)
    pair = 2 * Q_ROWS

    for h in range(ATT_HEADS):
        qall_ref[:, h * Q_ROWS:(h + 1) * Q_ROWS] = qt_ref[h * KV_LATENT:(h + 1) * KV_LATENT, :]
    for h in range(IDX_HEADS):
        iqall_ref[:, h * Q_ROWS:(h + 1) * Q_ROWS] = iqt_ref[h * LANES:(h + 1) * LANES, :]
    head_w = [iwt_ref[IDX_DIM + h:IDX_DIM + h + 1, :] for h in range(IDX_HEADS)]

    def index_body(j, carry):
        k0 = pl.multiple_of(j * KEY_CHUNK, KEY_CHUNK)
        ikc = ik_ref[pl.ds(k0, KEY_CHUNK), :]
        isc = jnp.zeros((KEY_CHUNK, Q_ROWS), F32)
        for hp in range(IDX_HEADS // 2):
            z = _dot(ikc, iqall_ref[:, hp * pair:(hp + 1) * pair])
            for hh in range(2):
                isc = isc + head_w[2 * hp + hh] * jnp.maximum(z[:, hh * Q_ROWS:(hh + 1) * Q_ROWS], 0.0)
        isc = jnp.where(isc == 0.0, 0.0, isc)
        bits = pltpu.bitcast(isc, I32)
        key = jnp.bitwise_xor(bits, jnp.bitwise_and(jnp.right_shift(bits, 31), 0x7FFFFFFF))
        key = jnp.where(k0 + key_i <= tq, key, INT_MIN)
        keys_ref[pl.ds(k0, KEY_CHUNK), :] = key
        return carry

    lax.fori_loop(0, nch, index_body, 0)

    def count(pred):
        def body(j, acc):
            k0 = pl.multiple_of(j * KEY_CHUNK, KEY_CHUNK)
            hit = jnp.where(pred(keys_ref[pl.ds(k0, KEY_CHUNK), :], k0), 1.0, 0.0)
            return acc + jnp.sum(hit.reshape(KEY_CHUNK // COUNT_ROWS, COUNT_ROWS, Q_ROWS), axis=0)
        acc = lax.fori_loop(0, nch, body, jnp.zeros((COUNT_ROWS, Q_ROWS), F32))
        return jnp.sum(acc, axis=0, keepdims=True)

    def count_ge(cand):
        return count(lambda k, k0: k >= cand)

    def masked_max(bound):
        def body(j, acc):
            k0 = pl.multiple_of(j * KEY_CHUNK, KEY_CHUNK)
            key = keys_ref[pl.ds(k0, KEY_CHUNK), :]
            kept = jnp.where(key <= bound, key, INT_MIN)
            return jnp.maximum(acc, jnp.max(kept.reshape(KEY_CHUNK // COUNT_ROWS, COUNT_ROWS, Q_ROWS), axis=0))
        acc = lax.fori_loop(0, nch, body, jnp.full((COUNT_ROWS, Q_ROWS), INT_MIN, I32))
        return jnp.max(acc, axis=0, keepdims=True)

    n_all = (nch * KEY_CHUNK).astype(F32)
    n0 = count_ge(jnp.zeros((1, Q_ROWS), I32))
    pos0 = n0 >= kf
    state0 = (jnp.where(pos0, 0, INT_MIN).astype(I32), jnp.where(pos0, n0, n_all), jnp.where(pos0, 0.0, n0))

    def bracket_body(i, state):
        thr, n_lo, n_hi = state
        cand = thr + lax.shift_left(jnp.int32(1), 30 - i)
        cnt = count_ge(cand)
        ge = cnt >= kf
        return jnp.where(ge, cand, thr), jnp.where(ge, cnt, n_lo), jnp.where(ge, n_hi, cnt)

    thr1, n_lo, n_hi = lax.fori_loop(0, SEARCH_BITS, bracket_body, state0)
    low_bits = 31 - SEARCH_BITS
    has_topk = tq[0:1, :] + 1 >= topk
    in_bracket = n_lo - n_hi
    exact_cut = n_lo == kf
    few = jnp.logical_or(jnp.logical_not(has_topk), jnp.logical_or(exact_cut, in_bracket <= float(EXTRACT_MAX)))
    thr_ref[...] = jnp.broadcast_to(thr1, thr_ref.shape)
    redo_ref[0] = jnp.where(jnp.min(jnp.where(few, 1.0, 0.0)) > 0.0, 0, 1)

    @pl.when(redo_ref[0] == 0)
    def _():
        need = kf - n_hi
        bound = thr1 + (2 ** low_bits - 1)
        found = jnp.zeros((1, Q_ROWS), F32)
        thr_x = thr1
        for j in range(1, EXTRACT_MAX + 1):
            cur = masked_max(bound)
            inside = cur >= thr1
            found = found + jnp.where(inside, 1.0, 0.0)
            thr_x = jnp.where(jnp.logical_and(inside, need == float(j)), cur, thr_x)
            bound = jnp.where(inside, cur - 1, INT_MIN)
        use_x = jnp.logical_and(has_topk, jnp.logical_not(exact_cut))
        clash = jnp.logical_and(use_x, found != in_bracket)
        thr_ref[...] = jnp.broadcast_to(jnp.where(use_x, thr_x, thr1), thr_ref.shape)
        redo_ref[0] = jnp.where(jnp.max(jnp.where(clash, 1.0, 0.0)) > 0.0, 1, 0)

    @pl.when(redo_ref[0] == 1)
    def _():
        def bit_body(i, thr):
            cand = thr + lax.shift_left(jnp.int32(1), low_bits - 1 - i)
            return jnp.where(count_ge(cand) >= kf, cand, thr)

        thr_ref[...] = jnp.broadcast_to(lax.fori_loop(0, low_bits, bit_body, thr1), thr_ref.shape)

    thr = thr_ref[0:1, :]
    live = thr > INT_MIN
    cut_ref[...] = jnp.broadcast_to(jnp.where(live, jnp.int32(2 ** 30), -1), cut_ref.shape)
    n_ge = count_ge(thr)
    tie = jnp.max(jnp.where(jnp.logical_and(live, n_ge > kf), 1.0, 0.0))

    @pl.when(tie > 0.0)
    def _():
        need = kf - count(lambda k, k0: k > thr)

        def cut_body(i, cut):
            cand = cut + lax.shift_left(jnp.int32(1), pos_bits - 1 - i)
            below = count(lambda k, k0: jnp.logical_and(k == thr, k0 + key_i < cand))
            return jnp.where(below < need, cand, cut)

        cut = lax.fori_loop(0, pos_bits, cut_body, jnp.zeros((1, Q_ROWS), I32))
        cut_ref[...] = jnp.broadcast_to(jnp.where(live, cut, -1), cut_ref.shape)

    m_ref[...] = jnp.full(m_ref.shape, NEG_BIG, F32)
    l_ref[...] = jnp.zeros(l_ref.shape, F32)
    acc_ref[...] = jnp.zeros(acc_ref.shape, F32)
    cut_b = cut_ref[0:1, :]

    def attn_body(j, carry):
        k0 = pl.multiple_of(j * KEY_CHUNK, KEY_CHUNK)
        key = keys_ref[pl.ds(k0, KEY_CHUNK), :]
        pos = k0 + key_i
        sel = jnp.logical_or(key > thr, jnp.logical_and(key == thr, pos <= cut_b))
        rel = jnp.where(sel, (pos - t0).astype(F32), NEG_BIG)
        cc = c_ref[pl.ds(k0, KEY_CHUNK), :]
        cct = ct_ref[:, pl.ds(k0, KEY_CHUNK)]
        for hp in range(ATT_HEADS // 2):
            cs = slice(hp * pair, (hp + 1) * pair)
            s2 = _dot(cc, qall_ref[:, cs])
            ps, alphas = [], []
            for hh in range(2):
                h = 2 * hp + hh
                slope = LOG2_E * 2.0 ** (-(8.0 / ATT_HEADS) * (h + 1))
                s = s2[:, hh * Q_ROWS:(hh + 1) * Q_ROWS] + slope * rel
                m_old = m_ref[h:h + 1, :]
                m_new = jnp.maximum(m_old, jnp.max(s, axis=0, keepdims=True))
                alpha = jnp.exp2(m_old - m_new)
                p = jnp.exp2(s - m_new)
                l_ref[h:h + 1, :] = alpha * l_ref[h:h + 1, :] + jnp.sum(p, axis=0, keepdims=True)
                m_ref[h:h + 1, :] = m_new
                ps.append(p.astype(BF16))
                alphas.append(alpha)
            acc_ref[:, cs] = (acc_ref[:, cs] * jnp.concatenate(alphas, axis=1)
                              + _dot(cct, jnp.concatenate(ps, axis=1)))
        return carry

    lax.fori_loop(0, nch, attn_body, 0)

    yt = []
    for h in range(ATT_HEADS):
        o_lat = (acc_ref[:, h * Q_ROWS:(h + 1) * Q_ROWS] / l_ref[h:h + 1, :]).astype(BF16)
        yt.append(_dot(wuvt_ref[h], o_lat))
    o_ref[...] = jnp.concatenate(yt, axis=0).T.astype(o_ref.dtype)


def _dsa(qt, iqt, iwt, ik, c, ct, wuvt_p, *, bsz, t):
    n = bsz * t
    nq = t // Q_ROWS
    topk = min(TOPK_MAX, t // 4)
    assert t % KEY_CHUNK == 0
    width = ATT_HEADS * ATT_HEAD_DIM
    qcols = lambda h: pl.BlockSpec((h, Q_ROWS), lambda b, i: (0, b * nq + i))
    return pl.pallas_call(
        functools.partial(_dsa_kernel, topk=topk, pos_bits=(t - 1).bit_length()),
        grid=(bsz, nq),
        in_specs=[qcols(ATT_HEADS * KV_LATENT), qcols(IDX_HEADS * LANES), qcols(LANES),
                  pl.BlockSpec((t, LANES), lambda b, i: (b, 0)),
                  pl.BlockSpec((t, KV_LATENT), lambda b, i: (b, 0)),
                  pl.BlockSpec((KV_LATENT, t), lambda b, i: (0, b)),
                  pl.BlockSpec(wuvt_p.shape, lambda b, i: (0, 0, 0))],
        out_specs=pl.BlockSpec((Q_ROWS, width), lambda b, i: (b * nq + i, 0)),
        out_shape=jax.ShapeDtypeStruct((n, width), BF16),
        scratch_shapes=[pltpu.VMEM((t, Q_ROWS), I32),
                        pltpu.VMEM((8, Q_ROWS), I32),
                        pltpu.VMEM((8, Q_ROWS), I32),
                        pltpu.SMEM((1,), I32),
                        pltpu.VMEM((KV_LATENT, ATT_HEADS * Q_ROWS), BF16),
                        pltpu.VMEM((LANES, IDX_HEADS * Q_ROWS), BF16),
                        pltpu.VMEM((8, Q_ROWS), F32),
                        pltpu.VMEM((8, Q_ROWS), F32),
                        pltpu.VMEM((KV_LATENT, ATT_HEADS * Q_ROWS), F32)],
        compiler_params=pltpu.CompilerParams(dimension_semantics=("arbitrary", "arbitrary"),
                                             vmem_limit_bytes=VMEM_LIMIT),
        name="dsa",
    )(qt, iqt, iwt, ik, c, ct, wuvt_p)


def _mix_router_kernel(x_ref, ya_ref, yr_ref, wo_ref, g_ref, b_ref, wr_ref, br_ref,
                       x1_ref, x1w_ref, route_ref, tot_ref, carry_ref, *, alpha, tm):
    @pl.when(pl.program_id(0) == 0)
    def _():
        carry_ref[...] = jnp.zeros_like(carry_ref)

    half = ya_ref.shape[1]
    mix = _dot(ya_ref[...], wo_ref[0:half, :]) + _dot(yr_ref[...], wo_ref[half:, :])
    x1 = _layer_norm(alpha * x_ref[...] + mix, g_ref[...], b_ref[...])
    x1_ref[...] = x1
    x1w_ref[...] = _pack_words(x1)

    x_hi = x1.astype(BF16)
    x_lo = (x1 - x_hi.astype(F32)).astype(BF16)
    hh_hl = _dot(x_hi, wr_ref[...])
    logits = (hh_hl[:, :LANES] + hh_hl[:, LANES:] + _dot(x_lo, wr_ref[:, :LANES])) + br_ref[...]
    lane = lax.broadcasted_iota(I32, logits.shape, 1)
    lane_f = lane.astype(F32)
    work = jnp.where(lane < N_EXPERTS, logits, -jnp.inf)
    sel_f = jnp.zeros(logits.shape, F32)
    denom = jnp.zeros((tm, 1), F32)
    hits, experts, weights = [], [], []
    for _ in range(TOP_K):
        mx = jnp.max(work, axis=-1, keepdims=True)
        first = jnp.min(jnp.where(work == mx, lane_f, float(LANES)), axis=-1, keepdims=True)
        hit = lane_f == first
        e = jnp.exp(mx - (weights[0][1] if weights else mx))
        hits.append(hit)
        experts.append(first)
        weights.append((e, mx))
        denom = denom + e
        sel_f = sel_f + jnp.where(hit, 1.0, 0.0)
        work = jnp.where(hit, -jnp.inf, work)

    r2 = lax.broadcasted_iota(I32, (tm, tm), 0)
    c2 = lax.broadcasted_iota(I32, (tm, tm), 1)
    before = jnp.where(c2 < r2, 1.0, 0.0).astype(BF16)
    rank = _dot(before, sel_f.astype(BF16)) + carry_ref[...]
    carry = carry_ref[...] + jnp.sum(sel_f, axis=0, keepdims=True)
    carry_ref[...] = carry
    tot_ref[...] = jnp.broadcast_to(carry, tot_ref.shape)

    route = jnp.zeros(logits.shape, F32)
    for k in range(TOP_K):
        rank_k = jnp.sum(jnp.where(hits[k], rank, 0.0), axis=-1, keepdims=True)
        route = jnp.where(lane == k, experts[k], route)
        route = jnp.where(lane == TOP_K + k, rank_k, route)
        route = jnp.where(lane == 2 * TOP_K + k, weights[k][0] / denom, route)
    route_ref[...] = route


def _mix_router(x2, ya, yr, wo, g, b, wr_p, br_p, *, alpha, tm):
    n, d = x2.shape
    half = ya.shape[1]
    rows = lambda w: pl.BlockSpec((tm, w), lambda i: (i, 0))
    full = lambda shape: pl.BlockSpec(shape, lambda i: (0,) * len(shape))
    return pl.pallas_call(
        functools.partial(_mix_router_kernel, alpha=alpha, tm=tm),
        grid=(n // tm,),
        in_specs=[rows(d), rows(half), rows(half), full(wo.shape), full(g.shape), full(b.shape),
                  full(wr_p.shape), full(br_p.shape)],
        out_specs=[rows(d), rows(d // 2), rows(LANES), full((8, LANES))],
        out_shape=[jax.ShapeDtypeStruct((n, d), F32),
                   jax.ShapeDtypeStruct((n, d // 2), I32),
                   jax.ShapeDtypeStruct((n, LANES), F32),
                   jax.ShapeDtypeStruct((8, LANES), F32)],
        scratch_shapes=[pltpu.VMEM((1, LANES), F32)],
        compiler_params=pltpu.CompilerParams(dimension_semantics=("arbitrary",),
                                             vmem_limit_bytes=VMEM_LIMIT),
        name="mix_router",
    )(x2, ya, yr, wo, g, b, wr_p, br_p)


def _sc_gather_rows(table, idx):
    n_idx = idx.shape[0]
    width = table.shape[1]
    per_worker = n_idx // SC_WORKERS
    assert per_worker * SC_WORKERS == n_idx and per_worker % SC_ROWS == 0
    mesh = plsc.VectorSubcoreMesh(core_axis_name="c", subcore_axis_name="s",
                                  num_cores=SC_CORES, num_subcores=SC_SUBCORES)

    @functools.partial(
        pl.kernel, mesh=mesh,
        out_type=jax.ShapeDtypeStruct((n_idx, width), table.dtype),
        scratch_types=[pltpu.VMEM((SC_ROWS,), I32),
                       pltpu.VMEM((SC_ROWS, width), table.dtype),
                       pltpu.SemaphoreType.DMA])
    def gather(table_hbm, idx_hbm, out_hbm, idx_v, rows_v, sem):
        worker = lax.axis_index("s") * SC_CORES + lax.axis_index("c")

        @pl.loop(0, per_worker // SC_ROWS)
        def _(step):
            base = pl.multiple_of(worker * per_worker + step * SC_ROWS, SC_ROWS)
            pltpu.sync_copy(idx_hbm.at[pl.ds(base, SC_ROWS)], idx_v)
            pltpu.async_copy(table_hbm.at[idx_v], rows_v, sem).wait()
            pltpu.sync_copy(rows_v, out_hbm.at[pl.ds(base, SC_ROWS)])

    return gather(table, idx)


def _sc_scatter_rows(rows, dest3, cap):
    n, width = rows.shape
    chunks, picks, r = dest3.shape
    assert r == SC_SCATTER_ROWS and chunks * r == n and chunks % SC_WORKERS == 0
    per_worker = chunks // SC_WORKERS
    mesh = plsc.VectorSubcoreMesh(core_axis_name="c", subcore_axis_name="s",
                                  num_cores=SC_CORES, num_subcores=SC_SUBCORES)

    @functools.partial(
        pl.kernel, mesh=mesh,
        out_type=jax.ShapeDtypeStruct((cap, width), rows.dtype),
        scratch_types=[pltpu.VMEM((picks, r), I32),
                       pltpu.VMEM((r, width), rows.dtype),
                       pltpu.SemaphoreType.DMA])
    def scatter(rows_hbm, dest_hbm, out_hbm, idx_v, rows_v, sem):
        worker = lax.axis_index("s") * SC_CORES + lax.axis_index("c")

        @pl.loop(0, per_worker)
        def _(step):
            chunk = worker * per_worker + step
            base = pl.multiple_of(chunk * r, r)
            pltpu.sync_copy(rows_hbm.at[pl.ds(base, r)], rows_v)
            pltpu.sync_copy(dest_hbm.at[chunk], idx_v)
            for k in range(picks):
                pltpu.async_copy(rows_v, out_hbm.at[idx_v.at[k]], sem).wait()

    return scatter(rows, dest3)


def _pack_words(v):
    k = v.shape[1] // 2
    hi = pltpu.bitcast(v[:, :k].astype(BF16).astype(F32), I32)
    lo = pltpu.bitcast(v[:, k:].astype(BF16).astype(F32), I32)
    return jnp.bitwise_or(hi, lax.shift_right_logical(lo, jnp.int32(16)))


def _unpack_words(w, dtype=BF16):
    hi = pltpu.bitcast(jnp.bitwise_and(w, jnp.int32(-65536)), F32)
    lo = pltpu.bitcast(jnp.left_shift(w, 16), F32)
    return jnp.concatenate([hi, lo], axis=1).astype(dtype)


def _expert_kernel(exp_ref, blk_ref, new_ref, x_ref, wup_ref, bup_ref, wdn_ref, bdn_ref, o_ref,
                   wup_s, wdn_s, *, d_ff):
    del exp_ref, blk_ref

    @pl.when(new_ref[pl.program_id(0)] == 1)
    def _():
        wup_s[...] = wup_ref[0].astype(BF16)
        wdn_s[...] = wdn_ref[0].astype(BF16)

    half = MOE_ROWS // 2
    for r in range(2):
        rows = slice(r * half, (r + 1) * half)
        h = _dot(_unpack_words(x_ref[rows, :]), wup_s[...]) + bup_ref[0]
        glu = jnp.minimum(h[:, :d_ff], SWIGLU_LIMIT)
        lin = jnp.clip(h[:, d_ff:], -SWIGLU_LIMIT, SWIGLU_LIMIT)
        a = glu * _sigmoid(SWIGLU_ALPHA * glu) * (lin + 1.0)
        o_ref[rows, :] = _pack_words(_dot(a.astype(BF16), wdn_s[...]) + bdn_ref[0])


def _experts(blk_expert, blk_index, blk_new, xs, wup, bup, wdn, bdn):
    cap, words = xs.shape
    d = 2 * words
    d_ff = wdn.shape[1]
    row_map = lambda i, exp, blk, new: (blk[i], 0)
    exp_map3 = lambda i, exp, blk, new: (exp[i], 0, 0)
    grid_spec = pltpu.PrefetchScalarGridSpec(
        num_scalar_prefetch=3,
        grid=(cap // MOE_ROWS,),
        in_specs=[pl.BlockSpec((MOE_ROWS, words), row_map),
                  pl.BlockSpec((1, d, 2 * d_ff), exp_map3),
                  pl.BlockSpec((1, 1, 2 * d_ff), exp_map3),
                  pl.BlockSpec((1, d_ff, d), exp_map3),
                  pl.BlockSpec((1, 1, d), exp_map3)],
        out_specs=pl.BlockSpec((MOE_ROWS, words), row_map),
        scratch_shapes=[pltpu.VMEM((d, 2 * d_ff), BF16), pltpu.VMEM((d_ff, d), BF16)],
    )
    return pl.pallas_call(
        functools.partial(_expert_kernel, d_ff=d_ff),
        grid_spec=grid_spec,
        out_shape=jax.ShapeDtypeStruct((cap, words), I32),
        compiler_params=pltpu.CompilerParams(dimension_semantics=("arbitrary",),
                                             vmem_limit_bytes=VMEM_LIMIT),
        name="experts",
    )(blk_expert, blk_index, blk_new, xs, wup, bup, wdn, bdn)


def _route_tables(route, totals, *, cap):
    n = route.shape[0]
    expert = route[:, 0:TOP_K].astype(I32)
    rank = route[:, TOP_K:2 * TOP_K].astype(I32)
    counts = totals[0, :N_EXPERTS].astype(I32)
    padded = (counts + MOE_ROWS - 1) // MOE_ROWS * MOE_ROWS
    pad_end = jnp.cumsum(padded)
    pad_start = pad_end - padded
    dest = pad_start[expert] + rank
    dest_chunks = dest.reshape(n // SC_SCATTER_ROWS, SC_SCATTER_ROWS, TOP_K).transpose(0, 2, 1)
    n_blk = cap // MOE_ROWS
    used = pad_end[-1] // MOE_ROWS
    blk = jnp.minimum(jnp.arange(n_blk, dtype=I32), used - 1)
    blk_expert = jnp.minimum(jnp.sum(pad_end[None, :] <= (blk * MOE_ROWS)[:, None], axis=1),
                             N_EXPERTS - 1).astype(I32)
    blk_new = jnp.concatenate([jnp.ones((1,), I32), (blk_expert[1:] != blk_expert[:-1]).astype(I32)])
    return dest, dest_chunks, blk_expert, blk, blk_new


def _final_kernel(x1_ref, ya_ref, yb_ref, yc_ref, yd_ref, route_ref, p_ref, g1_ref, b1_ref,
                  wp_ref, wg_ref, g2_ref, b2_ref, *rest, alpha):
    o_ref = rest[-1]
    ple_lin = _dot(p_ref[...].astype(BF16), wp_ref[...])
    ffn = jnp.zeros(x1_ref.shape, F32)
    for k, yk_ref in enumerate((ya_ref, yb_ref, yc_ref, yd_ref)):
        gate_k = route_ref[:, 2 * TOP_K + k:2 * TOP_K + k + 1]
        ffn = ffn + gate_k * _unpack_words(yk_ref[...], F32)
    x2 = _layer_norm(alpha * x1_ref[...] + ffn, g1_ref[...], b1_ref[...])
    gate = _sigmoid(_dot(x2.astype(BF16), wg_ref[...]))
    o_ref[...] = _layer_norm(alpha * x2 + ple_lin * gate, g2_ref[...], b2_ref[...])


def _final(x1, y4w, route, p2, g1, b1, wp, wg, g2, b2, *, alpha, tm, part, parts, prev=None):
    n, d = x1.shape
    assert TOP_K == 4
    steps = n // parts // tm
    off = part * steps
    rows = lambda w: pl.BlockSpec((tm, w), lambda i: (off + i, 0))
    full = lambda shape: pl.BlockSpec(shape, lambda i: (0,) * len(shape))
    pick = lambda k: pl.BlockSpec((tm, d // 2), lambda i: (k * steps + i, 0))
    in_specs = [rows(d), pick(0), pick(1), pick(2), pick(3), rows(LANES), rows(p2.shape[1]),
                full(g1.shape), full(b1.shape),
                full(wp.shape), full(wg.shape), full(g2.shape), full(b2.shape)]
    args = [x1, y4w, y4w, y4w, y4w, route, p2, g1, b1, wp, wg, g2, b2]
    aliases = {}
    if prev is not None:
        in_specs.append(pl.BlockSpec(memory_space=pl.ANY))
        args.append(prev)
        aliases = {len(args) - 1: 0}
    return pl.pallas_call(
        functools.partial(_final_kernel, alpha=alpha),
        grid=(steps,),
        in_specs=in_specs,
        out_specs=rows(d),
        out_shape=jax.ShapeDtypeStruct((n, d), F32),
        input_output_aliases=aliases,
        compiler_params=pltpu.CompilerParams(dimension_semantics=("arbitrary",),
                                             vmem_limit_bytes=VMEM_LIMIT),
        name="final",
    )(*args)


def _pack_w_in(w_in):
    d = w_in.shape[0]
    sizes = (ATT_HEADS * ATT_HEAD_DIM, KV_LATENT, IDX_HEADS * IDX_DIM, IDX_DIM, IDX_HEADS,
             HG_HEADS * HG_DIM, HG_HEADS * HG_DIM, HG_HEADS * HG_DIM, HG_HEADS * HG_DIM)
    offs = [0]
    for s in sizes:
        offs.append(offs[-1] + s)
    sec = [w_in[:, offs[k]:offs[k + 1]] for k in range(len(sizes))]
    iq = sec[2].reshape(d, IDX_HEADS, IDX_DIM)
    iq = jnp.pad(iq, ((0, 0), (0, 0), (0, LANES - IDX_DIM))).reshape(d, IDX_HEADS * LANES)
    ikw = jnp.pad(jnp.concatenate([sec[3], sec[4]], axis=1), ((0, 0), (0, LANES - IDX_DIM - IDX_HEADS)))
    w_n = jnp.concatenate([sec[1], ikw, sec[5], sec[6], sec[7], sec[8]], axis=1).astype(BF16)
    w_t = jnp.concatenate([sec[0], iq, ikw, sec[1]], axis=1).T.astype(BF16)
    return w_n, w_t


def _block_diag_uk_t(w_uk):
    eye = jnp.eye(ATT_HEADS, dtype=w_uk.dtype)
    bd = jnp.einsum("rhd,hg->hrgd", w_uk, eye)
    return bd.reshape(ATT_HEADS * KV_LATENT, ATT_HEADS * ATT_HEAD_DIM).astype(BF16)


def _uv_t(w_uv):
    return jnp.transpose(w_uv, (1, 2, 0)).astype(BF16)


def _layer(x, p_l, w_in, kv_g, ik_g, ik_b, w_uk, w_uv, lb, hg_ng, w_o, ln_mix_g, ln_mix_b,
           w_router, b_router, w_up, b_up, w_down, b_down, ln_ffn_g, ln_ffn_b,
           w_ple_proj, w_ple_gate, ln_ple_g, ln_ple_b, *, alpha, tm, hg_tb):
    bsz, t, d = x.shape
    n = bsz * t
    x2 = x.reshape(n, d)
    row = lambda v: v.reshape(1, -1).astype(F32)
    pad_lane = lambda v: jnp.pad(row(v), ((0, 0), (0, LANES - v.shape[-1])))

    w_n, w_t = _pack_w_in(w_in)
    qt, iqt, iwt, c, ct, ik, hg = _inproj(x2, w_n, w_t, _block_diag_uk_t(w_uk), row(kv_g),
                                          kv_g.reshape(-1, 1).astype(F32), pad_lane(ik_g), pad_lane(ik_b), tm=tm)
    y_rec = _hgrn2(hg.reshape(bsz, t, -1), row(lb), row(hg_ng), tb=hg_tb)
    y_att = _dsa(qt, iqt, iwt, ik, c, ct, _uv_t(w_uv), bsz=bsz, t=t)

    wr_f = jnp.pad(w_router.astype(F32), ((0, 0), (0, LANES - N_EXPERTS)))
    wr_hi = wr_f.astype(BF16)
    wr_p = jnp.concatenate([wr_hi, (wr_f - wr_hi.astype(F32)).astype(BF16)], axis=1)
    x1, x1w, route, totals = _mix_router(
        x2, y_att, y_rec.reshape(n, -1), w_o.astype(BF16), row(ln_mix_g), row(ln_mix_b),
        wr_p, pad_lane(b_router), alpha=alpha, tm=tm)

    step = SC_WORKERS * SC_ROWS
    cap = -(-(n * TOP_K + N_EXPERTS * MOE_ROWS) // step) * step
    dest, dest_chunks, blk_expert, blk_index, blk_new = _route_tables(route, totals, cap=cap)
    d_ff = w_down.shape[1]
    xs = _sc_scatter_rows(x1w, dest_chunks, cap)
    outw = _experts(blk_expert, blk_index, blk_new, xs,
                    w_up.astype(F32), b_up.reshape(N_EXPERTS, 1, 2 * d_ff).astype(F32),
                    w_down.astype(F32), b_down.reshape(N_EXPERTS, 1, d).astype(F32))

    unit = max(tm, SC_WORKERS * SC_ROWS // TOP_K)
    parts = next(c for c in (COMBINE_PARTS, 2, 1) if n % (c * unit) == 0)
    out = None
    for part in range(parts):
        rows_p = slice(part * (n // parts), (part + 1) * (n // parts))
        y4w = _sc_gather_rows(outw, dest[rows_p].T.reshape(-1))
        out = _final(x1, y4w, route, p_l.reshape(n, -1), row(ln_ffn_g), row(ln_ffn_b),
                     w_ple_proj.astype(BF16), w_ple_gate.astype(BF16), row(ln_ple_g), row(ln_ple_b),
                     alpha=alpha, tm=tm, part=part, parts=parts, prev=out)
    return out.reshape(bsz, t, d)


def kernel(x, p, w_in, kv_norm_g, idx_k_norm_g, idx_k_norm_b, w_uk, w_uv, hg_lb_logits, hg_norm_g, w_o,
           ln_mix_g, ln_mix_b, w_router, b_router, w_up, b_up, w_down, b_down, ln_ffn_g, ln_ffn_b,
           w_ple_proj, w_ple_gate, ln_ple_g, ln_ple_b):
    depth = w_in.shape[0]
    alpha = (2.0 * depth) ** 0.25
    lower_bounds = jnp.cumsum(jax.nn.softmax(hg_lb_logits.astype(F32), axis=0), axis=0)
    n = x.shape[0] * x.shape[1]
    tm = min(512, n)
    hg_tb = min(512, x.shape[1])
    for l in range(depth):
        x = _layer(x, p[l], w_in[l], kv_norm_g[l], idx_k_norm_g[l], idx_k_norm_b[l], w_uk[l], w_uv[l],
                   lower_bounds[l], hg_norm_g[l], w_o[l], ln_mix_g[l], ln_mix_b[l], w_router[l],
                   b_router[l], w_up[l], b_up[l], w_down[l], b_down[l], ln_ffn_g[l], ln_ffn_b[l],
                   w_ple_proj[l], w_ple_gate[l], ln_ple_g[l], ln_ple_b[l],
                   alpha=alpha, tm=tm, hg_tb=hg_tb)
    return x
```

```python
import functools

import jax
import jax.numpy as jnp
from jax import lax
from jax.experimental import pallas as pl
from jax.experimental.pallas import tpu as pltpu
from jax.experimental.pallas import tpu_sc as plsc

F32 = jnp.float32
BF16 = jnp.bfloat16
I32 = jnp.int32

ATT_HEADS = 8
ATT_HEAD_DIM = 64
KV_LATENT = 256
IDX_HEADS = 8
IDX_DIM = 64
TOPK_MAX = 256
HG_HEADS = 4
HG_DIM = 128
HG_CHUNK = 32
N_EXPERTS = 32
TOP_K = 4
SWIGLU_LIMIT = 7.0
SWIGLU_ALPHA = 1.702
LN_EPS = 1e-5
RMS_EPS = 1e-6

LANES = 128
Q_ROWS = 128
KEY_CHUNK = 512
COUNT_ROWS = 128
SEARCH_BITS = 20
EXTRACT_MAX = 3
MOE_ROWS = 512
SC_CORES = 2
SC_SUBCORES = 16
SC_WORKERS = SC_CORES * SC_SUBCORES
SC_ROWS = 128
SC_SCATTER_ROWS = 128
COMBINE_PARTS = 8
VMEM_LIMIT = 56 * 1024 * 1024

INT_MIN = -(2 ** 31)
NEG_BIG = -1e30
LOG2_E = 1.4426950408889634

_N_KV = 0
_N_IKW = _N_KV + KV_LATENT
_N_HG = _N_IKW + LANES
_N_END = _N_HG + 4 * HG_HEADS * HG_DIM
_T_AQ = 0
_T_IQ = _T_AQ + ATT_HEADS * ATT_HEAD_DIM
_T_IKW = _T_IQ + IDX_HEADS * LANES
_T_KV = _T_IKW + LANES
_T_END = _T_KV + KV_LATENT


def _dot(a, b):
    return jnp.dot(a, b, preferred_element_type=F32)


def _dot_nt(a, b):
    return lax.dot_general(a, b, (((1,), (1,)), ((), ())), preferred_element_type=F32)


def _dot_tn(a, b):
    return lax.dot_general(a, b, (((0,), (0,)), ((), ())), preferred_element_type=F32)


def _layer_norm(z, g, b):
    mu = jnp.mean(z, axis=-1, keepdims=True)
    d = z - mu
    var = jnp.mean(d * d, axis=-1, keepdims=True)
    return d * lax.rsqrt(var + LN_EPS) * g + b


def _sigmoid(x):
    return 1.0 / (1.0 + jnp.exp(-x))


def _inproj_kernel(x_ref, wn_ref, wt_ref, wukt_ref, kvg_ref, kvgc_ref, ikg_ref, ikb_ref,
                   qt_ref, iqt_ref, iwt_ref, c_ref, ct_ref, ik_ref, hg_ref):
    xb = x_ref[...].astype(BF16)

    aqt = _dot_nt(wt_ref[_T_AQ:_T_IQ, :], xb).astype(BF16)
    iqt_ref[...] = (_dot_nt(wt_ref[_T_IQ:_T_IKW, :], xb) * (IDX_DIM ** -0.5)).astype(BF16)
    iwt_ref[...] = _dot_nt(wt_ref[_T_IKW:_T_KV, :], xb) * (IDX_HEADS ** -0.5)
    act = _dot_nt(wt_ref[_T_KV:_T_END, :], xb)
    ct = act * lax.rsqrt(jnp.mean(act * act, axis=0, keepdims=True) + RMS_EPS) * kvgc_ref[...]
    ct_ref[...] = ct.astype(BF16)
    qt_ref[...] = (_dot(wukt_ref[...], aqt) * (LOG2_E * ATT_HEAD_DIM ** -0.5)).astype(BF16)

    ac = _dot(xb, wn_ref[:, _N_KV:_N_IKW])
    c = ac * lax.rsqrt(jnp.mean(ac * ac, axis=-1, keepdims=True) + RMS_EPS) * kvg_ref[...]
    c_ref[...] = c.astype(BF16)

    ikw = _dot(xb, wn_ref[:, _N_IKW:_N_HG])
    lane = lax.broadcasted_iota(I32, ikw.shape, 1)
    is_k = lane < IDX_DIM
    mu = jnp.sum(jnp.where(is_k, ikw, 0.0), axis=-1, keepdims=True) * (1.0 / IDX_DIM)
    d = jnp.where(is_k, ikw - mu, 0.0)
    var = jnp.sum(d * d, axis=-1, keepdims=True) * (1.0 / IDX_DIM)
    ik = d * lax.rsqrt(var + LN_EPS) * ikg_ref[...] + ikb_ref[...]
    ik_ref[...] = ik.astype(BF16)

    hg_ref[...] = _dot(xb, wn_ref[:, _N_HG:_N_END])


def _inproj(x2, w_n, w_t, wuk_t, kvg, kvg_col, ikg, ikb, *, tm):
    n, d = x2.shape
    grid = (n // tm,)
    full = lambda shape: pl.BlockSpec(shape, lambda i: (0,) * len(shape))
    rows = lambda w: pl.BlockSpec((tm, w), lambda i: (i, 0))
    cols = lambda h: pl.BlockSpec((h, tm), lambda i: (0, i))
    return pl.pallas_call(
        _inproj_kernel,
        grid=grid,
        in_specs=[rows(d), full(w_n.shape), full(w_t.shape), full(wuk_t.shape), full(kvg.shape),
                  full(kvg_col.shape), full(ikg.shape), full(ikb.shape)],
        out_specs=[cols(ATT_HEADS * KV_LATENT), cols(IDX_HEADS * LANES), cols(LANES),
                   rows(KV_LATENT), cols(KV_LATENT), rows(LANES), rows(4 * HG_HEADS * HG_DIM)],
        out_shape=[jax.ShapeDtypeStruct((ATT_HEADS * KV_LATENT, n), BF16),
                   jax.ShapeDtypeStruct((IDX_HEADS * LANES, n), BF16),
                   jax.ShapeDtypeStruct((LANES, n), F32),
                   jax.ShapeDtypeStruct((n, KV_LATENT), BF16),
                   jax.ShapeDtypeStruct((KV_LATENT, n), BF16),
                   jax.ShapeDtypeStruct((n, LANES), BF16),
                   jax.ShapeDtypeStruct((n, 4 * HG_HEADS * HG_DIM), F32)],
        compiler_params=pltpu.CompilerParams(dimension_semantics=("arbitrary",),
                                             vmem_limit_bytes=VMEM_LIMIT),
        name="inproj",
    )(x2, w_n, w_t, wuk_t, kvg, kvg_col, ikg, ikb)


def _hgrn2_kernel(hg_ref, lb_ref, ng_ref, o_ref, state_ref, *, tb):
    @pl.when(pl.program_id(1) == 0)
    def _():
        state_ref[...] = jnp.zeros_like(state_ref)

    nchunk = tb // HG_CHUNK
    width = HG_HEADS * HG_DIM
    row = lax.broadcasted_iota(I32, (tb, HG_DIM), 0)
    rin = jnp.bitwise_and(row, HG_CHUNK - 1)
    r2 = lax.broadcasted_iota(I32, (tb, tb), 0)
    c2 = lax.broadcasted_iota(I32, (tb, tb), 1)
    intra = jnp.logical_and(r2 // HG_CHUNK == c2 // HG_CHUNK, c2 <= r2)

    for h in range(HG_HEADS):
        sl = slice(h * HG_DIM, (h + 1) * HG_DIM)
        gq = hg_ref[0, :, h * HG_DIM:(h + 1) * HG_DIM]
        gf = hg_ref[0, :, width + h * HG_DIM:width + (h + 1) * HG_DIM]
        gi = hg_ref[0, :, 2 * width + h * HG_DIM:2 * width + (h + 1) * HG_DIM]
        gg = hg_ref[0, :, 3 * width + h * HG_DIM:3 * width + (h + 1) * HG_DIM]
        lb = lb_ref[:, sl]
        forget = lb + (1.0 - lb) * _sigmoid(gf)
        logf = jnp.log(forget)
        b = logf
        s = 1
        while s < HG_CHUNK:
            b = b + jnp.where(rin >= s, pltpu.roll(b, s, axis=0), 0.0)
            s *= 2
        tot = jnp.sum(logf.reshape(nchunk, HG_CHUNK, HG_DIM), axis=1, keepdims=True)
        bl = jnp.broadcast_to(tot, (nchunk, HG_CHUNK, HG_DIM)).reshape(tb, HG_DIM)
        kk = 1.0 - forget
        q_dec = (gq * _sigmoid(gq) * jnp.exp(b)).astype(BF16)
        k_inv = (kk * jnp.exp(-b)).astype(BF16)
        k_end = (kk * jnp.exp(bl - b)).astype(BF16)
        vb = gi.astype(BF16)
        scores = jnp.where(intra, _dot_nt(q_dec, k_inv), 0.0).astype(BF16)
        o = _dot(scores, vb)
        decay = jnp.exp(tot)
        chunk_rows = [slice(n * HG_CHUNK, (n + 1) * HG_CHUNK) for n in range(nchunk)]
        incs = [_dot_tn(vb[rs], k_end[rs]) for rs in chunk_rows]
        st = state_ref[h]
        starts = []
        for n in range(nchunk):
            starts.append(st.astype(BF16))
            st = st * decay[n] + incs[n]
        state_ref[h] = st
        inter = [_dot_nt(q_dec[rs], starts[n]) for n, rs in enumerate(chunk_rows)]
        o = o + jnp.concatenate(inter, axis=0)
        o = o * lax.rsqrt(jnp.mean(o * o, axis=-1, keepdims=True) + RMS_EPS)
        o = o * ng_ref[:, sl] * (gg * _sigmoid(gg))
        o_ref[0, :, sl] = o.astype(o_ref.dtype)


def _hgrn2(hg3, lb, ng, *, tb):
    bsz, t, w4 = hg3.shape
    width = HG_HEADS * HG_DIM
    return pl.pallas_call(
        functools.partial(_hgrn2_kernel, tb=tb),
        grid=(bsz, t // tb),
        in_specs=[pl.BlockSpec((1, tb, w4), lambda b, i: (b, i, 0)),
                  pl.BlockSpec((1, width), lambda b, i: (0, 0)),
                  pl.BlockSpec((1, width), lambda b, i: (0, 0))],
        out_specs=pl.BlockSpec((1, tb, width), lambda b, i: (b, i, 0)),
        out_shape=jax.ShapeDtypeStruct((bsz, t, width), BF16),
        scratch_shapes=[pltpu.VMEM((HG_HEADS, HG_DIM, HG_DIM), F32)],
        compiler_params=pltpu.CompilerParams(dimension_semantics=("arbitrary", "arbitrary"),
                                             vmem_limit_bytes=VMEM_LIMIT),
        name="hgrn2",
    )(hg3, lb, ng)


def _dsa_kernel(qt_ref, iqt_ref, iwt_ref, ik_ref, c_ref, ct_ref, wuvt_ref, o_ref,
                keys_ref, cut_ref, thr_ref, redo_ref, qall_ref, iqall_ref, m_ref, l_ref, acc_ref,
                *, topk, pos_bits):
    n = pl.program_id(1)
    t0 = n * Q_ROWS
    nch = (t0 + Q_ROWS + KEY_CHUNK - 1) // KEY_CHUNK
    kf = float(topk)
    key_i = lax.broadcasted_iota(I32, (KEY_CHUNK, Q_ROWS), 0)
    tq = t0 + lax.broadcasted_iota(I32, (KEY_CHUNK, Q_ROWS), 1)
    pair = 2 * Q_ROWS

    for h in range(ATT_HEADS):
        qall_ref[:, h * Q_ROWS:(h + 1) * Q_ROWS] = qt_ref[h * KV_LATENT:(h + 1) * KV_LATENT, :]
    for h in range(IDX_HEADS):
        iqall_ref[:, h * Q_ROWS:(h + 1) * Q_ROWS] = iqt_ref[h * LANES:(h + 1) * LANES, :]
    head_w = [iwt_ref[IDX_DIM + h:IDX_DIM + h + 1, :] for h in range(IDX_HEADS)]

    def index_body(j, carry):
        k0 = pl.multiple_of(j * KEY_CHUNK, KEY_CHUNK)
        ikc = ik_ref[pl.ds(k0, KEY_CHUNK), :]
        isc = jnp.zeros((KEY_CHUNK, Q_ROWS), F32)
        for hp in range(IDX_HEADS // 2):
            z = _dot(ikc, iqall_ref[:, hp * pair:(hp + 1) * pair])
            for hh in range(2):
                isc = isc + head_w[2 * hp + hh] * jnp.maximum(z[:, hh * Q_ROWS:(hh + 1) * Q_ROWS], 0.0)
        isc = jnp.where(isc == 0.0, 0.0, isc)
        bits = pltpu.bitcast(isc, I32)
        key = jnp.bitwise_xor(bits, jnp.bitwise_and(jnp.right_shift(bits, 31), 0x7FFFFFFF))
        key = jnp.where(k0 + key_i <= tq, key, INT_MIN)
        keys_ref[pl.ds(k0, KEY_CHUNK), :] = key
        return carry

    lax.fori_loop(0, nch, index_body, 0)

    def count(pred):
        def body(j, acc):
            k0 = pl.multiple_of(j * KEY_CHUNK, KEY_CHUNK)
            hit = jnp.where(pred(keys_ref[pl.ds(k0, KEY_CHUNK), :], k0), 1.0, 0.0)
            return acc + jnp.sum(hit.reshape(KEY_CHUNK // COUNT_ROWS, COUNT_ROWS, Q_ROWS), axis=0)
        acc = lax.fori_loop(0, nch, body, jnp.zeros((COUNT_ROWS, Q_ROWS), F32))
        return jnp.sum(acc, axis=0, keepdims=True)

    def count_ge(cand):
        return count(lambda k, k0: k >= cand)

    def masked_max(bound):
        def body(j, acc):
            k0 = pl.multiple_of(j * KEY_CHUNK, KEY_CHUNK)
            key = keys_ref[pl.ds(k0, KEY_CHUNK), :]
            kept = jnp.where(key <= bound, key, INT_MIN)
            return jnp.maximum(acc, jnp.max(kept.reshape(KEY_CHUNK // COUNT_ROWS, COUNT_ROWS, Q_ROWS), axis=0))
        acc = lax.fori_loop(0, nch, body, jnp.full((COUNT_ROWS, Q_ROWS), INT_MIN, I32))
        return jnp.max(acc, axis=0, keepdims=True)

    n_all = (nch * KEY_CHUNK).astype(F32)
    n0 = count_ge(jnp.zeros((1, Q_ROWS), I32))
    pos0 = n0 >= kf
    state0 = (jnp.where(pos0, 0, INT_MIN).astype(I32), jnp.where(pos0, n0, n_all), jnp.where(pos0, 0.0, n0))

    def bracket_body(i, state):
        thr, n_lo, n_hi = state
        cand = thr + lax.shift_left(jnp.int32(1), 30 - i)
        cnt = count_ge(cand)
        ge = cnt >= kf
        return jnp.where(ge, cand, thr), jnp.where(ge, cnt, n_lo), jnp.where(ge, n_hi, cnt)

    thr1, n_lo, n_hi = lax.fori_loop(0, SEARCH_BITS, bracket_body, state0)
    low_bits = 31 - SEARCH_BITS
    has_topk = tq[0:1, :] + 1 >= topk
    in_bracket = n_lo - n_hi
    exact_cut = n_lo == kf
    few = jnp.logical_or(jnp.logical_not(has_topk), jnp.logical_or(exact_cut, in_bracket <= float(EXTRACT_MAX)))
    thr_ref[...] = jnp.broadcast_to(thr1, thr_ref.shape)
    redo_ref[0] = jnp.where(jnp.min(jnp.where(few, 1.0, 0.0)) > 0.0, 0, 1)

    @pl.when(redo_ref[0] == 0)
    def _():
        need = kf - n_hi
        bound = thr1 + (2 ** low_bits - 1)
        found = jnp.zeros((1, Q_ROWS), F32)
        thr_x = thr1
        for j in range(1, EXTRACT_MAX + 1):
            cur = masked_max(bound)
            inside = cur >= thr1
            found = found + jnp.where(inside, 1.0, 0.0)
            thr_x = jnp.where(jnp.logical_and(inside, need == float(j)), cur, thr_x)
            bound = jnp.where(inside, cur - 1, INT_MIN)
        use_x = jnp.logical_and(has_topk, jnp.logical_not(exact_cut))
        clash = jnp.logical_and(use_x, found != in_bracket)
        thr_ref[...] = jnp.broadcast_to(jnp.where(use_x, thr_x, thr1), thr_ref.shape)
        redo_ref[0] = jnp.where(jnp.max(jnp.where(clash, 1.0, 0.0)) > 0.0, 1, 0)

    @pl.when(redo_ref[0] == 1)
    def _():
        def bit_body(i, thr):
            cand = thr + lax.shift_left(jnp.int32(1), low_bits - 1 - i)
            return jnp.where(count_ge(cand) >= kf, cand, thr)

        thr_ref[...] = jnp.broadcast_to(lax.fori_loop(0, low_bits, bit_body, thr1), thr_ref.shape)

    thr = thr_ref[0:1, :]
    live = thr > INT_MIN
    cut_ref[...] = jnp.broadcast_to(jnp.where(live, jnp.int32(2 ** 30), -1), cut_ref.shape)
    n_ge = count_ge(thr)
    tie = jnp.max(jnp.where(jnp.logical_and(live, n_ge > kf), 1.0, 0.0))

    @pl.when(tie > 0.0)
    def _():
        need = kf - count(lambda k, k0: k > thr)

        def cut_body(i, cut):
            cand = cut + lax.shift_left(jnp.int32(1), pos_bits - 1 - i)
            below = count(lambda k, k0: jnp.logical_and(k == thr, k0 + key_i < cand))
            return jnp.where(below < need, cand, cut)

        cut = lax.fori_loop(0, pos_bits, cut_body, jnp.zeros((1, Q_ROWS), I32))
        cut_ref[...] = jnp.broadcast_to(jnp.where(live, cut, -1), cut_ref.shape)

    m_ref[...] = jnp.full(m_ref.shape, NEG_BIG, F32)
    l_ref[...] = jnp.zeros(l_ref.shape, F32)
    acc_ref[...] = jnp.zeros(acc_ref.shape, F32)
    cut_b = cut_ref[0:1, :]

    def attn_body(j, carry):
        k0 = pl.multiple_of(j * KEY_CHUNK, KEY_CHUNK)
        key = keys_ref[pl.ds(k0, KEY_CHUNK), :]
        pos = k0 + key_i
        sel = jnp.logical_or(key > thr, jnp.logical_and(key == thr, pos <= cut_b))
        rel = jnp.where(sel, (pos - t0).astype(F32), NEG_BIG)
        cc = c_ref[pl.ds(k0, KEY_CHUNK), :]
        cct = ct_ref[:, pl.ds(k0, KEY_CHUNK)]
        for hp in range(ATT_HEADS // 2):
            cs = slice(hp * pair, (hp + 1) * pair)
            s2 = _dot(cc, qall_ref[:, cs])
            ps, alphas = [], []
            for hh in range(2):
                h = 2 * hp + hh
                slope = LOG2_E * 2.0 ** (-(8.0 / ATT_HEADS) * (h + 1))
                s = s2[:, hh * Q_ROWS:(hh + 1) * Q_ROWS] + slope * rel
                m_old = m_ref[h:h + 1, :]
                m_new = jnp.maximum(m_old, jnp.max(s, axis=0, keepdims=True))
                alpha = jnp.exp2(m_old - m_new)
                p = jnp.exp2(s - m_new)
                l_ref[h:h + 1, :] = alpha * l_ref[h:h + 1, :] + jnp.sum(p, axis=0, keepdims=True)
                m_ref[h:h + 1, :] = m_new
                ps.append(p.astype(BF16))
                alphas.append(alpha)
            acc_ref[:, cs] = (acc_ref[:, cs] * jnp.concatenate(alphas, axis=1)
                              + _dot(cct, jnp.concatenate(ps, axis=1)))
        return carry

    lax.fori_loop(0, nch, attn_body, 0)

    yt = []
    for h in range(ATT_HEADS):
        o_lat = (acc_ref[:, h * Q_ROWS:(h + 1) * Q_ROWS] / l_ref[h:h + 1, :]).astype(BF16)
        yt.append(_dot(wuvt_ref[h], o_lat))
    o_ref[...] = jnp.concatenate(yt, axis=0).T.astype(o_ref.dtype)


def _dsa(qt, iqt, iwt, ik, c, ct, wuvt_p, *, bsz, t):
    n = bsz * t
    nq = t // Q_ROWS
    topk = min(TOPK_MAX, t // 4)
    assert t % KEY_CHUNK == 0
    width = ATT_HEADS * ATT_HEAD_DIM
    qcols = lambda h: pl.BlockSpec((h, Q_ROWS), lambda b, i: (0, b * nq + i))
    return pl.pallas_call(
        functools.partial(_dsa_kernel, topk=topk, pos_bits=(t - 1).bit_length()),
        grid=(bsz, nq),
        in_specs=[qcols(ATT_HEADS * KV_LATENT), qcols(IDX_HEADS * LANES), qcols(LANES),
                  pl.BlockSpec((t, LANES), lambda b, i: (b, 0)),
                  pl.BlockSpec((t, KV_LATENT), lambda b, i: (b, 0)),
                  pl.BlockSpec((KV_LATENT, t), lambda b, i: (0, b)),
                  pl.BlockSpec(wuvt_p.shape, lambda b, i: (0, 0, 0))],
        out_specs=pl.BlockSpec((Q_ROWS, width), lambda b, i: (b * nq + i, 0)),
        out_shape=jax.ShapeDtypeStruct((n, width), BF16),
        scratch_shapes=[pltpu.VMEM((t, Q_ROWS), I32),
                        pltpu.VMEM((8, Q_ROWS), I32),
                        pltpu.VMEM((8, Q_ROWS), I32),
                        pltpu.SMEM((1,), I32),
                        pltpu.VMEM((KV_LATENT, ATT_HEADS * Q_ROWS), BF16),
                        pltpu.VMEM((LANES, IDX_HEADS * Q_ROWS), BF16),
                        pltpu.VMEM((8, Q_ROWS), F32),
                        pltpu.VMEM((8, Q_ROWS), F32),
                        pltpu.VMEM((KV_LATENT, ATT_HEADS * Q_ROWS), F32)],
        compiler_params=pltpu.CompilerParams(dimension_semantics=("arbitrary", "arbitrary"),
                                             vmem_limit_bytes=VMEM_LIMIT),
        name="dsa",
    )(qt, iqt, iwt, ik, c, ct, wuvt_p)


def _mix_router_kernel(x_ref, ya_ref, yr_ref, wo_ref, g_ref, b_ref, wr_ref, br_ref,
                       x1_ref, x1w_ref, route_ref, tot_ref, carry_ref, *, alpha, tm):
    @pl.when(pl.program_id(0) == 0)
    def _():
        carry_ref[...] = jnp.zeros_like(carry_ref)

    half = ya_ref.shape[1]
    mix = _dot(ya_ref[...], wo_ref[0:half, :]) + _dot(yr_ref[...], wo_ref[half:, :])
    x1 = _layer_norm(alpha * x_ref[...] + mix, g_ref[...], b_ref[...])
    x1_ref[...] = x1
    x1w_ref[...] = _pack_words(x1)

    x_hi = x1.astype(BF16)
    x_lo = (x1 - x_hi.astype(F32)).astype(BF16)
    hh_hl = _dot(x_hi, wr_ref[...])
    logits = (hh_hl[:, :LANES] + hh_hl[:, LANES:] + _dot(x_lo, wr_ref[:, :LANES])) + br_ref[...]
    lane = lax.broadcasted_iota(I32, logits.shape, 1)
    lane_f = lane.astype(F32)
    work = jnp.where(lane < N_EXPERTS, logits, -jnp.inf)
    sel_f = jnp.zeros(logits.shape, F32)
    denom = jnp.zeros((tm, 1), F32)
    hits, experts, weights = [], [], []
    for _ in range(TOP_K):
        mx = jnp.max(work, axis=-1, keepdims=True)
        first = jnp.min(jnp.where(work == mx, lane_f, float(LANES)), axis=-1, keepdims=True)
        hit = lane_f == first
        e = jnp.exp(mx - (weights[0][1] if weights else mx))
        hits.append(hit)
        experts.append(first)
        weights.append((e, mx))
        denom = denom + e
        sel_f = sel_f + jnp.where(hit, 1.0, 0.0)
        work = jnp.where(hit, -jnp.inf, work)

    r2 = lax.broadcasted_iota(I32, (tm, tm), 0)
    c2 = lax.broadcasted_iota(I32, (tm, tm), 1)
    before = jnp.where(c2 < r2, 1.0, 0.0).astype(BF16)
    rank = _dot(before, sel_f.astype(BF16)) + carry_ref[...]
    carry = carry_ref[...] + jnp.sum(sel_f, axis=0, keepdims=True)
    carry_ref[...] = carry
    tot_ref[...] = jnp.broadcast_to(carry, tot_ref.shape)

    route = jnp.zeros(logits.shape, F32)
    for k in range(TOP_K):
        rank_k = jnp.sum(jnp.where(hits[k], rank, 0.0), axis=-1, keepdims=True)
        route = jnp.where(lane == k, experts[k], route)
        route = jnp.where(lane == TOP_K + k, rank_k, route)
        route = jnp.where(lane == 2 * TOP_K + k, weights[k][0] / denom, route)
    route_ref[...] = route


def _mix_router(x2, ya, yr, wo, g, b, wr_p, br_p, *, alpha, tm):
    n, d = x2.shape
    half = ya.shape[1]
    rows = lambda w: pl.BlockSpec((tm, w), lambda i: (i, 0))
    full = lambda shape: pl.BlockSpec(shape, lambda i: (0,) * len(shape))
    return pl.pallas_call(
        functools.partial(_mix_router_kernel, alpha=alpha, tm=tm),
        grid=(n // tm,),
        in_specs=[rows(d), rows(half), rows(half), full(wo.shape), full(g.shape), full(b.shape),
                  full(wr_p.shape), full(br_p.shape)],
        out_specs=[rows(d), rows(d // 2), rows(LANES), full((8, LANES))],
        out_shape=[jax.ShapeDtypeStruct((n, d), F32),
                   jax.ShapeDtypeStruct((n, d // 2), I32),
                   jax.ShapeDtypeStruct((n, LANES), F32),
                   jax.ShapeDtypeStruct((8, LANES), F32)],
        scratch_shapes=[pltpu.VMEM((1, LANES), F32)],
        compiler_params=pltpu.CompilerParams(dimension_semantics=("arbitrary",),
                                             vmem_limit_bytes=VMEM_LIMIT),
        name="mix_router",
    )(x2, ya, yr, wo, g, b, wr_p, br_p)


def _sc_gather_rows(table, idx):
    n_idx = idx.shape[0]
    width = table.shape[1]
    per_worker = n_idx // SC_WORKERS
    assert per_worker * SC_WORKERS == n_idx and per_worker % SC_ROWS == 0
    mesh = plsc.VectorSubcoreMesh(core_axis_name="c", subcore_axis_name="s",
                                  num_cores=SC_CORES, num_subcores=SC_SUBCORES)

    @functools.partial(
        pl.kernel, mesh=mesh,
        out_type=jax.ShapeDtypeStruct((n_idx, width), table.dtype),
        scratch_types=[pltpu.VMEM((SC_ROWS,), I32),
                       pltpu.VMEM((SC_ROWS, width), table.dtype),
                       pltpu.SemaphoreType.DMA])
    def gather(table_hbm, idx_hbm, out_hbm, idx_v, rows_v, sem):
        worker = lax.axis_index("s") * SC_CORES + lax.axis_index("c")

        @pl.loop(0, per_worker // SC_ROWS)
        def _(step):
            base = pl.multiple_of(worker * per_worker + step * SC_ROWS, SC_ROWS)
            pltpu.sync_copy(idx_hbm.at[pl.ds(base, SC_ROWS)], idx_v)
            pltpu.async_copy(table_hbm.at[idx_v], rows_v, sem).wait()
            pltpu.sync_copy(rows_v, out_hbm.at[pl.ds(base, SC_ROWS)])

    return gather(table, idx)


def _sc_scatter_rows(rows, dest3, cap):
    n, width = rows.shape
    chunks, picks, r = dest3.shape
    assert r == SC_SCATTER_ROWS and chunks * r == n and chunks % SC_WORKERS == 0
    per_worker = chunks // SC_WORKERS
    mesh = plsc.VectorSubcoreMesh(core_axis_name="c", subcore_axis_name="s",
                                  num_cores=SC_CORES, num_subcores=SC_SUBCORES)

    @functools.partial(
        pl.kernel, mesh=mesh,
        out_type=jax.ShapeDtypeStruct((cap, width), rows.dtype),
        scratch_types=[pltpu.VMEM((picks, r), I32),
                       pltpu.VMEM((r, width), rows.dtype),
                       pltpu.SemaphoreType.DMA])
    def scatter(rows_hbm, dest_hbm, out_hbm, idx_v, rows_v, sem):
        worker = lax.axis_index("s") * SC_CORES + lax.axis_index("c")

        @pl.loop(0, per_worker)
        def _(step):
            chunk = worker * per_worker + step
            base = pl.multiple_of(chunk * r, r)
            pltpu.sync_copy(rows_hbm.at[pl.ds(base, r)], rows_v)
            pltpu.sync_copy(dest_hbm.at[chunk], idx_v)
            for k in range(picks):
                pltpu.async_copy(rows_v, out_hbm.at[idx_v.at[k]], sem).wait()

    return scatter(rows, dest3)


def _pack_words(v):
    k = v.shape[1] // 2
    hi = pltpu.bitcast(v[:, :k].astype(BF16).astype(F32), I32)
    lo = pltpu.bitcast(v[:, k:].astype(BF16).astype(F32), I32)
    return jnp.bitwise_or(hi, lax.shift_right_logical(lo, jnp.int32(16)))


def _unpack_words(w, dtype=BF16):
    hi = pltpu.bitcast(jnp.bitwise_and(w, jnp.int32(-65536)), F32)
    lo = pltpu.bitcast(jnp.left_shift(w, 16), F32)
    return jnp.concatenate([hi, lo], axis=1).astype(dtype)


def _expert_kernel(exp_ref, blk_ref, new_ref, x_ref, wup_ref, bup_ref, wdn_ref, bdn_ref, o_ref,
                   wup_s, wdn_s, *, d_ff):
    del exp_ref, blk_ref

    @pl.when(new_ref[pl.program_id(0)] == 1)
    def _():
        wup_s[...] = wup_ref[0].astype(BF16)
        wdn_s[...] = wdn_ref[0].astype(BF16)

    half = MOE_ROWS // 2
    for r in range(2):
        rows = slice(r * half, (r + 1) * half)
        h = _dot(_unpack_words(x_ref[rows, :]), wup_s[...]) + bup_ref[0]
        glu = jnp.minimum(h[:, :d_ff], SWIGLU_LIMIT)
        lin = jnp.clip(h[:, d_ff:], -SWIGLU_LIMIT, SWIGLU_LIMIT)
        a = glu * _sigmoid(SWIGLU_ALPHA * glu) * (lin + 1.0)
        o_ref[rows, :] = _pack_words(_dot(a.astype(BF16), wdn_s[...]) + bdn_ref[0])


def _experts(blk_expert, blk_index, blk_new, xs, wup, bup, wdn, bdn):
    cap, words = xs.shape
    d = 2 * words
    d_ff = wdn.shape[1]
    row_map = lambda i, exp, blk, new: (blk[i], 0)
    exp_map3 = lambda i, exp, blk, new: (exp[i], 0, 0)
    grid_spec = pltpu.PrefetchScalarGridSpec(
        num_scalar_prefetch=3,
        grid=(cap // MOE_ROWS,),
        in_specs=[pl.BlockSpec((MOE_ROWS, words), row_map),
                  pl.BlockSpec((1, d, 2 * d_ff), exp_map3),
                  pl.BlockSpec((1, 1, 2 * d_ff), exp_map3),
                  pl.BlockSpec((1, d_ff, d), exp_map3),
                  pl.BlockSpec((1, 1, d), exp_map3)],
        out_specs=pl.BlockSpec((MOE_ROWS, words), row_map),
        scratch_shapes=[pltpu.VMEM((d, 2 * d_ff), BF16), pltpu.VMEM((d_ff, d), BF16)],
    )
    return pl.pallas_call(
        functools.partial(_expert_kernel, d_ff=d_ff),
        grid_spec=grid_spec,
        out_shape=jax.ShapeDtypeStruct((cap, words), I32),
        compiler_params=pltpu.CompilerParams(dimension_semantics=("arbitrary",),
                                             vmem_limit_bytes=VMEM_LIMIT),
        name="experts",
    )(blk_expert, blk_index, blk_new, xs, wup, bup, wdn, bdn)


def _route_tables(route, totals, *, cap):
    n = route.shape[0]
    expert = route[:, 0:TOP_K].astype(I32)
    rank = route[:, TOP_K:2 * TOP_K].astype(I32)
    counts = totals[0, :N_EXPERTS].astype(I32)
    padded = (counts + MOE_ROWS - 1) // MOE_ROWS * MOE_ROWS
    pad_end = jnp.cumsum(padded)
    pad_start = pad_end - padded
    dest = pad_start[expert] + rank
    dest_chunks = dest.reshape(n // SC_SCATTER_ROWS, SC_SCATTER_ROWS, TOP_K).transpose(0, 2, 1)
    n_blk = cap // MOE_ROWS
    used = pad_end[-1] // MOE_ROWS
    blk = jnp.minimum(jnp.arange(n_blk, dtype=I32), used - 1)
    blk_expert = jnp.minimum(jnp.sum(pad_end[None, :] <= (blk * MOE_ROWS)[:, None], axis=1),
                             N_EXPERTS - 1).astype(I32)
    blk_new = jnp.concatenate([jnp.ones((1,), I32), (blk_expert[1:] != blk_expert[:-1]).astype(I32)])
    return dest, dest_chunks, blk_expert, blk, blk_new


def _final_kernel(x1_ref, ya_ref, yb_ref, yc_ref, yd_ref, route_ref, p_ref, g1_ref, b1_ref,
                  wp_ref, wg_ref, g2_ref, b2_ref, *rest, alpha):
    o_ref = rest[-1]
    ple_lin = _dot(p_ref[...].astype(BF16), wp_ref[...])
    ffn = jnp.zeros(x1_ref.shape, F32)
    for k, yk_ref in enumerate((ya_ref, yb_ref, yc_ref, yd_ref)):
        gate_k = route_ref[:, 2 * TOP_K + k:2 * TOP_K + k + 1]
        ffn = ffn + gate_k * _unpack_words(yk_ref[...], F32)
    x2 = _layer_norm(alpha * x1_ref[...] + ffn, g1_ref[...], b1_ref[...])
    gate = _sigmoid(_dot(x2.astype(BF16), wg_ref[...]))
    o_ref[...] = _layer_norm(alpha * x2 + ple_lin * gate, g2_ref[...], b2_ref[...])


def _final(x1, y4w, route, p2, g1, b1, wp, wg, g2, b2, *, alpha, tm, part, parts, prev=None):
    n, d = x1.shape
    assert TOP_K == 4
    steps = n // parts // tm
    off = part * steps
    rows = lambda w: pl.BlockSpec((tm, w), lambda i: (off + i, 0))
    full = lambda shape: pl.BlockSpec(shape, lambda i: (0,) * len(shape))
    pick = lambda k: pl.BlockSpec((tm, d // 2), lambda i: (k * steps + i, 0))
    in_specs = [rows(d), pick(0), pick(1), pick(2), pick(3), rows(LANES), rows(p2.shape[1]),
                full(g1.shape), full(b1.shape),
                full(wp.shape), full(wg.shape), full(g2.shape), full(b2.shape)]
    args = [x1, y4w, y4w, y4w, y4w, route, p2, g1, b1, wp, wg, g2, b2]
    aliases = {}
    if prev is not None:
        in_specs.append(pl.BlockSpec(memory_space=pl.ANY))
        args.append(prev)
        aliases = {len(args) - 1: 0}
    return pl.pallas_call(
        functools.partial(_final_kernel, alpha=alpha),
        grid=(steps,),
        in_specs=in_specs,
        out_specs=rows(d),
        out_shape=jax.ShapeDtypeStruct((n, d), F32),
        input_output_aliases=aliases,
        compiler_params=pltpu.CompilerParams(dimension_semantics=("arbitrary",),
                                             vmem_limit_bytes=VMEM_LIMIT),
        name="final",
    )(*args)


def _pack_w_in(w_in):
    d = w_in.shape[0]
    sizes = (ATT_HEADS * ATT_HEAD_DIM, KV_LATENT, IDX_HEADS * IDX_DIM, IDX_DIM, IDX_HEADS,
             HG_HEADS * HG_DIM, HG_HEADS * HG_DIM, HG_HEADS * HG_DIM, HG_HEADS * HG_DIM)
    offs = [0]
    for s in sizes:
        offs.append(offs[-1] + s)
    sec = [w_in[:, offs[k]:offs[k + 1]] for k in range(len(sizes))]
    iq = sec[2].reshape(d, IDX_HEADS, IDX_DIM)
    iq = jnp.pad(iq, ((0, 0), (0, 0), (0, LANES - IDX_DIM))).reshape(d, IDX_HEADS * LANES)
    ikw = jnp.pad(jnp.concatenate([sec[3], sec[4]], axis=1), ((0, 0), (0, LANES - IDX_DIM - IDX_HEADS)))
    w_n = jnp.concatenate([sec[1], ikw, sec[5], sec[6], sec[7], sec[8]], axis=1).astype(BF16)
    w_t = jnp.concatenate([sec[0], iq, ikw, sec[1]], axis=1).T.astype(BF16)
    return w_n, w_t


def _block_diag_uk_t(w_uk):
    eye = jnp.eye(ATT_HEADS, dtype=w_uk.dtype)
    bd = jnp.einsum("rhd,hg->hrgd", w_uk, eye)
    return bd.reshape(ATT_HEADS * KV_LATENT, ATT_HEADS * ATT_HEAD_DIM).astype(BF16)


def _uv_t(w_uv):
    return jnp.transpose(w_uv, (1, 2, 0)).astype(BF16)


def _layer(x, p_l, w_in, kv_g, ik_g, ik_b, w_uk, w_uv, lb, hg_ng, w_o, ln_mix_g, ln_mix_b,
           w_router, b_router, w_up, b_up, w_down, b_down, ln_ffn_g, ln_ffn_b,
           w_ple_proj, w_ple_gate, ln_ple_g, ln_ple_b, *, alpha, tm, hg_tb):
    bsz, t, d = x.shape
    n = bsz * t
    x2 = x.reshape(n, d)
    row = lambda v: v.reshape(1, -1).astype(F32)
    pad_lane = lambda v: jnp.pad(row(v), ((0, 0), (0, LANES - v.shape[-1])))

    w_n, w_t = _pack_w_in(w_in)
    qt, iqt, iwt, c, ct, ik, hg = _inproj(x2, w_n, w_t, _block_diag_uk_t(w_uk), row(kv_g),
                                          kv_g.reshape(-1, 1).astype(F32), pad_lane(ik_g), pad_lane(ik_b), tm=tm)
    y_rec = _hgrn2(hg.reshape(bsz, t, -1), row(lb), row(hg_ng), tb=hg_tb)
    y_att = _dsa(qt, iqt, iwt, ik, c, ct, _uv_t(w_uv), bsz=bsz, t=t)

    wr_f = jnp.pad(w_router.astype(F32), ((0, 0), (0, LANES - N_EXPERTS)))
    wr_hi = wr_f.astype(BF16)
    wr_p = jnp.concatenate([wr_hi, (wr_f - wr_hi.astype(F32)).astype(BF16)], axis=1)
    x1, x1w, route, totals = _mix_router(
        x2, y_att, y_rec.reshape(n, -1), w_o.astype(BF16), row(ln_mix_g), row(ln_mix_b),
        wr_p, pad_lane(b_router), alpha=alpha, tm=tm)

    step = SC_WORKERS * SC_ROWS
    cap = -(-(n * TOP_K + N_EXPERTS * MOE_ROWS) // step) * step
    dest, dest_chunks, blk_expert, blk_index, blk_new = _route_tables(route, totals, cap=cap)
    d_ff = w_down.shape[1]
    xs = _sc_scatter_rows(x1w, dest_chunks, cap)
    outw = _experts(blk_expert, blk_index, blk_new, xs,
                    w_up.astype(F32), b_up.reshape(N_EXPERTS, 1, 2 * d_ff).astype(F32),
                    w_down.astype(F32), b_down.reshape(N_EXPERTS, 1, d).astype(F32))

    unit = max(tm, SC_WORKERS * SC_ROWS // TOP_K)
    parts = next(c for c in (COMBINE_PARTS, 2, 1) if n % (c * unit) == 0)
    out = None
    for part in range(parts):
        rows_p = slice(part * (n // parts), (part + 1) * (n // parts))
        y4w = _sc_gather_rows(outw, dest[rows_p].T.reshape(-1))
        out = _final(x1, y4w, route, p_l.reshape(n, -1), row(ln_ffn_g), row(ln_ffn_b),
                     w_ple_proj.astype(BF16), w_ple_gate.astype(BF16), row(ln_ple_g), row(ln_ple_b),
                     alpha=alpha, tm=tm, part=part, parts=parts, prev=out)
    return out.reshape(bsz, t, d)


def kernel(x, p, w_in, kv_norm_g, idx_k_norm_g, idx_k_norm_b, w_uk, w_uv, hg_lb_logits, hg_norm_g, w_o,
           ln_mix_g, ln_mix_b, w_router, b_router, w_up, b_up, w_down, b_down, ln_ffn_g, ln_ffn_b,
           w_ple_proj, w_ple_gate, ln_ple_g, ln_ple_b):
    depth = w_in.shape[0]
    alpha = (2.0 * depth) ** 0.25
    lower_bounds = jnp.cumsum(jax.nn.softmax(hg_lb_logits.astype(F32), axis=0), axis=0)
    n = x.shape[0] * x.shape[1]
    tm = min(512, n)
    hg_tb = min(512, x.shape[1])
    for l in range(depth):
        x = _layer(x, p[l], w_in[l], kv_norm_g[l], idx_k_norm_g[l], idx_k_norm_b[l], w_uk[l], w_uv[l],
                   lower_bounds[l], hg_norm_g[l], w_o[l], ln_mix_g[l], ln_mix_b[l], w_router[l],
                   b_router[l], w_up[l], b_up[l], w_down[l], b_down[l], ln_ffn_g[l], ln_ffn_b[l],
                   w_ple_proj[l], w_ple_gate[l], ln_ple_g[l], ln_ple_b[l],
                   alpha=alpha, tm=tm, hg_tb=hg_tb)
    return x
```

```python
import functools

import jax
import jax.numpy as jnp
from jax import lax
from jax.experimental import pallas as pl
from jax.experimental.pallas import tpu as pltpu
from jax.experimental.pallas import tpu_sc as plsc

F32 = jnp.float32
BF16 = jnp.bfloat16
I32 = jnp.int32

ATT_HEADS = 8
ATT_HEAD_DIM = 64
KV_LATENT = 256
IDX_HEADS = 8
IDX_DIM = 64
TOPK_MAX = 256
HG_HEADS = 4
HG_DIM = 128
HG_CHUNK = 32
N_EXPERTS = 32
TOP_K = 4
SWIGLU_LIMIT = 7.0
SWIGLU_ALPHA = 1.702
LN_EPS = 1e-5
RMS_EPS = 1e-6

LANES = 128
Q_ROWS = 128
KEY_CHUNK = 512
COUNT_ROWS = 128
SEARCH_BITS = 20
EXTRACT_MAX = 3
MOE_ROWS = 512
SC_CORES = 2
SC_SUBCORES = 16
SC_WORKERS = SC_CORES * SC_SUBCORES
SC_ROWS = 128
SC_SCATTER_ROWS = 128
COMBINE_PARTS = 8
VMEM_LIMIT = 56 * 1024 * 1024

INT_MIN = -(2 ** 31)
NEG_BIG = -1e30
LOG2_E = 1.4426950408889634

_N_KV = 0
_N_IKW = _N_KV + KV_LATENT
_N_HG = _N_IKW + LANES
_N_END = _N_HG + 4 * HG_HEADS * HG_DIM
_T_AQ = 0
_T_IQ = _T_AQ + ATT_HEADS * ATT_HEAD_DIM
_T_IKW = _T_IQ + IDX_HEADS * LANES
_T_KV = _T_IKW + LANES
_T_END = _T_KV + KV_LATENT


def _dot(a, b):
    return jnp.dot(a, b, preferred_element_type=F32)


def _dot_nt(a, b):
    return lax.dot_general(a, b, (((1,), (1,)), ((), ())), preferred_element_type=F32)


def _dot_tn(a, b):
    return lax.dot_general(a, b, (((0,), (0,)), ((), ())), preferred_element_type=F32)


def _layer_norm(z, g, b):
    mu = jnp.mean(z, axis=-1, keepdims=True)
    d = z - mu
    var = jnp.mean(d * d, axis=-1, keepdims=True)
    return d * lax.rsqrt(var + LN_EPS) * g + b


def _sigmoid(x):
    return 1.0 / (1.0 + jnp.exp(-x))


def _inproj_kernel(x_ref, wn_ref, wt_ref, wukt_ref, kvg_ref, kvgc_ref, ikg_ref, ikb_ref,
                   qt_ref, iqt_ref, iwt_ref, c_ref, ct_ref, ik_ref, hg_ref):
    xb = x_ref[...].astype(BF16)

    aqt = _dot_nt(wt_ref[_T_AQ:_T_IQ, :], xb).astype(BF16)
    iqt_ref[...] = (_dot_nt(wt_ref[_T_IQ:_T_IKW, :], xb) * (IDX_DIM ** -0.5)).astype(BF16)
    iwt_ref[...] = _dot_nt(wt_ref[_T_IKW:_T_KV, :], xb) * (IDX_HEADS ** -0.5)
    act = _dot_nt(wt_ref[_T_KV:_T_END, :], xb)
    ct = act * lax.rsqrt(jnp.mean(act * act, axis=0, keepdims=True) + RMS_EPS) * kvgc_ref[...]
    ct_ref[...] = ct.astype(BF16)
    qt_ref[...] = (_dot(wukt_ref[...], aqt) * (LOG2_E * ATT_HEAD_DIM ** -0.5)).astype(BF16)

    ac = _dot(xb, wn_ref[:, _N_KV:_N_IKW])
    c = ac * lax.rsqrt(jnp.mean(ac * ac, axis=-1, keepdims=True) + RMS_EPS) * kvg_ref[...]
    c_ref[...] = c.astype(BF16)

    ikw = _dot(xb, wn_ref[:, _N_IKW:_N_HG])
    lane = lax.broadcasted_iota(I32, ikw.shape, 1)
    is_k = lane < IDX_DIM
    mu = jnp.sum(jnp.where(is_k, ikw, 0.0), axis=-1, keepdims=True) * (1.0 / IDX_DIM)
    d = jnp.where(is_k, ikw - mu, 0.0)
    var = jnp.sum(d * d, axis=-1, keepdims=True) * (1.0 / IDX_DIM)
    ik = d * lax.rsqrt(var + LN_EPS) * ikg_ref[...] + ikb_ref[...]
    ik_ref[...] = ik.astype(BF16)

    hg_ref[...] = _dot(xb, wn_ref[:, _N_HG:_N_END])


def _inproj(x2, w_n, w_t, wuk_t, kvg, kvg_col, ikg, ikb, *, tm):
    n, d = x2.shape
    grid = (n // tm,)
    full = lambda shape: pl.BlockSpec(shape, lambda i: (0,) * len(shape))
    rows = lambda w: pl.BlockSpec((tm, w), lambda i: (i, 0))
    cols = lambda h: pl.BlockSpec((h, tm), lambda i: (0, i))
    return pl.pallas_call(
        _inproj_kernel,
        grid=grid,
        in_specs=[rows(d), full(w_n.shape), full(w_t.shape), full(wuk_t.shape), full(kvg.shape),
                  full(kvg_col.shape), full(ikg.shape), full(ikb.shape)],
        out_specs=[cols(ATT_HEADS * KV_LATENT), cols(IDX_HEADS * LANES), cols(LANES),
                   rows(KV_LATENT), cols(KV_LATENT), rows(LANES), rows(4 * HG_HEADS * HG_DIM)],
        out_shape=[jax.ShapeDtypeStruct((ATT_HEADS * KV_LATENT, n), BF16),
                   jax.ShapeDtypeStruct((IDX_HEADS * LANES, n), BF16),
                   jax.ShapeDtypeStruct((LANES, n), F32),
                   jax.ShapeDtypeStruct((n, KV_LATENT), BF16),
                   jax.ShapeDtypeStruct((KV_LATENT, n), BF16),
                   jax.ShapeDtypeStruct((n, LANES), BF16),
                   jax.ShapeDtypeStruct((n, 4 * HG_HEADS * HG_DIM), F32)],
        compiler_params=pltpu.CompilerParams(dimension_semantics=("arbitrary",),
                                             vmem_limit_bytes=VMEM_LIMIT),
        name="inproj",
    )(x2, w_n, w_t, wuk_t, kvg, kvg_col, ikg, ikb)


def _hgrn2_kernel(hg_ref, lb_ref, ng_ref, o_ref, state_ref, *, tb):
    @pl.when(pl.program_id(1) == 0)
    def _():
        state_ref[...] = jnp.zeros_like(state_ref)

    nchunk = tb // HG_CHUNK
    width = HG_HEADS * HG_DIM
    row = lax.broadcasted_iota(I32, (tb, HG_DIM), 0)
    rin = jnp.bitwise_and(row, HG_CHUNK - 1)
    r2 = lax.broadcasted_iota(I32, (tb, tb), 0)
    c2 = lax.broadcasted_iota(I32, (tb, tb), 1)
    intra = jnp.logical_and(r2 // HG_CHUNK == c2 // HG_CHUNK, c2 <= r2)

    for h in range(HG_HEADS):
        sl = slice(h * HG_DIM, (h + 1) * HG_DIM)
        gq = hg_ref[0, :, h * HG_DIM:(h + 1) * HG_DIM]
        gf = hg_ref[0, :, width + h * HG_DIM:width + (h + 1) * HG_DIM]
        gi = hg_ref[0, :, 2 * width + h * HG_DIM:2 * width + (h + 1) * HG_DIM]
        gg = hg_ref[0, :, 3 * width + h * HG_DIM:3 * width + (h + 1) * HG_DIM]
        lb = lb_ref[:, sl]
        forget = lb + (1.0 - lb) * _sigmoid(gf)
        logf = jnp.log(forget)
        b = logf
        s = 1
        while s < HG_CHUNK:
            b = b + jnp.where(rin >= s, pltpu.roll(b, s, axis=0), 0.0)
            s *= 2
        tot = jnp.sum(logf.reshape(nchunk, HG_CHUNK, HG_DIM), axis=1, keepdims=True)
        bl = jnp.broadcast_to(tot, (nchunk, HG_CHUNK, HG_DIM)).reshape(tb, HG_DIM)
        kk = 1.0 - forget
        q_dec = (gq * _sigmoid(gq) * jnp.exp(b)).astype(BF16)
        k_inv = (kk * jnp.exp(-b)).astype(BF16)
        k_end = (kk * jnp.exp(bl - b)).astype(BF16)
        vb = gi.astype(BF16)
        scores = jnp.where(intra, _dot_nt(q_dec, k_inv), 0.0).astype(BF16)
        o = _dot(scores, vb)
        decay = jnp.exp(tot)
        chunk_rows = [slice(n * HG_CHUNK, (n + 1) * HG_CHUNK) for n in range(nchunk)]
        incs = [_dot_tn(vb[rs], k_end[rs]) for rs in chunk_rows]
        st = state_ref[h]
        starts = []
        for n in range(nchunk):
            starts.append(st.astype(BF16))
            st = st * decay[n] + incs[n]
        state_ref[h] = st
        inter = [_dot_nt(q_dec[rs], starts[n]) for n, rs in enumerate(chunk_rows)]
        o = o + jnp.concatenate(inter, axis=0)
        o = o * lax.rsqrt(jnp.mean(o * o, axis=-1, keepdims=True) + RMS_EPS)
        o = o * ng_ref[:, sl] * (gg * _sigmoid(gg))
        o_ref[0, :, sl] = o.astype(o_ref.dtype)


def _hgrn2(hg3, lb, ng, *, tb):
    bsz, t, w4 = hg3.shape
    width = HG_HEADS * HG_DIM
    return pl.pallas_call(
        functools.partial(_hgrn2_kernel, tb=tb),
        grid=(bsz, t // tb),
        in_specs=[pl.BlockSpec((1, tb, w4), lambda b, i: (b, i, 0)),
                  pl.BlockSpec((1, width), lambda b, i: (0, 0)),
                  pl.BlockSpec((1, width), lambda b, i: (0, 0))],
        out_specs=pl.BlockSpec((1, tb, width), lambda b, i: (b, i, 0)),
        out_shape=jax.ShapeDtypeStruct((bsz, t, width), BF16),
        scratch_shapes=[pltpu.VMEM((HG_HEADS, HG_DIM, HG_DIM), F32)],
        compiler_params=pltpu.CompilerParams(dimension_semantics=("arbitrary", "arbitrary"),
                                             vmem_limit_bytes=VMEM_LIMIT),
        name="hgrn2",
    )(hg3, lb, ng)


def _dsa_kernel(qt_ref, iqt_ref, iwt_ref, ik_ref, c_ref, ct_ref, wuvt_ref, o_ref,
                keys_ref, cut_ref, thr_ref, redo_ref, qall_ref, iqall_ref, m_ref, l_ref, acc_ref,
                *, topk, pos_bits):
    n = pl.program_id(1)
    t0 = n * Q_ROWS
    nch = (t0 + Q_ROWS + KEY_CHUNK - 1) // KEY_CHUNK
    kf = float(topk)
    key_i = lax.broadcasted_iota(I32, (KEY_CHUNK, Q_ROWS), 0)
    tq = t0 + lax.broadcasted_iota(I32, (KEY_CHUNK, Q_ROWS), 1)
    pair = 2 * Q_ROWS

    for h in range(ATT_HEADS):
        qall_ref[:, h * Q_ROWS:(h + 1) * Q_ROWS] = qt_ref[h * KV_LATENT:(h + 1) * KV_LATENT, :]
    for h in range(IDX_HEADS):
        iqall_ref[:, h * Q_ROWS:(h + 1) * Q_ROWS] = iqt_ref[h * LANES:(h + 1) * LANES, :]
    head_w = [iwt_ref[IDX_DIM + h:IDX_DIM + h + 1, :] for h in range(IDX_HEADS)]

    def index_body(j, carry):
        k0 = pl.multiple_of(j * KEY_CHUNK, KEY_CHUNK)
        ikc = ik_ref[pl.ds(k0, KEY_CHUNK), :]
        isc = jnp.zeros((KEY_CHUNK, Q_ROWS), F32)
        for hp in range(IDX_HEADS // 2):
            z = _dot(ikc, iqall_ref[:, hp * pair:(hp + 1) * pair])
            for hh in range(2):
                isc = isc + head_w[2 * hp + hh] * jnp.maximum(z[:, hh * Q_ROWS:(hh + 1) * Q_ROWS], 0.0)
        isc = jnp.where(isc == 0.0, 0.0, isc)
        bits = pltpu.bitcast(isc, I32)
        key = jnp.bitwise_xor(bits, jnp.bitwise_and(jnp.right_shift(bits, 31), 0x7FFFFFFF))
        key = jnp.where(k0 + key_i <= tq, key, INT_MIN)
        keys_ref[pl.ds(k0, KEY_CHUNK), :] = key
        return carry

    lax.fori_loop(0, nch, index_body, 0)

    def count(pred):
        def body(j, acc):
            k0 = pl.multiple_of(j * KEY_CHUNK, KEY_CHUNK)
            hit = jnp.where(pred(keys_ref[pl.ds(k0, KEY_CHUNK), :], k0), 1.0, 0.0)
            return acc + jnp.sum(hit.reshape(KEY_CHUNK // COUNT_ROWS, COUNT_ROWS, Q_ROWS), axis=0)
        acc = lax.fori_loop(0, nch, body, jnp.zeros((COUNT_ROWS, Q_ROWS), F32))
        return jnp.sum(acc, axis=0, keepdims=True)

    def count_ge(cand):
        return count(lambda k, k0: k >= cand)

    def masked_max(bound):
        def body(j, acc):
            k0 = pl.multiple_of(j * KEY_CHUNK, KEY_CHUNK)
            key = keys_ref[pl.ds(k0, KEY_CHUNK), :]
            kept = jnp.where(key <= bound, key, INT_MIN)
            return jnp.maximum(acc, jnp.max(kept.reshape(KEY_CHUNK // COUNT_ROWS, COUNT_ROWS, Q_ROWS), axis=0))
        acc = lax.fori_loop(0, nch, body, jnp.full((COUNT_ROWS, Q_ROWS), INT_MIN, I32))
        return jnp.max(acc, axis=0, keepdims=True)

    n_all = (nch * KEY_CHUNK).astype(F32)
    n0 = count_ge(jnp.zeros((1, Q_ROWS), I32))
    pos0 = n0 >= kf
    state0 = (jnp.where(pos0, 0, INT_MIN).astype(I32), jnp.where(pos0, n0, n_all), jnp.where(pos0, 0.0, n0))

    def bracket_body(i, state):
        thr, n_lo, n_hi = state
        cand = thr + lax.shift_left(jnp.int32(1), 30 - i)
        cnt = count_ge(cand)
        ge = cnt >= kf
        return jnp.where(ge, cand, thr), jnp.where(ge, cnt, n_lo), jnp.where(ge, n_hi, cnt)

    thr1, n_lo, n_hi = lax.fori_loop(0, SEARCH_BITS, bracket_body, state0)
    low_bits = 31 - SEARCH_BITS
    has_topk = tq[0:1, :] + 1 >= topk
    in_bracket = n_lo - n_hi
    exact_cut = n_lo == kf
    few = jnp.logical_or(jnp.logical_not(has_topk), jnp.logical_or(exact_cut, in_bracket <= float(EXTRACT_MAX)))
    thr_ref[...] = jnp.broadcast_to(thr1, thr_ref.shape)
    redo_ref[0] = jnp.where(jnp.min(jnp.where(few, 1.0, 0.0)) > 0.0, 0, 1)

    @pl.when(redo_ref[0] == 0)
    def _():
        need = kf - n_hi
        bound = thr1 + (2 ** low_bits - 1)
        found = jnp.zeros((1, Q_ROWS), F32)
        thr_x = thr1
        for j in range(1, EXTRACT_MAX + 1):
            cur = masked_max(bound)
            inside = cur >= thr1
            found = found + jnp.where(inside, 1.0, 0.0)
            thr_x = jnp.where(jnp.logical_and(inside, need == float(j)), cur, thr_x)
            bound = jnp.where(inside, cur - 1, INT_MIN)
        use_x = jnp.logical_and(has_topk, jnp.logical_not(exact_cut))
        clash = jnp.logical_and(use_x, found != in_bracket)
        thr_ref[...] = jnp.broadcast_to(jnp.where(use_x, thr_x, thr1), thr_ref.shape)
        redo_ref[0] = jnp.where(jnp.max(jnp.where(clash, 1.0, 0.0)) > 0.0, 1, 0)

    @pl.when(redo_ref[0] == 1)
    def _():
        def bit_body(i, thr):
            cand = thr + lax.shift_left(jnp.int32(1), low_bits - 1 - i)
            return jnp.where(count_ge(cand) >= kf, cand, thr)

        thr_ref[...] = jnp.broadcast_to(lax.fori_loop(0, low_bits, bit_body, thr1), thr_ref.shape)

    thr = thr_ref[0:1, :]
    live = thr > INT_MIN
    cut_ref[...] = jnp.broadcast_to(jnp.where(live, jnp.int32(2 ** 30), -1), cut_ref.shape)

    @pl.when(redo_ref[0] == 1)
    def _():
        n_ge = count_ge(thr)
        tie = jnp.max(jnp.where(jnp.logical_and(live, n_ge > kf), 1.0, 0.0))

        @pl.when(tie > 0.0)
        def _():
            need = kf - count(lambda k, k0: k > thr)

            def cut_body(i, cut):
                cand = cut + lax.shift_left(jnp.int32(1), pos_bits - 1 - i)
                below = count(lambda k, k0: jnp.logical_and(k == thr, k0 + key_i < cand))
                return jnp.where(below < need, cand, cut)

            cut = lax.fori_loop(0, pos_bits, cut_body, jnp.zeros((1, Q_ROWS), I32))
            cut_ref[...] = jnp.broadcast_to(jnp.where(live, cut, -1), cut_ref.shape)

    m_ref[...] = jnp.full(m_ref.shape, NEG_BIG, F32)
    l_ref[...] = jnp.zeros(l_ref.shape, F32)
    acc_ref[...] = jnp.zeros(acc_ref.shape, F32)
    cut_b = cut_ref[0:1, :]

    def attn_body(j, carry):
        k0 = pl.multiple_of(j * KEY_CHUNK, KEY_CHUNK)
        key = keys_ref[pl.ds(k0, KEY_CHUNK), :]
        pos = k0 + key_i
        sel = jnp.logical_or(key > thr, jnp.logical_and(key == thr, pos <= cut_b))
        rel = jnp.where(sel, (pos - t0).astype(F32), NEG_BIG)
        cc = c_ref[pl.ds(k0, KEY_CHUNK), :]
        cct = ct_ref[:, pl.ds(k0, KEY_CHUNK)]
        for hp in range(ATT_HEADS // 2):
            cs = slice(hp * pair, (hp + 1) * pair)
            s2 = _dot(cc, qall_ref[:, cs])
            ps, alphas = [], []
            for hh in range(2):
                h = 2 * hp + hh
                slope = LOG2_E * 2.0 ** (-(8.0 / ATT_HEADS) * (h + 1))
                s = s2[:, hh * Q_ROWS:(hh + 1) * Q_ROWS] + slope * rel
                m_old = m_ref[h:h + 1, :]
                m_new = jnp.maximum(m_old, jnp.max(s, axis=0, keepdims=True))
                alpha = jnp.exp2(m_old - m_new)
                p = jnp.exp2(s - m_new)
                l_ref[h:h + 1, :] = alpha * l_ref[h:h + 1, :] + jnp.sum(p, axis=0, keepdims=True)
                m_ref[h:h + 1, :] = m_new
                ps.append(p.astype(BF16))
                alphas.append(alpha)
            acc_ref[:, cs] = (acc_ref[:, cs] * jnp.concatenate(alphas, axis=1)
                              + _dot(cct, jnp.concatenate(ps, axis=1)))
        return carry

    lax.fori_loop(0, nch, attn_body, 0)

    yt = []
    for h in range(ATT_HEADS):
        o_lat = (acc_ref[:, h * Q_ROWS:(h + 1) * Q_ROWS] / l_ref[h:h + 1, :]).astype(BF16)
        yt.append(_dot(wuvt_ref[h], o_lat))
    o_ref[...] = jnp.concatenate(yt, axis=0).T.astype(o_ref.dtype)


def _dsa(qt, iqt, iwt, ik, c, ct, wuvt_p, *, bsz, t):
    n = bsz * t
    nq = t // Q_ROWS
    topk = min(TOPK_MAX, t // 4)
    assert t % KEY_CHUNK == 0
    width = ATT_HEADS * ATT_HEAD_DIM
    qcols = lambda h: pl.BlockSpec((h, Q_ROWS), lambda b, i: (0, b * nq + i))
    return pl.pallas_call(
        functools.partial(_dsa_kernel, topk=topk, pos_bits=(t - 1).bit_length()),
        grid=(bsz, nq),
        in_specs=[qcols(ATT_HEADS * KV_LATENT), qcols(IDX_HEADS * LANES), qcols(LANES),
                  pl.BlockSpec((t, LANES), lambda b, i: (b, 0)),
                  pl.BlockSpec((t, KV_LATENT), lambda b, i: (b, 0)),
                  pl.BlockSpec((KV_LATENT, t), lambda b, i: (0, b)),
                  pl.BlockSpec(wuvt_p.shape, lambda b, i: (0, 0, 0))],
        out_specs=pl.BlockSpec((Q_ROWS, width), lambda b, i: (b * nq + i, 0)),
        out_shape=jax.ShapeDtypeStruct((n, width), BF16),
        scratch_shapes=[pltpu.VMEM((t, Q_ROWS), I32),
                        pltpu.VMEM((8, Q_ROWS), I32),
                        pltpu.VMEM((8, Q_ROWS), I32),
                        pltpu.SMEM((1,), I32),
                        pltpu.VMEM((KV_LATENT, ATT_HEADS * Q_ROWS), BF16),
                        pltpu.VMEM((LANES, IDX_HEADS * Q_ROWS), BF16),
                        pltpu.VMEM((8, Q_ROWS), F32),
                        pltpu.VMEM((8, Q_ROWS), F32),
                        pltpu.VMEM((KV_LATENT, ATT_HEADS * Q_ROWS), F32)],
        compiler_params=pltpu.CompilerParams(dimension_semantics=("arbitrary", "arbitrary"),
                                             vmem_limit_bytes=VMEM_LIMIT),
        name="dsa",
    )(qt, iqt, iwt, ik, c, ct, wuvt_p)


def _mix_router_kernel(x_ref, ya_ref, yr_ref, wo_ref, g_ref, b_ref, wr_ref, br_ref,
                       x1_ref, x1w_ref, route_ref, tot_ref, carry_ref, *, alpha, tm):
    @pl.when(pl.program_id(0) == 0)
    def _():
        carry_ref[...] = jnp.zeros_like(carry_ref)

    half = ya_ref.shape[1]
    mix = _dot(ya_ref[...], wo_ref[0:half, :]) + _dot(yr_ref[...], wo_ref[half:, :])
    x1 = _layer_norm(alpha * x_ref[...] + mix, g_ref[...], b_ref[...])
    x1_ref[...] = x1
    x1w_ref[...] = _pack_words(x1)

    x_hi = x1.astype(BF16)
    x_lo = (x1 - x_hi.astype(F32)).astype(BF16)
    hh_hl = _dot(x_hi, wr_ref[...])
    logits = (hh_hl[:, :LANES] + hh_hl[:, LANES:] + _dot(x_lo, wr_ref[:, :LANES])) + br_ref[...]
    lane = lax.broadcasted_iota(I32, logits.shape, 1)
    lane_f = lane.astype(F32)
    work = jnp.where(lane < N_EXPERTS, logits, -jnp.inf)
    sel_f = jnp.zeros(logits.shape, F32)
    denom = jnp.zeros((tm, 1), F32)
    hits, experts, weights = [], [], []
    for _ in range(TOP_K):
        mx = jnp.max(work, axis=-1, keepdims=True)
        first = jnp.min(jnp.where(work == mx, lane_f, float(LANES)), axis=-1, keepdims=True)
        hit = lane_f == first
        e = jnp.exp(mx - (weights[0][1] if weights else mx))
        hits.append(hit)
        experts.append(first)
        weights.append((e, mx))
        denom = denom + e
        sel_f = sel_f + jnp.where(hit, 1.0, 0.0)
        work = jnp.where(hit, -jnp.inf, work)

    r2 = lax.broadcasted_iota(I32, (tm, tm), 0)
    c2 = lax.broadcasted_iota(I32, (tm, tm), 1)
    before = jnp.where(c2 < r2, 1.0, 0.0).astype(BF16)
    rank = _dot(before, sel_f.astype(BF16)) + carry_ref[...]
    carry = carry_ref[...] + jnp.sum(sel_f, axis=0, keepdims=True)
    carry_ref[...] = carry
    tot_ref[...] = jnp.broadcast_to(carry, tot_ref.shape)

    route = jnp.zeros(logits.shape, F32)
    for k in range(TOP_K):
        rank_k = jnp.sum(jnp.where(hits[k], rank, 0.0), axis=-1, keepdims=True)
        route = jnp.where(lane == k, experts[k], route)
        route = jnp.where(lane == TOP_K + k, rank_k, route)
        route = jnp.where(lane == 2 * TOP_K + k, weights[k][0] / denom, route)
    route_ref[...] = route


def _mix_router(x2, ya, yr, wo, g, b, wr_p, br_p, *, alpha, tm):
    n, d = x2.shape
    half = ya.shape[1]
    rows = lambda w: pl.BlockSpec((tm, w), lambda i: (i, 0))
    full = lambda shape: pl.BlockSpec(shape, lambda i: (0,) * len(shape))
    return pl.pallas_call(
        functools.partial(_mix_router_kernel, alpha=alpha, tm=tm),
        grid=(n // tm,),
        in_specs=[rows(d), rows(half), rows(half), full(wo.shape), full(g.shape), full(b.shape),
                  full(wr_p.shape), full(br_p.shape)],
        out_specs=[rows(d), rows(d // 2), rows(LANES), full((8, LANES))],
        out_shape=[jax.ShapeDtypeStruct((n, d), F32),
                   jax.ShapeDtypeStruct((n, d // 2), I32),
                   jax.ShapeDtypeStruct((n, LANES), F32),
                   jax.ShapeDtypeStruct((8, LANES), F32)],
        scratch_shapes=[pltpu.VMEM((1, LANES), F32)],
        compiler_params=pltpu.CompilerParams(dimension_semantics=("arbitrary",),
                                             vmem_limit_bytes=VMEM_LIMIT),
        name="mix_router",
    )(x2, ya, yr, wo, g, b, wr_p, br_p)


def _sc_gather_rows(table, idx):
    n_idx = idx.shape[0]
    width = table.shape[1]
    per_worker = n_idx // SC_WORKERS
    assert per_worker * SC_WORKERS == n_idx and per_worker % SC_ROWS == 0
    mesh = plsc.VectorSubcoreMesh(core_axis_name="c", subcore_axis_name="s",
                                  num_cores=SC_CORES, num_subcores=SC_SUBCORES)

    @functools.partial(
        pl.kernel, mesh=mesh,
        out_type=jax.ShapeDtypeStruct((n_idx, width), table.dtype),
        scratch_types=[pltpu.VMEM((SC_ROWS,), I32),
                       pltpu.VMEM((SC_ROWS, width), table.dtype),
                       pltpu.SemaphoreType.DMA])
    def gather(table_hbm, idx_hbm, out_hbm, idx_v, rows_v, sem):
        worker = lax.axis_index("s") * SC_CORES + lax.axis_index("c")

        @pl.loop(0, per_worker // SC_ROWS)
        def _(step):
            base = pl.multiple_of(worker * per_worker + step * SC_ROWS, SC_ROWS)
            pltpu.sync_copy(idx_hbm.at[pl.ds(base, SC_ROWS)], idx_v)
            pltpu.async_copy(table_hbm.at[idx_v], rows_v, sem).wait()
            pltpu.sync_copy(rows_v, out_hbm.at[pl.ds(base, SC_ROWS)])

    return gather(table, idx)


def _sc_scatter_rows(rows, dest3, cap):
    n, width = rows.shape
    chunks, picks, r = dest3.shape
    assert r == SC_SCATTER_ROWS and chunks * r == n and chunks % SC_WORKERS == 0
    per_worker = chunks // SC_WORKERS
    mesh = plsc.VectorSubcoreMesh(core_axis_name="c", subcore_axis_name="s",
                                  num_cores=SC_CORES, num_subcores=SC_SUBCORES)

    @functools.partial(
        pl.kernel, mesh=mesh,
        out_type=jax.ShapeDtypeStruct((cap, width), rows.dtype),
        scratch_types=[pltpu.VMEM((picks, r), I32),
                       pltpu.VMEM((r, width), rows.dtype),
                       pltpu.SemaphoreType.DMA])
    def scatter(rows_hbm, dest_hbm, out_hbm, idx_v, rows_v, sem):
        worker = lax.axis_index("s") * SC_CORES + lax.axis_index("c")

        @pl.loop(0, per_worker)
        def _(step):
            chunk = worker * per_worker + step
            base = pl.multiple_of(chunk * r, r)
            pltpu.sync_copy(rows_hbm.at[pl.ds(base, r)], rows_v)
            pltpu.sync_copy(dest_hbm.at[chunk], idx_v)
            for k in range(picks):
                pltpu.async_copy(rows_v, out_hbm.at[idx_v.at[k]], sem).wait()

    return scatter(rows, dest3)


def _pack_words(v):
    k = v.shape[1] // 2
    hi = pltpu.bitcast(v[:, :k].astype(BF16).astype(F32), I32)
    lo = pltpu.bitcast(v[:, k:].astype(BF16).astype(F32), I32)
    return jnp.bitwise_or(hi, lax.shift_right_logical(lo, jnp.int32(16)))


def _unpack_words(w, dtype=BF16):
    hi = pltpu.bitcast(jnp.bitwise_and(w, jnp.int32(-65536)), F32)
    lo = pltpu.bitcast(jnp.left_shift(w, 16), F32)
    return jnp.concatenate([hi, lo], axis=1).astype(dtype)


def _expert_kernel(exp_ref, blk_ref, new_ref, x_ref, wup_ref, bup_ref, wdn_ref, bdn_ref, o_ref,
                   wup_s, wdn_s, *, d_ff):
    del exp_ref, blk_ref

    @pl.when(new_ref[pl.program_id(0)] == 1)
    def _():
        wup_s[...] = wup_ref[0].astype(BF16)
        wdn_s[...] = wdn_ref[0].astype(BF16)

    half = MOE_ROWS // 2
    for r in range(2):
        rows = slice(r * half, (r + 1) * half)
        h = _dot(_unpack_words(x_ref[rows, :]), wup_s[...]) + bup_ref[0]
        glu = jnp.minimum(h[:, :d_ff], SWIGLU_LIMIT)
        lin = jnp.clip(h[:, d_ff:], -SWIGLU_LIMIT, SWIGLU_LIMIT)
        a = glu * _sigmoid(SWIGLU_ALPHA * glu) * (lin + 1.0)
        o_ref[rows, :] = _pack_words(_dot(a.astype(BF16), wdn_s[...]) + bdn_ref[0])


def _experts(blk_expert, blk_index, blk_new, xs, wup, bup, wdn, bdn):
    cap, words = xs.shape
    d = 2 * words
    d_ff = wdn.shape[1]
    row_map = lambda i, exp, blk, new: (blk[i], 0)
    exp_map3 = lambda i, exp, blk, new: (exp[i], 0, 0)
    grid_spec = pltpu.PrefetchScalarGridSpec(
        num_scalar_prefetch=3,
        grid=(cap // MOE_ROWS,),
        in_specs=[pl.BlockSpec((MOE_ROWS, words), row_map),
                  pl.BlockSpec((1, d, 2 * d_ff), exp_map3),
                  pl.BlockSpec((1, 1, 2 * d_ff), exp_map3),
                  pl.BlockSpec((1, d_ff, d), exp_map3),
                  pl.BlockSpec((1, 1, d), exp_map3)],
        out_specs=pl.BlockSpec((MOE_ROWS, words), row_map),
        scratch_shapes=[pltpu.VMEM((d, 2 * d_ff), BF16), pltpu.VMEM((d_ff, d), BF16)],
    )
    return pl.pallas_call(
        functools.partial(_expert_kernel, d_ff=d_ff),
        grid_spec=grid_spec,
        out_shape=jax.ShapeDtypeStruct((cap, words), I32),
        compiler_params=pltpu.CompilerParams(dimension_semantics=("arbitrary",),
                                             vmem_limit_bytes=VMEM_LIMIT),
        name="experts",
    )(blk_expert, blk_index, blk_new, xs, wup, bup, wdn, bdn)


def _route_tables(route, totals, *, cap):
    n = route.shape[0]
    expert = route[:, 0:TOP_K].astype(I32)
    rank = route[:, TOP_K:2 * TOP_K].astype(I32)
    counts = totals[0, :N_EXPERTS].astype(I32)
    padded = (counts + MOE_ROWS - 1) // MOE_ROWS * MOE_ROWS
    pad_end = jnp.cumsum(padded)
    pad_start = pad_end - padded
    dest = pad_start[expert] + rank
    dest_chunks = dest.reshape(n // SC_SCATTER_ROWS, SC_SCATTER_ROWS, TOP_K).transpose(0, 2, 1)
    n_blk = cap // MOE_ROWS
    used = pad_end[-1] // MOE_ROWS
    blk = jnp.minimum(jnp.arange(n_blk, dtype=I32), used - 1)
    blk_expert = jnp.minimum(jnp.sum(pad_end[None, :] <= (blk * MOE_ROWS)[:, None], axis=1),
                             N_EXPERTS - 1).astype(I32)
    blk_new = jnp.concatenate([jnp.ones((1,), I32), (blk_expert[1:] != blk_expert[:-1]).astype(I32)])
    return dest, dest_chunks, blk_expert, blk, blk_new


def _final_kernel(x1_ref, ya_ref, yb_ref, yc_ref, yd_ref, route_ref, p_ref, g1_ref, b1_ref,
                  wp_ref, wg_ref, g2_ref, b2_ref, *rest, alpha):
    o_ref = rest[-1]
    ple_lin = _dot(p_ref[...].astype(BF16), wp_ref[...])
    ffn = jnp.zeros(x1_ref.shape, F32)
    for k, yk_ref in enumerate((ya_ref, yb_ref, yc_ref, yd_ref)):
        gate_k = route_ref[:, 2 * TOP_K + k:2 * TOP_K + k + 1]
        ffn = ffn + gate_k * _unpack_words(yk_ref[...], F32)
    x2 = _layer_norm(alpha * x1_ref[...] + ffn, g1_ref[...], b1_ref[...])
    gate = _sigmoid(_dot(x2.astype(BF16), wg_ref[...]))
    o_ref[...] = _layer_norm(alpha * x2 + ple_lin * gate, g2_ref[...], b2_ref[...])


def _final(x1, y4w, route, p2, g1, b1, wp, wg, g2, b2, *, alpha, tm, part, parts, prev=None):
    n, d = x1.shape
    assert TOP_K == 4
    steps = n // parts // tm
    off = part * steps
    rows = lambda w: pl.BlockSpec((tm, w), lambda i: (off + i, 0))
    full = lambda shape: pl.BlockSpec(shape, lambda i: (0,) * len(shape))
    pick = lambda k: pl.BlockSpec((tm, d // 2), lambda i: (k * steps + i, 0))
    in_specs = [rows(d), pick(0), pick(1), pick(2), pick(3), rows(LANES), rows(p2.shape[1]),
                full(g1.shape), full(b1.shape),
                full(wp.shape), full(wg.shape), full(g2.shape), full(b2.shape)]
    args = [x1, y4w, y4w, y4w, y4w, route, p2, g1, b1, wp, wg, g2, b2]
    aliases = {}
    if prev is not None:
        in_specs.append(pl.BlockSpec(memory_space=pl.ANY))
        args.append(prev)
        aliases = {len(args) - 1: 0}
    return pl.pallas_call(
        functools.partial(_final_kernel, alpha=alpha),
        grid=(steps,),
        in_specs=in_specs,
        out_specs=rows(d),
        out_shape=jax.ShapeDtypeStruct((n, d), F32),
        input_output_aliases=aliases,
        compiler_params=pltpu.CompilerParams(dimension_semantics=("arbitrary",),
                                             vmem_limit_bytes=VMEM_LIMIT),
        name="final",
    )(*args)


def _pack_w_in(w_in):
    d = w_in.shape[0]
    sizes = (ATT_HEADS * ATT_HEAD_DIM, KV_LATENT, IDX_HEADS * IDX_DIM, IDX_DIM, IDX_HEADS,
             HG_HEADS * HG_DIM, HG_HEADS * HG_DIM, HG_HEADS * HG_DIM, HG_HEADS * HG_DIM)
    offs = [0]
    for s in sizes:
        offs.append(offs[-1] + s)
    sec = [w_in[:, offs[k]:offs[k + 1]] for k in range(len(sizes))]
    iq = sec[2].reshape(d, IDX_HEADS, IDX_DIM)
    iq = jnp.pad(iq, ((0, 0), (0, 0), (0, LANES - IDX_DIM))).reshape(d, IDX_HEADS * LANES)
    ikw = jnp.pad(jnp.concatenate([sec[3], sec[4]], axis=1), ((0, 0), (0, LANES - IDX_DIM - IDX_HEADS)))
    w_n = jnp.concatenate([sec[1], ikw, sec[5], sec[6], sec[7], sec[8]], axis=1).astype(BF16)
    w_t = jnp.concatenate([sec[0], iq, ikw, sec[1]], axis=1).T.astype(BF16)
    return w_n, w_t


def _block_diag_uk_t(w_uk):
    eye = jnp.eye(ATT_HEADS, dtype=w_uk.dtype)
    bd = jnp.einsum("rhd,hg->hrgd", w_uk, eye)
    return bd.reshape(ATT_HEADS * KV_LATENT, ATT_HEADS * ATT_HEAD_DIM).astype(BF16)


def _uv_t(w_uv):
    return jnp.transpose(w_uv, (1, 2, 0)).astype(BF16)


def _layer(x, p_l, w_in, kv_g, ik_g, ik_b, w_uk, w_uv, lb, hg_ng, w_o, ln_mix_g, ln_mix_b,
           w_router, b_router, w_up, b_up, w_down, b_down, ln_ffn_g, ln_ffn_b,
           w_ple_proj, w_ple_gate, ln_ple_g, ln_ple_b, *, alpha, tm, hg_tb):
    bsz, t, d = x.shape
    n = bsz * t
    x2 = x.reshape(n, d)
    row = lambda v: v.reshape(1, -1).astype(F32)
    pad_lane = lambda v: jnp.pad(row(v), ((0, 0), (0, LANES - v.shape[-1])))

    w_n, w_t = _pack_w_in(w_in)
    qt, iqt, iwt, c, ct, ik, hg = _inproj(x2, w_n, w_t, _block_diag_uk_t(w_uk), row(kv_g),
                                          kv_g.reshape(-1, 1).astype(F32), pad_lane(ik_g), pad_lane(ik_b), tm=tm)
    y_rec = _hgrn2(hg.reshape(bsz, t, -1), row(lb), row(hg_ng), tb=hg_tb)
    y_att = _dsa(qt, iqt, iwt, ik, c, ct, _uv_t(w_uv), bsz=bsz, t=t)

    wr_f = jnp.pad(w_router.astype(F32), ((0, 0), (0, LANES - N_EXPERTS)))
    wr_hi = wr_f.astype(BF16)
    wr_p = jnp.concatenate([wr_hi, (wr_f - wr_hi.astype(F32)).astype(BF16)], axis=1)
    x1, x1w, route, totals = _mix_router(
        x2, y_att, y_rec.reshape(n, -1), w_o.astype(BF16), row(ln_mix_g), row(ln_mix_b),
        wr_p, pad_lane(b_router), alpha=alpha, tm=tm)

    step = SC_WORKERS * SC_ROWS
    cap = -(-(n * TOP_K + N_EXPERTS * MOE_ROWS) // step) * step
    dest, dest_chunks, blk_expert, blk_index, blk_new = _route_tables(route, totals, cap=cap)
    d_ff = w_down.shape[1]
    xs = _sc_scatter_rows(x1w, dest_chunks, cap)
    outw = _experts(blk_expert, blk_index, blk_new, xs,
                    w_up.astype(F32), b_up.reshape(N_EXPERTS, 1, 2 * d_ff).astype(F32),
                    w_down.astype(F32), b_down.reshape(N_EXPERTS, 1, d).astype(F32))

    unit = max(tm, SC_WORKERS * SC_ROWS // TOP_K)
    parts = next(c for c in (COMBINE_PARTS, 2, 1) if n % (c * unit) == 0)
    out = None
    for part in range(parts):
        rows_p = slice(part * (n // parts), (part + 1) * (n // parts))
        y4w = _sc_gather_rows(outw, dest[rows_p].T.reshape(-1))
        out = _final(x1, y4w, route, p_l.reshape(n, -1), row(ln_ffn_g), row(ln_ffn_b),
                     w_ple_proj.astype(BF16), w_ple_gate.astype(BF16), row(ln_ple_g), row(ln_ple_b),
                     alpha=alpha, tm=tm, part=part, parts=parts, prev=out)
    return out.reshape(bsz, t, d)


def kernel(x, p, w_in, kv_norm_g, idx_k_norm_g, idx_k_norm_b, w_uk, w_uv, hg_lb_logits, hg_norm_g, w_o,
           ln_mix_g, ln_mix_b, w_router, b_router, w_up, b_up, w_down, b_down, ln_ffn_g, ln_ffn_b,
           w_ple_proj, w_ple_gate, ln_ple_g, ln_ple_b):
    depth = w_in.shape[0]
    alpha = (2.0 * depth) ** 0.25
    lower_bounds = jnp.cumsum(jax.nn.softmax(hg_lb_logits.astype(F32), axis=0), axis=0)
    n = x.shape[0] * x.shape[1]
    tm = min(512, n)
    hg_tb = min(512, x.shape[1])
    for l in range(depth):
        x = _layer(x, p[l], w_in[l], kv_norm_g[l], idx_k_norm_g[l], idx_k_norm_b[l], w_uk[l], w_uv[l],
                   lower_bounds[l], hg_norm_g[l], w_o[l], ln_mix_g[l], ln_mix_b[l], w_router[l],
                   b_router[l], w_up[l], b_up[l], w_down[l], b_down[l], ln_ffn_g[l], ln_ffn_b[l],
                   w_ple_proj[l], w_ple_gate[l], ln_ple_g[l], ln_ple_b[l],
                   alpha=alpha, tm=tm, hg_tb=hg_tb)
    return x
```

```python
import functools

import jax
import jax.numpy as jnp
from jax import lax
from jax.experimental import pallas as pl
from jax.experimental.pallas import tpu as pltpu
from jax.experimental.pallas import tpu_sc as plsc

F32 = jnp.float32
BF16 = jnp.bfloat16
I32 = jnp.int32

ATT_HEADS = 8
ATT_HEAD_DIM = 64
KV_LATENT = 256
IDX_HEADS = 8
IDX_DIM = 64
TOPK_MAX = 256
HG_HEADS = 4
HG_DIM = 128
HG_CHUNK = 32
N_EXPERTS = 32
TOP_K = 4
SWIGLU_LIMIT = 7.0
SWIGLU_ALPHA = 1.702
LN_EPS = 1e-5
RMS_EPS = 1e-6

LANES = 128
Q_ROWS = 128
KEY_CHUNK = 512
COUNT_ROWS = 32
SEARCH_BITS = 20
EXTRACT_MAX = 3
MOE_ROWS = 512
SC_CORES = 2
SC_SUBCORES = 16
SC_WORKERS = SC_CORES * SC_SUBCORES
SC_ROWS = 128
SC_SCATTER_ROWS = 128
COMBINE_PARTS = 8
VMEM_LIMIT = 56 * 1024 * 1024

INT_MIN = -(2 ** 31)
NEG_BIG = -1e30
LOG2_E = 1.4426950408889634

_N_KV = 0
_N_IKW = _N_KV + KV_LATENT
_N_HG = _N_IKW + LANES
_N_END = _N_HG + 4 * HG_HEADS * HG_DIM
_T_AQ = 0
_T_IQ = _T_AQ + ATT_HEADS * ATT_HEAD_DIM
_T_IKW = _T_IQ + IDX_HEADS * LANES
_T_KV = _T_IKW + LANES
_T_END = _T_KV + KV_LATENT


def _dot(a, b):
    return jnp.dot(a, b, preferred_element_type=F32)


def _dot_nt(a, b):
    return lax.dot_general(a, b, (((1,), (1,)), ((), ())), preferred_element_type=F32)


def _dot_tn(a, b):
    return lax.dot_general(a, b, (((0,), (0,)), ((), ())), preferred_element_type=F32)


def _layer_norm(z, g, b):
    mu = jnp.mean(z, axis=-1, keepdims=True)
    d = z - mu
    var = jnp.mean(d * d, axis=-1, keepdims=True)
    return d * lax.rsqrt(var + LN_EPS) * g + b


def _sigmoid(x):
    return 1.0 / (1.0 + jnp.exp(-x))


def _inproj_kernel(x_ref, wn_ref, wt_ref, wukt_ref, kvg_ref, kvgc_ref, ikg_ref, ikb_ref,
                   qt_ref, iqt_ref, iwt_ref, c_ref, ct_ref, ik_ref, hg_ref):
    xb = x_ref[...].astype(BF16)

    aqt = _dot_nt(wt_ref[_T_AQ:_T_IQ, :], xb).astype(BF16)
    iqt_ref[...] = (_dot_nt(wt_ref[_T_IQ:_T_IKW, :], xb) * (IDX_DIM ** -0.5)).astype(BF16)
    iwt_ref[...] = _dot_nt(wt_ref[_T_IKW:_T_KV, :], xb) * (IDX_HEADS ** -0.5)
    act = _dot_nt(wt_ref[_T_KV:_T_END, :], xb)
    ct = act * lax.rsqrt(jnp.mean(act * act, axis=0, keepdims=True) + RMS_EPS) * kvgc_ref[...]
    ct_ref[...] = ct.astype(BF16)
    qt_ref[...] = (_dot(wukt_ref[...], aqt) * (LOG2_E * ATT_HEAD_DIM ** -0.5)).astype(BF16)

    ac = _dot(xb, wn_ref[:, _N_KV:_N_IKW])
    c = ac * lax.rsqrt(jnp.mean(ac * ac, axis=-1, keepdims=True) + RMS_EPS) * kvg_ref[...]
    c_ref[...] = c.astype(BF16)

    ikw = _dot(xb, wn_ref[:, _N_IKW:_N_HG])
    lane = lax.broadcasted_iota(I32, ikw.shape, 1)
    is_k = lane < IDX_DIM
    mu = jnp.sum(jnp.where(is_k, ikw, 0.0), axis=-1, keepdims=True) * (1.0 / IDX_DIM)
    d = jnp.where(is_k, ikw - mu, 0.0)
    var = jnp.sum(d * d, axis=-1, keepdims=True) * (1.0 / IDX_DIM)
    ik = d * lax.rsqrt(var + LN_EPS) * ikg_ref[...] + ikb_ref[...]
    ik_ref[...] = ik.astype(BF16)

    hg_ref[...] = _dot(xb, wn_ref[:, _N_HG:_N_END])


def _inproj(x2, w_n, w_t, wuk_t, kvg, kvg_col, ikg, ikb, *, tm):
    n, d = x2.shape
    grid = (n // tm,)
    full = lambda shape: pl.BlockSpec(shape, lambda i: (0,) * len(shape))
    rows = lambda w: pl.BlockSpec((tm, w), lambda i: (i, 0))
    cols = lambda h: pl.BlockSpec((h, tm), lambda i: (0, i))
    return pl.pallas_call(
        _inproj_kernel,
        grid=grid,
        in_specs=[rows(d), full(w_n.shape), full(w_t.shape), full(wuk_t.shape), full(kvg.shape),
                  full(kvg_col.shape), full(ikg.shape), full(ikb.shape)],
        out_specs=[cols(ATT_HEADS * KV_LATENT), cols(IDX_HEADS * LANES), cols(LANES),
                   rows(KV_LATENT), cols(KV_LATENT), rows(LANES), rows(4 * HG_HEADS * HG_DIM)],
        out_shape=[jax.ShapeDtypeStruct((ATT_HEADS * KV_LATENT, n), BF16),
                   jax.ShapeDtypeStruct((IDX_HEADS * LANES, n), BF16),
                   jax.ShapeDtypeStruct((LANES, n), F32),
                   jax.ShapeDtypeStruct((n, KV_LATENT), BF16),
                   jax.ShapeDtypeStruct((KV_LATENT, n), BF16),
                   jax.ShapeDtypeStruct((n, LANES), BF16),
                   jax.ShapeDtypeStruct((n, 4 * HG_HEADS * HG_DIM), F32)],
        compiler_params=pltpu.CompilerParams(dimension_semantics=("arbitrary",),
                                             vmem_limit_bytes=VMEM_LIMIT),
        name="inproj",
    )(x2, w_n, w_t, wuk_t, kvg, kvg_col, ikg, ikb)


def _hgrn2_kernel(hg_ref, lb_ref, ng_ref, o_ref, state_ref, *, tb):
    @pl.when(pl.program_id(1) == 0)
    def _():
        state_ref[...] = jnp.zeros_like(state_ref)

    nchunk = tb // HG_CHUNK
    width = HG_HEADS * HG_DIM
    row = lax.broadcasted_iota(I32, (tb, HG_DIM), 0)
    rin = jnp.bitwise_and(row, HG_CHUNK - 1)
    r2 = lax.broadcasted_iota(I32, (tb, tb), 0)
    c2 = lax.broadcasted_iota(I32, (tb, tb), 1)
    intra = jnp.logical_and(r2 // HG_CHUNK == c2 // HG_CHUNK, c2 <= r2)

    for h in range(HG_HEADS):
        sl = slice(h * HG_DIM, (h + 1) * HG_DIM)
        gq = hg_ref[0, :, h * HG_DIM:(h + 1) * HG_DIM]
        gf = hg_ref[0, :, width + h * HG_DIM:width + (h + 1) * HG_DIM]
        gi = hg_ref[0, :, 2 * width + h * HG_DIM:2 * width + (h + 1) * HG_DIM]
        gg = hg_ref[0, :, 3 * width + h * HG_DIM:3 * width + (h + 1) * HG_DIM]
        lb = lb_ref[:, sl]
        forget = lb + (1.0 - lb) * _sigmoid(gf)
        logf = jnp.log(forget)
        b = logf
        s = 1
        while s < HG_CHUNK:
            b = b + jnp.where(rin >= s, pltpu.roll(b, s, axis=0), 0.0)
            s *= 2
        tot = jnp.sum(logf.reshape(nchunk, HG_CHUNK, HG_DIM), axis=1, keepdims=True)
        bl = jnp.broadcast_to(tot, (nchunk, HG_CHUNK, HG_DIM)).reshape(tb, HG_DIM)
        kk = 1.0 - forget
        q_dec = (gq * _sigmoid(gq) * jnp.exp(b)).astype(BF16)
        k_inv = (kk * jnp.exp(-b)).astype(BF16)
        k_end = (kk * jnp.exp(bl - b)).astype(BF16)
        vb = gi.astype(BF16)
        scores = jnp.where(intra, _dot_nt(q_dec, k_inv), 0.0).astype(BF16)
        o = _dot(scores, vb)
        decay = jnp.exp(tot)
        chunk_rows = [slice(n * HG_CHUNK, (n + 1) * HG_CHUNK) for n in range(nchunk)]
        incs = [_dot_tn(vb[rs], k_end[rs]) for rs in chunk_rows]
        st = state_ref[h]
        starts = []
        for n in range(nchunk):
            starts.append(st.astype(BF16))
            st = st * decay[n] + incs[n]
        state_ref[h] = st
        inter = [_dot_nt(q_dec[rs], starts[n]) for n, rs in enumerate(chunk_rows)]
        o = o + jnp.concatenate(inter, axis=0)
        o = o * lax.rsqrt(jnp.mean(o * o, axis=-1, keepdims=True) + RMS_EPS)
        o = o * ng_ref[:, sl] * (gg * _sigmoid(gg))
        o_ref[0, :, sl] = o.astype(o_ref.dtype)


def _hgrn2(hg3, lb, ng, *, tb):
    bsz, t, w4 = hg3.shape
    width = HG_HEADS * HG_DIM
    return pl.pallas_call(
        functools.partial(_hgrn2_kernel, tb=tb),
        grid=(bsz, t // tb),
        in_specs=[pl.BlockSpec((1, tb, w4), lambda b, i: (b, i, 0)),
                  pl.BlockSpec((1, width), lambda b, i: (0, 0)),
                  pl.BlockSpec((1, width), lambda b, i: (0, 0))],
        out_specs=pl.BlockSpec((1, tb, width), lambda b, i: (b, i, 0)),
        out_shape=jax.ShapeDtypeStruct((bsz, t, width), BF16),
        scratch_shapes=[pltpu.VMEM((HG_HEADS, HG_DIM, HG_DIM), F32)],
        compiler_params=pltpu.CompilerParams(dimension_semantics=("arbitrary", "arbitrary"),
                                             vmem_limit_bytes=VMEM_LIMIT),
        name="hgrn2",
    )(hg3, lb, ng)


def _dsa_kernel(qt_ref, iqt_ref, iwt_ref, ik_ref, c_ref, ct_ref, wuvt_ref, o_ref,
                keys_ref, cut_ref, thr_ref, redo_ref, qall_ref, iqall_ref, m_ref, l_ref, acc_ref,
                *, topk, pos_bits):
    n = pl.program_id(1)
    t0 = n * Q_ROWS
    nch = (t0 + Q_ROWS + KEY_CHUNK - 1) // KEY_CHUNK
    kf = float(topk)
    key_i = lax.broadcasted_iota(I32, (KEY_CHUNK, Q_ROWS), 0)
    tq = t0 + lax.broadcasted_iota(I32, (KEY_CHUNK, Q_ROWS), 1)
    pair = 2 * Q_ROWS

    for h in range(ATT_HEADS):
        qall_ref[:, h * Q_ROWS:(h + 1) * Q_ROWS] = qt_ref[h * KV_LATENT:(h + 1) * KV_LATENT, :]
    for h in range(IDX_HEADS):
        iqall_ref[:, h * Q_ROWS:(h + 1) * Q_ROWS] = iqt_ref[h * LANES:(h + 1) * LANES, :]
    head_w = [iwt_ref[IDX_DIM + h:IDX_DIM + h + 1, :] for h in range(IDX_HEADS)]

    def index_body(j, carry):
        k0 = pl.multiple_of(j * KEY_CHUNK, KEY_CHUNK)
        ikc = ik_ref[pl.ds(k0, KEY_CHUNK), :]
        isc = jnp.zeros((KEY_CHUNK, Q_ROWS), F32)
        for hp in range(IDX_HEADS // 2):
            z = _dot(ikc, iqall_ref[:, hp * pair:(hp + 1) * pair])
            for hh in range(2):
                isc = isc + head_w[2 * hp + hh] * jnp.maximum(z[:, hh * Q_ROWS:(hh + 1) * Q_ROWS], 0.0)
        isc = jnp.where(isc == 0.0, 0.0, isc)
        bits = pltpu.bitcast(isc, I32)
        key = jnp.bitwise_xor(bits, jnp.bitwise_and(jnp.right_shift(bits, 31), 0x7FFFFFFF))
        key = jnp.where(k0 + key_i <= tq, key, INT_MIN)
        keys_ref[pl.ds(k0, KEY_CHUNK), :] = key
        return carry

    lax.fori_loop(0, nch, index_body, 0)

    def count(pred):
        def body(j, acc):
            k0 = pl.multiple_of(j * KEY_CHUNK, KEY_CHUNK)
            hit = jnp.where(pred(keys_ref[pl.ds(k0, KEY_CHUNK), :], k0), 1.0, 0.0)
            return acc + jnp.sum(hit.reshape(KEY_CHUNK // COUNT_ROWS, COUNT_ROWS, Q_ROWS), axis=0)
        acc = lax.fori_loop(0, nch, body, jnp.zeros((COUNT_ROWS, Q_ROWS), F32))
        return jnp.sum(acc, axis=0, keepdims=True)

    def count_ge(cand):
        return count(lambda k, k0: k >= cand)

    def masked_max(bound):
        def body(j, acc):
            k0 = pl.multiple_of(j * KEY_CHUNK, KEY_CHUNK)
            key = keys_ref[pl.ds(k0, KEY_CHUNK), :]
            kept = jnp.where(key <= bound, key, INT_MIN)
            return jnp.maximum(acc, jnp.max(kept.reshape(KEY_CHUNK // COUNT_ROWS, COUNT_ROWS, Q_ROWS), axis=0))
        acc = lax.fori_loop(0, nch, body, jnp.full((COUNT_ROWS, Q_ROWS), INT_MIN, I32))
        return jnp.max(acc, axis=0, keepdims=True)

    n_all = (nch * KEY_CHUNK).astype(F32)
    n0 = count_ge(jnp.zeros((1, Q_ROWS), I32))
    pos0 = n0 >= kf
    state0 = (jnp.where(pos0, 0, INT_MIN).astype(I32), jnp.where(pos0, n0, n_all), jnp.where(pos0, 0.0, n0))

    def bracket_body(i, state):
        thr, n_lo, n_hi = state
        cand = thr + lax.shift_left(jnp.int32(1), 30 - i)
        cnt = count_ge(cand)
        ge = cnt >= kf
        return jnp.where(ge, cand, thr), jnp.where(ge, cnt, n_lo), jnp.where(ge, n_hi, cnt)

    thr1, n_lo, n_hi = lax.fori_loop(0, SEARCH_BITS, bracket_body, state0)
    low_bits = 31 - SEARCH_BITS
    has_topk = tq[0:1, :] + 1 >= topk
    in_bracket = n_lo - n_hi
    exact_cut = n_lo == kf
    few = jnp.logical_or(jnp.logical_not(has_topk), jnp.logical_or(exact_cut, in_bracket <= float(EXTRACT_MAX)))
    thr_ref[...] = jnp.broadcast_to(thr1, thr_ref.shape)
    redo_ref[0] = jnp.where(jnp.min(jnp.where(few, 1.0, 0.0)) > 0.0, 0, 1)

    @pl.when(redo_ref[0] == 0)
    def _():
        need = kf - n_hi
        bound = thr1 + (2 ** low_bits - 1)
        found = jnp.zeros((1, Q_ROWS), F32)
        thr_x = thr1
        for j in range(1, EXTRACT_MAX + 1):
            cur = masked_max(bound)
            inside = cur >= thr1
            found = found + jnp.where(inside, 1.0, 0.0)
            thr_x = jnp.where(jnp.logical_and(inside, need == float(j)), cur, thr_x)
            bound = jnp.where(inside, cur - 1, INT_MIN)
        use_x = jnp.logical_and(has_topk, jnp.logical_not(exact_cut))
        clash = jnp.logical_and(use_x, found != in_bracket)
        thr_ref[...] = jnp.broadcast_to(jnp.where(use_x, thr_x, thr1), thr_ref.shape)
        redo_ref[0] = jnp.where(jnp.max(jnp.where(clash, 1.0, 0.0)) > 0.0, 1, 0)

    @pl.when(redo_ref[0] == 1)
    def _():
        def bit_body(i, thr):
            cand = thr + lax.shift_left(jnp.int32(1), low_bits - 1 - i)
            return jnp.where(count_ge(cand) >= kf, cand, thr)

        thr_ref[...] = jnp.broadcast_to(lax.fori_loop(0, low_bits, bit_body, thr1), thr_ref.shape)

    thr = thr_ref[0:1, :]
    live = thr > INT_MIN
    cut_ref[...] = jnp.broadcast_to(jnp.where(live, jnp.int32(2 ** 30), -1), cut_ref.shape)

    @pl.when(redo_ref[0] == 1)
    def _():
        n_ge = count_ge(thr)
        tie = jnp.max(jnp.where(jnp.logical_and(live, n_ge > kf), 1.0, 0.0))

        @pl.when(tie > 0.0)
        def _():
            need = kf - count(lambda k, k0: k > thr)

            def cut_body(i, cut):
                cand = cut + lax.shift_left(jnp.int32(1), pos_bits - 1 - i)
                below = count(lambda k, k0: jnp.logical_and(k == thr, k0 + key_i < cand))
                return jnp.where(below < need, cand, cut)

            cut = lax.fori_loop(0, pos_bits, cut_body, jnp.zeros((1, Q_ROWS), I32))
            cut_ref[...] = jnp.broadcast_to(jnp.where(live, cut, -1), cut_ref.shape)

    m_ref[...] = jnp.full(m_ref.shape, NEG_BIG, F32)
    l_ref[...] = jnp.zeros(l_ref.shape, F32)
    acc_ref[...] = jnp.zeros(acc_ref.shape, F32)
    cut_b = cut_ref[0:1, :]

    def attn_body(j, carry):
        k0 = pl.multiple_of(j * KEY_CHUNK, KEY_CHUNK)
        key = keys_ref[pl.ds(k0, KEY_CHUNK), :]
        pos = k0 + key_i
        sel = jnp.logical_or(key > thr, jnp.logical_and(key == thr, pos <= cut_b))
        rel = jnp.where(sel, (pos - t0).astype(F32), NEG_BIG)
        cc = c_ref[pl.ds(k0, KEY_CHUNK), :]
        cct = ct_ref[:, pl.ds(k0, KEY_CHUNK)]
        for hp in range(ATT_HEADS // 2):
            cs = slice(hp * pair, (hp + 1) * pair)
            s2 = _dot(cc, qall_ref[:, cs])
            ps, alphas = [], []
            for hh in range(2):
                h = 2 * hp + hh
                slope = LOG2_E * 2.0 ** (-(8.0 / ATT_HEADS) * (h + 1))
                s = s2[:, hh * Q_ROWS:(hh + 1) * Q_ROWS] + slope * rel
                m_old = m_ref[h:h + 1, :]
                m_new = jnp.maximum(m_old, jnp.max(s, axis=0, keepdims=True))
                alpha = jnp.exp2(m_old - m_new)
                p = jnp.exp2(s - m_new)
                l_ref[h:h + 1, :] = alpha * l_ref[h:h + 1, :] + jnp.sum(p, axis=0, keepdims=True)
                m_ref[h:h + 1, :] = m_new
                ps.append(p.astype(BF16))
                alphas.append(alpha)
            acc_ref[:, cs] = (acc_ref[:, cs] * jnp.concatenate(alphas, axis=1)
                              + _dot(cct, jnp.concatenate(ps, axis=1)))
        return carry

    lax.fori_loop(0, nch, attn_body, 0)

    yt = []
    for h in range(ATT_HEADS):
        o_lat = (acc_ref[:, h * Q_ROWS:(h + 1) * Q_ROWS] / l_ref[h:h + 1, :]).astype(BF16)
        yt.append(_dot(wuvt_ref[h], o_lat))
    o_ref[...] = jnp.concatenate(yt, axis=0).T.astype(o_ref.dtype)


def _dsa(qt, iqt, iwt, ik, c, ct, wuvt_p, *, bsz, t):
    n = bsz * t
    nq = t // Q_ROWS
    topk = min(TOPK_MAX, t // 4)
    assert t % KEY_CHUNK == 0
    width = ATT_HEADS * ATT_HEAD_DIM
    qcols = lambda h: pl.BlockSpec((h, Q_ROWS), lambda b, i: (0, b * nq + i))
    return pl.pallas_call(
        functools.partial(_dsa_kernel, topk=topk, pos_bits=(t - 1).bit_length()),
        grid=(bsz, nq),
        in_specs=[qcols(ATT_HEADS * KV_LATENT), qcols(IDX_HEADS * LANES), qcols(LANES),
                  pl.BlockSpec((t, LANES), lambda b, i: (b, 0)),
                  pl.BlockSpec((t, KV_LATENT), lambda b, i: (b, 0)),
                  pl.BlockSpec((KV_LATENT, t), lambda b, i: (0, b)),
                  pl.BlockSpec(wuvt_p.shape, lambda b, i: (0, 0, 0))],
        out_specs=pl.BlockSpec((Q_ROWS, width), lambda b, i: (b * nq + i, 0)),
        out_shape=jax.ShapeDtypeStruct((n, width), BF16),
        scratch_shapes=[pltpu.VMEM((t, Q_ROWS), I32),
                        pltpu.VMEM((8, Q_ROWS), I32),
                        pltpu.VMEM((8, Q_ROWS), I32),
                        pltpu.SMEM((1,), I32),
                        pltpu.VMEM((KV_LATENT, ATT_HEADS * Q_ROWS), BF16),
                        pltpu.VMEM((LANES, IDX_HEADS * Q_ROWS), BF16),
                        pltpu.VMEM((8, Q_ROWS), F32),
                        pltpu.VMEM((8, Q_ROWS), F32),
                        pltpu.VMEM((KV_LATENT, ATT_HEADS * Q_ROWS), F32)],
        compiler_params=pltpu.CompilerParams(dimension_semantics=("arbitrary", "arbitrary"),
                                             vmem_limit_bytes=VMEM_LIMIT),
        name="dsa",
    )(qt, iqt, iwt, ik, c, ct, wuvt_p)


def _mix_router_kernel(x_ref, ya_ref, yr_ref, wo_ref, g_ref, b_ref, wr_ref, br_ref,
                       x1_ref, x1w_ref, route_ref, tot_ref, carry_ref, *, alpha, tm):
    @pl.when(pl.program_id(0) == 0)
    def _():
        carry_ref[...] = jnp.zeros_like(carry_ref)

    half = ya_ref.shape[1]
    mix = _dot(ya_ref[...], wo_ref[0:half, :]) + _dot(yr_ref[...], wo_ref[half:, :])
    x1 = _layer_norm(alpha * x_ref[...] + mix, g_ref[...], b_ref[...])
    x1_ref[...] = x1
    x1w_ref[...] = _pack_words(x1)

    x_hi = x1.astype(BF16)
    x_lo = (x1 - x_hi.astype(F32)).astype(BF16)
    hh_hl = _dot(x_hi, wr_ref[...])
    logits = (hh_hl[:, :LANES] + hh_hl[:, LANES:] + _dot(x_lo, wr_ref[:, :LANES])) + br_ref[...]
    lane = lax.broadcasted_iota(I32, logits.shape, 1)
    lane_f = lane.astype(F32)
    work = jnp.where(lane < N_EXPERTS, logits, -jnp.inf)
    sel_f = jnp.zeros(logits.shape, F32)
    denom = jnp.zeros((tm, 1), F32)
    hits, experts, weights = [], [], []
    for _ in range(TOP_K):
        mx = jnp.max(work, axis=-1, keepdims=True)
        first = jnp.min(jnp.where(work == mx, lane_f, float(LANES)), axis=-1, keepdims=True)
        hit = lane_f == first
        e = jnp.exp(mx - (weights[0][1] if weights else mx))
        hits.append(hit)
        experts.append(first)
        weights.append((e, mx))
        denom = denom + e
        sel_f = sel_f + jnp.where(hit, 1.0, 0.0)
        work = jnp.where(hit, -jnp.inf, work)

    r2 = lax.broadcasted_iota(I32, (tm, tm), 0)
    c2 = lax.broadcasted_iota(I32, (tm, tm), 1)
    before = jnp.where(c2 < r2, 1.0, 0.0).astype(BF16)
    rank = _dot(before, sel_f.astype(BF16)) + carry_ref[...]
    carry = carry_ref[...] + jnp.sum(sel_f, axis=0, keepdims=True)
    carry_ref[...] = carry
    tot_ref[...] = jnp.broadcast_to(carry, tot_ref.shape)

    route = jnp.zeros(logits.shape, F32)
    for k in range(TOP_K):
        rank_k = jnp.sum(jnp.where(hits[k], rank, 0.0), axis=-1, keepdims=True)
        route = jnp.where(lane == k, experts[k], route)
        route = jnp.where(lane == TOP_K + k, rank_k, route)
        route = jnp.where(lane == 2 * TOP_K + k, weights[k][0] / denom, route)
    route_ref[...] = route


def _mix_router(x2, ya, yr, wo, g, b, wr_p, br_p, *, alpha, tm):
    n, d = x2.shape
    half = ya.shape[1]
    rows = lambda w: pl.BlockSpec((tm, w), lambda i: (i, 0))
    full = lambda shape: pl.BlockSpec(shape, lambda i: (0,) * len(shape))
    return pl.pallas_call(
        functools.partial(_mix_router_kernel, alpha=alpha, tm=tm),
        grid=(n // tm,),
        in_specs=[rows(d), rows(half), rows(half), full(wo.shape), full(g.shape), full(b.shape),
                  full(wr_p.shape), full(br_p.shape)],
        out_specs=[rows(d), rows(d // 2), rows(LANES), full((8, LANES))],
        out_shape=[jax.ShapeDtypeStruct((n, d), F32),
                   jax.ShapeDtypeStruct((n, d // 2), I32),
                   jax.ShapeDtypeStruct((n, LANES), F32),
                   jax.ShapeDtypeStruct((8, LANES), F32)],
        scratch_shapes=[pltpu.VMEM((1, LANES), F32)],
        compiler_params=pltpu.CompilerParams(dimension_semantics=("arbitrary",),
                                             vmem_limit_bytes=VMEM_LIMIT),
        name="mix_router",
    )(x2, ya, yr, wo, g, b, wr_p, br_p)


def _sc_gather_rows(table, idx):
    n_idx = idx.shape[0]
    width = table.shape[1]
    per_worker = n_idx // SC_WORKERS
    assert per_worker * SC_WORKERS == n_idx and per_worker % SC_ROWS == 0
    mesh = plsc.VectorSubcoreMesh(core_axis_name="c", subcore_axis_name="s",
                                  num_cores=SC_CORES, num_subcores=SC_SUBCORES)

    @functools.partial(
        pl.kernel, mesh=mesh,
        out_type=jax.ShapeDtypeStruct((n_idx, width), table.dtype),
        scratch_types=[pltpu.VMEM((SC_ROWS,), I32),
                       pltpu.VMEM((SC_ROWS, width), table.dtype),
                       pltpu.SemaphoreType.DMA])
    def gather(table_hbm, idx_hbm, out_hbm, idx_v, rows_v, sem):
        worker = lax.axis_index("s") * SC_CORES + lax.axis_index("c")

        @pl.loop(0, per_worker // SC_ROWS)
        def _(step):
            base = pl.multiple_of(worker * per_worker + step * SC_ROWS, SC_ROWS)
            pltpu.sync_copy(idx_hbm.at[pl.ds(base, SC_ROWS)], idx_v)
            pltpu.async_copy(table_hbm.at[idx_v], rows_v, sem).wait()
            pltpu.sync_copy(rows_v, out_hbm.at[pl.ds(base, SC_ROWS)])

    return gather(table, idx)


def _sc_scatter_rows(rows, dest3, cap):
    n, width = rows.shape
    chunks, picks, r = dest3.shape
    assert r == SC_SCATTER_ROWS and chunks * r == n and chunks % SC_WORKERS == 0
    per_worker = chunks // SC_WORKERS
    mesh = plsc.VectorSubcoreMesh(core_axis_name="c", subcore_axis_name="s",
                                  num_cores=SC_CORES, num_subcores=SC_SUBCORES)

    @functools.partial(
        pl.kernel, mesh=mesh,
        out_type=jax.ShapeDtypeStruct((cap, width), rows.dtype),
        scratch_types=[pltpu.VMEM((picks, r), I32),
                       pltpu.VMEM((r, width), rows.dtype),
                       pltpu.SemaphoreType.DMA])
    def scatter(rows_hbm, dest_hbm, out_hbm, idx_v, rows_v, sem):
        worker = lax.axis_index("s") * SC_CORES + lax.axis_index("c")

        @pl.loop(0, per_worker)
        def _(step):
            chunk = worker * per_worker + step
            base = pl.multiple_of(chunk * r, r)
            pltpu.sync_copy(rows_hbm.at[pl.ds(base, r)], rows_v)
            pltpu.sync_copy(dest_hbm.at[chunk], idx_v)
            for k in range(picks):
                pltpu.async_copy(rows_v, out_hbm.at[idx_v.at[k]], sem).wait()

    return scatter(rows, dest3)


def _pack_words(v):
    k = v.shape[1] // 2
    hi = pltpu.bitcast(v[:, :k].astype(BF16).astype(F32), I32)
    lo = pltpu.bitcast(v[:, k:].astype(BF16).astype(F32), I32)
    return jnp.bitwise_or(hi, lax.shift_right_logical(lo, jnp.int32(16)))


def _unpack_words(w, dtype=BF16):
    hi = pltpu.bitcast(jnp.bitwise_and(w, jnp.int32(-65536)), F32)
    lo = pltpu.bitcast(jnp.left_shift(w, 16), F32)
    return jnp.concatenate([hi, lo], axis=1).astype(dtype)


def _expert_kernel(exp_ref, blk_ref, new_ref, x_ref, wup_ref, bup_ref, wdn_ref, bdn_ref, o_ref,
                   wup_s, wdn_s, *, d_ff):
    del exp_ref, blk_ref

    @pl.when(new_ref[pl.program_id(0)] == 1)
    def _():
        wup_s[...] = wup_ref[0].astype(BF16)
        wdn_s[...] = wdn_ref[0].astype(BF16)

    half = MOE_ROWS // 2
    for r in range(2):
        rows = slice(r * half, (r + 1) * half)
        h = _dot(_unpack_words(x_ref[rows, :]), wup_s[...]) + bup_ref[0]
        glu = jnp.minimum(h[:, :d_ff], SWIGLU_LIMIT)
        lin = jnp.clip(h[:, d_ff:], -SWIGLU_LIMIT, SWIGLU_LIMIT)
        a = glu * _sigmoid(SWIGLU_ALPHA * glu) * (lin + 1.0)
        o_ref[rows, :] = _pack_words(_dot(a.astype(BF16), wdn_s[...]) + bdn_ref[0])


def _experts(blk_expert, blk_index, blk_new, xs, wup, bup, wdn, bdn):
    cap, words = xs.shape
    d = 2 * words
    d_ff = wdn.shape[1]
    row_map = lambda i, exp, blk, new: (blk[i], 0)
    exp_map3 = lambda i, exp, blk, new: (exp[i], 0, 0)
    grid_spec = pltpu.PrefetchScalarGridSpec(
        num_scalar_prefetch=3,
        grid=(cap // MOE_ROWS,),
        in_specs=[pl.BlockSpec((MOE_ROWS, words), row_map),
                  pl.BlockSpec((1, d, 2 * d_ff), exp_map3),
                  pl.BlockSpec((1, 1, 2 * d_ff), exp_map3),
                  pl.BlockSpec((1, d_ff, d), exp_map3),
                  pl.BlockSpec((1, 1, d), exp_map3)],
        out_specs=pl.BlockSpec((MOE_ROWS, words), row_map),
        scratch_shapes=[pltpu.VMEM((d, 2 * d_ff), BF16), pltpu.VMEM((d_ff, d), BF16)],
    )
    return pl.pallas_call(
        functools.partial(_expert_kernel, d_ff=d_ff),
        grid_spec=grid_spec,
        out_shape=jax.ShapeDtypeStruct((cap, words), I32),
        compiler_params=pltpu.CompilerParams(dimension_semantics=("arbitrary",),
                                             vmem_limit_bytes=VMEM_LIMIT),
        name="experts",
    )(blk_expert, blk_index, blk_new, xs, wup, bup, wdn, bdn)


def _route_tables(route, totals, *, cap):
    n = route.shape[0]
    expert = route[:, 0:TOP_K].astype(I32)
    rank = route[:, TOP_K:2 * TOP_K].astype(I32)
    counts = totals[0, :N_EXPERTS].astype(I32)
    padded = (counts + MOE_ROWS - 1) // MOE_ROWS * MOE_ROWS
    pad_end = jnp.cumsum(padded)
    pad_start = pad_end - padded
    dest = pad_start[expert] + rank
    dest_chunks = dest.reshape(n // SC_SCATTER_ROWS, SC_SCATTER_ROWS, TOP_K).transpose(0, 2, 1)
    n_blk = cap // MOE_ROWS
    used = pad_end[-1] // MOE_ROWS
    blk = jnp.minimum(jnp.arange(n_blk, dtype=I32), used - 1)
    blk_expert = jnp.minimum(jnp.sum(pad_end[None, :] <= (blk * MOE_ROWS)[:, None], axis=1),
                             N_EXPERTS - 1).astype(I32)
    blk_new = jnp.concatenate([jnp.ones((1,), I32), (blk_expert[1:] != blk_expert[:-1]).astype(I32)])
    return dest, dest_chunks, blk_expert, blk, blk_new


def _final_kernel(x1_ref, ya_ref, yb_ref, yc_ref, yd_ref, route_ref, p_ref, g1_ref, b1_ref,
                  wp_ref, wg_ref, g2_ref, b2_ref, *rest, alpha):
    o_ref = rest[-1]
    ple_lin = _dot(p_ref[...].astype(BF16), wp_ref[...])
    ffn = jnp.zeros(x1_ref.shape, F32)
    for k, yk_ref in enumerate((ya_ref, yb_ref, yc_ref, yd_ref)):
        gate_k = route_ref[:, 2 * TOP_K + k:2 * TOP_K + k + 1]
        ffn = ffn + gate_k * _unpack_words(yk_ref[...], F32)
    x2 = _layer_norm(alpha * x1_ref[...] + ffn, g1_ref[...], b1_ref[...])
    gate = _sigmoid(_dot(x2.astype(BF16), wg_ref[...]))
    o_ref[...] = _layer_norm(alpha * x2 + ple_lin * gate, g2_ref[...], b2_ref[...])


def _final(x1, y4w, route, p2, g1, b1, wp, wg, g2, b2, *, alpha, tm, part, parts, prev=None):
    n, d = x1.shape
    assert TOP_K == 4
    steps = n // parts // tm
    off = part * steps
    rows = lambda w: pl.BlockSpec((tm, w), lambda i: (off + i, 0))
    full = lambda shape: pl.BlockSpec(shape, lambda i: (0,) * len(shape))
    pick = lambda k: pl.BlockSpec((tm, d // 2), lambda i: (k * steps + i, 0))
    in_specs = [rows(d), pick(0), pick(1), pick(2), pick(3), rows(LANES), rows(p2.shape[1]),
                full(g1.shape), full(b1.shape),
                full(wp.shape), full(wg.shape), full(g2.shape), full(b2.shape)]
    args = [x1, y4w, y4w, y4w, y4w, route, p2, g1, b1, wp, wg, g2, b2]
    aliases = {}
    if prev is not None:
        in_specs.append(pl.BlockSpec(memory_space=pl.ANY))
        args.append(prev)
        aliases = {len(args) - 1: 0}
    return pl.pallas_call(
        functools.partial(_final_kernel, alpha=alpha),
        grid=(steps,),
        in_specs=in_specs,
        out_specs=rows(d),
        out_shape=jax.ShapeDtypeStruct((n, d), F32),
        input_output_aliases=aliases,
        compiler_params=pltpu.CompilerParams(dimension_semantics=("arbitrary",),
                                             vmem_limit_bytes=VMEM_LIMIT),
        name="final",
    )(*args)


def _pack_w_in(w_in):
    d = w_in.shape[0]
    sizes = (ATT_HEADS * ATT_HEAD_DIM, KV_LATENT, IDX_HEADS * IDX_DIM, IDX_DIM, IDX_HEADS,
             HG_HEADS * HG_DIM, HG_HEADS * HG_DIM, HG_HEADS * HG_DIM, HG_HEADS * HG_DIM)
    offs = [0]
    for s in sizes:
        offs.append(offs[-1] + s)
    sec = [w_in[:, offs[k]:offs[k + 1]] for k in range(len(sizes))]
    iq = sec[2].reshape(d, IDX_HEADS, IDX_DIM)
    iq = jnp.pad(iq, ((0, 0), (0, 0), (0, LANES - IDX_DIM))).reshape(d, IDX_HEADS * LANES)
    ikw = jnp.pad(jnp.concatenate([sec[3], sec[4]], axis=1), ((0, 0), (0, LANES - IDX_DIM - IDX_HEADS)))
    w_n = jnp.concatenate([sec[1], ikw, sec[5], sec[6], sec[7], sec[8]], axis=1).astype(BF16)
    w_t = jnp.concatenate([sec[0], iq, ikw, sec[1]], axis=1).T.astype(BF16)
    return w_n, w_t


def _block_diag_uk_t(w_uk):
    eye = jnp.eye(ATT_HEADS, dtype=w_uk.dtype)
    bd = jnp.einsum("rhd,hg->hrgd", w_uk, eye)
    return bd.reshape(ATT_HEADS * KV_LATENT, ATT_HEADS * ATT_HEAD_DIM).astype(BF16)


def _uv_t(w_uv):
    return jnp.transpose(w_uv, (1, 2, 0)).astype(BF16)


def _layer(x, p_l, w_in, kv_g, ik_g, ik_b, w_uk, w_uv, lb, hg_ng, w_o, ln_mix_g, ln_mix_b,
           w_router, b_router, w_up, b_up, w_down, b_down, ln_ffn_g, ln_ffn_b,
           w_ple_proj, w_ple_gate, ln_ple_g, ln_ple_b, *, alpha, tm, hg_tb):
    bsz, t, d = x.shape
    n = bsz * t
    x2 = x.reshape(n, d)
    row = lambda v: v.reshape(1, -1).astype(F32)
    pad_lane = lambda v: jnp.pad(row(v), ((0, 0), (0, LANES - v.shape[-1])))

    w_n, w_t = _pack_w_in(w_in)
    qt, iqt, iwt, c, ct, ik, hg = _inproj(x2, w_n, w_t, _block_diag_uk_t(w_uk), row(kv_g),
                                          kv_g.reshape(-1, 1).astype(F32), pad_lane(ik_g), pad_lane(ik_b), tm=tm)
    y_rec = _hgrn2(hg.reshape(bsz, t, -1), row(lb), row(hg_ng), tb=hg_tb)
    y_att = _dsa(qt, iqt, iwt, ik, c, ct, _uv_t(w_uv), bsz=bsz, t=t)

    wr_f = jnp.pad(w_router.astype(F32), ((0, 0), (0, LANES - N_EXPERTS)))
    wr_hi = wr_f.astype(BF16)
    wr_p = jnp.concatenate([wr_hi, (wr_f - wr_hi.astype(F32)).astype(BF16)], axis=1)
    x1, x1w, route, totals = _mix_router(
        x2, y_att, y_rec.reshape(n, -1), w_o.astype(BF16), row(ln_mix_g), row(ln_mix_b),
        wr_p, pad_lane(b_router), alpha=alpha, tm=tm)

    step = SC_WORKERS * SC_ROWS
    cap = -(-(n * TOP_K + N_EXPERTS * MOE_ROWS) // step) * step
    dest, dest_chunks, blk_expert, blk_index, blk_new = _route_tables(route, totals, cap=cap)
    d_ff = w_down.shape[1]
    xs = _sc_scatter_rows(x1w, dest_chunks, cap)
    outw = _experts(blk_expert, blk_index, blk_new, xs,
                    w_up.astype(F32), b_up.reshape(N_EXPERTS, 1, 2 * d_ff).astype(F32),
                    w_down.astype(F32), b_down.reshape(N_EXPERTS, 1, d).astype(F32))

    unit = max(tm, SC_WORKERS * SC_ROWS // TOP_K)
    parts = next(c for c in (COMBINE_PARTS, 2, 1) if n % (c * unit) == 0)
    out = None
    for part in range(parts):
        rows_p = slice(part * (n // parts), (part + 1) * (n // parts))
        y4w = _sc_gather_rows(outw, dest[rows_p].T.reshape(-1))
        out = _final(x1, y4w, route, p_l.reshape(n, -1), row(ln_ffn_g), row(ln_ffn_b),
                     w_ple_proj.astype(BF16), w_ple_gate.astype(BF16), row(ln_ple_g), row(ln_ple_b),
                     alpha=alpha, tm=tm, part=part, parts=parts, prev=out)
    return out.reshape(bsz, t, d)


def kernel(x, p, w_in, kv_norm_g, idx_k_norm_g, idx_k_norm_b, w_uk, w_uv, hg_lb_logits, hg_norm_g, w_o,
           ln_mix_g, ln_mix_b, w_router, b_router, w_up, b_up, w_down, b_down, ln_ffn_g, ln_ffn_b,
           w_ple_proj, w_ple_gate, ln_ple_g, ln_ple_b):
    depth = w_in.shape[0]
    alpha = (2.0 * depth) ** 0.25
    lower_bounds = jnp.cumsum(jax.nn.softmax(hg_lb_logits.astype(F32), axis=0), axis=0)
    n = x.shape[0] * x.shape[1]
    tm = min(512, n)
    hg_tb = min(512, x.shape[1])
    for l in range(depth):
        x = _layer(x, p[l], w_in[l], kv_norm_g[l], idx_k_norm_g[l], idx_k_norm_b[l], w_uk[l], w_uv[l],
                   lower_bounds[l], hg_norm_g[l], w_o[l], ln_mix_g[l], ln_mix_b[l], w_router[l],
                   b_router[l], w_up[l], b_up[l], w_down[l], b_down[l], ln_ffn_g[l], ln_ffn_b[l],
                   w_ple_proj[l], w_ple_gate[l], ln_ple_g[l], ln_ple_b[l],
                   alpha=alpha, tm=tm, hg_tb=hg_tb)
    return x
```

```python
import functools

import jax
import jax.numpy as jnp
from jax import lax
from jax.experimental import pallas as pl
from jax.experimental.pallas import tpu as pltpu
from jax.experimental.pallas import tpu_sc as plsc

F32 = jnp.float32
BF16 = jnp.bfloat16
I32 = jnp.int32

ATT_HEADS = 8
ATT_HEAD_DIM = 64
KV_LATENT = 256
IDX_HEADS = 8
IDX_DIM = 64
TOPK_MAX = 256
HG_HEADS = 4
HG_DIM = 128
HG_CHUNK = 32
N_EXPERTS = 32
TOP_K = 4
SWIGLU_LIMIT = 7.0
SWIGLU_ALPHA = 1.702
LN_EPS = 1e-5
RMS_EPS = 1e-6

LANES = 128
Q_ROWS = 128
KEY_CHUNK = 512
INDEX_ROWS = 256
COUNT_ROWS = 32
SEARCH_BITS = 20
EXTRACT_MAX = 3
MOE_ROWS = 512
SC_CORES = 2
SC_SUBCORES = 16
SC_WORKERS = SC_CORES * SC_SUBCORES
SC_ROWS = 128
SC_SCATTER_ROWS = 128
COMBINE_PARTS = 8
VMEM_LIMIT = 56 * 1024 * 1024

INT_MIN = -(2 ** 31)
NEG_BIG = -1e30
LOG2_E = 1.4426950408889634

_N_KV = 0
_N_IKW = _N_KV + KV_LATENT
_N_HG = _N_IKW + LANES
_N_END = _N_HG + 4 * HG_HEADS * HG_DIM
_T_AQ = 0
_T_IQ = _T_AQ + ATT_HEADS * ATT_HEAD_DIM
_T_IKW = _T_IQ + IDX_HEADS * LANES
_T_KV = _T_IKW + LANES
_T_END = _T_KV + KV_LATENT


def _dot(a, b):
    return jnp.dot(a, b, preferred_element_type=F32)


def _dot_nt(a, b):
    return lax.dot_general(a, b, (((1,), (1,)), ((), ())), preferred_element_type=F32)


def _dot_tn(a, b):
    return lax.dot_general(a, b, (((0,), (0,)), ((), ())), preferred_element_type=F32)


def _layer_norm(z, g, b):
    mu = jnp.mean(z, axis=-1, keepdims=True)
    d = z - mu
    var = jnp.mean(d * d, axis=-1, keepdims=True)
    return d * lax.rsqrt(var + LN_EPS) * g + b


def _sigmoid(x):
    return 1.0 / (1.0 + jnp.exp(-x))


def _inproj_kernel(x_ref, wn_ref, wt_ref, wukt_ref, kvg_ref, kvgc_ref, ikg_ref, ikb_ref,
                   qt_ref, iqt_ref, iwt_ref, c_ref, ct_ref, ik_ref, hg_ref):
    xb = x_ref[...].astype(BF16)

    aqt = _dot_nt(wt_ref[_T_AQ:_T_IQ, :], xb).astype(BF16)
    iqt_ref[...] = (_dot_nt(wt_ref[_T_IQ:_T_IKW, :], xb) * (IDX_DIM ** -0.5)).astype(BF16)
    iwt_ref[...] = _dot_nt(wt_ref[_T_IKW:_T_KV, :], xb) * (IDX_HEADS ** -0.5)
    act = _dot_nt(wt_ref[_T_KV:_T_END, :], xb)
    ct = act * lax.rsqrt(jnp.mean(act * act, axis=0, keepdims=True) + RMS_EPS) * kvgc_ref[...]
    ct_ref[...] = ct.astype(BF16)
    qt_ref[...] = (_dot(wukt_ref[...], aqt) * (LOG2_E * ATT_HEAD_DIM ** -0.5)).astype(BF16)

    ac = _dot(xb, wn_ref[:, _N_KV:_N_IKW])
    c = ac * lax.rsqrt(jnp.mean(ac * ac, axis=-1, keepdims=True) + RMS_EPS) * kvg_ref[...]
    c_ref[...] = c.astype(BF16)

    ikw = _dot(xb, wn_ref[:, _N_IKW:_N_HG])
    lane = lax.broadcasted_iota(I32, ikw.shape, 1)
    is_k = lane < IDX_DIM
    mu = jnp.sum(jnp.where(is_k, ikw, 0.0), axis=-1, keepdims=True) * (1.0 / IDX_DIM)
    d = jnp.where(is_k, ikw - mu, 0.0)
    var = jnp.sum(d * d, axis=-1, keepdims=True) * (1.0 / IDX_DIM)
    ik = d * lax.rsqrt(var + LN_EPS) * ikg_ref[...] + ikb_ref[...]
    ik_ref[...] = ik.astype(BF16)

    hg_ref[...] = _dot(xb, wn_ref[:, _N_HG:_N_END])


def _inproj(x2, w_n, w_t, wuk_t, kvg, kvg_col, ikg, ikb, *, tm):
    n, d = x2.shape
    grid = (n // tm,)
    full = lambda shape: pl.BlockSpec(shape, lambda i: (0,) * len(shape))
    rows = lambda w: pl.BlockSpec((tm, w), lambda i: (i, 0))
    cols = lambda h: pl.BlockSpec((h, tm), lambda i: (0, i))
    return pl.pallas_call(
        _inproj_kernel,
        grid=grid,
        in_specs=[rows(d), full(w_n.shape), full(w_t.shape), full(wuk_t.shape), full(kvg.shape),
                  full(kvg_col.shape), full(ikg.shape), full(ikb.shape)],
        out_specs=[cols(ATT_HEADS * KV_LATENT), cols(IDX_HEADS * LANES), cols(LANES),
                   rows(KV_LATENT), cols(KV_LATENT), rows(LANES), rows(4 * HG_HEADS * HG_DIM)],
        out_shape=[jax.ShapeDtypeStruct((ATT_HEADS * KV_LATENT, n), BF16),
                   jax.ShapeDtypeStruct((IDX_HEADS * LANES, n), BF16),
                   jax.ShapeDtypeStruct((LANES, n), F32),
                   jax.ShapeDtypeStruct((n, KV_LATENT), BF16),
                   jax.ShapeDtypeStruct((KV_LATENT, n), BF16),
                   jax.ShapeDtypeStruct((n, LANES), BF16),
                   jax.ShapeDtypeStruct((n, 4 * HG_HEADS * HG_DIM), F32)],
        compiler_params=pltpu.CompilerParams(dimension_semantics=("arbitrary",),
                                             vmem_limit_bytes=VMEM_LIMIT),
        name="inproj",
    )(x2, w_n, w_t, wuk_t, kvg, kvg_col, ikg, ikb)


def _hgrn2_kernel(hg_ref, lb_ref, ng_ref, o_ref, state_ref, *, tb):
    @pl.when(pl.program_id(1) == 0)
    def _():
        state_ref[...] = jnp.zeros_like(state_ref)

    nchunk = tb // HG_CHUNK
    width = HG_HEADS * HG_DIM
    row = lax.broadcasted_iota(I32, (tb, HG_DIM), 0)
    rin = jnp.bitwise_and(row, HG_CHUNK - 1)
    r2 = lax.broadcasted_iota(I32, (tb, tb), 0)
    c2 = lax.broadcasted_iota(I32, (tb, tb), 1)
    intra = jnp.logical_and(r2 // HG_CHUNK == c2 // HG_CHUNK, c2 <= r2)

    for h in range(HG_HEADS):
        sl = slice(h * HG_DIM, (h + 1) * HG_DIM)
        gq = hg_ref[0, :, h * HG_DIM:(h + 1) * HG_DIM]
        gf = hg_ref[0, :, width + h * HG_DIM:width + (h + 1) * HG_DIM]
        gi = hg_ref[0, :, 2 * width + h * HG_DIM:2 * width + (h + 1) * HG_DIM]
        gg = hg_ref[0, :, 3 * width + h * HG_DIM:3 * width + (h + 1) * HG_DIM]
        lb = lb_ref[:, sl]
        forget = lb + (1.0 - lb) * _sigmoid(gf)
        logf = jnp.log(forget)
        b = logf
        s = 1
        while s < HG_CHUNK:
            b = b + jnp.where(rin >= s, pltpu.roll(b, s, axis=0), 0.0)
            s *= 2
        tot = jnp.sum(logf.reshape(nchunk, HG_CHUNK, HG_DIM), axis=1, keepdims=True)
        bl = jnp.broadcast_to(tot, (nchunk, HG_CHUNK, HG_DIM)).reshape(tb, HG_DIM)
        kk = 1.0 - forget
        q_dec = (gq * _sigmoid(gq) * jnp.exp(b)).astype(BF16)
        k_inv = (kk * jnp.exp(-b)).astype(BF16)
        k_end = (kk * jnp.exp(bl - b)).astype(BF16)
        vb = gi.astype(BF16)
        scores = jnp.where(intra, _dot_nt(q_dec, k_inv), 0.0).astype(BF16)
        o = _dot(scores, vb)
        decay = jnp.exp(tot)
        chunk_rows = [slice(n * HG_CHUNK, (n + 1) * HG_CHUNK) for n in range(nchunk)]
        incs = [_dot_tn(vb[rs], k_end[rs]) for rs in chunk_rows]
        st = state_ref[h]
        starts = []
        for n in range(nchunk):
            starts.append(st.astype(BF16))
            st = st * decay[n] + incs[n]
        state_ref[h] = st
        inter = [_dot_nt(q_dec[rs], starts[n]) for n, rs in enumerate(chunk_rows)]
        o = o + jnp.concatenate(inter, axis=0)
        o = o * lax.rsqrt(jnp.mean(o * o, axis=-1, keepdims=True) + RMS_EPS)
        o = o * ng_ref[:, sl] * (gg * _sigmoid(gg))
        o_ref[0, :, sl] = o.astype(o_ref.dtype)


def _hgrn2(hg3, lb, ng, *, tb):
    bsz, t, w4 = hg3.shape
    width = HG_HEADS * HG_DIM
    return pl.pallas_call(
        functools.partial(_hgrn2_kernel, tb=tb),
        grid=(bsz, t // tb),
        in_specs=[pl.BlockSpec((1, tb, w4), lambda b, i: (b, i, 0)),
                  pl.BlockSpec((1, width), lambda b, i: (0, 0)),
                  pl.BlockSpec((1, width), lambda b, i: (0, 0))],
        out_specs=pl.BlockSpec((1, tb, width), lambda b, i: (b, i, 0)),
        out_shape=jax.ShapeDtypeStruct((bsz, t, width), BF16),
        scratch_shapes=[pltpu.VMEM((HG_HEADS, HG_DIM, HG_DIM), F32)],
        compiler_params=pltpu.CompilerParams(dimension_semantics=("arbitrary", "arbitrary"),
                                             vmem_limit_bytes=VMEM_LIMIT),
        name="hgrn2",
    )(hg3, lb, ng)


def _dsa_kernel(qt_ref, iqt_ref, iwt_ref, ik_ref, c_ref, ct_ref, wuvt_ref, o_ref,
                keys_ref, cut_ref, thr_ref, redo_ref, qall_ref, iqall_ref, m_ref, l_ref, acc_ref,
                *, topk, pos_bits):
    n = pl.program_id(1)
    t0 = n * Q_ROWS
    nch = (t0 + Q_ROWS + KEY_CHUNK - 1) // KEY_CHUNK
    kf = float(topk)
    key_i = lax.broadcasted_iota(I32, (KEY_CHUNK, Q_ROWS), 0)
    tq = t0 + lax.broadcasted_iota(I32, (KEY_CHUNK, Q_ROWS), 1)
    pair = 2 * Q_ROWS

    for h in range(ATT_HEADS):
        qall_ref[:, h * Q_ROWS:(h + 1) * Q_ROWS] = qt_ref[h * KV_LATENT:(h + 1) * KV_LATENT, :]
    for h in range(IDX_HEADS):
        iqall_ref[:, h * Q_ROWS:(h + 1) * Q_ROWS] = iqt_ref[h * LANES:(h + 1) * LANES, :]
    head_w = [iwt_ref[IDX_DIM + h:IDX_DIM + h + 1, :] for h in range(IDX_HEADS)]

    key_idx = lax.broadcasted_iota(I32, (INDEX_ROWS, Q_ROWS), 0)
    tq_idx = t0 + lax.broadcasted_iota(I32, (INDEX_ROWS, Q_ROWS), 1)

    def index_body(j, carry):
        for r in range(KEY_CHUNK // INDEX_ROWS):
            k0 = pl.multiple_of(j * KEY_CHUNK + r * INDEX_ROWS, INDEX_ROWS)
            ikc = ik_ref[pl.ds(k0, INDEX_ROWS), :]
            isc = jnp.zeros((INDEX_ROWS, Q_ROWS), F32)
            for hp in range(IDX_HEADS // 2):
                z = _dot(ikc, iqall_ref[:, hp * pair:(hp + 1) * pair])
                for hh in range(2):
                    isc = isc + head_w[2 * hp + hh] * jnp.maximum(z[:, hh * Q_ROWS:(hh + 1) * Q_ROWS], 0.0)
            isc = jnp.where(isc == 0.0, 0.0, isc)
            bits = pltpu.bitcast(isc, I32)
            key = jnp.bitwise_xor(bits, jnp.bitwise_and(jnp.right_shift(bits, 31), 0x7FFFFFFF))
            key = jnp.where(k0 + key_idx <= tq_idx, key, INT_MIN)
            keys_ref[pl.ds(k0, INDEX_ROWS), :] = key
        return carry

    lax.fori_loop(0, nch, index_body, 0)

    def count(pred):
        def body(j, acc):
            k0 = pl.multiple_of(j * KEY_CHUNK, KEY_CHUNK)
            hit = jnp.where(pred(keys_ref[pl.ds(k0, KEY_CHUNK), :], k0), 1.0, 0.0)
            return acc + jnp.sum(hit.reshape(KEY_CHUNK // COUNT_ROWS, COUNT_ROWS, Q_ROWS), axis=0)
        acc = lax.fori_loop(0, nch, body, jnp.zeros((COUNT_ROWS, Q_ROWS), F32))
        return jnp.sum(acc, axis=0, keepdims=True)

    def count_ge(cand):
        return count(lambda k, k0: k >= cand)

    def masked_max(bound):
        def body(j, acc):
            k0 = pl.multiple_of(j * KEY_CHUNK, KEY_CHUNK)
            key = keys_ref[pl.ds(k0, KEY_CHUNK), :]
            kept = jnp.where(key <= bound, key, INT_MIN)
            return jnp.maximum(acc, jnp.max(kept.reshape(KEY_CHUNK // COUNT_ROWS, COUNT_ROWS, Q_ROWS), axis=0))
        acc = lax.fori_loop(0, nch, body, jnp.full((COUNT_ROWS, Q_ROWS), INT_MIN, I32))
        return jnp.max(acc, axis=0, keepdims=True)

    n_all = (nch * KEY_CHUNK).astype(F32)
    n0 = count_ge(jnp.zeros((1, Q_ROWS), I32))
    pos0 = n0 >= kf
    state0 = (jnp.where(pos0, 0, INT_MIN).astype(I32), jnp.where(pos0, n0, n_all), jnp.where(pos0, 0.0, n0))

    def bracket_body(i, state):
        thr, n_lo, n_hi = state
        cand = thr + lax.shift_left(jnp.int32(1), 30 - i)
        cnt = count_ge(cand)
        ge = cnt >= kf
        return jnp.where(ge, cand, thr), jnp.where(ge, cnt, n_lo), jnp.where(ge, n_hi, cnt)

    thr1, n_lo, n_hi = lax.fori_loop(0, SEARCH_BITS, bracket_body, state0)
    low_bits = 31 - SEARCH_BITS
    has_topk = tq[0:1, :] + 1 >= topk
    in_bracket = n_lo - n_hi
    exact_cut = n_lo == kf
    few = jnp.logical_or(jnp.logical_not(has_topk), jnp.logical_or(exact_cut, in_bracket <= float(EXTRACT_MAX)))
    thr_ref[...] = jnp.broadcast_to(thr1, thr_ref.shape)
    redo_ref[0] = jnp.where(jnp.min(jnp.where(few, 1.0, 0.0)) > 0.0, 0, 1)

    @pl.when(redo_ref[0] == 0)
    def _():
        need = kf - n_hi
        bound = thr1 + (2 ** low_bits - 1)
        found = jnp.zeros((1, Q_ROWS), F32)
        thr_x = thr1
        for j in range(1, EXTRACT_MAX + 1):
            cur = masked_max(bound)
            inside = cur >= thr1
            found = found + jnp.where(inside, 1.0, 0.0)
            thr_x = jnp.where(jnp.logical_and(inside, need == float(j)), cur, thr_x)
            bound = jnp.where(inside, cur - 1, INT_MIN)
        use_x = jnp.logical_and(has_topk, jnp.logical_not(exact_cut))
        clash = jnp.logical_and(use_x, found != in_bracket)
        thr_ref[...] = jnp.broadcast_to(jnp.where(use_x, thr_x, thr1), thr_ref.shape)
        redo_ref[0] = jnp.where(jnp.max(jnp.where(clash, 1.0, 0.0)) > 0.0, 1, 0)

    @pl.when(redo_ref[0] == 1)
    def _():
        def bit_body(i, thr):
            cand = thr + lax.shift_left(jnp.int32(1), low_bits - 1 - i)
            return jnp.where(count_ge(cand) >= kf, cand, thr)

        thr_ref[...] = jnp.broadcast_to(lax.fori_loop(0, low_bits, bit_body, thr1), thr_ref.shape)

    thr = thr_ref[0:1, :]
    live = thr > INT_MIN
    cut_ref[...] = jnp.broadcast_to(jnp.where(live, jnp.int32(2 ** 30), -1), cut_ref.shape)

    @pl.when(redo_ref[0] == 1)
    def _():
        n_ge = count_ge(thr)
        tie = jnp.max(jnp.where(jnp.logical_and(live, n_ge > kf), 1.0, 0.0))

        @pl.when(tie > 0.0)
        def _():
            need = kf - count(lambda k, k0: k > thr)

            def cut_body(i, cut):
                cand = cut + lax.shift_left(jnp.int32(1), pos_bits - 1 - i)
                below = count(lambda k, k0: jnp.logical_and(k == thr, k0 + key_i < cand))
                return jnp.where(below < need, cand, cut)

            cut = lax.fori_loop(0, pos_bits, cut_body, jnp.zeros((1, Q_ROWS), I32))
            cut_ref[...] = jnp.broadcast_to(jnp.where(live, cut, -1), cut_ref.shape)

    m_ref[...] = jnp.full(m_ref.shape, NEG_BIG, F32)
    l_ref[...] = jnp.zeros(l_ref.shape, F32)
    acc_ref[...] = jnp.zeros(acc_ref.shape, F32)
    cut_b = cut_ref[0:1, :]

    def attn_body(j, carry):
        k0 = pl.multiple_of(j * KEY_CHUNK, KEY_CHUNK)
        key = keys_ref[pl.ds(k0, KEY_CHUNK), :]
        pos = k0 + key_i
        sel = jnp.logical_or(key > thr, jnp.logical_and(key == thr, pos <= cut_b))
        rel = jnp.where(sel, (pos - t0).astype(F32), NEG_BIG)
        cc = c_ref[pl.ds(k0, KEY_CHUNK), :]
        cct = ct_ref[:, pl.ds(k0, KEY_CHUNK)]
        for hp in range(ATT_HEADS // 2):
            cs = slice(hp * pair, (hp + 1) * pair)
            s2 = _dot(cc, qall_ref[:, cs])
            ps, alphas = [], []
            for hh in range(2):
                h = 2 * hp + hh
                slope = LOG2_E * 2.0 ** (-(8.0 / ATT_HEADS) * (h + 1))
                s = s2[:, hh * Q_ROWS:(hh + 1) * Q_ROWS] + slope * rel
                m_old = m_ref[h:h + 1, :]
                m_new = jnp.maximum(m_old, jnp.max(s, axis=0, keepdims=True))
                alpha = jnp.exp2(m_old - m_new)
                p = jnp.exp2(s - m_new)
                l_ref[h:h + 1, :] = alpha * l_ref[h:h + 1, :] + jnp.sum(p, axis=0, keepdims=True)
                m_ref[h:h + 1, :] = m_new
                ps.append(p.astype(BF16))
                alphas.append(alpha)
            acc_ref[:, cs] = (acc_ref[:, cs] * jnp.concatenate(alphas, axis=1)
                              + _dot(cct, jnp.concatenate(ps, axis=1)))
        return carry

    lax.fori_loop(0, nch, attn_body, 0)

    yt = []
    for h in range(ATT_HEADS):
        o_lat = (acc_ref[:, h * Q_ROWS:(h + 1) * Q_ROWS] / l_ref[h:h + 1, :]).astype(BF16)
        yt.append(_dot(wuvt_ref[h], o_lat))
    o_ref[...] = jnp.concatenate(yt, axis=0).T.astype(o_ref.dtype)


def _dsa(qt, iqt, iwt, ik, c, ct, wuvt_p, *, bsz, t):
    n = bsz * t
    nq = t // Q_ROWS
    topk = min(TOPK_MAX, t // 4)
    assert t % KEY_CHUNK == 0
    width = ATT_HEADS * ATT_HEAD_DIM
    qcols = lambda h: pl.BlockSpec((h, Q_ROWS), lambda b, i: (0, b * nq + i))
    return pl.pallas_call(
        functools.partial(_dsa_kernel, topk=topk, pos_bits=(t - 1).bit_length()),
        grid=(bsz, nq),
        in_specs=[qcols(ATT_HEADS * KV_LATENT), qcols(IDX_HEADS * LANES), qcols(LANES),
                  pl.BlockSpec((t, LANES), lambda b, i: (b, 0)),
                  pl.BlockSpec((t, KV_LATENT), lambda b, i: (b, 0)),
                  pl.BlockSpec((KV_LATENT, t), lambda b, i: (0, b)),
                  pl.BlockSpec(wuvt_p.shape, lambda b, i: (0, 0, 0))],
        out_specs=pl.BlockSpec((Q_ROWS, width), lambda b, i: (b * nq + i, 0)),
        out_shape=jax.ShapeDtypeStruct((n, width), BF16),
        scratch_shapes=[pltpu.VMEM((t, Q_ROWS), I32),
                        pltpu.VMEM((8, Q_ROWS), I32),
                        pltpu.VMEM((8, Q_ROWS), I32),
                        pltpu.SMEM((1,), I32),
                        pltpu.VMEM((KV_LATENT, ATT_HEADS * Q_ROWS), BF16),
                        pltpu.VMEM((LANES, IDX_HEADS * Q_ROWS), BF16),
                        pltpu.VMEM((8, Q_ROWS), F32),
                        pltpu.VMEM((8, Q_ROWS), F32),
                        pltpu.VMEM((KV_LATENT, ATT_HEADS * Q_ROWS), F32)],
        compiler_params=pltpu.CompilerParams(dimension_semantics=("arbitrary", "arbitrary"),
                                             vmem_limit_bytes=VMEM_LIMIT),
        name="dsa",
    )(qt, iqt, iwt, ik, c, ct, wuvt_p)


def _mix_router_kernel(x_ref, ya_ref, yr_ref, wo_ref, g_ref, b_ref, wr_ref, br_ref,
                       x1_ref, x1w_ref, route_ref, tot_ref, carry_ref, *, alpha, tm):
    @pl.when(pl.program_id(0) == 0)
    def _():
        carry_ref[...] = jnp.zeros_like(carry_ref)

    half = ya_ref.shape[1]
    mix = _dot(ya_ref[...], wo_ref[0:half, :]) + _dot(yr_ref[...], wo_ref[half:, :])
    x1 = _layer_norm(alpha * x_ref[...] + mix, g_ref[...], b_ref[...])
    x1_ref[...] = x1
    x1w_ref[...] = _pack_words(x1)

    x_hi = x1.astype(BF16)
    x_lo = (x1 - x_hi.astype(F32)).astype(BF16)
    hh_hl = _dot(x_hi, wr_ref[...])
    logits = (hh_hl[:, :LANES] + hh_hl[:, LANES:] + _dot(x_lo, wr_ref[:, :LANES])) + br_ref[...]
    lane = lax.broadcasted_iota(I32, logits.shape, 1)
    lane_f = lane.astype(F32)
    work = jnp.where(lane < N_EXPERTS, logits, -jnp.inf)
    sel_f = jnp.zeros(logits.shape, F32)
    denom = jnp.zeros((tm, 1), F32)
    hits, experts, weights = [], [], []
    for _ in range(TOP_K):
        mx = jnp.max(work, axis=-1, keepdims=True)
        first = jnp.min(jnp.where(work == mx, lane_f, float(LANES)), axis=-1, keepdims=True)
        hit = lane_f == first
        e = jnp.exp(mx - (weights[0][1] if weights else mx))
        hits.append(hit)
        experts.append(first)
        weights.append((e, mx))
        denom = denom + e
        sel_f = sel_f + jnp.where(hit, 1.0, 0.0)
        work = jnp.where(hit, -jnp.inf, work)

    r2 = lax.broadcasted_iota(I32, (tm, tm), 0)
    c2 = lax.broadcasted_iota(I32, (tm, tm), 1)
    before = jnp.where(c2 < r2, 1.0, 0.0).astype(BF16)
    rank = _dot(before, sel_f.astype(BF16)) + carry_ref[...]
    carry = carry_ref[...] + jnp.sum(sel_f, axis=0, keepdims=True)
    carry_ref[...] = carry
    tot_ref[...] = jnp.broadcast_to(carry, tot_ref.shape)

    route = jnp.zeros(logits.shape, F32)
    for k in range(TOP_K):
        rank_k = jnp.sum(jnp.where(hits[k], rank, 0.0), axis=-1, keepdims=True)
        route = jnp.where(lane == k, experts[k], route)
        route = jnp.where(lane == TOP_K + k, rank_k, route)
        route = jnp.where(lane == 2 * TOP_K + k, weights[k][0] / denom, route)
    route_ref[...] = route


def _mix_router(x2, ya, yr, wo, g, b, wr_p, br_p, *, alpha, tm):
    n, d = x2.shape
    half = ya.shape[1]
    rows = lambda w: pl.BlockSpec((tm, w), lambda i: (i, 0))
    full = lambda shape: pl.BlockSpec(shape, lambda i: (0,) * len(shape))
    return pl.pallas_call(
        functools.partial(_mix_router_kernel, alpha=alpha, tm=tm),
        grid=(n // tm,),
        in_specs=[rows(d), rows(half), rows(half), full(wo.shape), full(g.shape), full(b.shape),
                  full(wr_p.shape), full(br_p.shape)],
        out_specs=[rows(d), rows(d // 2), rows(LANES), full((8, LANES))],
        out_shape=[jax.ShapeDtypeStruct((n, d), F32),
                   jax.ShapeDtypeStruct((n, d // 2), I32),
                   jax.ShapeDtypeStruct((n, LANES), F32),
                   jax.ShapeDtypeStruct((8, LANES), F32)],
        scratch_shapes=[pltpu.VMEM((1, LANES), F32)],
        compiler_params=pltpu.CompilerParams(dimension_semantics=("arbitrary",),
                                             vmem_limit_bytes=VMEM_LIMIT),
        name="mix_router",
    )(x2, ya, yr, wo, g, b, wr_p, br_p)


def _sc_gather_rows(table, idx):
    n_idx = idx.shape[0]
    width = table.shape[1]
    per_worker = n_idx // SC_WORKERS
    assert per_worker * SC_WORKERS == n_idx and per_worker % SC_ROWS == 0
    mesh = plsc.VectorSubcoreMesh(core_axis_name="c", subcore_axis_name="s",
                                  num_cores=SC_CORES, num_subcores=SC_SUBCORES)

    @functools.partial(
        pl.kernel, mesh=mesh,
        out_type=jax.ShapeDtypeStruct((n_idx, width), table.dtype),
        scratch_types=[pltpu.VMEM((SC_ROWS,), I32),
                       pltpu.VMEM((SC_ROWS, width), table.dtype),
                       pltpu.SemaphoreType.DMA])
    def gather(table_hbm, idx_hbm, out_hbm, idx_v, rows_v, sem):
        worker = lax.axis_index("s") * SC_CORES + lax.axis_index("c")

        @pl.loop(0, per_worker // SC_ROWS)
        def _(step):
            base = pl.multiple_of(worker * per_worker + step * SC_ROWS, SC_ROWS)
            pltpu.sync_copy(idx_hbm.at[pl.ds(base, SC_ROWS)], idx_v)
            pltpu.async_copy(table_hbm.at[idx_v], rows_v, sem).wait()
            pltpu.sync_copy(rows_v, out_hbm.at[pl.ds(base, SC_ROWS)])

    return gather(table, idx)


def _sc_scatter_rows(rows, dest3, cap):
    n, width = rows.shape
    chunks, picks, r = dest3.shape
    assert r == SC_SCATTER_ROWS and chunks * r == n and chunks % SC_WORKERS == 0
    per_worker = chunks // SC_WORKERS
    mesh = plsc.VectorSubcoreMesh(core_axis_name="c", subcore_axis_name="s",
                                  num_cores=SC_CORES, num_subcores=SC_SUBCORES)

    @functools.partial(
        pl.kernel, mesh=mesh,
        out_type=jax.ShapeDtypeStruct((cap, width), rows.dtype),
        scratch_types=[pltpu.VMEM((picks, r), I32),
                       pltpu.VMEM((r, width), rows.dtype),
                       pltpu.SemaphoreType.DMA])
    def scatter(rows_hbm, dest_hbm, out_hbm, idx_v, rows_v, sem):
        worker = lax.axis_index("s") * SC_CORES + lax.axis_index("c")

        @pl.loop(0, per_worker)
        def _(step):
            chunk = worker * per_worker + step
            base = pl.multiple_of(chunk * r, r)
            pltpu.sync_copy(rows_hbm.at[pl.ds(base, r)], rows_v)
            pltpu.sync_copy(dest_hbm.at[chunk], idx_v)
            for k in range(picks):
                pltpu.async_copy(rows_v, out_hbm.at[idx_v.at[k]], sem).wait()

    return scatter(rows, dest3)


def _pack_words(v):
    k = v.shape[1] // 2
    hi = pltpu.bitcast(v[:, :k].astype(BF16).astype(F32), I32)
    lo = pltpu.bitcast(v[:, k:].astype(BF16).astype(F32), I32)
    return jnp.bitwise_or(hi, lax.shift_right_logical(lo, jnp.int32(16)))


def _unpack_words(w, dtype=BF16):
    hi = pltpu.bitcast(jnp.bitwise_and(w, jnp.int32(-65536)), F32)
    lo = pltpu.bitcast(jnp.left_shift(w, 16), F32)
    return jnp.concatenate([hi, lo], axis=1).astype(dtype)


def _expert_kernel(exp_ref, blk_ref, new_ref, x_ref, wup_ref, bup_ref, wdn_ref, bdn_ref, o_ref,
                   wup_s, wdn_s, *, d_ff):
    del exp_ref, blk_ref

    @pl.when(new_ref[pl.program_id(0)] == 1)
    def _():
        wup_s[...] = wup_ref[0].astype(BF16)
        wdn_s[...] = wdn_ref[0].astype(BF16)

    half = MOE_ROWS // 2
    for r in range(2):
        rows = slice(r * half, (r + 1) * half)
        h = _dot(_unpack_words(x_ref[rows, :]), wup_s[...]) + bup_ref[0]
        glu = jnp.minimum(h[:, :d_ff], SWIGLU_LIMIT)
        lin = jnp.clip(h[:, d_ff:], -SWIGLU_LIMIT, SWIGLU_LIMIT)
        a = glu * _sigmoid(SWIGLU_ALPHA * glu) * (lin + 1.0)
        o_ref[rows, :] = _pack_words(_dot(a.astype(BF16), wdn_s[...]) + bdn_ref[0])


def _experts(blk_expert, blk_index, blk_new, xs, wup, bup, wdn, bdn):
    cap, words = xs.shape
    d = 2 * words
    d_ff = wdn.shape[1]
    row_map = lambda i, exp, blk, new: (blk[i], 0)
    exp_map3 = lambda i, exp, blk, new: (exp[i], 0, 0)
    grid_spec = pltpu.PrefetchScalarGridSpec(
        num_scalar_prefetch=3,
        grid=(cap // MOE_ROWS,),
        in_specs=[pl.BlockSpec((MOE_ROWS, words), row_map),
                  pl.BlockSpec((1, d, 2 * d_ff), exp_map3),
                  pl.BlockSpec((1, 1, 2 * d_ff), exp_map3),
                  pl.BlockSpec((1, d_ff, d), exp_map3),
                  pl.BlockSpec((1, 1, d), exp_map3)],
        out_specs=pl.BlockSpec((MOE_ROWS, words), row_map),
        scratch_shapes=[pltpu.VMEM((d, 2 * d_ff), BF16), pltpu.VMEM((d_ff, d), BF16)],
    )
    return pl.pallas_call(
        functools.partial(_expert_kernel, d_ff=d_ff),
        grid_spec=grid_spec,
        out_shape=jax.ShapeDtypeStruct((cap, words), I32),
        compiler_params=pltpu.CompilerParams(dimension_semantics=("arbitrary",),
                                             vmem_limit_bytes=VMEM_LIMIT),
        name="experts",
    )(blk_expert, blk_index, blk_new, xs, wup, bup, wdn, bdn)


def _route_tables(route, totals, *, cap):
    n = route.shape[0]
    expert = route[:, 0:TOP_K].astype(I32)
    rank = route[:, TOP_K:2 * TOP_K].astype(I32)
    counts = totals[0, :N_EXPERTS].astype(I32)
    padded = (counts + MOE_ROWS - 1) // MOE_ROWS * MOE_ROWS
    pad_end = jnp.cumsum(padded)
    pad_start = pad_end - padded
    dest = pad_start[expert] + rank
    dest_chunks = dest.reshape(n // SC_SCATTER_ROWS, SC_SCATTER_ROWS, TOP_K).transpose(0, 2, 1)
    n_blk = cap // MOE_ROWS
    used = pad_end[-1] // MOE_ROWS
    blk = jnp.minimum(jnp.arange(n_blk, dtype=I32), used - 1)
    blk_expert = jnp.minimum(jnp.sum(pad_end[None, :] <= (blk * MOE_ROWS)[:, None], axis=1),
                             N_EXPERTS - 1).astype(I32)
    blk_new = jnp.concatenate([jnp.ones((1,), I32), (blk_expert[1:] != blk_expert[:-1]).astype(I32)])
    return dest, dest_chunks, blk_expert, blk, blk_new


def _final_kernel(x1_ref, ya_ref, yb_ref, yc_ref, yd_ref, route_ref, p_ref, g1_ref, b1_ref,
                  wp_ref, wg_ref, g2_ref, b2_ref, *rest, alpha):
    o_ref = rest[-1]
    ple_lin = _dot(p_ref[...].astype(BF16), wp_ref[...])
    ffn = jnp.zeros(x1_ref.shape, F32)
    for k, yk_ref in enumerate((ya_ref, yb_ref, yc_ref, yd_ref)):
        gate_k = route_ref[:, 2 * TOP_K + k:2 * TOP_K + k + 1]
        ffn = ffn + gate_k * _unpack_words(yk_ref[...], F32)
    x2 = _layer_norm(alpha * x1_ref[...] + ffn, g1_ref[...], b1_ref[...])
    gate = _sigmoid(_dot(x2.astype(BF16), wg_ref[...]))
    o_ref[...] = _layer_norm(alpha * x2 + ple_lin * gate, g2_ref[...], b2_ref[...])


def _final(x1, y4w, route, p2, g1, b1, wp, wg, g2, b2, *, alpha, tm, part, parts, prev=None):
    n, d = x1.shape
    assert TOP_K == 4
    steps = n // parts // tm
    off = part * steps
    rows = lambda w: pl.BlockSpec((tm, w), lambda i: (off + i, 0))
    full = lambda shape: pl.BlockSpec(shape, lambda i: (0,) * len(shape))
    pick = lambda k: pl.BlockSpec((tm, d // 2), lambda i: (k * steps + i, 0))
    in_specs = [rows(d), pick(0), pick(1), pick(2), pick(3), rows(LANES), rows(p2.shape[1]),
                full(g1.shape), full(b1.shape),
                full(wp.shape), full(wg.shape), full(g2.shape), full(b2.shape)]
    args = [x1, y4w, y4w, y4w, y4w, route, p2, g1, b1, wp, wg, g2, b2]
    aliases = {}
    if prev is not None:
        in_specs.append(pl.BlockSpec(memory_space=pl.ANY))
        args.append(prev)
        aliases = {len(args) - 1: 0}
    return pl.pallas_call(
        functools.partial(_final_kernel, alpha=alpha),
        grid=(steps,),
        in_specs=in_specs,
        out_specs=rows(d),
        out_shape=jax.ShapeDtypeStruct((n, d), F32),
        input_output_aliases=aliases,
        compiler_params=pltpu.CompilerParams(dimension_semantics=("arbitrary",),
                                             vmem_limit_bytes=VMEM_LIMIT),
        name="final",
    )(*args)


def _pack_w_in(w_in):
    d = w_in.shape[0]
    sizes = (ATT_HEADS * ATT_HEAD_DIM, KV_LATENT, IDX_HEADS * IDX_DIM, IDX_DIM, IDX_HEADS,
             HG_HEADS * HG_DIM, HG_HEADS * HG_DIM, HG_HEADS * HG_DIM, HG_HEADS * HG_DIM)
    offs = [0]
    for s in sizes:
        offs.append(offs[-1] + s)
    sec = [w_in[:, offs[k]:offs[k + 1]] for k in range(len(sizes))]
    iq = sec[2].reshape(d, IDX_HEADS, IDX_DIM)
    iq = jnp.pad(iq, ((0, 0), (0, 0), (0, LANES - IDX_DIM))).reshape(d, IDX_HEADS * LANES)
    ikw = jnp.pad(jnp.concatenate([sec[3], sec[4]], axis=1), ((0, 0), (0, LANES - IDX_DIM - IDX_HEADS)))
    w_n = jnp.concatenate([sec[1], ikw, sec[5], sec[6], sec[7], sec[8]], axis=1).astype(BF16)
    w_t = jnp.concatenate([sec[0], iq, ikw, sec[1]], axis=1).T.astype(BF16)
    return w_n, w_t


def _block_diag_uk_t(w_uk):
    eye = jnp.eye(ATT_HEADS, dtype=w_uk.dtype)
    bd = jnp.einsum("rhd,hg->hrgd", w_uk, eye)
    return bd.reshape(ATT_HEADS * KV_LATENT, ATT_HEADS * ATT_HEAD_DIM).astype(BF16)


def _uv_t(w_uv):
    return jnp.transpose(w_uv, (1, 2, 0)).astype(BF16)


def _layer(x, p_l, w_in, kv_g, ik_g, ik_b, w_uk, w_uv, lb, hg_ng, w_o, ln_mix_g, ln_mix_b,
           w_router, b_router, w_up, b_up, w_down, b_down, ln_ffn_g, ln_ffn_b,
           w_ple_proj, w_ple_gate, ln_ple_g, ln_ple_b, *, alpha, tm, hg_tb):
    bsz, t, d = x.shape
    n = bsz * t
    x2 = x.reshape(n, d)
    row = lambda v: v.reshape(1, -1).astype(F32)
    pad_lane = lambda v: jnp.pad(row(v), ((0, 0), (0, LANES - v.shape[-1])))

    w_n, w_t = _pack_w_in(w_in)
    qt, iqt, iwt, c, ct, ik, hg = _inproj(x2, w_n, w_t, _block_diag_uk_t(w_uk), row(kv_g),
                                          kv_g.reshape(-1, 1).astype(F32), pad_lane(ik_g), pad_lane(ik_b), tm=tm)
    y_rec = _hgrn2(hg.reshape(bsz, t, -1), row(lb), row(hg_ng), tb=hg_tb)
    y_att = _dsa(qt, iqt, iwt, ik, c, ct, _uv_t(w_uv), bsz=bsz, t=t)

    wr_f = jnp.pad(w_router.astype(F32), ((0, 0), (0, LANES - N_EXPERTS)))
    wr_hi = wr_f.astype(BF16)
    wr_p = jnp.concatenate([wr_hi, (wr_f - wr_hi.astype(F32)).astype(BF16)], axis=1)
    x1, x1w, route, totals = _mix_router(
        x2, y_att, y_rec.reshape(n, -1), w_o.astype(BF16), row(ln_mix_g), row(ln_mix_b),
        wr_p, pad_lane(b_router), alpha=alpha, tm=tm)

    step = SC_WORKERS * SC_ROWS
    cap = -(-(n * TOP_K + N_EXPERTS * MOE_ROWS) // step) * step
    dest, dest_chunks, blk_expert, blk_index, blk_new = _route_tables(route, totals, cap=cap)
    d_ff = w_down.shape[1]
    xs = _sc_scatter_rows(x1w, dest_chunks, cap)
    outw = _experts(blk_expert, blk_index, blk_new, xs,
                    w_up.astype(F32), b_up.reshape(N_EXPERTS, 1, 2 * d_ff).astype(F32),
                    w_down.astype(F32), b_down.reshape(N_EXPERTS, 1, d).astype(F32))

    unit = max(tm, SC_WORKERS * SC_ROWS // TOP_K)
    parts = next(c for c in (COMBINE_PARTS, 2, 1) if n % (c * unit) == 0)
    out = None
    for part in range(parts):
        rows_p = slice(part * (n // parts), (part + 1) * (n // parts))
        y4w = _sc_gather_rows(outw, dest[rows_p].T.reshape(-1))
        out = _final(x1, y4w, route, p_l.reshape(n, -1), row(ln_ffn_g), row(ln_ffn_b),
                     w_ple_proj.astype(BF16), w_ple_gate.astype(BF16), row(ln_ple_g), row(ln_ple_b),
                     alpha=alpha, tm=tm, part=part, parts=parts, prev=out)
    return out.reshape(bsz, t, d)


def kernel(x, p, w_in, kv_norm_g, idx_k_norm_g, idx_k_norm_b, w_uk, w_uv, hg_lb_logits, hg_norm_g, w_o,
           ln_mix_g, ln_mix_b, w_router, b_router, w_up, b_up, w_down, b_down, ln_ffn_g, ln_ffn_b,
           w_ple_proj, w_ple_gate, ln_ple_g, ln_ple_b):
    depth = w_in.shape[0]
    alpha = (2.0 * depth) ** 0.25
    lower_bounds = jnp.cumsum(jax.nn.softmax(hg_lb_logits.astype(F32), axis=0), axis=0)
    n = x.shape[0] * x.shape[1]
    tm = min(512, n)
    hg_tb = min(512, x.shape[1])
    for l in range(depth):
        x = _layer(x, p[l], w_in[l], kv_norm_g[l], idx_k_norm_g[l], idx_k_norm_b[l], w_uk[l], w_uv[l],
                   lower_bounds[l], hg_norm_g[l], w_o[l], ln_mix_g[l], ln_mix_b[l], w_router[l],
                   b_router[l], w_up[l], b_up[l], w_down[l], b_down[l], ln_ffn_g[l], ln_ffn_b[l],
                   w_ple_proj[l], w_ple_gate[l], ln_ple_g[l], ln_ple_b[l],
                   alpha=alpha, tm=tm, hg_tb=hg_tb)
    return x
```

```python
import functools

import jax
import jax.numpy as jnp
from jax import lax
from jax.experimental import pallas as pl
from jax.experimental.pallas import tpu as pltpu
from jax.experimental.pallas import tpu_sc as plsc

F32 = jnp.float32
BF16 = jnp.bfloat16
I32 = jnp.int32

ATT_HEADS = 8
ATT_HEAD_DIM = 64
KV_LATENT = 256
IDX_HEADS = 8
IDX_DIM = 64
TOPK_MAX = 256
HG_HEADS = 4
HG_DIM = 128
HG_CHUNK = 32
N_EXPERTS = 32
TOP_K = 4
SWIGLU_LIMIT = 7.0
SWIGLU_ALPHA = 1.702
LN_EPS = 1e-5
RMS_EPS = 1e-6

LANES = 128
Q_ROWS = 128
KEY_CHUNK = 512
INDEX_ROWS = 256
COUNT_ROWS = 32
SEARCH_BITS = 20
EXTRACT_MAX = 3
MOE_ROWS = 512
SC_CORES = 2
SC_SUBCORES = 16
SC_WORKERS = SC_CORES * SC_SUBCORES
SC_ROWS = 128
SC_SCATTER_ROWS = 128
COMBINE_PARTS = 8
VMEM_LIMIT = 56 * 1024 * 1024

INT_MIN = -(2 ** 31)
NEG_BIG = -1e30
LOG2_E = 1.4426950408889634

_N_KV = 0
_N_IKW = _N_KV + KV_LATENT
_N_HG = _N_IKW + LANES
_N_END = _N_HG + 4 * HG_HEADS * HG_DIM
_T_AQ = 0
_T_IQ = _T_AQ + ATT_HEADS * ATT_HEAD_DIM
_T_IKW = _T_IQ + IDX_HEADS * LANES
_T_KV = _T_IKW + LANES
_T_END = _T_KV + KV_LATENT


def _dot(a, b):
    return jnp.dot(a, b, preferred_element_type=F32)


def _dot_nt(a, b):
    return lax.dot_general(a, b, (((1,), (1,)), ((), ())), preferred_element_type=F32)


def _dot_tn(a, b):
    return lax.dot_general(a, b, (((0,), (0,)), ((), ())), preferred_element_type=F32)


def _layer_norm(z, g, b):
    mu = jnp.mean(z, axis=-1, keepdims=True)
    d = z - mu
    var = jnp.mean(d * d, axis=-1, keepdims=True)
    return d * lax.rsqrt(var + LN_EPS) * g + b


def _sigmoid(x):
    return 1.0 / (1.0 + jnp.exp(-x))


def _inproj_kernel(x_ref, wn_ref, wt_ref, wukt_ref, kvg_ref, kvgc_ref, ikg_ref, ikb_ref,
                   qt_ref, iqt_ref, iwt_ref, c_ref, ct_ref, ik_ref, hg_ref):
    xb = x_ref[...].astype(BF16)

    aqt = _dot_nt(wt_ref[_T_AQ:_T_IQ, :], xb).astype(BF16)
    iqt_ref[...] = (_dot_nt(wt_ref[_T_IQ:_T_IKW, :], xb) * (IDX_DIM ** -0.5)).astype(BF16)
    iwt_ref[...] = _dot_nt(wt_ref[_T_IKW:_T_KV, :], xb) * (IDX_HEADS ** -0.5)
    act = _dot_nt(wt_ref[_T_KV:_T_END, :], xb)
    ct = act * lax.rsqrt(jnp.mean(act * act, axis=0, keepdims=True) + RMS_EPS) * kvgc_ref[...]
    ct_ref[...] = ct.astype(BF16)
    qt_ref[...] = (_dot(wukt_ref[...], aqt) * (LOG2_E * ATT_HEAD_DIM ** -0.5)).astype(BF16)

    ac = _dot(xb, wn_ref[:, _N_KV:_N_IKW])
    c = ac * lax.rsqrt(jnp.mean(ac * ac, axis=-1, keepdims=True) + RMS_EPS) * kvg_ref[...]
    c_ref[...] = c.astype(BF16)

    ikw = _dot(xb, wn_ref[:, _N_IKW:_N_HG])
    lane = lax.broadcasted_iota(I32, ikw.shape, 1)
    is_k = lane < IDX_DIM
    mu = jnp.sum(jnp.where(is_k, ikw, 0.0), axis=-1, keepdims=True) * (1.0 / IDX_DIM)
    d = jnp.where(is_k, ikw - mu, 0.0)
    var = jnp.sum(d * d, axis=-1, keepdims=True) * (1.0 / IDX_DIM)
    ik = d * lax.rsqrt(var + LN_EPS) * ikg_ref[...] + ikb_ref[...]
    ik_ref[...] = ik.astype(BF16)

    hg_ref[...] = _dot(xb, wn_ref[:, _N_HG:_N_END])


def _inproj(x2, w_n, w_t, wuk_t, kvg, kvg_col, ikg, ikb, *, tm):
    n, d = x2.shape
    grid = (n // tm,)
    full = lambda shape: pl.BlockSpec(shape, lambda i: (0,) * len(shape))
    rows = lambda w: pl.BlockSpec((tm, w), lambda i: (i, 0))
    cols = lambda h: pl.BlockSpec((h, tm), lambda i: (0, i))
    return pl.pallas_call(
        _inproj_kernel,
        grid=grid,
        in_specs=[rows(d), full(w_n.shape), full(w_t.shape), full(wuk_t.shape), full(kvg.shape),
                  full(kvg_col.shape), full(ikg.shape), full(ikb.shape)],
        out_specs=[cols(ATT_HEADS * KV_LATENT), cols(IDX_HEADS * LANES), cols(LANES),
                   rows(KV_LATENT), cols(KV_LATENT), rows(LANES), rows(4 * HG_HEADS * HG_DIM)],
        out_shape=[jax.ShapeDtypeStruct((ATT_HEADS * KV_LATENT, n), BF16),
                   jax.ShapeDtypeStruct((IDX_HEADS * LANES, n), BF16),
                   jax.ShapeDtypeStruct((LANES, n), F32),
                   jax.ShapeDtypeStruct((n, KV_LATENT), BF16),
                   jax.ShapeDtypeStruct((KV_LATENT, n), BF16),
                   jax.ShapeDtypeStruct((n, LANES), BF16),
                   jax.ShapeDtypeStruct((n, 4 * HG_HEADS * HG_DIM), F32)],
        compiler_params=pltpu.CompilerParams(dimension_semantics=("arbitrary",),
                                             vmem_limit_bytes=VMEM_LIMIT),
        name="inproj",
    )(x2, w_n, w_t, wuk_t, kvg, kvg_col, ikg, ikb)


def _hgrn2_kernel(hg_ref, lb_ref, ng_ref, o_ref, state_ref, *, tb):
    @pl.when(pl.program_id(1) == 0)
    def _():
        state_ref[...] = jnp.zeros_like(state_ref)

    nchunk = tb // HG_CHUNK
    width = HG_HEADS * HG_DIM
    row = lax.broadcasted_iota(I32, (tb, HG_DIM), 0)
    rin = jnp.bitwise_and(row, HG_CHUNK - 1)
    r2 = lax.broadcasted_iota(I32, (tb, tb), 0)
    c2 = lax.broadcasted_iota(I32, (tb, tb), 1)
    intra = jnp.logical_and(r2 // HG_CHUNK == c2 // HG_CHUNK, c2 <= r2)

    for h in range(HG_HEADS):
        sl = slice(h * HG_DIM, (h + 1) * HG_DIM)
        gq = hg_ref[0, :, h * HG_DIM:(h + 1) * HG_DIM]
        gf = hg_ref[0, :, width + h * HG_DIM:width + (h + 1) * HG_DIM]
        gi = hg_ref[0, :, 2 * width + h * HG_DIM:2 * width + (h + 1) * HG_DIM]
        gg = hg_ref[0, :, 3 * width + h * HG_DIM:3 * width + (h + 1) * HG_DIM]
        lb = lb_ref[:, sl]
        forget = lb + (1.0 - lb) * _sigmoid(gf)
        logf = jnp.log(forget)
        b = logf
        s = 1
        while s < HG_CHUNK:
            b = b + jnp.where(rin >= s, pltpu.roll(b, s, axis=0), 0.0)
            s *= 2
        tot = jnp.sum(logf.reshape(nchunk, HG_CHUNK, HG_DIM), axis=1, keepdims=True)
        bl = jnp.broadcast_to(tot, (nchunk, HG_CHUNK, HG_DIM)).reshape(tb, HG_DIM)
        kk = 1.0 - forget
        q_dec = (gq * _sigmoid(gq) * jnp.exp(b)).astype(BF16)
        k_inv = (kk * jnp.exp(-b)).astype(BF16)
        k_end = (kk * jnp.exp(bl - b)).astype(BF16)
        vb = gi.astype(BF16)
        scores = jnp.where(intra, _dot_nt(q_dec, k_inv), 0.0).astype(BF16)
        o = _dot(scores, vb)
        decay = jnp.exp(tot)
        chunk_rows = [slice(n * HG_CHUNK, (n + 1) * HG_CHUNK) for n in range(nchunk)]
        incs = [_dot_tn(vb[rs], k_end[rs]) for rs in chunk_rows]
        st = state_ref[h]
        starts = []
        for n in range(nchunk):
            starts.append(st.astype(BF16))
            st = st * decay[n] + incs[n]
        state_ref[h] = st
        inter = [_dot_nt(q_dec[rs], starts[n]) for n, rs in enumerate(chunk_rows)]
        o = o + jnp.concatenate(inter, axis=0)
        o = o * lax.rsqrt(jnp.mean(o * o, axis=-1, keepdims=True) + RMS_EPS)
        o = o * ng_ref[:, sl] * (gg * _sigmoid(gg))
        o_ref[0, :, sl] = o.astype(o_ref.dtype)


def _hgrn2(hg3, lb, ng, *, tb):
    bsz, t, w4 = hg3.shape
    width = HG_HEADS * HG_DIM
    return pl.pallas_call(
        functools.partial(_hgrn2_kernel, tb=tb),
        grid=(bsz, t // tb),
        in_specs=[pl.BlockSpec((1, tb, w4), lambda b, i: (b, i, 0)),
                  pl.BlockSpec((1, width), lambda b, i: (0, 0)),
                  pl.BlockSpec((1, width), lambda b, i: (0, 0))],
        out_specs=pl.BlockSpec((1, tb, width), lambda b, i: (b, i, 0)),
        out_shape=jax.ShapeDtypeStruct((bsz, t, width), BF16),
        scratch_shapes=[pltpu.VMEM((HG_HEADS, HG_DIM, HG_DIM), F32)],
        compiler_params=pltpu.CompilerParams(dimension_semantics=("arbitrary", "arbitrary"),
                                             vmem_limit_bytes=VMEM_LIMIT),
        name="hgrn2",
    )(hg3, lb, ng)


def _dsa_kernel(qt_ref, iqt_ref, iwt_ref, ik_ref, c_ref, ct_ref, wuvt_ref, o_ref,
                keys_ref, cut_ref, thr_ref, redo_ref, qall_ref, iqall_ref, m_ref, l_ref, acc_ref,
                *, topk, pos_bits):
    n = pl.program_id(1)
    t0 = n * Q_ROWS
    nch = (t0 + Q_ROWS + KEY_CHUNK - 1) // KEY_CHUNK
    kf = float(topk)
    key_i = lax.broadcasted_iota(I32, (KEY_CHUNK, Q_ROWS), 0)
    tq = t0 + lax.broadcasted_iota(I32, (KEY_CHUNK, Q_ROWS), 1)
    pair = 2 * Q_ROWS

    for h in range(ATT_HEADS):
        qall_ref[:, h * Q_ROWS:(h + 1) * Q_ROWS] = qt_ref[h * KV_LATENT:(h + 1) * KV_LATENT, :]
    for h in range(IDX_HEADS):
        iqall_ref[:, h * Q_ROWS:(h + 1) * Q_ROWS] = iqt_ref[h * LANES:(h + 1) * LANES, :]
    head_w = [iwt_ref[IDX_DIM + h:IDX_DIM + h + 1, :] for h in range(IDX_HEADS)]

    key_idx = lax.broadcasted_iota(I32, (INDEX_ROWS, Q_ROWS), 0)
    tq_idx = t0 + lax.broadcasted_iota(I32, (INDEX_ROWS, Q_ROWS), 1)

    def index_body(j, carry):
        for r in range(KEY_CHUNK // INDEX_ROWS):
            k0 = pl.multiple_of(j * KEY_CHUNK + r * INDEX_ROWS, INDEX_ROWS)
            ikc = ik_ref[pl.ds(k0, INDEX_ROWS), :]
            isc = jnp.zeros((INDEX_ROWS, Q_ROWS), F32)
            for hp in range(IDX_HEADS // 2):
                z = _dot(ikc, iqall_ref[:, hp * pair:(hp + 1) * pair])
                for hh in range(2):
                    isc = isc + head_w[2 * hp + hh] * jnp.maximum(z[:, hh * Q_ROWS:(hh + 1) * Q_ROWS], 0.0)
            isc = jnp.where(isc == 0.0, 0.0, isc)
            bits = pltpu.bitcast(isc, I32)
            key = jnp.bitwise_xor(bits, jnp.bitwise_and(jnp.right_shift(bits, 31), 0x7FFFFFFF))
            key = jnp.where(k0 + key_idx <= tq_idx, key, INT_MIN)
            keys_ref[pl.ds(k0, INDEX_ROWS), :] = key
        return carry

    lax.fori_loop(0, nch, index_body, 0)

    def count(pred):
        def body(j, acc):
            k0 = pl.multiple_of(j * KEY_CHUNK, KEY_CHUNK)
            hit = jnp.where(pred(keys_ref[pl.ds(k0, KEY_CHUNK), :], k0), 1.0, 0.0)
            return acc + jnp.sum(hit.reshape(KEY_CHUNK // COUNT_ROWS, COUNT_ROWS, Q_ROWS), axis=0)
        acc = lax.fori_loop(0, nch, body, jnp.zeros((COUNT_ROWS, Q_ROWS), F32))
        return jnp.sum(acc, axis=0, keepdims=True)

    def count_ge(cand):
        return count(lambda k, k0: k >= cand)

    def masked_max(bound):
        def body(j, acc):
            k0 = pl.multiple_of(j * KEY_CHUNK, KEY_CHUNK)
            key = keys_ref[pl.ds(k0, KEY_CHUNK), :]
            kept = jnp.where(key <= bound, key, INT_MIN)
            return jnp.maximum(acc, jnp.max(kept.reshape(KEY_CHUNK // COUNT_ROWS, COUNT_ROWS, Q_ROWS), axis=0))
        acc = lax.fori_loop(0, nch, body, jnp.full((COUNT_ROWS, Q_ROWS), INT_MIN, I32))
        return jnp.max(acc, axis=0, keepdims=True)

    n_all = (nch * KEY_CHUNK).astype(F32)
    n0 = count_ge(jnp.zeros((1, Q_ROWS), I32))
    pos0 = n0 >= kf
    state0 = (jnp.where(pos0, 0, INT_MIN).astype(I32), jnp.where(pos0, n0, n_all), jnp.where(pos0, 0.0, n0))

    def bracket_body(i, state):
        thr, n_lo, n_hi = state
        cand = thr + lax.shift_left(jnp.int32(1), 30 - i)
        cnt = count_ge(cand)
        ge = cnt >= kf
        return jnp.where(ge, cand, thr), jnp.where(ge, cnt, n_lo), jnp.where(ge, n_hi, cnt)

    thr1, n_lo, n_hi = lax.fori_loop(0, SEARCH_BITS, bracket_body, state0)
    low_bits = 31 - SEARCH_BITS
    has_topk = tq[0:1, :] + 1 >= topk
    in_bracket = n_lo - n_hi
    exact_cut = n_lo == kf
    few = jnp.logical_or(jnp.logical_not(has_topk), jnp.logical_or(exact_cut, in_bracket <= float(EXTRACT_MAX)))
    thr_ref[...] = jnp.broadcast_to(thr1, thr_ref.shape)
    redo_ref[0] = jnp.where(jnp.min(jnp.where(few, 1.0, 0.0)) > 0.0, 0, 1)

    @pl.when(redo_ref[0] == 0)
    def _():
        need = kf - n_hi
        bound = thr1 + (2 ** low_bits - 1)
        found = jnp.zeros((1, Q_ROWS), F32)
        thr_x = thr1
        for j in range(1, EXTRACT_MAX + 1):
            cur = masked_max(bound)
            inside = cur >= thr1
            found = found + jnp.where(inside, 1.0, 0.0)
            thr_x = jnp.where(jnp.logical_and(inside, need == float(j)), cur, thr_x)
            bound = jnp.where(inside, cur - 1, INT_MIN)
        use_x = jnp.logical_and(has_topk, jnp.logical_not(exact_cut))
        clash = jnp.logical_and(use_x, found != in_bracket)
        thr_ref[...] = jnp.broadcast_to(jnp.where(use_x, thr_x, thr1), thr_ref.shape)
        redo_ref[0] = jnp.where(jnp.max(jnp.where(clash, 1.0, 0.0)) > 0.0, 1, 0)

    @pl.when(redo_ref[0] == 1)
    def _():
        def bit_body(i, thr):
            cand = thr + lax.shift_left(jnp.int32(1), low_bits - 1 - i)
            return jnp.where(count_ge(cand) >= kf, cand, thr)

        thr_ref[...] = jnp.broadcast_to(lax.fori_loop(0, low_bits, bit_body, thr1), thr_ref.shape)

    thr = thr_ref[0:1, :]
    live = thr > INT_MIN
    cut_ref[...] = jnp.broadcast_to(jnp.where(live, jnp.int32(2 ** 30), -1), cut_ref.shape)

    @pl.when(redo_ref[0] == 1)
    def _():
        n_ge = count_ge(thr)
        tie = jnp.max(jnp.where(jnp.logical_and(live, n_ge > kf), 1.0, 0.0))

        @pl.when(tie > 0.0)
        def _():
            need = kf - count(lambda k, k0: k > thr)

            def cut_body(i, cut):
                cand = cut + lax.shift_left(jnp.int32(1), pos_bits - 1 - i)
                below = count(lambda k, k0: jnp.logical_and(k == thr, k0 + key_i < cand))
                return jnp.where(below < need, cand, cut)

            cut = lax.fori_loop(0, pos_bits, cut_body, jnp.zeros((1, Q_ROWS), I32))
            cut_ref[...] = jnp.broadcast_to(jnp.where(live, cut, -1), cut_ref.shape)

    m_ref[...] = jnp.full(m_ref.shape, NEG_BIG, F32)
    l_ref[...] = jnp.zeros(l_ref.shape, F32)
    acc_ref[...] = jnp.zeros(acc_ref.shape, F32)
    cut_b = cut_ref[0:1, :]

    def attn_body(j, carry):
        k0 = pl.multiple_of(j * KEY_CHUNK, KEY_CHUNK)
        key = keys_ref[pl.ds(k0, KEY_CHUNK), :]
        pos = k0 + key_i
        sel = jnp.logical_or(key > thr, jnp.logical_and(key == thr, pos <= cut_b))
        rel = jnp.where(sel, (pos - t0).astype(F32), NEG_BIG)
        cc = c_ref[pl.ds(k0, KEY_CHUNK), :]
        cct = ct_ref[:, pl.ds(k0, KEY_CHUNK)]
        for hp in range(ATT_HEADS // 2):
            cs = slice(hp * pair, (hp + 1) * pair)
            s2 = _dot(cc, qall_ref[:, cs])
            ps, alphas = [], []
            for hh in range(2):
                h = 2 * hp + hh
                slope = LOG2_E * 2.0 ** (-(8.0 / ATT_HEADS) * (h + 1))
                s = s2[:, hh * Q_ROWS:(hh + 1) * Q_ROWS] + slope * rel
                m_old = m_ref[h:h + 1, :]
                m_new = jnp.maximum(m_old, jnp.max(s, axis=0, keepdims=True))
                alpha = jnp.exp2(m_old - m_new)
                p = jnp.exp2(s - m_new)
                l_ref[h:h + 1, :] = alpha * l_ref[h:h + 1, :] + jnp.sum(p, axis=0, keepdims=True)
                m_ref[h:h + 1, :] = m_new
                ps.append(p.astype(BF16))
                alphas.append(alpha)
            acc_ref[:, cs] = (acc_ref[:, cs] * jnp.concatenate(alphas, axis=1)
                              + _dot(cct, jnp.concatenate(ps, axis=1)))
        return carry

    lax.fori_loop(0, nch, attn_body, 0)

    yt = []
    for h in range(ATT_HEADS):
        o_lat = (acc_ref[:, h * Q_ROWS:(h + 1) * Q_ROWS] / l_ref[h:h + 1, :]).astype(BF16)
        yt.append(_dot(wuvt_ref[h], o_lat))
    o_ref[...] = jnp.concatenate(yt, axis=0).T.astype(o_ref.dtype)


def _dsa(qt, iqt, iwt, ik, c, ct, wuvt_p, *, bsz, t):
    n = bsz * t
    nq = t // Q_ROWS
    topk = min(TOPK_MAX, t // 4)
    assert t % KEY_CHUNK == 0
    width = ATT_HEADS * ATT_HEAD_DIM
    qcols = lambda h: pl.BlockSpec((h, Q_ROWS), lambda b, i: (0, b * nq + i))
    return pl.pallas_call(
        functools.partial(_dsa_kernel, topk=topk, pos_bits=(t - 1).bit_length()),
        grid=(bsz, nq),
        in_specs=[qcols(ATT_HEADS * KV_LATENT), qcols(IDX_HEADS * LANES), qcols(LANES),
                  pl.BlockSpec((t, LANES), lambda b, i: (b, 0)),
                  pl.BlockSpec((t, KV_LATENT), lambda b, i: (b, 0)),
                  pl.BlockSpec((KV_LATENT, t), lambda b, i: (0, b)),
                  pl.BlockSpec(wuvt_p.shape, lambda b, i: (0, 0, 0))],
        out_specs=pl.BlockSpec((Q_ROWS, width), lambda b, i: (b * nq + i, 0)),
        out_shape=jax.ShapeDtypeStruct((n, width), BF16),
        scratch_shapes=[pltpu.VMEM((t, Q_ROWS), I32),
                        pltpu.VMEM((8, Q_ROWS), I32),
                        pltpu.VMEM((8, Q_ROWS), I32),
                        pltpu.SMEM((1,), I32),
                        pltpu.VMEM((KV_LATENT, ATT_HEADS * Q_ROWS), BF16),
                        pltpu.VMEM((LANES, IDX_HEADS * Q_ROWS), BF16),
                        pltpu.VMEM((8, Q_ROWS), F32),
                        pltpu.VMEM((8, Q_ROWS), F32),
                        pltpu.VMEM((KV_LATENT, ATT_HEADS * Q_ROWS), F32)],
        compiler_params=pltpu.CompilerParams(dimension_semantics=("arbitrary", "arbitrary"),
                                             vmem_limit_bytes=VMEM_LIMIT),
        name="dsa",
    )(qt, iqt, iwt, ik, c, ct, wuvt_p)


def _mix_router_kernel(x_ref, ya_ref, yr_ref, wo_ref, g_ref, b_ref, wr_ref, br_ref,
                       x1_ref, x1w_ref, route_ref, tot_ref, carry_ref, *, alpha, tm):
    @pl.when(pl.program_id(0) == 0)
    def _():
        carry_ref[...] = jnp.zeros_like(carry_ref)

    half = ya_ref.shape[1]
    mix = _dot(ya_ref[...], wo_ref[0:half, :]) + _dot(yr_ref[...], wo_ref[half:, :])
    x1 = _layer_norm(alpha * x_ref[...] + mix, g_ref[...], b_ref[...])
    x1_ref[...] = x1
    x1w_ref[...] = _pack_words(x1)

    x_hi = x1.astype(BF16)
    x_lo = (x1 - x_hi.astype(F32)).astype(BF16)
    hh_hl = _dot(x_hi, wr_ref[...])
    logits = (hh_hl[:, :LANES] + hh_hl[:, LANES:] + _dot(x_lo, wr_ref[:, :LANES])) + br_ref[...]
    lane = lax.broadcasted_iota(I32, logits.shape, 1)
    lane_f = lane.astype(F32)
    work = jnp.where(lane < N_EXPERTS, logits, -jnp.inf)
    sel_f = jnp.zeros(logits.shape, F32)
    denom = jnp.zeros((tm, 1), F32)
    hits, experts, weights = [], [], []
    for _ in range(TOP_K):
        mx = jnp.max(work, axis=-1, keepdims=True)
        first = jnp.min(jnp.where(work == mx, lane_f, float(LANES)), axis=-1, keepdims=True)
        hit = lane_f == first
        e = jnp.exp(mx - (weights[0][1] if weights else mx))
        hits.append(hit)
        experts.append(first)
        weights.append((e, mx))
        denom = denom + e
        sel_f = sel_f + jnp.where(hit, 1.0, 0.0)
        work = jnp.where(hit, -jnp.inf, work)

    r2 = lax.broadcasted_iota(I32, (tm, tm), 0)
    c2 = lax.broadcasted_iota(I32, (tm, tm), 1)
    before = jnp.where(c2 < r2, 1.0, 0.0).astype(BF16)
    rank = _dot(before, sel_f.astype(BF16)) + carry_ref[...]
    carry = carry_ref[...] + jnp.sum(sel_f, axis=0, keepdims=True)
    carry_ref[...] = carry
    tot_ref[...] = jnp.broadcast_to(carry, tot_ref.shape)

    route = jnp.zeros(logits.shape, F32)
    for k in range(TOP_K):
        rank_k = jnp.sum(jnp.where(hits[k], rank, 0.0), axis=-1, keepdims=True)
        route = jnp.where(lane == k, experts[k], route)
        route = jnp.where(lane == TOP_K + k, rank_k, route)
        route = jnp.where(lane == 2 * TOP_K + k, weights[k][0] / denom, route)
    route_ref[...] = route


def _mix_router(x2, ya, yr, wo, g, b, wr_p, br_p, *, alpha, tm):
    n, d = x2.shape
    half = ya.shape[1]
    rows = lambda w: pl.BlockSpec((tm, w), lambda i: (i, 0))
    full = lambda shape: pl.BlockSpec(shape, lambda i: (0,) * len(shape))
    return pl.pallas_call(
        functools.partial(_mix_router_kernel, alpha=alpha, tm=tm),
        grid=(n // tm,),
        in_specs=[rows(d), rows(half), rows(half), full(wo.shape), full(g.shape), full(b.shape),
                  full(wr_p.shape), full(br_p.shape)],
        out_specs=[rows(d), rows(d // 2), rows(LANES), full((8, LANES))],
        out_shape=[jax.ShapeDtypeStruct((n, d), F32),
                   jax.ShapeDtypeStruct((n, d // 2), I32),
                   jax.ShapeDtypeStruct((n, LANES), F32),
                   jax.ShapeDtypeStruct((8, LANES), F32)],
        scratch_shapes=[pltpu.VMEM((1, LANES), F32)],
        compiler_params=pltpu.CompilerParams(dimension_semantics=("arbitrary",),
                                             vmem_limit_bytes=VMEM_LIMIT),
        name="mix_router",
    )(x2, ya, yr, wo, g, b, wr_p, br_p)


def _sc_gather_rows(table, idx):
    n_idx = idx.shape[0]
    width = table.shape[1]
    per_worker = n_idx // SC_WORKERS
    assert per_worker * SC_WORKERS == n_idx and per_worker % SC_ROWS == 0
    mesh = plsc.VectorSubcoreMesh(core_axis_name="c", subcore_axis_name="s",
                                  num_cores=SC_CORES, num_subcores=SC_SUBCORES)

    @functools.partial(
        pl.kernel, mesh=mesh,
        out_type=jax.ShapeDtypeStruct((n_idx, width), table.dtype),
        scratch_types=[pltpu.VMEM((SC_ROWS,), I32),
                       pltpu.VMEM((SC_ROWS, width), table.dtype),
                       pltpu.SemaphoreType.DMA])
    def gather(table_hbm, idx_hbm, out_hbm, idx_v, rows_v, sem):
        worker = lax.axis_index("s") * SC_CORES + lax.axis_index("c")

        @pl.loop(0, per_worker // SC_ROWS)
        def _(step):
            base = pl.multiple_of(worker * per_worker + step * SC_ROWS, SC_ROWS)
            pltpu.sync_copy(idx_hbm.at[pl.ds(base, SC_ROWS)], idx_v)
            pltpu.async_copy(table_hbm.at[idx_v], rows_v, sem).wait()
            pltpu.sync_copy(rows_v, out_hbm.at[pl.ds(base, SC_ROWS)])

    return gather(table, idx)


def _sc_scatter_rows(rows, dest3, cap):
    n, width = rows.shape
    chunks, picks, r = dest3.shape
    assert r == SC_SCATTER_ROWS and chunks * r == n and chunks % SC_WORKERS == 0
    per_worker = chunks // SC_WORKERS
    mesh = plsc.VectorSubcoreMesh(core_axis_name="c", subcore_axis_name="s",
                                  num_cores=SC_CORES, num_subcores=SC_SUBCORES)

    @functools.partial(
        pl.kernel, mesh=mesh,
        out_type=jax.ShapeDtypeStruct((cap, width), rows.dtype),
        scratch_types=[pltpu.VMEM((picks, r), I32),
                       pltpu.VMEM((r, width), rows.dtype),
                       pltpu.SemaphoreType.DMA])
    def scatter(rows_hbm, dest_hbm, out_hbm, idx_v, rows_v, sem):
        worker = lax.axis_index("s") * SC_CORES + lax.axis_index("c")

        @pl.loop(0, per_worker)
        def _(step):
            chunk = worker * per_worker + step
            base = pl.multiple_of(chunk * r, r)
            pltpu.sync_copy(rows_hbm.at[pl.ds(base, r)], rows_v)
            pltpu.sync_copy(dest_hbm.at[chunk], idx_v)
            for k in range(picks):
                pltpu.async_copy(rows_v, out_hbm.at[idx_v.at[k]], sem).wait()

    return scatter(rows, dest3)


def _pack_words(v):
    k = v.shape[1] // 2
    hi = pltpu.bitcast(v[:, :k].astype(BF16).astype(F32), I32)
    lo = pltpu.bitcast(v[:, k:].astype(BF16).astype(F32), I32)
    return jnp.bitwise_or(hi, lax.shift_right_logical(lo, jnp.int32(16)))


def _unpack_words(w, dtype=BF16):
    hi = pltpu.bitcast(jnp.bitwise_and(w, jnp.int32(-65536)), F32)
    lo = pltpu.bitcast(jnp.left_shift(w, 16), F32)
    return jnp.concatenate([hi, lo], axis=1).astype(dtype)


def _expert_kernel(exp_ref, blk_ref, new_ref, x_ref, wup_ref, bup_ref, wdn_ref, bdn_ref, o_ref,
                   wup_s, wdn_s, *, d_ff):
    del exp_ref, blk_ref

    @pl.when(new_ref[pl.program_id(0)] == 1)
    def _():
        wup_s[...] = wup_ref[0].astype(BF16)
        wdn_s[...] = wdn_ref[0].astype(BF16)

    half = MOE_ROWS // 2
    for r in range(2):
        rows = slice(r * half, (r + 1) * half)
        h = _dot(_unpack_words(x_ref[rows, :]), wup_s[...]) + bup_ref[0]
        glu = jnp.minimum(h[:, :d_ff], SWIGLU_LIMIT)
        lin = jnp.clip(h[:, d_ff:], -SWIGLU_LIMIT, SWIGLU_LIMIT)
        a = glu * _sigmoid(SWIGLU_ALPHA * glu) * (lin + 1.0)
        o_ref[rows, :] = _pack_words(_dot(a.astype(BF16), wdn_s[...]) + bdn_ref[0])


def _experts(blk_expert, blk_index, blk_new, xs, wup, bup, wdn, bdn):
    cap, words = xs.shape
    d = 2 * words
    d_ff = wdn.shape[1]
    row_map = lambda i, exp, blk, new: (blk[i], 0)
    exp_map3 = lambda i, exp, blk, new: (exp[i], 0, 0)
    grid_spec = pltpu.PrefetchScalarGridSpec(
        num_scalar_prefetch=3,
        grid=(cap // MOE_ROWS,),
        in_specs=[pl.BlockSpec((MOE_ROWS, words), row_map),
                  pl.BlockSpec((1, d, 2 * d_ff), exp_map3),
                  pl.BlockSpec((1, 1, 2 * d_ff), exp_map3),
                  pl.BlockSpec((1, d_ff, d), exp_map3),
                  pl.BlockSpec((1, 1, d), exp_map3)],
        out_specs=pl.BlockSpec((MOE_ROWS, words), row_map),
        scratch_shapes=[pltpu.VMEM((d, 2 * d_ff), BF16), pltpu.VMEM((d_ff, d), BF16)],
    )
    return pl.pallas_call(
        functools.partial(_expert_kernel, d_ff=d_ff),
        grid_spec=grid_spec,
        out_shape=jax.ShapeDtypeStruct((cap, words), I32),
        compiler_params=pltpu.CompilerParams(dimension_semantics=("arbitrary",),
                                             vmem_limit_bytes=VMEM_LIMIT),
        name="experts",
    )(blk_expert, blk_index, blk_new, xs, wup, bup, wdn, bdn)


def _route_tables(route, totals, *, cap):
    n = route.shape[0]
    expert = route[:, 0:TOP_K].astype(I32)
    rank = route[:, TOP_K:2 * TOP_K].astype(I32)
    counts = totals[0, :N_EXPERTS].astype(I32)
    padded = (counts + MOE_ROWS - 1) // MOE_ROWS * MOE_ROWS
    pad_end = jnp.cumsum(padded)
    pad_start = pad_end - padded
    dest = pad_start[expert] + rank
    dest_chunks = dest.reshape(n // SC_SCATTER_ROWS, SC_SCATTER_ROWS, TOP_K).transpose(0, 2, 1)
    n_blk = cap // MOE_ROWS
    used = pad_end[-1] // MOE_ROWS
    blk = jnp.minimum(jnp.arange(n_blk, dtype=I32), used - 1)
    blk_expert = jnp.minimum(jnp.sum(pad_end[None, :] <= (blk * MOE_ROWS)[:, None], axis=1),
                             N_EXPERTS - 1).astype(I32)
    blk_new = jnp.concatenate([jnp.ones((1,), I32), (blk_expert[1:] != blk_expert[:-1]).astype(I32)])
    return dest, dest_chunks, blk_expert, blk, blk_new


def _final_kernel(x1_ref, ya_ref, yb_ref, yc_ref, yd_ref, route_ref, p_ref, g1_ref, b1_ref,
                  wp_ref, wg_ref, g2_ref, b2_ref, *rest, alpha):
    o_ref = rest[-1]
    half = x1_ref.shape[0] // 2
    for r in range(2):
        rows = slice(r * half, (r + 1) * half)
        ple_lin = _dot(p_ref[rows, :].astype(BF16), wp_ref[...])
        ffn = jnp.zeros((half, x1_ref.shape[1]), F32)
        for k, yk_ref in enumerate((ya_ref, yb_ref, yc_ref, yd_ref)):
            gate_k = route_ref[rows, 2 * TOP_K + k:2 * TOP_K + k + 1]
            ffn = ffn + gate_k * _unpack_words(yk_ref[rows, :], F32)
        x2 = _layer_norm(alpha * x1_ref[rows, :] + ffn, g1_ref[...], b1_ref[...])
        gate = _sigmoid(_dot(x2.astype(BF16), wg_ref[...]))
        o_ref[rows, :] = _layer_norm(alpha * x2 + ple_lin * gate, g2_ref[...], b2_ref[...])


def _final(x1, y4w, route, p2, g1, b1, wp, wg, g2, b2, *, alpha, tm, part, parts, prev=None):
    n, d = x1.shape
    assert TOP_K == 4
    steps = n // parts // tm
    off = part * steps
    rows = lambda w: pl.BlockSpec((tm, w), lambda i: (off + i, 0))
    full = lambda shape: pl.BlockSpec(shape, lambda i: (0,) * len(shape))
    pick = lambda k: pl.BlockSpec((tm, d // 2), lambda i: (k * steps + i, 0))
    in_specs = [rows(d), pick(0), pick(1), pick(2), pick(3), rows(LANES), rows(p2.shape[1]),
                full(g1.shape), full(b1.shape),
                full(wp.shape), full(wg.shape), full(g2.shape), full(b2.shape)]
    args = [x1, y4w, y4w, y4w, y4w, route, p2, g1, b1, wp, wg, g2, b2]
    aliases = {}
    if prev is not None:
        in_specs.append(pl.BlockSpec(memory_space=pl.ANY))
        args.append(prev)
        aliases = {len(args) - 1: 0}
    return pl.pallas_call(
        functools.partial(_final_kernel, alpha=alpha),
        grid=(steps,),
        in_specs=in_specs,
        out_specs=rows(d),
        out_shape=jax.ShapeDtypeStruct((n, d), F32),
        input_output_aliases=aliases,
        compiler_params=pltpu.CompilerParams(dimension_semantics=("arbitrary",),
                                             vmem_limit_bytes=VMEM_LIMIT),
        name="final",
    )(*args)


def _pack_w_in(w_in):
    d = w_in.shape[0]
    sizes = (ATT_HEADS * ATT_HEAD_DIM, KV_LATENT, IDX_HEADS * IDX_DIM, IDX_DIM, IDX_HEADS,
             HG_HEADS * HG_DIM, HG_HEADS * HG_DIM, HG_HEADS * HG_DIM, HG_HEADS * HG_DIM)
    offs = [0]
    for s in sizes:
        offs.append(offs[-1] + s)
    sec = [w_in[:, offs[k]:offs[k + 1]] for k in range(len(sizes))]
    iq = sec[2].reshape(d, IDX_HEADS, IDX_DIM)
    iq = jnp.pad(iq, ((0, 0), (0, 0), (0, LANES - IDX_DIM))).reshape(d, IDX_HEADS * LANES)
    ikw = jnp.pad(jnp.concatenate([sec[3], sec[4]], axis=1), ((0, 0), (0, LANES - IDX_DIM - IDX_HEADS)))
    w_n = jnp.concatenate([sec[1], ikw, sec[5], sec[6], sec[7], sec[8]], axis=1).astype(BF16)
    w_t = jnp.concatenate([sec[0], iq, ikw, sec[1]], axis=1).T.astype(BF16)
    return w_n, w_t


def _block_diag_uk_t(w_uk):
    eye = jnp.eye(ATT_HEADS, dtype=w_uk.dtype)
    bd = jnp.einsum("rhd,hg->hrgd", w_uk, eye)
    return bd.reshape(ATT_HEADS * KV_LATENT, ATT_HEADS * ATT_HEAD_DIM).astype(BF16)


def _uv_t(w_uv):
    return jnp.transpose(w_uv, (1, 2, 0)).astype(BF16)


def _layer(x, p_l, w_in, kv_g, ik_g, ik_b, w_uk, w_uv, lb, hg_ng, w_o, ln_mix_g, ln_mix_b,
           w_router, b_router, w_up, b_up, w_down, b_down, ln_ffn_g, ln_ffn_b,
           w_ple_proj, w_ple_gate, ln_ple_g, ln_ple_b, *, alpha, tm, hg_tb):
    bsz, t, d = x.shape
    n = bsz * t
    x2 = x.reshape(n, d)
    row = lambda v: v.reshape(1, -1).astype(F32)
    pad_lane = lambda v: jnp.pad(row(v), ((0, 0), (0, LANES - v.shape[-1])))

    w_n, w_t = _pack_w_in(w_in)
    qt, iqt, iwt, c, ct, ik, hg = _inproj(x2, w_n, w_t, _block_diag_uk_t(w_uk), row(kv_g),
                                          kv_g.reshape(-1, 1).astype(F32), pad_lane(ik_g), pad_lane(ik_b), tm=tm)
    y_rec = _hgrn2(hg.reshape(bsz, t, -1), row(lb), row(hg_ng), tb=hg_tb)
    y_att = _dsa(qt, iqt, iwt, ik, c, ct, _uv_t(w_uv), bsz=bsz, t=t)

    wr_f = jnp.pad(w_router.astype(F32), ((0, 0), (0, LANES - N_EXPERTS)))
    wr_hi = wr_f.astype(BF16)
    wr_p = jnp.concatenate([wr_hi, (wr_f - wr_hi.astype(F32)).astype(BF16)], axis=1)
    x1, x1w, route, totals = _mix_router(
        x2, y_att, y_rec.reshape(n, -1), w_o.astype(BF16), row(ln_mix_g), row(ln_mix_b),
        wr_p, pad_lane(b_router), alpha=alpha, tm=tm)

    step = SC_WORKERS * SC_ROWS
    cap = -(-(n * TOP_K + N_EXPERTS * MOE_ROWS) // step) * step
    dest, dest_chunks, blk_expert, blk_index, blk_new = _route_tables(route, totals, cap=cap)
    d_ff = w_down.shape[1]
    xs = _sc_scatter_rows(x1w, dest_chunks, cap)
    outw = _experts(blk_expert, blk_index, blk_new, xs,
                    w_up.astype(F32), b_up.reshape(N_EXPERTS, 1, 2 * d_ff).astype(F32),
                    w_down.astype(F32), b_down.reshape(N_EXPERTS, 1, d).astype(F32))

    unit = max(tm, SC_WORKERS * SC_ROWS // TOP_K)
    parts = next(c for c in (COMBINE_PARTS, 2, 1) if n % (c * unit) == 0)
    out = None
    for part in range(parts):
        rows_p = slice(part * (n // parts), (part + 1) * (n // parts))
        y4w = _sc_gather_rows(outw, dest[rows_p].T.reshape(-1))
        out = _final(x1, y4w, route, p_l.reshape(n, -1), row(ln_ffn_g), row(ln_ffn_b),
                     w_ple_proj.astype(BF16), w_ple_gate.astype(BF16), row(ln_ple_g), row(ln_ple_b),
                     alpha=alpha, tm=tm, part=part, parts=parts, prev=out)
    return out.reshape(bsz, t, d)


def kernel(x, p, w_in, kv_norm_g, idx_k_norm_g, idx_k_norm_b, w_uk, w_uv, hg_lb_logits, hg_norm_g, w_o,
           ln_mix_g, ln_mix_b, w_router, b_router, w_up, b_up, w_down, b_down, ln_ffn_g, ln_ffn_b,
           w_ple_proj, w_ple_gate, ln_ple_g, ln_ple_b):
    depth = w_in.shape[0]
    alpha = (2.0 * depth) ** 0.25
    lower_bounds = jnp.cumsum(jax.nn.softmax(hg_lb_logits.astype(F32), axis=0), axis=0)
    n = x.shape[0] * x.shape[1]
    tm = min(512, n)
    hg_tb = min(512, x.shape[1])
    for l in range(depth):
        x = _layer(x, p[l], w_in[l], kv_norm_g[l], idx_k_norm_g[l], idx_k_norm_b[l], w_uk[l], w_uv[l],
                   lower_bounds[l], hg_norm_g[l], w_o[l], ln_mix_g[l], ln_mix_b[l], w_router[l],
                   b_router[l], w_up[l], b_up[l], w_down[l], b_down[l], ln_ffn_g[l], ln_ffn_b[l],
                   w_ple_proj[l], w_ple_gate[l], ln_ple_g[l], ln_ple_b[l],
                   alpha=alpha, tm=tm, hg_tb=hg_tb)
    return x
```
